```python
import jax, jax.numpy as jnp
from jax import lax
import numpy as np

D_MODEL = 2048
BATCH = 4
SEQ = 4096
DEPTH = 2

N_META = 16
POOL_WINDOWS = (2, 4, 8, 16)
POOL_WIDTH = D_MODEL // 2
POOL_GROUP = POOL_WIDTH // len(POOL_WINDOWS)
N_HEADS = 16
N_KV_HEADS = 4
HEAD_DIM = 64
Q_PER_KV = N_HEADS // N_KV_HEADS
WINDOW = 128
BLOCK = 128
NEG = -1e30
LRU_WIDTH = D_MODEL // 2
LRU_BLOCKS = 4
LRU_BLOCK = LRU_WIDTH // LRU_BLOCKS
CONV_WIDTH = 4
LRU_C = 8.0
N_GROUPS = 4
EXPERTS_PER_GROUP = 8
N_EXPERTS = N_GROUPS * EXPERTS_PER_GROUP
TOP_K = 2
D_EXPERT = D_MODEL // 4
MOE_BLOCK = 128
LN_EPS = 1e-5
IN_SIZES = (POOL_WIDTH, N_HEADS * HEAD_DIM, N_KV_HEADS * HEAD_DIM, N_KV_HEADS * HEAD_DIM,
            LRU_WIDTH, LRU_WIDTH, 3 * D_MODEL)
IN_COLS = int(sum(IN_SIZES))
IN_OFFSETS = tuple(int(o) for o in np.cumsum(IN_SIZES)[:-1])

kernel_name = 'hybrid_pool_swa_rglru_hmoe_deepnorm'


def layer_norm(x, g, b):
    xf = x.astype(jnp.float32)
    mu = xf.mean(-1, keepdims=True)
    var = jnp.square(xf - mu).mean(-1, keepdims=True)
    y = (xf - mu) * lax.rsqrt(var + LN_EPS) * g.astype(jnp.float32) + b.astype(jnp.float32)
    return y.astype(x.dtype)


def pool_mixer(u, pool_w, pool_scale):
    B, L, _ = u.shape
    uf = u.astype(jnp.float32).reshape(B, L, len(POOL_WINDOWS), POOL_GROUP)
    maxw = max(POOL_WINDOWS)
    cs = jnp.pad(jnp.cumsum(uf, axis=1), ((0, 0), (maxw, 0), (0, 0), (0, 0)))
    t = jnp.arange(L)
    outs = []
    for gi, w in enumerate(POOL_WINDOWS):
        win_sum = cs[:, maxw:, gi] - cs[:, maxw - w:maxw - w + L, gi]
        cnt = jnp.minimum(t + 1, w).astype(jnp.float32)
        outs.append(win_sum / cnt[None, :, None])
    delta = (jnp.stack(outs, axis=2) - uf).astype(u.dtype)
    mixed = jnp.einsum('blgc,gce->blge', delta, pool_w).reshape(B, L, POOL_WIDTH)
    return mixed * pool_scale


def swa_attention(q, k, v, sink):
    B, L = q.shape[:2]
    pad = BLOCK - N_META
    Lp = L + pad
    nb = Lp // BLOCK
    f32 = jnp.float32
    qb = jnp.pad(q.astype(f32), ((0, 0), (pad, 0), (0, 0), (0, 0))).reshape(
        B, nb, BLOCK, N_KV_HEADS, Q_PER_KV, HEAD_DIM)
    kb = jnp.pad(k.astype(f32), ((0, 0), (pad + BLOCK, 0), (0, 0), (0, 0))).reshape(
        B, nb + 1, BLOCK, N_KV_HEADS, HEAD_DIM)
    vb = jnp.pad(v.astype(f32), ((0, 0), (pad + BLOCK, 0), (0, 0), (0, 0))).reshape(
        B, nb + 1, BLOCK, N_KV_HEADS, HEAD_DIM)
    kw = jnp.concatenate([kb[:, :-1], kb[:, 1:]], axis=2)
    vw = jnp.concatenate([vb[:, :-1], vb[:, 1:]], axis=2)
    dist = BLOCK + jnp.arange(BLOCK)[:, None] - jnp.arange(2 * BLOCK)[None, :]
    in_window = (dist >= 0) & (dist < WINDOW)
    k_pos = (jnp.arange(nb)[:, None] - 1) * BLOCK + jnp.arange(2 * BLOCK)[None, :]
    mask = in_window[None] & (k_pos >= pad)[:, None, :]
    slopes = 2.0 ** (-8.0 * jnp.arange(1, N_HEADS + 1, dtype=f32) / N_HEADS)
    alibi = -slopes.reshape(N_KV_HEADS, Q_PER_KV, 1, 1) * dist.astype(f32)
    s = jnp.einsum('bnqkgd,bnskd->bnkgqs', qb, kw) * (HEAD_DIM ** -0.5) + alibi
    s = jnp.where(mask[None, :, None, None], s, NEG)
    sk = sink.astype(f32).reshape(N_KV_HEADS, Q_PER_KV, 1)
    m = jnp.maximum(s.max(-1), sk)
    p = jnp.exp(s - m[..., None])
    den = p.sum(-1) + jnp.exp(sk - m)
    o = jnp.einsum('bnkgqs,bnskd->bnkgqd', p, vw) / den[..., None]
    o = o.transpose(0, 1, 4, 2, 3, 5).reshape(B, Lp, N_HEADS * HEAD_DIM)[:, pad:]
    return o.astype(q.dtype)


def rglru_branch(xr, yr, conv_w, conv_b, wa, ba, wx, bx, lam):
    B, L, C = xr.shape
    xc = lax.conv_general_dilated(xr, conv_w[:, None, :], window_strides=(1,),
                                  padding=[(CONV_WIDTH - 1, 0)],
                                  dimension_numbers=('NWC', 'WIO', 'NWC'),
                                  feature_group_count=C) + conv_b
    xblk = xc.reshape(B, L, LRU_BLOCKS, LRU_BLOCK)
    gate_a = jax.nn.sigmoid(jnp.einsum('blhc,hce->blhe', xblk, wa).reshape(B, L, C) + ba)
    gate_x = jax.nn.sigmoid(jnp.einsum('blhc,hce->blhe', xblk, wx).reshape(B, L, C) + bx)
    log_a = -LRU_C * gate_a.astype(jnp.float32) * jax.nn.softplus(-lam.astype(jnp.float32))
    a = jnp.exp(log_a)
    b_in = jnp.sqrt(-jnp.expm1(2.0 * log_a)) * gate_x.astype(jnp.float32) * xc.astype(jnp.float32)

    def combine(left, right):
        return (left[0] * right[0], right[0] * left[1] + right[1])

    _, h = lax.associative_scan(combine, (a, b_in), axis=1)
    return (h * jax.nn.gelu(yr.astype(jnp.float32))).astype(xr.dtype)


def hier_moe(u, wg, bg, we, be, w_gate, w_up, w_down):
    B, L, D = u.shape
    N = B * L
    f32 = jnp.float32
    xf = u.reshape(N, D)
    grp_logits = (xf @ wg).astype(f32) + bg.astype(f32)
    p_grp = jax.nn.softmax(grp_logits, axis=-1)
    g = jnp.argmax(grp_logits, axis=-1)
    p_g = jnp.take_along_axis(p_grp, g[:, None], axis=1)
    exp_logits = ((xf @ we).astype(f32) + be.astype(f32)).reshape(N, N_GROUPS, EXPERTS_PER_GROUP)
    sel = jnp.take_along_axis(exp_logits, g[:, None, None], axis=1)[:, 0]
    top_p, top_i = lax.top_k(jax.nn.softmax(sel, axis=-1), TOP_K)
    wts = p_g * top_p / top_p.sum(-1, keepdims=True)
    eid = g[:, None] * EXPERTS_PER_GROUP + top_i
    A = N * TOP_K
    flat_e = eid.reshape(A).astype(jnp.int32)
    flat_w = wts.reshape(A)
    order = jnp.argsort(flat_e)
    se = flat_e[order]
    counts = jnp.bincount(flat_e, length=N_EXPERTS)
    start = jnp.cumsum(counts) - counts
    padded = (counts + MOE_BLOCK - 1) // MOE_BLOCK * MOE_BLOCK
    pad_end = jnp.cumsum(padded)
    pad_start = pad_end - padded
    dest = pad_start[se] + jnp.arange(A) - start[se]
    n_blocks = -(-A // MOE_BLOCK) + N_EXPERTS
    cap = n_blocks * MOE_BLOCK
    slot_tok = jnp.full((cap,), N, jnp.int32).at[dest].set((order // TOP_K).astype(jnp.int32))
    slot_w = jnp.zeros((cap,), f32).at[dest].set(flat_w[order])
    block_e = jnp.minimum(jnp.searchsorted(pad_end, jnp.arange(n_blocks) * MOE_BLOCK, side='right'),
                          N_EXPERTS - 1)
    x_pad = jnp.concatenate([xf, jnp.zeros((1, D), xf.dtype)], axis=0)

    def run_block(args):
        tok, wt, e = args
        xb = x_pad[tok]
        hdn = jax.nn.silu(xb @ w_gate[e]) * (xb @ w_up[e])
        return (hdn @ w_down[e]) * wt[:, None].astype(xb.dtype)

    yb = lax.map(run_block, (slot_tok.reshape(n_blocks, MOE_BLOCK),
                             slot_w.reshape(n_blocks, MOE_BLOCK), block_e))
    y = jnp.zeros((N + 1, D), u.dtype).at[slot_tok].add(yb.reshape(cap, D).astype(u.dtype))
    return y[:N].reshape(B, L, D)


def setup_inputs(seed: int = 0) -> dict:
    key = jax.random.key(seed)
    ks = iter(jax.random.split(key, 40))
    f32 = jnp.float32
    D = D_MODEL
    beta = (8.0 * DEPTH) ** -0.25

    def nrm(shape, scale):
        return jax.random.normal(next(ks), shape, f32) * scale

    u = jax.random.uniform(next(ks), (DEPTH, LRU_WIDTH), f32, 0.9, 0.999)
    a0 = u ** (1.0 / LRU_C)
    lru_lambda = jnp.log(a0) - jnp.log1p(-a0)
    return {
        'x': nrm((BATCH, SEQ, D), 1.0),
        'meta': nrm((N_META, D), 1.0),
        'ln_emb_g': 1.0 + nrm((D,), 0.02),
        'ln_emb_b': nrm((D,), 0.02),
        'w_in': nrm((DEPTH, D, IN_COLS), D ** -0.5),
        'pool_w': nrm((DEPTH, len(POOL_WINDOWS), POOL_GROUP, POOL_GROUP), POOL_GROUP ** -0.5),
        'pool_scale': 1.0 + nrm((DEPTH, POOL_WIDTH), 0.1),
        'attn_sink': nrm((DEPTH, N_HEADS), 1.0),
        'conv_w': nrm((DEPTH, CONV_WIDTH, LRU_WIDTH), CONV_WIDTH ** -0.5),
        'conv_b': nrm((DEPTH, LRU_WIDTH), 0.02),
        'lru_wa': nrm((DEPTH, LRU_BLOCKS, LRU_BLOCK, LRU_BLOCK), LRU_BLOCK ** -0.5),
        'lru_ba': nrm((DEPTH, LRU_WIDTH), 0.02),
        'lru_wx': nrm((DEPTH, LRU_BLOCKS, LRU_BLOCK, LRU_BLOCK), LRU_BLOCK ** -0.5),
        'lru_bx': nrm((DEPTH, LRU_WIDTH), 0.02),
        'lru_lambda': lru_lambda,
        'proj_pool': nrm((DEPTH, POOL_WIDTH, D), beta * POOL_WIDTH ** -0.5),
        'proj_attn': nrm((DEPTH, N_HEADS * HEAD_DIM, D), beta * (N_HEADS * HEAD_DIM) ** -0.5),
        'proj_lru': nrm((DEPTH, LRU_WIDTH, D), beta * LRU_WIDTH ** -0.5),
        'w_out': nrm((DEPTH, D, D), beta * D ** -0.5),
        'ln1_g': 1.0 + nrm((DEPTH, D), 0.02),
        'ln1_b': nrm((DEPTH, D), 0.02),
        'router_grp_w': nrm((DEPTH, D, N_GROUPS), D ** -0.5),
        'router_grp_b': nrm((DEPTH, N_GROUPS), 0.01),
        'router_exp_w': nrm((DEPTH, D, N_EXPERTS), D ** -0.5),
        'router_exp_b': nrm((DEPTH, N_EXPERTS), 0.01),
        'exp_w_gate': nrm((DEPTH, N_EXPERTS, D, D_EXPERT), D ** -0.5),
        'exp_w_up': nrm((DEPTH, N_EXPERTS, D, D_EXPERT), beta * D ** -0.5),
        'exp_w_down': nrm((DEPTH, N_EXPERTS, D_EXPERT, D), beta * D_EXPERT ** -0.5),
        'ln2_g': 1.0 + nrm((DEPTH, D), 0.02),
        'ln2_b': nrm((DEPTH, D), 0.02),
    }


def reference(x, meta, ln_emb_g, ln_emb_b, w_in, pool_w, pool_scale, attn_sink, conv_w, conv_b,
              lru_wa, lru_ba, lru_wx, lru_bx, lru_lambda, proj_pool, proj_attn, proj_lru, w_out,
              ln1_g, ln1_b, router_grp_w, router_grp_b, router_exp_w, router_exp_b,
              exp_w_gate, exp_w_up, exp_w_down, ln2_g, ln2_b):
    B = x.shape[0]
    alpha = (2.0 * DEPTH) ** 0.25
    h = jnp.concatenate([jnp.broadcast_to(meta[None].astype(x.dtype), (B, N_META, D_MODEL)), x], axis=1)
    h = layer_norm(h, ln_emb_g, ln_emb_b)
    L = h.shape[1]
    for l in range(DEPTH):
        cols = h @ w_in[l]
        pool_in, q, k, v, lru_x, lru_y, gate_cols = jnp.split(cols, IN_OFFSETS, axis=-1)
        pool_o = pool_mixer(pool_in, pool_w[l], pool_scale[l])
        attn_o = swa_attention(q.reshape(B, L, N_HEADS, HEAD_DIM),
                               k.reshape(B, L, N_KV_HEADS, HEAD_DIM),
                               v.reshape(B, L, N_KV_HEADS, HEAD_DIM), attn_sink[l])
        lru_o = rglru_branch(lru_x, lru_y, conv_w[l], conv_b[l], lru_wa[l], lru_ba[l],
                             lru_wx[l], lru_bx[l], lru_lambda[l])
        gts = jax.nn.sigmoid(gate_cols).reshape(B, L, 3, D_MODEL)
        merged = (gts[:, :, 0] * (pool_o @ proj_pool[l])
                  + gts[:, :, 1] * (attn_o @ proj_attn[l])
                  + gts[:, :, 2] * (lru_o @ proj_lru[l]))
        h = layer_norm(alpha * h + merged @ w_out[l], ln1_g[l], ln1_b[l])
        ffn = hier_moe(h, router_grp_w[l], router_grp_b[l], router_exp_w[l], router_exp_b[l],
                       exp_w_gate[l], exp_w_up[l], exp_w_down[l])
        h = layer_norm(alpha * h + ffn, ln2_g[l], ln2_b[l])
    return h[:, N_META:]
```

```python
import functools

import jax
import jax.numpy as jnp
from jax import lax
from jax.experimental import pallas as pl
from jax.experimental.pallas import tpu as pltpu

F32 = jnp.float32
BF16 = jnp.bfloat16

D_MODEL = 2048
DEPTH = 2
N_META = 16
POOL_WINDOWS = (2, 4, 8, 16)
POOL_WIDTH = D_MODEL // 2
POOL_GROUP = POOL_WIDTH // len(POOL_WINDOWS)
N_HEADS = 16
N_KV_HEADS = 4
HEAD_DIM = 64
Q_PER_KV = N_HEADS // N_KV_HEADS
WINDOW = 128
BLOCK = 128
NEG = -1e30
LRU_WIDTH = D_MODEL // 2
LRU_BLOCKS = 4
LRU_BLOCK = LRU_WIDTH // LRU_BLOCKS
CONV_WIDTH = 4
LRU_C = 8.0
N_GROUPS = 4
EXPERTS_PER_GROUP = 8
N_EXPERTS = N_GROUPS * EXPERTS_PER_GROUP
TOP_K = 2
D_EXPERT = D_MODEL // 4
LN_EPS = 1e-5
ALPHA = (2.0 * DEPTH) ** 0.25

PAD = BLOCK - N_META
ATT_W = N_HEADS * HEAD_DIM
KV_W = N_KV_HEADS * HEAD_DIM
OFF_POOL = 0
OFF_Q = OFF_POOL + POOL_WIDTH
OFF_K = OFF_Q + ATT_W
OFF_V = OFF_K + KV_W
OFF_LX = OFF_V + KV_W
OFF_LY = OFF_LX + LRU_WIDTH
OFF_GATE = OFF_LY + LRU_WIDTH
IN_COLS = OFF_GATE + 3 * D_MODEL

VMEM_LIMIT_BYTES = 56 * 1024 * 1024

SEQ_TILE = 3 * BLOCK
INPROJ_BM = 1536
INPROJ_BN = 768
MERGE_BM = 768
MERGE_BN = 512
OUT_BM = 384
MOE_BM = 256
ROUTE_W = 128


def _cparams(*sem):
    return pltpu.CompilerParams(dimension_semantics=sem, vmem_limit_bytes=VMEM_LIMIT_BYTES)


def _layer_norm(x, g, b):
    mu = jnp.mean(x, axis=-1, keepdims=True)
    xc = x - mu
    var = jnp.mean(xc * xc, axis=-1, keepdims=True)
    return xc * lax.rsqrt(var + LN_EPS) * g + b


def _flat_valid_rows(row0, n_rows, batch, lp):
    r = row0 + lax.broadcasted_iota(jnp.int32, (n_rows, 1), 0)
    pad_row = jnp.zeros((n_rows, 1), jnp.bool_)
    for b in range(batch):
        pad_row = pad_row | ((r >= b * lp) & (r < b * lp + PAD))
    return jnp.logical_not(pad_row)


def _embed_kernel(x_ref, meta_ref, g_ref, b_ref, hf_ref, hb_ref):
    i = pl.program_id(1)
    src = jnp.where(i == 0, meta_ref[...], x_ref[...])
    y = _layer_norm(src, g_ref[...], b_ref[...])
    row = lax.broadcasted_iota(jnp.int32, (BLOCK, 1), 0)
    y = jnp.where((i > 0) | (row >= PAD), y, 0.0)
    hf_ref[...] = y
    hb_ref[...] = y.astype(BF16)


def _embed(x, meta, g, b):
    batch, seq, d = x.shape
    nblk = seq // BLOCK
    m = batch * (nblk + 1) * BLOCK
    meta_tile = jnp.concatenate([jnp.zeros((PAD, d), F32), meta.astype(F32)], axis=0)
    row_spec = pl.BlockSpec((BLOCK, d), lambda bi, i: (bi * (nblk + 1) + i, 0))
    vec_spec = pl.BlockSpec((1, d), lambda bi, i: (0, 0))
    return pl.pallas_call(
        _embed_kernel,
        grid=(batch, nblk + 1),
        in_specs=[
            pl.BlockSpec((BLOCK, d), lambda bi, i: (bi * nblk + jnp.maximum(i - 1, 0), 0)),
            pl.BlockSpec((BLOCK, d), lambda bi, i: (0, 0)),
            vec_spec, vec_spec,
        ],
        out_specs=[row_spec, row_spec],
        out_shape=[jax.ShapeDtypeStruct((m, d), F32), jax.ShapeDtypeStruct((m, d), BF16)],
        compiler_params=_cparams("parallel", "arbitrary"),
        name="embed_ln",
    )(x.reshape(batch * seq, d), meta_tile, g.reshape(1, d), b.reshape(1, d))


def _inproj_kernel(x_ref, w_ref, o_ref, *, first_gate_tile):
    acc = jnp.dot(x_ref[...], w_ref[...], preferred_element_type=F32)
    j = pl.program_id(1)

    @pl.when(j < first_gate_tile)
    def _():
        o_ref[...] = acc.astype(o_ref.dtype)

    @pl.when(j >= first_gate_tile)
    def _():
        o_ref[...] = jax.nn.sigmoid(acc).astype(o_ref.dtype)


def _inproj(hb, w_in_bf16, layer):
    m, d = hb.shape
    bm, bn = INPROJ_BM, INPROJ_BN
    assert m % bm == 0 and IN_COLS % bn == 0 and OFF_GATE % bn == 0
    return pl.pallas_call(
        functools.partial(_inproj_kernel, first_gate_tile=OFF_GATE // bn),
        grid=(m // bm, IN_COLS // bn),
        in_specs=[
            pl.BlockSpec((bm, d), lambda i, j: (i, 0)),
            pl.BlockSpec((None, d, bn), lambda i, j: (layer, 0, j)),
        ],
        out_specs=pl.BlockSpec((bm, bn), lambda i, j: (i, j)),
        out_shape=jax.ShapeDtypeStruct((m, IN_COLS), BF16),
        compiler_params=_cparams("parallel", "arbitrary"),
        name="in_proj",
    )(hb, w_in_bf16)


def _pool_kernel(u_ref, w_ref, scale_ref, o_ref, ext_ref):
    t = pl.program_id(1)
    tile = SEQ_TILE
    maxw = max(POOL_WINDOWS)

    @pl.when(t == 0)
    def _():
        ext_ref[pl.ds(0, maxw), :] = jnp.zeros((maxw, POOL_WIDTH), F32)

    @pl.when(t > 0)
    def _():
        ext_ref[pl.ds(0, maxw), :] = ext_ref[pl.ds(tile, maxw), :]

    ext_ref[pl.ds(maxw, tile), :] = u_ref[...].astype(F32)

    pos = t * tile + lax.broadcasted_iota(jnp.int32, (tile, 1), 0) - PAD
    for gi, w in enumerate(POOL_WINDOWS):
        cols = pl.ds(gi * POOL_GROUP, POOL_GROUP)
        u = ext_ref[pl.ds(maxw, tile), cols]
        win = u
        for k in range(1, w):
            win = win + ext_ref[pl.ds(maxw - k, tile), cols]
        cnt = jnp.clip(pos + 1, 1, w).astype(F32)
        delta = win / cnt - u
        mixed = jnp.dot(delta.astype(BF16), w_ref[gi], preferred_element_type=F32)
        o_ref[:, cols] = (mixed * scale_ref[:, cols]).astype(o_ref.dtype)


def _pool(cols, pool_w_bf16, pool_scale, batch, lp):
    m = cols.shape[0]
    nt = lp // SEQ_TILE
    maxw = max(POOL_WINDOWS)
    return pl.pallas_call(
        _pool_kernel,
        grid=(batch, nt),
        in_specs=[
            pl.BlockSpec((SEQ_TILE, POOL_WIDTH), lambda b, t: (b * nt + t, OFF_POOL // POOL_WIDTH)),
            pl.BlockSpec((len(POOL_WINDOWS), POOL_GROUP, POOL_GROUP), lambda b, t: (0, 0, 0)),
            pl.BlockSpec((1, POOL_WIDTH), lambda b, t: (0, 0)),
        ],
        out_specs=pl.BlockSpec((SEQ_TILE, POOL_WIDTH), lambda b, t: (b * nt + t, 0)),
        out_shape=jax.ShapeDtypeStruct((m, POOL_WIDTH), BF16),
        scratch_shapes=[pltpu.VMEM((SEQ_TILE + maxw, POOL_WIDTH), F32)],
        compiler_params=_cparams("parallel", "arbitrary"),
        name="pool_mixer",
    )(cols, pool_w_bf16, pool_scale.reshape(1, POOL_WIDTH))


def _attn_kernel(q_ref, kp_ref, kc_ref, vp_ref, vc_ref, sink_ref, o_ref):
    n = pl.program_id(1)
    qi = lax.broadcasted_iota(jnp.int32, (BLOCK, 2 * BLOCK), 0)
    kj = lax.broadcasted_iota(jnp.int32, (BLOCK, 2 * BLOCK), 1)
    dist = BLOCK + qi - kj
    k_pos = (n - 1) * BLOCK + kj
    mask = (dist >= 0) & (dist < WINDOW) & (k_pos >= PAD)
    dist_f = dist.astype(F32)
    scale = HEAD_DIM ** -0.5
    for kh in range(N_KV_HEADS):
        hs = pl.ds(kh * HEAD_DIM, HEAD_DIM)
        k2 = jnp.concatenate([kp_ref[:, hs], kc_ref[:, hs]], axis=0)
        v2 = jnp.concatenate([vp_ref[:, hs], vc_ref[:, hs]], axis=0)
        for g in range(Q_PER_KV):
            h = kh * Q_PER_KV + g
            slope = 2.0 ** (-8.0 * (h + 1) / N_HEADS)
            qh = q_ref[:, pl.ds(h * HEAD_DIM, HEAD_DIM)]
            s = lax.dot_general(qh, k2, (((1,), (1,)), ((), ())), preferred_element_type=F32)
            s = s * scale - slope * dist_f
            s = jnp.where(mask, s, NEG)
            sk = sink_ref[0, h]
            mx = jnp.maximum(jnp.max(s, axis=-1, keepdims=True), sk)
            p = jnp.exp(s - mx)
            den = jnp.sum(p, axis=-1, keepdims=True) + jnp.exp(sk - mx)
            o = jnp.dot(p.astype(BF16), v2, preferred_element_type=F32) / den
            o_ref[:, pl.ds(h * HEAD_DIM, HEAD_DIM)] = o.astype(o_ref.dtype)


def _attention(cols, sink, batch, lp):
    m = cols.shape[0]
    nb = lp // BLOCK
    cur = lambda cb: (lambda b, n: (b * nb + n, cb))
    prev = lambda cb: (lambda b, n: (b * nb + jnp.maximum(n - 1, 0), cb))
    return pl.pallas_call(
        _attn_kernel,
        grid=(batch, nb),
        in_specs=[
            pl.BlockSpec((BLOCK, ATT_W), cur(OFF_Q // ATT_W)),
            pl.BlockSpec((BLOCK, KV_W), prev(OFF_K // KV_W)),
            pl.BlockSpec((BLOCK, KV_W), cur(OFF_K // KV_W)),
            pl.BlockSpec((BLOCK, KV_W), prev(OFF_V // KV_W)),
            pl.BlockSpec((BLOCK, KV_W), cur(OFF_V // KV_W)),
            pl.BlockSpec(memory_space=pltpu.SMEM),
        ],
        out_specs=pl.BlockSpec((BLOCK, ATT_W), lambda b, n: (b * nb + n, 0)),
        out_shape=jax.ShapeDtypeStruct((m, ATT_W), BF16),
        compiler_params=_cparams("parallel", "arbitrary"),
        name="swa_attention",
    )(cols, cols, cols, cols, cols, sink.reshape(1, N_HEADS).astype(F32))


LRU_HALF = LRU_WIDTH // 2
LRU_HALO = 8


def _gelu_tanh(x):
    return 0.5 * x * (1.0 + jnp.tanh(0.7978845608028654 * (x + 0.044715 * (x * x * x))))


def _lru_kernel(x_ref, y_ref, cw_ref, cb_ref, wa_ref, ba_ref, wx_ref, bx_ref, lam_ref, o_ref,
                ext_ref, a_ref, b_ref, carry_ref):
    t = pl.program_id(2)
    tile = SEQ_TILE
    width = LRU_HALF

    @pl.when(t == 0)
    def _():
        ext_ref[pl.ds(0, LRU_HALO), :] = jnp.zeros((LRU_HALO, width), F32)
        carry_ref[...] = jnp.zeros((1, width), F32)

    @pl.when(t > 0)
    def _():
        ext_ref[pl.ds(0, LRU_HALO), :] = ext_ref[pl.ds(tile, LRU_HALO), :]

    ext_ref[pl.ds(LRU_HALO, tile), :] = x_ref[...].astype(F32)

    xc = cb_ref[...] + cw_ref[pl.ds(CONV_WIDTH - 1, 1), :] * ext_ref[pl.ds(LRU_HALO, tile), :]
    for j in range(CONV_WIDTH - 1):
        shift = CONV_WIDTH - 1 - j
        xc = xc + cw_ref[pl.ds(j, 1), :] * ext_ref[pl.ds(LRU_HALO - shift, tile), :]

    xcb = xc.astype(BF16)
    ga_parts, gx_parts = [], []
    for blk in range(width // LRU_BLOCK):
        xb = xcb[:, blk * LRU_BLOCK:(blk + 1) * LRU_BLOCK]
        ga_parts.append(jnp.dot(xb, wa_ref[blk], preferred_element_type=F32))
        gx_parts.append(jnp.dot(xb, wx_ref[blk], preferred_element_type=F32))
    gate_a = jax.nn.sigmoid(jnp.concatenate(ga_parts, axis=1) + ba_ref[...])
    gate_x = jax.nn.sigmoid(jnp.concatenate(gx_parts, axis=1) + bx_ref[...])

    neg_lam = -lam_ref[...]
    softplus = jnp.maximum(neg_lam, 0.0) + jnp.log1p(jnp.exp(-jnp.abs(neg_lam)))
    log_a = (-LRU_C) * gate_a * softplus
    a = jnp.exp(log_a)
    b_in = jnp.sqrt(1.0 - jnp.exp(2.0 * log_a)) * gate_x * xc
    pos = t * tile + lax.broadcasted_iota(jnp.int32, (tile, 1), 0)
    b_in = jnp.where(pos >= PAD, b_in, 0.0)
    a_ref[...] = a
    b_ref[...] = b_in

    row = lax.broadcasted_iota(jnp.int32, (8, width), 0)

    def group(r, carry):
        rows = pl.ds(pl.multiple_of(r * 8, 8), 8)
        av = a_ref[rows, :]
        bv = b_ref[rows, :]
        for k in (1, 2, 4):
            a_sh = jnp.where(row >= k, pltpu.roll(av, k, 0), 1.0)
            b_sh = jnp.where(row >= k, pltpu.roll(bv, k, 0), 0.0)
            bv = av * b_sh + bv
            av = av * a_sh
        hv = av * carry + bv
        b_ref[rows, :] = hv
        return hv[7:8, :]

    carry_ref[...] = lax.fori_loop(0, tile // 8, group, carry_ref[...])
    o_ref[...] = (b_ref[...] * _gelu_tanh(y_ref[...].astype(F32))).astype(o_ref.dtype)


def _lru(cols, conv_w, conv_b, wa_bf16, ba, wx_bf16, bx, lam, batch, lp):
    m = cols.shape[0]
    nt = lp // SEQ_TILE
    nhalf = LRU_WIDTH // LRU_HALF
    blocks_per_half = LRU_HALF // LRU_BLOCK
    vec = lambda v: v.reshape(1, LRU_WIDTH).astype(F32)
    vec_spec = pl.BlockSpec((1, LRU_HALF), lambda b, c, t: (0, c))
    w_spec = pl.BlockSpec((blocks_per_half, LRU_BLOCK, LRU_BLOCK), lambda b, c, t: (c, 0, 0))
    return pl.pallas_call(
        _lru_kernel,
        grid=(batch, nhalf, nt),
        in_specs=[
            pl.BlockSpec((SEQ_TILE, LRU_HALF), lambda b, c, t: (b * nt + t, OFF_LX // LRU_HALF + c)),
            pl.BlockSpec((SEQ_TILE, LRU_HALF), lambda b, c, t: (b * nt + t, OFF_LY // LRU_HALF + c)),
            pl.BlockSpec((CONV_WIDTH, LRU_HALF), lambda b, c, t: (0, c)),
            vec_spec, w_spec, vec_spec, w_spec, vec_spec, vec_spec,
        ],
        out_specs=pl.BlockSpec((SEQ_TILE, LRU_HALF), lambda b, c, t: (b * nt + t, c)),
        out_shape=jax.ShapeDtypeStruct((m, LRU_WIDTH), BF16),
        scratch_shapes=[
            pltpu.VMEM((SEQ_TILE + LRU_HALO, LRU_HALF), F32),
            pltpu.VMEM((SEQ_TILE, LRU_HALF), F32),
            pltpu.VMEM((SEQ_TILE, LRU_HALF), F32),
            pltpu.VMEM((1, LRU_HALF), F32),
        ],
        compiler_params=_cparams("parallel", "parallel", "arbitrary"),
        name="rglru",
    )(cols, cols, conv_w.astype(F32), vec(conv_b), wa_bf16, vec(ba), wx_bf16, vec(bx), vec(lam))


def _merge_kernel(p_ref, a_ref, r_ref, gp_ref, ga_ref, gr_ref, wp_ref, wa_ref, wr_ref, o_ref):
    acc = gp_ref[...].astype(F32) * jnp.dot(p_ref[...], wp_ref[...], preferred_element_type=F32)
    acc += ga_ref[...].astype(F32) * jnp.dot(a_ref[...], wa_ref[...], preferred_element_type=F32)
    acc += gr_ref[...].astype(F32) * jnp.dot(r_ref[...], wr_ref[...], preferred_element_type=F32)
    o_ref[...] = acc.astype(o_ref.dtype)


def _merge(pool_o, attn_o, lru_o, cols, wp, wa, wr, layer):
    m = pool_o.shape[0]
    bm, bn = MERGE_BM, MERGE_BN
    assert m % bm == 0 and OFF_GATE % bn == 0 and D_MODEL % bn == 0
    x_spec = pl.BlockSpec((bm, POOL_WIDTH), lambda i, j: (i, 0))
    gate_spec = lambda k: pl.BlockSpec((bm, bn), lambda i, j: (i, (OFF_GATE + k * D_MODEL) // bn + j))
    w_spec = pl.BlockSpec((None, POOL_WIDTH, bn), lambda i, j: (layer, 0, j))
    return pl.pallas_call(
        _merge_kernel,
        grid=(m // bm, D_MODEL // bn),
        in_specs=[x_spec, x_spec, x_spec, gate_spec(0), gate_spec(1), gate_spec(2), w_spec, w_spec, w_spec],
        out_specs=pl.BlockSpec((bm, bn), lambda i, j: (i, j)),
        out_shape=jax.ShapeDtypeStruct((m, D_MODEL), BF16),
        compiler_params=_cparams("parallel", "arbitrary"),
        name="gated_merge",
    )(pool_o, attn_o, lru_o, cols, cols, cols, wp, wa, wr)


def _outproj_kernel(x_ref, w_ref, h_ref, g_ref, b_ref, rw_ref, rb_ref, hf_ref, hb_ref, lg_ref, *, batch, lp):
    i = pl.program_id(0)
    bm = x_ref.shape[0]
    t = jnp.dot(x_ref[...], w_ref[...], preferred_element_type=F32)
    y = _layer_norm(ALPHA * h_ref[...] + t, g_ref[...], b_ref[...])
    y = jnp.where(_flat_valid_rows(i * bm, bm, batch, lp), y, 0.0)
    yb = y.astype(BF16)
    hf_ref[...] = y
    hb_ref[...] = yb
    lg_ref[...] = jnp.dot(yb, rw_ref[...], preferred_element_type=F32) + rb_ref[...]


def _outproj_ln(merged, w_out_bf16, h, g, b, route_w, route_b, layer, batch, lp):
    m, d = h.shape
    bm = OUT_BM
    assert m % bm == 0
    row_spec = pl.BlockSpec((bm, d), lambda i: (i, 0))
    vec_spec = pl.BlockSpec((1, d), lambda i: (0, 0))
    return pl.pallas_call(
        functools.partial(_outproj_kernel, batch=batch, lp=lp),
        grid=(m // bm,),
        in_specs=[
            row_spec,
            pl.BlockSpec((None, d, d), lambda i: (layer, 0, 0)),
            row_spec, vec_spec, vec_spec,
            pl.BlockSpec((d, ROUTE_W), lambda i: (0, 0)),
            pl.BlockSpec((1, ROUTE_W), lambda i: (0, 0)),
        ],
        out_specs=[row_spec, row_spec, pl.BlockSpec((bm, ROUTE_W), lambda i: (i, 0))],
        out_shape=[jax.ShapeDtypeStruct((m, d), F32), jax.ShapeDtypeStruct((m, d), BF16),
                   jax.ShapeDtypeStruct((m, ROUTE_W), F32)],
        compiler_params=_cparams("arbitrary"),
        name="out_proj_ln",
    )(merged, w_out_bf16, h, g.reshape(1, d), b.reshape(1, d), route_w, route_b)


def _expert_kernel(be_ref, nused_ref, x_ref, wg_ref, wu_ref, wd_ref, sw_ref, o_ref, wg_s, wu_s, wd_s):
    i = pl.program_id(0)
    prev = be_ref[jnp.maximum(i - 1, 0)]
    changed = (i == 0) | (be_ref[i] != prev)
    used = i < nused_ref[0]

    @pl.when(changed & used)
    def _():
        wg_s[...] = wg_ref[...].astype(BF16)
        wu_s[...] = wu_ref[...].astype(BF16)
        wd_s[...] = wd_ref[...].astype(BF16)

    @pl.when(used)
    def _():
        x = x_ref[...]
        gate = jnp.dot(x, wg_s[...], preferred_element_type=F32)
        up = jnp.dot(x, wu_s[...], preferred_element_type=F32)
        hdn = (gate * jax.nn.sigmoid(gate)) * up
        y = jnp.dot(hdn.astype(BF16), wd_s[...], preferred_element_type=F32)
        o_ref[...] = y * sw_ref[...]

    @pl.when(jnp.logical_not(used))
    def _():
        o_ref[...] = jnp.zeros(o_ref.shape, o_ref.dtype)


def _experts(xs, slot_w, block_e, n_used, w_gate, w_up, w_down, layer):
    cap, d = xs.shape
    bm = MOE_BM
    nblk = cap // bm
    grid_spec = pltpu.PrefetchScalarGridSpec(
        num_scalar_prefetch=2,
        grid=(nblk,),
        in_specs=[
            pl.BlockSpec((bm, d), lambda i, be, nu: (i, 0)),
            pl.BlockSpec((None, None, d, D_EXPERT), lambda i, be, nu: (layer, be[i], 0, 0)),
            pl.BlockSpec((None, None, d, D_EXPERT), lambda i, be, nu: (layer, be[i], 0, 0)),
            pl.BlockSpec((None, None, D_EXPERT, d), lambda i, be, nu: (layer, be[i], 0, 0)),
            pl.BlockSpec((bm, 1), lambda i, be, nu: (i, 0)),
        ],
        out_specs=pl.BlockSpec((bm, d), lambda i, be, nu: (i, 0)),
        scratch_shapes=[
            pltpu.VMEM((d, D_EXPERT), BF16),
            pltpu.VMEM((d, D_EXPERT), BF16),
            pltpu.VMEM((D_EXPERT, d), BF16),
        ],
    )
    return pl.pallas_call(
        _expert_kernel,
        grid_spec=grid_spec,
        out_shape=jax.ShapeDtypeStruct((cap, d), F32),
        compiler_params=_cparams("arbitrary"),
        name="expert_mlp",
    )(block_e, n_used, xs, w_gate, w_up, w_down, slot_w.reshape(cap, 1))


def _resid_ln_kernel(h_ref, y_ref, g_ref, b_ref, hf_ref, hb_ref, *, batch, lp):
    i = pl.program_id(0)
    bm = h_ref.shape[0]
    y = _layer_norm(ALPHA * h_ref[...] + y_ref[...], g_ref[...], b_ref[...])
    y = jnp.where(_flat_valid_rows(i * bm, bm, batch, lp), y, 0.0)
    hf_ref[...] = y
    hb_ref[...] = y.astype(BF16)


def _resid_ln(h, y, g, b, batch, lp):
    m, d = h.shape
    bm = OUT_BM
    row_spec = pl.BlockSpec((bm, d), lambda i: (i, 0))
    vec_spec = pl.BlockSpec((1, d), lambda i: (0, 0))
    return pl.pallas_call(
        functools.partial(_resid_ln_kernel, batch=batch, lp=lp),
        grid=(m // bm,),
        in_specs=[row_spec, row_spec, vec_spec, vec_spec],
        out_specs=[row_spec, row_spec],
        out_shape=[jax.ShapeDtypeStruct((m, d), F32), jax.ShapeDtypeStruct((m, d), BF16)],
        compiler_params=_cparams("parallel"),
        name="resid_ln",
    )(h, y, g.reshape(1, d), b.reshape(1, d))


def _route(logits, batch, lp):
    m = logits.shape[0]
    grp = logits[:, :N_GROUPS]
    p_grp = jax.nn.softmax(grp, axis=-1)
    g = jnp.argmax(grp, axis=-1)
    p_g = jnp.take_along_axis(p_grp, g[:, None], axis=1)
    exp_logits = logits[:, N_GROUPS:N_GROUPS + N_EXPERTS].reshape(m, N_GROUPS, EXPERTS_PER_GROUP)
    sel = jnp.take_along_axis(exp_logits, g[:, None, None], axis=1)[:, 0]
    top_p, top_i = lax.top_k(jax.nn.softmax(sel, axis=-1), TOP_K)
    wts = p_g * top_p / top_p.sum(-1, keepdims=True)
    eid = (g[:, None] * EXPERTS_PER_GROUP + top_i).astype(jnp.int32)

    pos = jnp.arange(m, dtype=jnp.int32) % lp
    valid = (pos >= PAD)[:, None]
    flat_e = jnp.where(valid, eid, N_EXPERTS).reshape(m * TOP_K)
    onehot = (flat_e[:, None] == jnp.arange(N_EXPERTS, dtype=jnp.int32)[None, :]).astype(jnp.int32)
    csum = jnp.cumsum(onehot, axis=0)
    counts = csum[-1]
    rank = jnp.take_along_axis(csum, jnp.minimum(flat_e, N_EXPERTS - 1)[:, None], axis=1)[:, 0] - 1
    bm = MOE_BM
    padded = (counts + bm - 1) // bm * bm
    pad_end = jnp.cumsum(padded)
    pad_start = pad_end - padded
    n_real = batch * (lp - PAD) * TOP_K
    nblk = -(-(n_real + N_EXPERTS * (bm - 1)) // bm)
    cap = nblk * bm
    dest = jnp.where(flat_e < N_EXPERTS, pad_start[jnp.minimum(flat_e, N_EXPERTS - 1)] + rank, cap)
    tok = jnp.arange(m * TOP_K, dtype=jnp.int32) // TOP_K
    slot_tok = jnp.zeros((cap,), jnp.int32).at[dest].set(tok, mode="drop")
    slot_w = jnp.zeros((cap,), F32).at[dest].set(wts.reshape(m * TOP_K), mode="drop")
    block_e = jnp.minimum(jnp.searchsorted(pad_end, jnp.arange(nblk, dtype=jnp.int32) * bm, side="right"),
                          N_EXPERTS - 1).astype(jnp.int32)
    n_used = (pad_end[-1] // bm).astype(jnp.int32).reshape(1)
    return dest.reshape(m, TOP_K), slot_tok, slot_w, block_e, n_used


def kernel(x, meta, ln_emb_g, ln_emb_b, w_in, pool_w, pool_scale, attn_sink, conv_w, conv_b, lru_wa, lru_ba,
           lru_wx, lru_bx, lru_lambda, proj_pool, proj_attn, proj_lru, w_out, ln1_g, ln1_b, router_grp_w,
           router_grp_b, router_exp_w, router_exp_b, exp_w_gate, exp_w_up, exp_w_down, ln2_g, ln2_b):
    batch, seq, d = x.shape
    lp = PAD + N_META + seq
    m = batch * lp

    hf, hb = _embed(x, meta, ln_emb_g, ln_emb_b)
    w_in_b = w_in.astype(BF16)
    pool_w_b = pool_w.astype(BF16)
    wa_b = lru_wa.astype(BF16)
    wx_b = lru_wx.astype(BF16)
    wp_b = proj_pool.astype(BF16)
    wat_b = proj_attn.astype(BF16)
    wl_b = proj_lru.astype(BF16)
    w_out_b = w_out.astype(BF16)
    route_pad = ROUTE_W - N_GROUPS - N_EXPERTS

    for l in range(DEPTH):
        cols = _inproj(hb, w_in_b, l)
        pool_o = _pool(cols, pool_w_b[l], pool_scale[l], batch, lp)
        attn_o = _attention(cols, attn_sink[l], batch, lp)
        lru_o = _lru(cols, conv_w[l], conv_b[l], wa_b[l], lru_ba[l], wx_b[l], lru_bx[l], lru_lambda[l],
                     batch, lp)
        merged = _merge(pool_o, attn_o, lru_o, cols, wp_b, wat_b, wl_b, l)
        route_w = jnp.concatenate(
            [router_grp_w[l], router_exp_w[l], jnp.zeros((d, route_pad), F32)], axis=1).astype(BF16)
        route_b = jnp.concatenate(
            [router_grp_b[l], router_exp_b[l], jnp.zeros((route_pad,), F32)]).reshape(1, ROUTE_W)
        h1f, h1b, logits = _outproj_ln(merged, w_out_b, hf, ln1_g[l], ln1_b[l], route_w, route_b, l,
                                       batch, lp)
        dest, slot_tok, slot_w, block_e, n_used = _route(logits, batch, lp)
        xs = jnp.take(h1b, slot_tok, axis=0)
        yb = _experts(xs, slot_w, block_e, n_used, exp_w_gate, exp_w_up, exp_w_down, l)
        dest_c = jnp.minimum(dest, yb.shape[0] - 1)
        ffn = jnp.take(yb, dest_c[:, 0], axis=0) + jnp.take(yb, dest_c[:, 1], axis=0)
        hf, hb = _resid_ln(h1f, ffn, ln2_g[l], ln2_b[l], batch, lp)

    return hf.reshape(batch, lp, d)[:, PAD + N_META:]
```

```python
import functools

import jax
import jax.numpy as jnp
from jax import lax
from jax.experimental import pallas as pl
from jax.experimental.pallas import tpu as pltpu

F32 = jnp.float32
BF16 = jnp.bfloat16

D_MODEL = 2048
DEPTH = 2
N_META = 16
POOL_WINDOWS = (2, 4, 8, 16)
POOL_WIDTH = D_MODEL // 2
POOL_GROUP = POOL_WIDTH // len(POOL_WINDOWS)
N_HEADS = 16
N_KV_HEADS = 4
HEAD_DIM = 64
Q_PER_KV = N_HEADS // N_KV_HEADS
WINDOW = 128
BLOCK = 128
NEG = -1e30
LRU_WIDTH = D_MODEL // 2
LRU_BLOCKS = 4
LRU_BLOCK = LRU_WIDTH // LRU_BLOCKS
CONV_WIDTH = 4
LRU_C = 8.0
N_GROUPS = 4
EXPERTS_PER_GROUP = 8
N_EXPERTS = N_GROUPS * EXPERTS_PER_GROUP
TOP_K = 2
D_EXPERT = D_MODEL // 4
LN_EPS = 1e-5
ALPHA = (2.0 * DEPTH) ** 0.25

PAD = BLOCK - N_META
ATT_W = N_HEADS * HEAD_DIM
KV_W = N_KV_HEADS * HEAD_DIM
OFF_POOL = 0
OFF_Q = OFF_POOL + POOL_WIDTH
OFF_K = OFF_Q + ATT_W
OFF_V = OFF_K + KV_W
OFF_LX = OFF_V + KV_W
OFF_LY = OFF_LX + LRU_WIDTH
OFF_GATE = OFF_LY + LRU_WIDTH
IN_COLS = OFF_GATE + 3 * D_MODEL

VMEM_LIMIT_BYTES = 56 * 1024 * 1024

SEQ_TILE = 3 * BLOCK
INPROJ_BM = 1536
INPROJ_BN = 768
MERGE_BM = 768
MERGE_BN = 512
OUT_BM = 384
MOE_BM = 256
DISPATCH_BM = 512
ROUTE_W = 128


def _cparams(*sem):
    return pltpu.CompilerParams(dimension_semantics=sem, vmem_limit_bytes=VMEM_LIMIT_BYTES)


def _layer_norm(x, g, b):
    mu = jnp.mean(x, axis=-1, keepdims=True)
    xc = x - mu
    var = jnp.mean(xc * xc, axis=-1, keepdims=True)
    return xc * lax.rsqrt(var + LN_EPS) * g + b


def _flat_valid_rows(row0, n_rows, batch, lp):
    r = row0 + lax.broadcasted_iota(jnp.int32, (n_rows, 1), 0)
    pad_row = jnp.zeros((n_rows, 1), jnp.bool_)
    for b in range(batch):
        pad_row = pad_row | ((r >= b * lp) & (r < b * lp + PAD))
    return jnp.logical_not(pad_row)


def _embed_kernel(x_ref, meta_ref, g_ref, b_ref, hf_ref, hb_ref):
    i = pl.program_id(1)
    src = jnp.where(i == 0, meta_ref[...], x_ref[...])
    y = _layer_norm(src, g_ref[...], b_ref[...])
    row = lax.broadcasted_iota(jnp.int32, (BLOCK, 1), 0)
    y = jnp.where((i > 0) | (row >= PAD), y, 0.0)
    hf_ref[...] = y
    hb_ref[...] = y.astype(BF16)


def _embed(x, meta, g, b):
    batch, seq, d = x.shape
    nblk = seq // BLOCK
    m = batch * (nblk + 1) * BLOCK
    meta_tile = jnp.concatenate([jnp.zeros((PAD, d), F32), meta.astype(F32)], axis=0)
    row_spec = pl.BlockSpec((BLOCK, d), lambda bi, i: (bi * (nblk + 1) + i, 0))
    vec_spec = pl.BlockSpec((1, d), lambda bi, i: (0, 0))
    return pl.pallas_call(
        _embed_kernel,
        grid=(batch, nblk + 1),
        in_specs=[
            pl.BlockSpec((BLOCK, d), lambda bi, i: (bi * nblk + jnp.maximum(i - 1, 0), 0)),
            pl.BlockSpec((BLOCK, d), lambda bi, i: (0, 0)),
            vec_spec, vec_spec,
        ],
        out_specs=[row_spec, row_spec],
        out_shape=[jax.ShapeDtypeStruct((m, d), F32), jax.ShapeDtypeStruct((m, d), BF16)],
        compiler_params=_cparams("parallel", "arbitrary"),
        name="embed_ln",
    )(x.reshape(batch * seq, d), meta_tile, g.reshape(1, d), b.reshape(1, d))


def _inproj_kernel(x_ref, w_ref, o_ref, *, first_gate_tile):
    acc = jnp.dot(x_ref[...], w_ref[...], preferred_element_type=F32)
    j = pl.program_id(1)

    @pl.when(j < first_gate_tile)
    def _():
        o_ref[...] = acc.astype(o_ref.dtype)

    @pl.when(j >= first_gate_tile)
    def _():
        o_ref[...] = jax.nn.sigmoid(acc).astype(o_ref.dtype)


def _inproj(hb, w_in_bf16, layer):
    m, d = hb.shape
    bm, bn = INPROJ_BM, INPROJ_BN
    assert m % bm == 0 and IN_COLS % bn == 0 and OFF_GATE % bn == 0
    return pl.pallas_call(
        functools.partial(_inproj_kernel, first_gate_tile=OFF_GATE // bn),
        grid=(m // bm, IN_COLS // bn),
        in_specs=[
            pl.BlockSpec((bm, d), lambda i, j: (i, 0)),
            pl.BlockSpec((None, d, bn), lambda i, j: (layer, 0, j)),
        ],
        out_specs=pl.BlockSpec((bm, bn), lambda i, j: (i, j)),
        out_shape=jax.ShapeDtypeStruct((m, IN_COLS), BF16),
        compiler_params=_cparams("parallel", "arbitrary"),
        name="in_proj",
    )(hb, w_in_bf16)


def _pool_kernel(u_ref, w_ref, scale_ref, o_ref, ext_ref):
    t = pl.program_id(1)
    tile = SEQ_TILE
    maxw = max(POOL_WINDOWS)

    @pl.when(t == 0)
    def _():
        ext_ref[pl.ds(0, maxw), :] = jnp.zeros((maxw, POOL_WIDTH), F32)

    @pl.when(t > 0)
    def _():
        ext_ref[pl.ds(0, maxw), :] = ext_ref[pl.ds(tile, maxw), :]

    ext_ref[pl.ds(maxw, tile), :] = u_ref[...].astype(F32)

    pos = t * tile + lax.broadcasted_iota(jnp.int32, (tile, 1), 0) - PAD
    for gi, w in enumerate(POOL_WINDOWS):
        cols = pl.ds(gi * POOL_GROUP, POOL_GROUP)
        u = ext_ref[pl.ds(maxw, tile), cols]
        win = u
        for k in range(1, w):
            win = win + ext_ref[pl.ds(maxw - k, tile), cols]
        cnt = jnp.clip(pos + 1, 1, w).astype(F32)
        delta = win / cnt - u
        mixed = jnp.dot(delta.astype(BF16), w_ref[gi], preferred_element_type=F32)
        o_ref[:, cols] = (mixed * scale_ref[:, cols]).astype(o_ref.dtype)


def _pool(cols, pool_w_bf16, pool_scale, batch, lp):
    m = cols.shape[0]
    nt = lp // SEQ_TILE
    maxw = max(POOL_WINDOWS)
    return pl.pallas_call(
        _pool_kernel,
        grid=(batch, nt),
        in_specs=[
            pl.BlockSpec((SEQ_TILE, POOL_WIDTH), lambda b, t: (b * nt + t, OFF_POOL // POOL_WIDTH)),
            pl.BlockSpec((len(POOL_WINDOWS), POOL_GROUP, POOL_GROUP), lambda b, t: (0, 0, 0)),
            pl.BlockSpec((1, POOL_WIDTH), lambda b, t: (0, 0)),
        ],
        out_specs=pl.BlockSpec((SEQ_TILE, POOL_WIDTH), lambda b, t: (b * nt + t, 0)),
        out_shape=jax.ShapeDtypeStruct((m, POOL_WIDTH), BF16),
        scratch_shapes=[pltpu.VMEM((SEQ_TILE + maxw, POOL_WIDTH), F32)],
        compiler_params=_cparams("parallel", "arbitrary"),
        name="pool_mixer",
    )(cols, pool_w_bf16, pool_scale.reshape(1, POOL_WIDTH))


def _attn_kernel(q_ref, kp_ref, kc_ref, vp_ref, vc_ref, sink_ref, o_ref):
    n = pl.program_id(1)
    qi = lax.broadcasted_iota(jnp.int32, (BLOCK, 2 * BLOCK), 0)
    kj = lax.broadcasted_iota(jnp.int32, (BLOCK, 2 * BLOCK), 1)
    dist = BLOCK + qi - kj
    k_pos = (n - 1) * BLOCK + kj
    mask = (dist >= 0) & (dist < WINDOW) & (k_pos >= PAD)
    dist_f = dist.astype(F32)
    scale = HEAD_DIM ** -0.5
    for kh in range(N_KV_HEADS):
        hs = pl.ds(kh * HEAD_DIM, HEAD_DIM)
        k2 = jnp.concatenate([kp_ref[:, hs], kc_ref[:, hs]], axis=0)
        v2 = jnp.concatenate([vp_ref[:, hs], vc_ref[:, hs]], axis=0)
        for g in range(Q_PER_KV):
            h = kh * Q_PER_KV + g
            slope = 2.0 ** (-8.0 * (h + 1) / N_HEADS)
            qh = q_ref[:, pl.ds(h * HEAD_DIM, HEAD_DIM)]
            s = lax.dot_general(qh, k2, (((1,), (1,)), ((), ())), preferred_element_type=F32)
            s = s * scale - slope * dist_f
            s = jnp.where(mask, s, NEG)
            sk = sink_ref[0, h]
            mx = jnp.maximum(jnp.max(s, axis=-1, keepdims=True), sk)
            p = jnp.exp(s - mx)
            den = jnp.sum(p, axis=-1, keepdims=True) + jnp.exp(sk - mx)
            o = jnp.dot(p.astype(BF16), v2, preferred_element_type=F32) / den
            o_ref[:, pl.ds(h * HEAD_DIM, HEAD_DIM)] = o.astype(o_ref.dtype)


def _attention(cols, sink, batch, lp):
    m = cols.shape[0]
    nb = lp // BLOCK
    cur = lambda cb: (lambda b, n: (b * nb + n, cb))
    prev = lambda cb: (lambda b, n: (b * nb + jnp.maximum(n - 1, 0), cb))
    return pl.pallas_call(
        _attn_kernel,
        grid=(batch, nb),
        in_specs=[
            pl.BlockSpec((BLOCK, ATT_W), cur(OFF_Q // ATT_W)),
            pl.BlockSpec((BLOCK, KV_W), prev(OFF_K // KV_W)),
            pl.BlockSpec((BLOCK, KV_W), cur(OFF_K // KV_W)),
            pl.BlockSpec((BLOCK, KV_W), prev(OFF_V // KV_W)),
            pl.BlockSpec((BLOCK, KV_W), cur(OFF_V // KV_W)),
            pl.BlockSpec(memory_space=pltpu.SMEM),
        ],
        out_specs=pl.BlockSpec((BLOCK, ATT_W), lambda b, n: (b * nb + n, 0)),
        out_shape=jax.ShapeDtypeStruct((m, ATT_W), BF16),
        compiler_params=_cparams("parallel", "arbitrary"),
        name="swa_attention",
    )(cols, cols, cols, cols, cols, sink.reshape(1, N_HEADS).astype(F32))


LRU_HALF = LRU_WIDTH // 2
LRU_HALO = 8


def _gelu_tanh(x):
    return 0.5 * x * (1.0 + jnp.tanh(0.7978845608028654 * (x + 0.044715 * (x * x * x))))


def _lru_kernel(x_ref, y_ref, cw_ref, cb_ref, wa_ref, ba_ref, wx_ref, bx_ref, lam_ref, o_ref,
                ext_ref, a_ref, b_ref, carry_ref):
    t = pl.program_id(2)
    tile = SEQ_TILE
    width = LRU_HALF

    @pl.when(t == 0)
    def _():
        ext_ref[pl.ds(0, LRU_HALO), :] = jnp.zeros((LRU_HALO, width), F32)
        carry_ref[...] = jnp.zeros((1, width), F32)

    @pl.when(t > 0)
    def _():
        ext_ref[pl.ds(0, LRU_HALO), :] = ext_ref[pl.ds(tile, LRU_HALO), :]

    ext_ref[pl.ds(LRU_HALO, tile), :] = x_ref[...].astype(F32)

    xc = cb_ref[...] + cw_ref[pl.ds(CONV_WIDTH - 1, 1), :] * ext_ref[pl.ds(LRU_HALO, tile), :]
    for j in range(CONV_WIDTH - 1):
        shift = CONV_WIDTH - 1 - j
        xc = xc + cw_ref[pl.ds(j, 1), :] * ext_ref[pl.ds(LRU_HALO - shift, tile), :]

    xcb = xc.astype(BF16)
    ga_parts, gx_parts = [], []
    for blk in range(width // LRU_BLOCK):
        xb = xcb[:, blk * LRU_BLOCK:(blk + 1) * LRU_BLOCK]
        ga_parts.append(jnp.dot(xb, wa_ref[blk], preferred_element_type=F32))
        gx_parts.append(jnp.dot(xb, wx_ref[blk], preferred_element_type=F32))
    gate_a = jax.nn.sigmoid(jnp.concatenate(ga_parts, axis=1) + ba_ref[...])
    gate_x = jax.nn.sigmoid(jnp.concatenate(gx_parts, axis=1) + bx_ref[...])

    neg_lam = -lam_ref[...]
    softplus = jnp.maximum(neg_lam, 0.0) + jnp.log1p(jnp.exp(-jnp.abs(neg_lam)))
    log_a = (-LRU_C) * gate_a * softplus
    a = jnp.exp(log_a)
    b_in = jnp.sqrt(1.0 - jnp.exp(2.0 * log_a)) * gate_x * xc
    pos = t * tile + lax.broadcasted_iota(jnp.int32, (tile, 1), 0)
    b_in = jnp.where(pos >= PAD, b_in, 0.0)
    a_ref[...] = a
    b_ref[...] = b_in

    row = lax.broadcasted_iota(jnp.int32, (8, width), 0)

    def group(r, carry):
        rows = pl.ds(pl.multiple_of(r * 8, 8), 8)
        av = a_ref[rows, :]
        bv = b_ref[rows, :]
        for k in (1, 2, 4):
            a_sh = jnp.where(row >= k, pltpu.roll(av, k, 0), 1.0)
            b_sh = jnp.where(row >= k, pltpu.roll(bv, k, 0), 0.0)
            bv = av * b_sh + bv
            av = av * a_sh
        hv = av * carry + bv
        b_ref[rows, :] = hv
        return hv[7:8, :]

    carry_ref[...] = lax.fori_loop(0, tile // 8, group, carry_ref[...])
    o_ref[...] = (b_ref[...] * _gelu_tanh(y_ref[...].astype(F32))).astype(o_ref.dtype)


def _lru(cols, conv_w, conv_b, wa_bf16, ba, wx_bf16, bx, lam, batch, lp):
    m = cols.shape[0]
    nt = lp // SEQ_TILE
    nhalf = LRU_WIDTH // LRU_HALF
    blocks_per_half = LRU_HALF // LRU_BLOCK
    vec = lambda v: v.reshape(1, LRU_WIDTH).astype(F32)
    vec_spec = pl.BlockSpec((1, LRU_HALF), lambda b, c, t: (0, c))
    w_spec = pl.BlockSpec((blocks_per_half, LRU_BLOCK, LRU_BLOCK), lambda b, c, t: (c, 0, 0))
    return pl.pallas_call(
        _lru_kernel,
        grid=(batch, nhalf, nt),
        in_specs=[
            pl.BlockSpec((SEQ_TILE, LRU_HALF), lambda b, c, t: (b * nt + t, OFF_LX // LRU_HALF + c)),
            pl.BlockSpec((SEQ_TILE, LRU_HALF), lambda b, c, t: (b * nt + t, OFF_LY // LRU_HALF + c)),
            pl.BlockSpec((CONV_WIDTH, LRU_HALF), lambda b, c, t: (0, c)),
            vec_spec, w_spec, vec_spec, w_spec, vec_spec, vec_spec,
        ],
        out_specs=pl.BlockSpec((SEQ_TILE, LRU_HALF), lambda b, c, t: (b * nt + t, c)),
        out_shape=jax.ShapeDtypeStruct((m, LRU_WIDTH), BF16),
        scratch_shapes=[
            pltpu.VMEM((SEQ_TILE + LRU_HALO, LRU_HALF), F32),
            pltpu.VMEM((SEQ_TILE, LRU_HALF), F32),
            pltpu.VMEM((SEQ_TILE, LRU_HALF), F32),
            pltpu.VMEM((1, LRU_HALF), F32),
        ],
        compiler_params=_cparams("parallel", "parallel", "arbitrary"),
        name="rglru",
    )(cols, cols, conv_w.astype(F32), vec(conv_b), wa_bf16, vec(ba), wx_bf16, vec(bx), vec(lam))


def _merge_kernel(p_ref, a_ref, r_ref, gp_ref, ga_ref, gr_ref, wp_ref, wa_ref, wr_ref, o_ref):
    acc = gp_ref[...].astype(F32) * jnp.dot(p_ref[...], wp_ref[...], preferred_element_type=F32)
    acc += ga_ref[...].astype(F32) * jnp.dot(a_ref[...], wa_ref[...], preferred_element_type=F32)
    acc += gr_ref[...].astype(F32) * jnp.dot(r_ref[...], wr_ref[...], preferred_element_type=F32)
    o_ref[...] = acc.astype(o_ref.dtype)


def _merge(pool_o, attn_o, lru_o, cols, wp, wa, wr, layer):
    m = pool_o.shape[0]
    bm, bn = MERGE_BM, MERGE_BN
    assert m % bm == 0 and OFF_GATE % bn == 0 and D_MODEL % bn == 0
    x_spec = pl.BlockSpec((bm, POOL_WIDTH), lambda i, j: (i, 0))
    gate_spec = lambda k: pl.BlockSpec((bm, bn), lambda i, j: (i, (OFF_GATE + k * D_MODEL) // bn + j))
    w_spec = pl.BlockSpec((None, POOL_WIDTH, bn), lambda i, j: (layer, 0, j))
    return pl.pallas_call(
        _merge_kernel,
        grid=(m // bm, D_MODEL // bn),
        in_specs=[x_spec, x_spec, x_spec, gate_spec(0), gate_spec(1), gate_spec(2), w_spec, w_spec, w_spec],
        out_specs=pl.BlockSpec((bm, bn), lambda i, j: (i, j)),
        out_shape=jax.ShapeDtypeStruct((m, D_MODEL), BF16),
        compiler_params=_cparams("parallel", "arbitrary"),
        name="gated_merge",
    )(pool_o, attn_o, lru_o, cols, cols, cols, wp, wa, wr)


def _pack_bf16_pairs(yb):
    c = yb.shape[1] // 2
    lo = lax.bitcast_convert_type(yb[:, :c].astype(F32), jnp.uint32)
    hi = lax.bitcast_convert_type(yb[:, c:].astype(F32), jnp.uint32)
    return (hi & jnp.uint32(0xFFFF0000)) | (lo >> 16)


def _unpack_bf16_pairs(words):
    lo = lax.bitcast_convert_type(words << 16, F32)
    hi = lax.bitcast_convert_type(words & jnp.uint32(0xFFFF0000), F32)
    return jnp.concatenate([lo, hi], axis=1).astype(BF16)


def _route_tile(logits, valid, base):
    bm = logits.shape[0]
    lane = lax.broadcasted_iota(jnp.int32, (bm, ROUTE_W), 1)
    lane_f = lane.astype(F32)
    ninf = -jnp.inf
    big = float(ROUTE_W)

    gl = jnp.where(lane < N_GROUPS, logits, ninf)
    gmax = jnp.max(gl, axis=-1, keepdims=True)
    g = jnp.min(jnp.where(gl == gmax, lane_f, big), axis=-1, keepdims=True)
    p_g = 1.0 / jnp.sum(jnp.exp(gl - gmax), axis=-1, keepdims=True)

    first = N_GROUPS + g * EXPERTS_PER_GROUP
    sl = jnp.where((lane_f >= first) & (lane_f < first + EXPERTS_PER_GROUP), logits, ninf)
    m1 = jnp.max(sl, axis=-1, keepdims=True)
    i1 = jnp.min(jnp.where(sl == m1, lane_f, big), axis=-1, keepdims=True)
    ssum = jnp.sum(jnp.exp(sl - m1), axis=-1, keepdims=True)
    sl2 = jnp.where(lane_f == i1, ninf, sl)
    m2 = jnp.max(sl2, axis=-1, keepdims=True)
    i2 = jnp.min(jnp.where(sl2 == m2, lane_f, big), axis=-1, keepdims=True)
    p1 = 1.0 / ssum
    p2 = jnp.exp(m2 - m1) / ssum
    w1 = p_g * p1 / (p1 + p2)
    w2 = p_g * p2 / (p1 + p2)
    e1 = i1 - N_GROUPS
    e2 = i2 - N_GROUPS

    oh1 = (lane_f == e1) & valid
    oh2 = (lane_f == e2) & valid
    both = (oh1 | oh2).astype(F32)
    earlier = (lax.broadcasted_iota(jnp.int32, (bm, bm), 0) > lax.broadcasted_iota(jnp.int32, (bm, bm), 1))
    prefix = jnp.dot(earlier.astype(BF16), both.astype(BF16), preferred_element_type=F32) + base
    r1 = jnp.sum(jnp.where(oh1, prefix, 0.0), axis=-1, keepdims=True)
    r2 = jnp.sum(jnp.where(oh2, prefix, 0.0), axis=-1, keepdims=True)

    ri = jnp.where(lane == 0, e1, jnp.where(lane == 1, e2, jnp.where(lane == 2, r1, jnp.where(lane == 3, r2, 0.0))))
    rw = jnp.where(lane == 0, w1, jnp.where(lane == 1, w2, 0.0))
    return ri.astype(jnp.int32), rw, jnp.sum(both, axis=0, keepdims=True)


def _outproj_kernel(x_ref, w_ref, h_ref, g_ref, b_ref, rw_ref, rb_ref, hf_ref, hp_ref, ri_ref, rwt_ref, cnt_ref,
                    *, batch, lp):
    i = pl.program_id(0)
    bm = x_ref.shape[0]
    t = jnp.dot(x_ref[...], w_ref[...], preferred_element_type=F32)
    y = _layer_norm(ALPHA * h_ref[...] + t, g_ref[...], b_ref[...])
    valid = _flat_valid_rows(i * bm, bm, batch, lp)
    y = jnp.where(valid, y, 0.0)
    yb = y.astype(BF16)
    hf_ref[...] = y
    hp_ref[...] = _pack_bf16_pairs(yb)
    logits = jnp.dot(yb, rw_ref[...], preferred_element_type=F32) + rb_ref[...]

    @pl.when(i == 0)
    def _():
        cnt_ref[...] = jnp.zeros(cnt_ref.shape, F32)

    ri, rw, tile_cnt = _route_tile(logits, valid, cnt_ref[...])
    ri_ref[...] = ri
    rwt_ref[...] = rw
    cnt_ref[...] += tile_cnt


def _outproj_ln(merged, w_out_bf16, h, g, b, route_w, route_b, layer, batch, lp):
    m, d = h.shape
    bm = OUT_BM
    assert m % bm == 0
    row_spec = pl.BlockSpec((bm, d), lambda i: (i, 0))
    vec_spec = pl.BlockSpec((1, d), lambda i: (0, 0))
    route_spec = pl.BlockSpec((bm, ROUTE_W), lambda i: (i, 0))
    return pl.pallas_call(
        functools.partial(_outproj_kernel, batch=batch, lp=lp),
        grid=(m // bm,),
        in_specs=[
            row_spec,
            pl.BlockSpec((None, d, d), lambda i: (layer, 0, 0)),
            row_spec, vec_spec, vec_spec,
            pl.BlockSpec((d, ROUTE_W), lambda i: (0, 0)),
            pl.BlockSpec((1, ROUTE_W), lambda i: (0, 0)),
        ],
        out_specs=[row_spec, pl.BlockSpec((bm, d // 2), lambda i: (i, 0)), route_spec, route_spec,
                   pl.BlockSpec((1, ROUTE_W), lambda i: (0, 0))],
        out_shape=[jax.ShapeDtypeStruct((m, d), F32), jax.ShapeDtypeStruct((m, d // 2), jnp.uint32),
                   jax.ShapeDtypeStruct((m, ROUTE_W), jnp.int32), jax.ShapeDtypeStruct((m, ROUTE_W), F32),
                   jax.ShapeDtypeStruct((1, ROUTE_W), F32)],
        compiler_params=_cparams("arbitrary"),
        name="out_proj_ln",
    )(merged, w_out_bf16, h, g.reshape(1, d), b.reshape(1, d), route_w, route_b)


def _dispatch_kernel(dest_hbm, hp_ref, xs_in_ref, xs_ref, idx_ref, idx_sem, row_sem):
    del xs_in_ref
    i = pl.program_id(0)
    bm = hp_ref.shape[0]
    n_idx = TOP_K * bm
    idx_copy = pltpu.make_async_copy(dest_hbm.at[pl.ds(pl.multiple_of(i * n_idx, n_idx), n_idx)], idx_ref, idx_sem)
    idx_copy.start()
    idx_copy.wait()

    def row_copy(r, slot):
        return pltpu.make_async_copy(hp_ref.at[pl.ds(r, 1)], xs_ref.at[pl.ds(slot, 1)], row_sem)

    def issue(r, carry):
        for k in range(TOP_K):
            row_copy(r, idx_ref[TOP_K * r + k]).start()
        return carry

    def drain(r, carry):
        for k in range(TOP_K):
            row_copy(r, idx_ref[TOP_K * r + k]).wait()
        return carry

    lax.fori_loop(0, bm, issue, 0, unroll=8)
    lax.fori_loop(0, bm, drain, 0, unroll=8)


def _dispatch(hp, dest, n_slots):
    m, c = hp.shape
    bm = DISPATCH_BM
    assert m % bm == 0
    slots = jnp.zeros((n_slots, c), jnp.uint32)
    return pl.pallas_call(
        _dispatch_kernel,
        grid=(m // bm,),
        in_specs=[
            pl.BlockSpec(memory_space=pl.ANY),
            pl.BlockSpec((bm, c), lambda i: (i, 0)),
            pl.BlockSpec(memory_space=pl.ANY),
        ],
        out_specs=pl.BlockSpec(memory_space=pl.ANY),
        out_shape=jax.ShapeDtypeStruct((n_slots, c), jnp.uint32),
        scratch_shapes=[pltpu.SMEM((TOP_K * bm,), jnp.int32), pltpu.SemaphoreType.DMA(()),
                        pltpu.SemaphoreType.DMA(())],
        input_output_aliases={2: 0},
        compiler_params=_cparams("arbitrary"),
        name="moe_dispatch",
    )(dest.reshape(m * TOP_K), hp, slots)


def _expert_kernel(be_ref, nused_ref, x_ref, wg_ref, wu_ref, wd_ref, o_ref, wg_s, wu_s, wd_s):
    i = pl.program_id(0)
    prev = be_ref[jnp.maximum(i - 1, 0)]
    changed = (i == 0) | (be_ref[i] != prev)
    used = i < nused_ref[0]

    @pl.when(changed & used)
    def _():
        wg_s[...] = wg_ref[...].astype(BF16)
        wu_s[...] = wu_ref[...].astype(BF16)
        wd_s[...] = wd_ref[...].astype(BF16)

    @pl.when(used)
    def _():
        x = _unpack_bf16_pairs(x_ref[...])
        gate = jnp.dot(x, wg_s[...], preferred_element_type=F32)
        up = jnp.dot(x, wu_s[...], preferred_element_type=F32)
        hdn = (gate * jax.nn.sigmoid(gate)) * up
        o_ref[...] = jnp.dot(hdn.astype(BF16), wd_s[...], preferred_element_type=F32)

    @pl.when(jnp.logical_not(used))
    def _():
        o_ref[...] = jnp.zeros(o_ref.shape, o_ref.dtype)


def _experts(xs, nblk, block_e, n_used, w_gate, w_up, w_down, layer):
    d = D_MODEL
    bm = MOE_BM
    cap = nblk * bm
    grid_spec = pltpu.PrefetchScalarGridSpec(
        num_scalar_prefetch=2,
        grid=(nblk,),
        in_specs=[
            pl.BlockSpec((bm, d // 2), lambda i, be, nu: (i, 0)),
            pl.BlockSpec((None, None, d, D_EXPERT), lambda i, be, nu: (layer, be[i], 0, 0)),
            pl.BlockSpec((None, None, d, D_EXPERT), lambda i, be, nu: (layer, be[i], 0, 0)),
            pl.BlockSpec((None, None, D_EXPERT, d), lambda i, be, nu: (layer, be[i], 0, 0)),
        ],
        out_specs=pl.BlockSpec((bm, d), lambda i, be, nu: (i, 0)),
        scratch_shapes=[
            pltpu.VMEM((d, D_EXPERT), BF16),
            pltpu.VMEM((d, D_EXPERT), BF16),
            pltpu.VMEM((D_EXPERT, d), BF16),
        ],
    )
    return pl.pallas_call(
        _expert_kernel,
        grid_spec=grid_spec,
        out_shape=jax.ShapeDtypeStruct((cap, d), F32),
        compiler_params=_cparams("arbitrary"),
        name="expert_mlp",
    )(block_e, n_used, xs, w_gate, w_up, w_down)


def _combine_ln_kernel(dest_hbm, h_ref, rw_ref, g_ref, b_ref, yb_hbm, hf_ref, hb_ref, idx_ref, rows_ref, idx_sem,
                       row_sem, *, batch, lp):
    i = pl.program_id(0)
    bm = h_ref.shape[0]
    n_idx = TOP_K * bm
    idx_copy = pltpu.make_async_copy(dest_hbm.at[pl.ds(pl.multiple_of(i * n_idx, n_idx), n_idx)], idx_ref, idx_sem)
    idx_copy.start()
    idx_copy.wait()

    def row_copy(r, k, slot):
        return pltpu.make_async_copy(yb_hbm.at[pl.ds(slot, 1)], rows_ref.at[k, pl.ds(r, 1)], row_sem)

    def issue(r, carry):
        for k in range(TOP_K):
            row_copy(r, k, idx_ref[TOP_K * r + k]).start()
        return carry

    def drain(r, carry):
        for k in range(TOP_K):
            row_copy(r, k, idx_ref[TOP_K * r + k]).wait()
        return carry

    lax.fori_loop(0, bm, issue, 0, unroll=8)
    lax.fori_loop(0, bm, drain, 0, unroll=8)

    ffn = rw_ref[:, 0:1] * rows_ref[0]
    for k in range(1, TOP_K):
        ffn = ffn + rw_ref[:, k:k + 1] * rows_ref[k]
    y = _layer_norm(ALPHA * h_ref[...] + ffn, g_ref[...], b_ref[...])
    y = jnp.where(_flat_valid_rows(i * bm, bm, batch, lp), y, 0.0)
    hf_ref[...] = y
    hb_ref[...] = y.astype(BF16)


def _combine_ln(h, yb, dest, rw, g, b, batch, lp):
    m, d = h.shape
    bm = DISPATCH_BM
    assert m % bm == 0
    row_spec = pl.BlockSpec((bm, d), lambda i: (i, 0))
    vec_spec = pl.BlockSpec((1, d), lambda i: (0, 0))
    return pl.pallas_call(
        functools.partial(_combine_ln_kernel, batch=batch, lp=lp),
        grid=(m // bm,),
        in_specs=[
            pl.BlockSpec(memory_space=pl.ANY),
            row_spec,
            pl.BlockSpec((bm, ROUTE_W), lambda i: (i, 0)),
            vec_spec, vec_spec,
            pl.BlockSpec(memory_space=pl.ANY),
        ],
        out_specs=[row_spec, row_spec],
        out_shape=[jax.ShapeDtypeStruct((m, d), F32), jax.ShapeDtypeStruct((m, d), BF16)],
        scratch_shapes=[pltpu.SMEM((TOP_K * bm,), jnp.int32), pltpu.VMEM((TOP_K, bm, d), F32),
                        pltpu.SemaphoreType.DMA(()), pltpu.SemaphoreType.DMA(())],
        compiler_params=_cparams("arbitrary"),
        name="moe_combine_ln",
    )(dest.reshape(m * TOP_K), h, rw, g.reshape(1, d), b.reshape(1, d), yb)


def _slot_tables(ri, cnt, batch, lp):
    m = ri.shape[0]
    bm = MOE_BM
    eid = ri[:, 0:TOP_K]
    rank = ri[:, TOP_K:2 * TOP_K]
    counts = cnt[0, :N_EXPERTS].astype(jnp.int32)
    padded = (counts + bm - 1) // bm * bm
    pad_end = jnp.cumsum(padded)
    pad_start = pad_end - padded
    n_real = batch * (lp - PAD) * TOP_K
    nblk = -(-(n_real + N_EXPERTS * (bm - 1)) // bm)
    cap = nblk * bm
    experts = jnp.arange(N_EXPERTS, dtype=jnp.int32)
    start = jnp.sum(jnp.where(eid[:, :, None] == experts[None, None, :], pad_start[None, None, :], 0), axis=-1)
    row = jnp.arange(m, dtype=jnp.int32)
    pos = row % lp
    spare = cap + ((row // lp) * PAD + pos)[:, None] * TOP_K + jnp.arange(TOP_K, dtype=jnp.int32)[None, :]
    dest = jnp.where((pos >= PAD)[:, None], start + rank, spare)
    block_e = jnp.minimum(jnp.sum((jnp.arange(nblk, dtype=jnp.int32)[:, None] * bm >= pad_end[None, :])
                                  .astype(jnp.int32), axis=1), N_EXPERTS - 1).astype(jnp.int32)
    n_used = (pad_end[-1] // bm).astype(jnp.int32).reshape(1)
    n_slots = cap + batch * PAD * TOP_K
    return dest, jnp.minimum(dest, cap - 1), block_e, n_used, nblk, n_slots


def kernel(x, meta, ln_emb_g, ln_emb_b, w_in, pool_w, pool_scale, attn_sink, conv_w, conv_b, lru_wa, lru_ba,
           lru_wx, lru_bx, lru_lambda, proj_pool, proj_attn, proj_lru, w_out, ln1_g, ln1_b, router_grp_w,
           router_grp_b, router_exp_w, router_exp_b, exp_w_gate, exp_w_up, exp_w_down, ln2_g, ln2_b):
    batch, seq, d = x.shape
    lp = PAD + N_META + seq
    m = batch * lp

    hf, hb = _embed(x, meta, ln_emb_g, ln_emb_b)
    w_in_b = w_in.astype(BF16)
    pool_w_b = pool_w.astype(BF16)
    wa_b = lru_wa.astype(BF16)
    wx_b = lru_wx.astype(BF16)
    wp_b = proj_pool.astype(BF16)
    wat_b = proj_attn.astype(BF16)
    wl_b = proj_lru.astype(BF16)
    w_out_b = w_out.astype(BF16)
    route_pad = ROUTE_W - N_GROUPS - N_EXPERTS

    for l in range(DEPTH):
        cols = _inproj(hb, w_in_b, l)
        pool_o = _pool(cols, pool_w_b[l], pool_scale[l], batch, lp)
        attn_o = _attention(cols, attn_sink[l], batch, lp)
        lru_o = _lru(cols, conv_w[l], conv_b[l], wa_b[l], lru_ba[l], wx_b[l], lru_bx[l], lru_lambda[l],
                     batch, lp)
        merged = _merge(pool_o, attn_o, lru_o, cols, wp_b, wat_b, wl_b, l)
        route_w = jnp.concatenate(
            [router_grp_w[l], router_exp_w[l], jnp.zeros((d, route_pad), F32)], axis=1).astype(BF16)
        route_b = jnp.concatenate(
            [router_grp_b[l], router_exp_b[l], jnp.zeros((route_pad,), F32)]).reshape(1, ROUTE_W)
        h1f, h1p, ri, rw, cnt = _outproj_ln(merged, w_out_b, hf, ln1_g[l], ln1_b[l], route_w, route_b, l,
                                            batch, lp)
        dest, dest_read, block_e, n_used, nblk, n_slots = _slot_tables(ri, cnt, batch, lp)
        xs = _dispatch(h1p, dest, n_slots)
        yb = _experts(xs, nblk, block_e, n_used, exp_w_gate, exp_w_up, exp_w_down, l)
        hf, hb = _combine_ln(h1f, yb, dest_read, rw, ln2_g[l], ln2_b[l], batch, lp)

    return hf.reshape(batch, lp, d)[:, PAD + N_META:]
```

```python
import functools

import jax
import jax.numpy as jnp
from jax import lax
from jax.experimental import pallas as pl
from jax.experimental.pallas import tpu as pltpu

F32 = jnp.float32
BF16 = jnp.bfloat16

D_MODEL = 2048
DEPTH = 2
N_META = 16
POOL_WINDOWS = (2, 4, 8, 16)
POOL_WIDTH = D_MODEL // 2
POOL_GROUP = POOL_WIDTH // len(POOL_WINDOWS)
N_HEADS = 16
N_KV_HEADS = 4
HEAD_DIM = 64
Q_PER_KV = N_HEADS // N_KV_HEADS
WINDOW = 128
BLOCK = 128
NEG = -1e30
LRU_WIDTH = D_MODEL // 2
LRU_BLOCKS = 4
LRU_BLOCK = LRU_WIDTH // LRU_BLOCKS
CONV_WIDTH = 4
LRU_C = 8.0
N_GROUPS = 4
EXPERTS_PER_GROUP = 8
N_EXPERTS = N_GROUPS * EXPERTS_PER_GROUP
TOP_K = 2
D_EXPERT = D_MODEL // 4
LN_EPS = 1e-5
ALPHA = (2.0 * DEPTH) ** 0.25

PAD = BLOCK - N_META
ATT_W = N_HEADS * HEAD_DIM
KV_W = N_KV_HEADS * HEAD_DIM
OFF_POOL = 0
OFF_Q = OFF_POOL + POOL_WIDTH
OFF_K = OFF_Q + ATT_W
OFF_V = OFF_K + KV_W
OFF_LX = OFF_V + KV_W
OFF_LY = OFF_LX + LRU_WIDTH
OFF_GATE = OFF_LY + LRU_WIDTH
IN_COLS = OFF_GATE + 3 * D_MODEL

VMEM_LIMIT_BYTES = 56 * 1024 * 1024

SEQ_TILE = 3 * BLOCK
INPROJ_BM = 1536
INPROJ_BN = 768
MERGE_BM = 768
MERGE_BN = 512
OUT_BM = 384
MOE_BM = 256
DISPATCH_BM = 1536
COMBINE_BM = 512
ROUTE_W = 128


def _cparams(*sem):
    return pltpu.CompilerParams(dimension_semantics=sem, vmem_limit_bytes=VMEM_LIMIT_BYTES)


def _layer_norm(x, g, b):
    mu = jnp.mean(x, axis=-1, keepdims=True)
    xc = x - mu
    var = jnp.mean(xc * xc, axis=-1, keepdims=True)
    return xc * lax.rsqrt(var + LN_EPS) * g + b


def _flat_valid_rows(row0, n_rows, batch, lp):
    r = row0 + lax.broadcasted_iota(jnp.int32, (n_rows, 1), 0)
    pad_row = jnp.zeros((n_rows, 1), jnp.bool_)
    for b in range(batch):
        pad_row = pad_row | ((r >= b * lp) & (r < b * lp + PAD))
    return jnp.logical_not(pad_row)


def _embed_kernel(x_ref, meta_ref, g_ref, b_ref, hf_ref, hb_ref):
    i = pl.program_id(1)
    src = jnp.where(i == 0, meta_ref[...], x_ref[...])
    y = _layer_norm(src, g_ref[...], b_ref[...])
    row = lax.broadcasted_iota(jnp.int32, (BLOCK, 1), 0)
    y = jnp.where((i > 0) | (row >= PAD), y, 0.0)
    hf_ref[...] = y
    hb_ref[...] = y.astype(BF16)


def _embed(x, meta, g, b):
    batch, seq, d = x.shape
    nblk = seq // BLOCK
    m = batch * (nblk + 1) * BLOCK
    meta_tile = jnp.concatenate([jnp.zeros((PAD, d), F32), meta.astype(F32)], axis=0)
    row_spec = pl.BlockSpec((BLOCK, d), lambda bi, i: (bi * (nblk + 1) + i, 0))
    vec_spec = pl.BlockSpec((1, d), lambda bi, i: (0, 0))
    return pl.pallas_call(
        _embed_kernel,
        grid=(batch, nblk + 1),
        in_specs=[
            pl.BlockSpec((BLOCK, d), lambda bi, i: (bi * nblk + jnp.maximum(i - 1, 0), 0)),
            pl.BlockSpec((BLOCK, d), lambda bi, i: (0, 0)),
            vec_spec, vec_spec,
        ],
        out_specs=[row_spec, row_spec],
        out_shape=[jax.ShapeDtypeStruct((m, d), F32), jax.ShapeDtypeStruct((m, d), BF16)],
        compiler_params=_cparams("parallel", "arbitrary"),
        name="embed_ln",
    )(x.reshape(batch * seq, d), meta_tile, g.reshape(1, d), b.reshape(1, d))


def _sigmoid(x):
    return 0.5 * jnp.tanh(0.5 * x) + 0.5


def _inproj_kernel(x_ref, w_ref, o_ref, *, gates):
    acc = jnp.dot(x_ref[...], w_ref[...], preferred_element_type=F32)
    o_ref[...] = (_sigmoid(acc) if gates else acc).astype(o_ref.dtype)


def _inproj(hb, w_in_bf16, layer, gates):
    m, d = hb.shape
    bm, bn = INPROJ_BM, INPROJ_BN
    col0, width = (OFF_GATE, IN_COLS - OFF_GATE) if gates else (0, OFF_GATE)
    assert m % bm == 0 and width % bn == 0 and col0 % bn == 0
    return pl.pallas_call(
        functools.partial(_inproj_kernel, gates=gates),
        grid=(m // bm, width // bn),
        in_specs=[
            pl.BlockSpec((bm, d), lambda i, j: (i, 0)),
            pl.BlockSpec((None, d, bn), lambda i, j: (layer, 0, col0 // bn + j)),
        ],
        out_specs=pl.BlockSpec((bm, bn), lambda i, j: (i, j)),
        out_shape=jax.ShapeDtypeStruct((m, width), BF16),
        compiler_params=_cparams("parallel", "arbitrary"),
        name="in_proj_gates" if gates else "in_proj",
    )(hb, w_in_bf16)


def _pool_kernel(u_ref, w_ref, scale_ref, o_ref, ext_ref):
    t = pl.program_id(1)
    tile = SEQ_TILE
    maxw = max(POOL_WINDOWS)

    @pl.when(t == 0)
    def _():
        ext_ref[pl.ds(0, maxw), :] = jnp.zeros((maxw, POOL_WIDTH), F32)

    @pl.when(t > 0)
    def _():
        ext_ref[pl.ds(0, maxw), :] = ext_ref[pl.ds(tile, maxw), :]

    ext_ref[pl.ds(maxw, tile), :] = u_ref[...].astype(F32)

    pos = t * tile + lax.broadcasted_iota(jnp.int32, (tile, 1), 0) - PAD
    for gi, w in enumerate(POOL_WINDOWS):
        cols = pl.ds(gi * POOL_GROUP, POOL_GROUP)
        u = ext_ref[pl.ds(maxw, tile), cols]
        win = u
        for k in range(1, w):
            win = win + ext_ref[pl.ds(maxw - k, tile), cols]
        cnt = jnp.clip(pos + 1, 1, w).astype(F32)
        delta = win / cnt - u
        mixed = jnp.dot(delta.astype(BF16), w_ref[gi], preferred_element_type=F32)
        o_ref[:, cols] = (mixed * scale_ref[:, cols]).astype(o_ref.dtype)


def _pool(cols, pool_w_bf16, pool_scale, batch, lp):
    m = cols.shape[0]
    nt = lp // SEQ_TILE
    maxw = max(POOL_WINDOWS)
    return pl.pallas_call(
        _pool_kernel,
        grid=(batch, nt),
        in_specs=[
            pl.BlockSpec((SEQ_TILE, POOL_WIDTH), lambda b, t: (b * nt + t, OFF_POOL // POOL_WIDTH)),
            pl.BlockSpec((len(POOL_WINDOWS), POOL_GROUP, POOL_GROUP), lambda b, t: (0, 0, 0)),
            pl.BlockSpec((1, POOL_WIDTH), lambda b, t: (0, 0)),
        ],
        out_specs=pl.BlockSpec((SEQ_TILE, POOL_WIDTH), lambda b, t: (b * nt + t, 0)),
        out_shape=jax.ShapeDtypeStruct((m, POOL_WIDTH), BF16),
        scratch_shapes=[pltpu.VMEM((SEQ_TILE + maxw, POOL_WIDTH), F32)],
        compiler_params=_cparams("parallel", "arbitrary"),
        name="pool_mixer",
    )(cols, pool_w_bf16, pool_scale.reshape(1, POOL_WIDTH))


def _attn_kernel(q_ref, kp_ref, kc_ref, vp_ref, vc_ref, sink_ref, o_ref):
    n = pl.program_id(1)
    qi = lax.broadcasted_iota(jnp.int32, (BLOCK, 2 * BLOCK), 0)
    kj = lax.broadcasted_iota(jnp.int32, (BLOCK, 2 * BLOCK), 1)
    dist = BLOCK + qi - kj
    k_pos = (n - 1) * BLOCK + kj
    mask = (dist >= 0) & (dist < WINDOW) & (k_pos >= PAD)
    dist_f = dist.astype(F32)
    scale = HEAD_DIM ** -0.5
    for kh in range(N_KV_HEADS):
        hs = pl.ds(kh * HEAD_DIM, HEAD_DIM)
        k2 = jnp.concatenate([kp_ref[:, hs], kc_ref[:, hs]], axis=0)
        v2 = jnp.concatenate([vp_ref[:, hs], vc_ref[:, hs]], axis=0)
        for g in range(Q_PER_KV):
            h = kh * Q_PER_KV + g
            slope = 2.0 ** (-8.0 * (h + 1) / N_HEADS)
            qh = q_ref[:, pl.ds(h * HEAD_DIM, HEAD_DIM)]
            s = lax.dot_general(qh, k2, (((1,), (1,)), ((), ())), preferred_element_type=F32)
            s = s * scale - slope * dist_f
            s = jnp.where(mask, s, NEG)
            sk = sink_ref[0, h]
            mx = jnp.maximum(jnp.max(s, axis=-1, keepdims=True), sk)
            p = jnp.exp(s - mx)
            den = jnp.sum(p, axis=-1, keepdims=True) + jnp.exp(sk - mx)
            o = jnp.dot(p.astype(BF16), v2, preferred_element_type=F32) / den
            o_ref[:, pl.ds(h * HEAD_DIM, HEAD_DIM)] = o.astype(o_ref.dtype)


def _attention(cols, sink, batch, lp):
    m = cols.shape[0]
    nb = lp // BLOCK
    cur = lambda cb: (lambda b, n: (b * nb + n, cb))
    prev = lambda cb: (lambda b, n: (b * nb + jnp.maximum(n - 1, 0), cb))
    return pl.pallas_call(
        _attn_kernel,
        grid=(batch, nb),
        in_specs=[
            pl.BlockSpec((BLOCK, ATT_W), cur(OFF_Q // ATT_W)),
            pl.BlockSpec((BLOCK, KV_W), prev(OFF_K // KV_W)),
            pl.BlockSpec((BLOCK, KV_W), cur(OFF_K // KV_W)),
            pl.BlockSpec((BLOCK, KV_W), prev(OFF_V // KV_W)),
            pl.BlockSpec((BLOCK, KV_W), cur(OFF_V // KV_W)),
            pl.BlockSpec(memory_space=pltpu.SMEM),
        ],
        out_specs=pl.BlockSpec((BLOCK, ATT_W), lambda b, n: (b * nb + n, 0)),
        out_shape=jax.ShapeDtypeStruct((m, ATT_W), BF16),
        compiler_params=_cparams("parallel", "arbitrary"),
        name="swa_attention",
    )(cols, cols, cols, cols, cols, sink.reshape(1, N_HEADS).astype(F32))


LRU_HALF = LRU_WIDTH // 2
LRU_HALO = 8


def _gelu_tanh(x):
    return 0.5 * x * (1.0 + jnp.tanh(0.7978845608028654 * (x + 0.044715 * (x * x * x))))


def _lru_kernel(x_ref, y_ref, cw_ref, cb_ref, wa_ref, ba_ref, wx_ref, bx_ref, lam_ref, o_ref,
                ext_ref, a_ref, b_ref, carry_ref):
    t = pl.program_id(2)
    tile = SEQ_TILE
    width = LRU_HALF

    @pl.when(t == 0)
    def _():
        ext_ref[pl.ds(0, LRU_HALO), :] = jnp.zeros((LRU_HALO, width), F32)
        carry_ref[...] = jnp.zeros((1, width), F32)

    @pl.when(t > 0)
    def _():
        ext_ref[pl.ds(0, LRU_HALO), :] = ext_ref[pl.ds(tile, LRU_HALO), :]

    ext_ref[pl.ds(LRU_HALO, tile), :] = x_ref[...].astype(F32)

    xc = cb_ref[...] + cw_ref[pl.ds(CONV_WIDTH - 1, 1), :] * ext_ref[pl.ds(LRU_HALO, tile), :]
    for j in range(CONV_WIDTH - 1):
        shift = CONV_WIDTH - 1 - j
        xc = xc + cw_ref[pl.ds(j, 1), :] * ext_ref[pl.ds(LRU_HALO - shift, tile), :]

    xcb = xc.astype(BF16)
    ga_parts, gx_parts = [], []
    for blk in range(width // LRU_BLOCK):
        xb = xcb[:, blk * LRU_BLOCK:(blk + 1) * LRU_BLOCK]
        ga_parts.append(jnp.dot(xb, wa_ref[blk], preferred_element_type=F32))
        gx_parts.append(jnp.dot(xb, wx_ref[blk], preferred_element_type=F32))
    gate_a = _sigmoid(jnp.concatenate(ga_parts, axis=1) + ba_ref[...])
    gate_x = _sigmoid(jnp.concatenate(gx_parts, axis=1) + bx_ref[...])

    neg_lam = -lam_ref[...]
    softplus = jnp.maximum(neg_lam, 0.0) + jnp.log1p(jnp.exp(-jnp.abs(neg_lam)))
    log_a = (-LRU_C) * gate_a * softplus
    a = jnp.exp(log_a)
    b_in = jnp.sqrt(1.0 - jnp.exp(2.0 * log_a)) * gate_x * xc
    pos = t * tile + lax.broadcasted_iota(jnp.int32, (tile, 1), 0)
    b_in = jnp.where(pos >= PAD, b_in, 0.0)
    a_ref[...] = a
    b_ref[...] = b_in

    row = lax.broadcasted_iota(jnp.int32, (8, width), 0)

    def group(r, carry):
        rows = pl.ds(pl.multiple_of(r * 8, 8), 8)
        av = a_ref[rows, :]
        bv = b_ref[rows, :]
        for k in (1, 2, 4):
            a_sh = jnp.where(row >= k, pltpu.roll(av, k, 0), 1.0)
            b_sh = jnp.where(row >= k, pltpu.roll(bv, k, 0), 0.0)
            bv = av * b_sh + bv
            av = av * a_sh
        hv = av * carry + bv
        b_ref[rows, :] = hv
        return hv[7:8, :]

    carry_ref[...] = lax.fori_loop(0, tile // 8, group, carry_ref[...])
    o_ref[...] = (b_ref[...] * _gelu_tanh(y_ref[...].astype(F32))).astype(o_ref.dtype)


def _lru(cols, conv_w, conv_b, wa_bf16, ba, wx_bf16, bx, lam, batch, lp):
    m = cols.shape[0]
    nt = lp // SEQ_TILE
    nhalf = LRU_WIDTH // LRU_HALF
    blocks_per_half = LRU_HALF // LRU_BLOCK
    vec = lambda v: v.reshape(1, LRU_WIDTH).astype(F32)
    vec_spec = pl.BlockSpec((1, LRU_HALF), lambda b, c, t: (0, c))
    w_spec = pl.BlockSpec((blocks_per_half, LRU_BLOCK, LRU_BLOCK), lambda b, c, t: (c, 0, 0))
    return pl.pallas_call(
        _lru_kernel,
        grid=(batch, nhalf, nt),
        in_specs=[
            pl.BlockSpec((SEQ_TILE, LRU_HALF), lambda b, c, t: (b * nt + t, OFF_LX // LRU_HALF + c)),
            pl.BlockSpec((SEQ_TILE, LRU_HALF), lambda b, c, t: (b * nt + t, OFF_LY // LRU_HALF + c)),
            pl.BlockSpec((CONV_WIDTH, LRU_HALF), lambda b, c, t: (0, c)),
            vec_spec, w_spec, vec_spec, w_spec, vec_spec, vec_spec,
        ],
        out_specs=pl.BlockSpec((SEQ_TILE, LRU_HALF), lambda b, c, t: (b * nt + t, c)),
        out_shape=jax.ShapeDtypeStruct((m, LRU_WIDTH), BF16),
        scratch_shapes=[
            pltpu.VMEM((SEQ_TILE + LRU_HALO, LRU_HALF), F32),
            pltpu.VMEM((SEQ_TILE, LRU_HALF), F32),
            pltpu.VMEM((SEQ_TILE, LRU_HALF), F32),
            pltpu.VMEM((1, LRU_HALF), F32),
        ],
        compiler_params=_cparams("parallel", "parallel", "arbitrary"),
        name="rglru",
    )(cols, cols, conv_w.astype(F32), vec(conv_b), wa_bf16, vec(ba), wx_bf16, vec(bx), vec(lam))


def _merge_kernel(p_ref, a_ref, r_ref, gp_ref, ga_ref, gr_ref, wp_ref, wa_ref, wr_ref, o_ref):
    acc = gp_ref[...].astype(F32) * jnp.dot(p_ref[...], wp_ref[...], preferred_element_type=F32)
    acc += ga_ref[...].astype(F32) * jnp.dot(a_ref[...], wa_ref[...], preferred_element_type=F32)
    acc += gr_ref[...].astype(F32) * jnp.dot(r_ref[...], wr_ref[...], preferred_element_type=F32)
    o_ref[...] = acc.astype(o_ref.dtype)


def _merge(pool_o, attn_o, lru_o, gates, wp, wa, wr, layer):
    m = pool_o.shape[0]
    bm, bn = MERGE_BM, MERGE_BN
    assert m % bm == 0 and D_MODEL % bn == 0
    x_spec = pl.BlockSpec((bm, POOL_WIDTH), lambda i, j: (i, 0))
    gate_spec = lambda k: pl.BlockSpec((bm, bn), lambda i, j: (i, k * D_MODEL // bn + j))
    w_spec = pl.BlockSpec((None, POOL_WIDTH, bn), lambda i, j: (layer, 0, j))
    return pl.pallas_call(
        _merge_kernel,
        grid=(m // bm, D_MODEL // bn),
        in_specs=[x_spec, x_spec, x_spec, gate_spec(0), gate_spec(1), gate_spec(2), w_spec, w_spec, w_spec],
        out_specs=pl.BlockSpec((bm, bn), lambda i, j: (i, j)),
        out_shape=jax.ShapeDtypeStruct((m, D_MODEL), BF16),
        compiler_params=_cparams("parallel", "arbitrary"),
        name="gated_merge",
    )(pool_o, attn_o, lru_o, gates, gates, gates, wp, wa, wr)


def _pack_bf16_pairs(yb):
    c = yb.shape[1] // 2
    lo = lax.bitcast_convert_type(yb[:, :c].astype(F32), jnp.uint32)
    hi = lax.bitcast_convert_type(yb[:, c:].astype(F32), jnp.uint32)
    return (hi & jnp.uint32(0xFFFF0000)) | (lo >> 16)


def _unpack_bf16_pairs(words):
    lo = lax.bitcast_convert_type(words << 16, F32)
    hi = lax.bitcast_convert_type(words & jnp.uint32(0xFFFF0000), F32)
    return jnp.concatenate([lo, hi], axis=1).astype(BF16)


def _route_tile(logits, valid, base):
    bm = logits.shape[0]
    lane = lax.broadcasted_iota(jnp.int32, (bm, ROUTE_W), 1)
    lane_f = lane.astype(F32)
    ninf = -jnp.inf
    big = float(ROUTE_W)

    gl = jnp.where(lane < N_GROUPS, logits, ninf)
    gmax = jnp.max(gl, axis=-1, keepdims=True)
    g = jnp.min(jnp.where(gl == gmax, lane_f, big), axis=-1, keepdims=True)
    p_g = 1.0 / jnp.sum(jnp.exp(gl - gmax), axis=-1, keepdims=True)

    first = N_GROUPS + g * EXPERTS_PER_GROUP
    sl = jnp.where((lane_f >= first) & (lane_f < first + EXPERTS_PER_GROUP), logits, ninf)
    m1 = jnp.max(sl, axis=-1, keepdims=True)
    i1 = jnp.min(jnp.where(sl == m1, lane_f, big), axis=-1, keepdims=True)
    ssum = jnp.sum(jnp.exp(sl - m1), axis=-1, keepdims=True)
    sl2 = jnp.where(lane_f == i1, ninf, sl)
    m2 = jnp.max(sl2, axis=-1, keepdims=True)
    i2 = jnp.min(jnp.where(sl2 == m2, lane_f, big), axis=-1, keepdims=True)
    p1 = 1.0 / ssum
    p2 = jnp.exp(m2 - m1) / ssum
    w1 = p_g * p1 / (p1 + p2)
    w2 = p_g * p2 / (p1 + p2)
    e1 = i1 - N_GROUPS
    e2 = i2 - N_GROUPS

    oh1 = (lane_f == e1) & valid
    oh2 = (lane_f == e2) & valid
    both = (oh1 | oh2).astype(F32)
    earlier = (lax.broadcasted_iota(jnp.int32, (bm, bm), 0) > lax.broadcasted_iota(jnp.int32, (bm, bm), 1))
    prefix = jnp.dot(earlier.astype(BF16), both.astype(BF16), preferred_element_type=F32) + base
    r1 = jnp.sum(jnp.where(oh1, prefix, 0.0), axis=-1, keepdims=True)
    r2 = jnp.sum(jnp.where(oh2, prefix, 0.0), axis=-1, keepdims=True)

    ri = jnp.where(lane == 0, e1, jnp.where(lane == 1, e2, jnp.where(lane == 2, r1, jnp.where(lane == 3, r2, 0.0))))
    rw = jnp.where(lane == 0, w1, jnp.where(lane == 1, w2, 0.0))
    return ri.astype(jnp.int32), rw, jnp.sum(both, axis=0, keepdims=True)


def _outproj_kernel(x_ref, w_ref, h_ref, g_ref, b_ref, rw_ref, rb_ref, hf_ref, hp_ref, ri_ref, rwt_ref, cnt_ref,
                    *, batch, lp):
    i = pl.program_id(0)
    bm = x_ref.shape[0]
    t = jnp.dot(x_ref[...], w_ref[...], preferred_element_type=F32)
    y = _layer_norm(ALPHA * h_ref[...] + t, g_ref[...], b_ref[...])
    valid = _flat_valid_rows(i * bm, bm, batch, lp)
    y = jnp.where(valid, y, 0.0)
    yb = y.astype(BF16)
    hf_ref[...] = y
    hp_ref[...] = _pack_bf16_pairs(yb)
    logits = jnp.dot(yb, rw_ref[...], preferred_element_type=F32) + rb_ref[...]

    @pl.when(i == 0)
    def _():
        cnt_ref[...] = jnp.zeros(cnt_ref.shape, F32)

    ri, rw, tile_cnt = _route_tile(logits, valid, cnt_ref[...])
    ri_ref[...] = ri
    rwt_ref[...] = rw
    cnt_ref[...] += tile_cnt


def _outproj_ln(merged, w_out_bf16, h, g, b, route_w, route_b, layer, batch, lp):
    m, d = h.shape
    bm = OUT_BM
    assert m % bm == 0
    row_spec = pl.BlockSpec((bm, d), lambda i: (i, 0))
    vec_spec = pl.BlockSpec((1, d), lambda i: (0, 0))
    route_spec = pl.BlockSpec((bm, ROUTE_W), lambda i: (i, 0))
    return pl.pallas_call(
        functools.partial(_outproj_kernel, batch=batch, lp=lp),
        grid=(m // bm,),
        in_specs=[
            row_spec,
            pl.BlockSpec((None, d, d), lambda i: (layer, 0, 0)),
            row_spec, vec_spec, vec_spec,
            pl.BlockSpec((d, ROUTE_W), lambda i: (0, 0)),
            pl.BlockSpec((1, ROUTE_W), lambda i: (0, 0)),
        ],
        out_specs=[row_spec, pl.BlockSpec((bm, d // 2), lambda i: (i, 0)), route_spec, route_spec,
                   pl.BlockSpec((1, ROUTE_W), lambda i: (0, 0))],
        out_shape=[jax.ShapeDtypeStruct((m, d), F32), jax.ShapeDtypeStruct((m, d // 2), jnp.uint32),
                   jax.ShapeDtypeStruct((m, ROUTE_W), jnp.int32), jax.ShapeDtypeStruct((m, ROUTE_W), F32),
                   jax.ShapeDtypeStruct((1, ROUTE_W), F32)],
        compiler_params=_cparams("arbitrary"),
        name="out_proj_ln",
    )(merged, w_out_bf16, h, g.reshape(1, d), b.reshape(1, d), route_w, route_b)


def _dispatch_kernel(dest_hbm, hp_ref, xs_in_ref, xs_ref, idx_ref, idx_sem, row_sem):
    del xs_in_ref
    i = pl.program_id(0)
    bm = hp_ref.shape[0]
    n_idx = TOP_K * bm
    idx_copy = pltpu.make_async_copy(dest_hbm.at[pl.ds(pl.multiple_of(i * n_idx, n_idx), n_idx)], idx_ref, idx_sem)
    idx_copy.start()
    idx_copy.wait()

    def issue(r, carry):
        for k in range(TOP_K):
            slot = idx_ref[TOP_K * r + k]
            pltpu.make_async_copy(hp_ref.at[pl.ds(r, 1)], xs_ref.at[pl.ds(slot, 1)], row_sem).start()
        return carry

    lax.fori_loop(0, bm, issue, 0, unroll=8)
    for k in range(TOP_K):
        pltpu.make_async_copy(hp_ref, xs_ref.at[pl.ds(0, bm)], row_sem).wait()


def _dispatch(hp, dest, n_slots):
    m, c = hp.shape
    bm = DISPATCH_BM
    assert m % bm == 0
    slots = jnp.zeros((n_slots, c), jnp.uint32)
    return pl.pallas_call(
        _dispatch_kernel,
        grid=(m // bm,),
        in_specs=[
            pl.BlockSpec(memory_space=pl.ANY),
            pl.BlockSpec((bm, c), lambda i: (i, 0)),
            pl.BlockSpec(memory_space=pl.ANY),
        ],
        out_specs=pl.BlockSpec(memory_space=pl.ANY),
        out_shape=jax.ShapeDtypeStruct((n_slots, c), jnp.uint32),
        scratch_shapes=[pltpu.SMEM((TOP_K * bm,), jnp.int32), pltpu.SemaphoreType.DMA(()),
                        pltpu.SemaphoreType.DMA(())],
        input_output_aliases={2: 0},
        compiler_params=_cparams("arbitrary"),
        name="moe_dispatch",
    )(dest.reshape(m * TOP_K), hp, slots)


def _expert_kernel(start_ref, nchunk_ref, xs_hbm, wg_ref, wu_ref, wd_ref, yb_hbm, wg_s, wu_s, wd_s, x_buf, o_buf,
                   in_sem, out_sem):
    e = pl.program_id(0)
    n = nchunk_ref[e]
    row0 = start_ref[e]
    bm = MOE_BM

    def rows(j):
        return pl.ds(pl.multiple_of(row0 + j * bm, bm), bm)

    def x_copy(j, slot):
        return pltpu.make_async_copy(xs_hbm.at[rows(j)], x_buf.at[slot], in_sem.at[slot])

    def o_copy(j, slot):
        return pltpu.make_async_copy(o_buf.at[slot], yb_hbm.at[rows(j)], out_sem.at[slot])

    @pl.when(n > 0)
    def _():
        x_copy(0, 0).start()
        wg_s[...] = wg_ref[...].astype(BF16)
        wu_s[...] = wu_ref[...].astype(BF16)
        wd_s[...] = wd_ref[...].astype(BF16)

        def chunk(j, carry):
            slot = j % 2
            x_copy(j, slot).wait()

            @pl.when(j + 1 < n)
            def _():
                x_copy(j + 1, 1 - slot).start()

            @pl.when(j >= 2)
            def _():
                o_copy(j - 2, slot).wait()

            x = _unpack_bf16_pairs(x_buf[slot])
            gate = jnp.dot(x, wg_s[...], preferred_element_type=F32)
            up = jnp.dot(x, wu_s[...], preferred_element_type=F32)
            hdn = (gate * _sigmoid(gate)) * up
            o_buf[slot] = jnp.dot(hdn.astype(BF16), wd_s[...], preferred_element_type=F32)
            o_copy(j, slot).start()
            return carry

        lax.fori_loop(0, n, chunk, 0)

        @pl.when(n >= 2)
        def _():
            o_copy(n - 2, n % 2).wait()

        o_copy(n - 1, (n - 1) % 2).wait()

    @pl.when(e == N_EXPERTS - 1)
    def _():
        o_buf[0] = jnp.zeros((bm, D_MODEL), F32)

        def fill(j, carry):
            o_copy(j, 0).start()
            o_copy(j, 0).wait()
            return carry

        lax.fori_loop(n, (yb_hbm.shape[0] - row0) // bm, fill, 0)


def _experts(xs, n_rows, chunk_start, n_chunks, w_gate, w_up, w_down, layer):
    d = D_MODEL
    bm = MOE_BM
    w_map = lambda e, st, nc: (layer, e, 0, 0)
    grid_spec = pltpu.PrefetchScalarGridSpec(
        num_scalar_prefetch=2,
        grid=(N_EXPERTS,),
        in_specs=[
            pl.BlockSpec(memory_space=pl.ANY),
            pl.BlockSpec((None, None, d, D_EXPERT), w_map),
            pl.BlockSpec((None, None, d, D_EXPERT), w_map),
            pl.BlockSpec((None, None, D_EXPERT, d), w_map),
        ],
        out_specs=pl.BlockSpec(memory_space=pl.ANY),
        scratch_shapes=[
            pltpu.VMEM((d, D_EXPERT), BF16),
            pltpu.VMEM((d, D_EXPERT), BF16),
            pltpu.VMEM((D_EXPERT, d), BF16),
            pltpu.VMEM((2, bm, d // 2), jnp.uint32),
            pltpu.VMEM((2, bm, d), F32),
            pltpu.SemaphoreType.DMA((2,)),
            pltpu.SemaphoreType.DMA((2,)),
        ],
    )
    return pl.pallas_call(
        _expert_kernel,
        grid_spec=grid_spec,
        out_shape=jax.ShapeDtypeStruct((n_rows, d), F32),
        compiler_params=_cparams("arbitrary"),
        name="expert_mlp",
    )(chunk_start, n_chunks, xs, w_gate, w_up, w_down)


def _combine_ln_kernel(dest_hbm, h_ref, rw_ref, g_ref, b_ref, yb_hbm, hf_ref, hb_ref, idx_ref, rows_ref, idx_sem,
                       row_sem, *, batch, lp):
    i = pl.program_id(0)
    n_tiles = pl.num_programs(0)
    bm = h_ref.shape[0]
    n_idx = TOP_K * bm
    buf = i % 2

    def start_gather(tile, into):
        idx_copy = pltpu.make_async_copy(
            dest_hbm.at[pl.ds(pl.multiple_of(tile * n_idx, n_idx), n_idx)], idx_ref, idx_sem)
        idx_copy.start()
        idx_copy.wait()

        def issue(r, carry):
            for k in range(TOP_K):
                slot = idx_ref[TOP_K * r + k]
                pltpu.make_async_copy(yb_hbm.at[pl.ds(slot, 1)], rows_ref.at[into, k, pl.ds(r, 1)],
                                      row_sem.at[into]).start()
            return carry

        lax.fori_loop(0, bm, issue, 0, unroll=8)

    @pl.when(i == 0)
    def _():
        start_gather(0, 0)

    @pl.when(i + 1 < n_tiles)
    def _():
        start_gather(i + 1, 1 - buf)

    for k in range(TOP_K):
        pltpu.make_async_copy(yb_hbm.at[pl.ds(0, bm)], rows_ref.at[buf, k], row_sem.at[buf]).wait()

    ffn = rw_ref[:, 0:1] * rows_ref[buf, 0]
    for k in range(1, TOP_K):
        ffn = ffn + rw_ref[:, k:k + 1] * rows_ref[buf, k]
    y = _layer_norm(ALPHA * h_ref[...] + ffn, g_ref[...], b_ref[...])
    y = jnp.where(_flat_valid_rows(i * bm, bm, batch, lp), y, 0.0)
    hf_ref[...] = y
    hb_ref[...] = y.astype(BF16)


def _combine_ln(h, yb, dest, rw, g, b, batch, lp):
    m, d = h.shape
    bm = COMBINE_BM
    assert m % bm == 0
    row_spec = pl.BlockSpec((bm, d), lambda i: (i, 0))
    vec_spec = pl.BlockSpec((1, d), lambda i: (0, 0))
    return pl.pallas_call(
        functools.partial(_combine_ln_kernel, batch=batch, lp=lp),
        grid=(m // bm,),
        in_specs=[
            pl.BlockSpec(memory_space=pl.ANY),
            row_spec,
            pl.BlockSpec((bm, ROUTE_W), lambda i: (i, 0)),
            vec_spec, vec_spec,
            pl.BlockSpec(memory_space=pl.ANY),
        ],
        out_specs=[row_spec, row_spec],
        out_shape=[jax.ShapeDtypeStruct((m, d), F32), jax.ShapeDtypeStruct((m, d), BF16)],
        scratch_shapes=[pltpu.SMEM((TOP_K * bm,), jnp.int32), pltpu.VMEM((2, TOP_K, bm, d), F32),
                        pltpu.SemaphoreType.DMA(()), pltpu.SemaphoreType.DMA((2,))],
        compiler_params=_cparams("arbitrary"),
        name="moe_combine_ln",
    )(dest.reshape(m * TOP_K), h, rw, g.reshape(1, d), b.reshape(1, d), yb)


def _slot_tables(ri, cnt, batch, lp):
    m = ri.shape[0]
    bm = MOE_BM
    eid = ri[:, 0:TOP_K]
    rank = ri[:, TOP_K:2 * TOP_K]
    counts = cnt[0, :N_EXPERTS].astype(jnp.int32)
    padded = (counts + bm - 1) // bm * bm
    pad_end = jnp.cumsum(padded)
    pad_start = pad_end - padded
    n_real = batch * (lp - PAD) * TOP_K
    nblk = -(-(n_real + N_EXPERTS * (bm - 1)) // bm)
    cap = nblk * bm
    experts = jnp.arange(N_EXPERTS, dtype=jnp.int32)
    start = jnp.sum(jnp.where(eid[:, :, None] == experts[None, None, :], pad_start[None, None, :], 0), axis=-1)
    row = jnp.arange(m, dtype=jnp.int32)
    pos = row % lp
    spare = cap + ((row // lp) * PAD + pos)[:, None] * TOP_K + jnp.arange(TOP_K, dtype=jnp.int32)[None, :]
    valid = (pos >= PAD)[:, None]
    dest = jnp.where(valid, start + rank, spare)
    dest_read = jnp.where(valid, dest, 0)
    n_slots = cap + batch * PAD * TOP_K
    return dest, dest_read, pad_start.astype(jnp.int32), (padded // bm).astype(jnp.int32), cap, n_slots


def kernel(x, meta, ln_emb_g, ln_emb_b, w_in, pool_w, pool_scale, attn_sink, conv_w, conv_b, lru_wa, lru_ba,
           lru_wx, lru_bx, lru_lambda, proj_pool, proj_attn, proj_lru, w_out, ln1_g, ln1_b, router_grp_w,
           router_grp_b, router_exp_w, router_exp_b, exp_w_gate, exp_w_up, exp_w_down, ln2_g, ln2_b):
    batch, seq, d = x.shape
    lp = PAD + N_META + seq
    m = batch * lp

    hf, hb = _embed(x, meta, ln_emb_g, ln_emb_b)
    w_in_b = w_in.astype(BF16)
    pool_w_b = pool_w.astype(BF16)
    wa_b = lru_wa.astype(BF16)
    wx_b = lru_wx.astype(BF16)
    wp_b = proj_pool.astype(BF16)
    wat_b = proj_attn.astype(BF16)
    wl_b = proj_lru.astype(BF16)
    w_out_b = w_out.astype(BF16)
    route_pad = ROUTE_W - N_GROUPS - N_EXPERTS

    for l in range(DEPTH):
        cols = _inproj(hb, w_in_b, l, gates=False)
        gates = _inproj(hb, w_in_b, l, gates=True)
        pool_o = _pool(cols, pool_w_b[l], pool_scale[l], batch, lp)
        attn_o = _attention(cols, attn_sink[l], batch, lp)
        lru_o = _lru(cols, conv_w[l], conv_b[l], wa_b[l], lru_ba[l], wx_b[l], lru_bx[l], lru_lambda[l],
                     batch, lp)
        merged = _merge(pool_o, attn_o, lru_o, gates, wp_b, wat_b, wl_b, l)
        route_w = jnp.concatenate(
            [router_grp_w[l], router_exp_w[l], jnp.zeros((d, route_pad), F32)], axis=1).astype(BF16)
        route_b = jnp.concatenate(
            [router_grp_b[l], router_exp_b[l], jnp.zeros((route_pad,), F32)]).reshape(1, ROUTE_W)
        h1f, h1p, ri, rw, cnt = _outproj_ln(merged, w_out_b, hf, ln1_g[l], ln1_b[l], route_w, route_b, l,
                                            batch, lp)
        dest, dest_read, chunk_start, n_chunks, cap, n_slots = _slot_tables(ri, cnt, batch, lp)
        xs = _dispatch(h1p, dest, n_slots)
        yb = _experts(xs, cap, chunk_start, n_chunks, exp_w_gate, exp_w_up, exp_w_down, l)
        hf, hb = _combine_ln(h1f, yb, dest_read, rw, ln2_g[l], ln2_b[l], batch, lp)

    return hf.reshape(batch, lp, d)[:, PAD + N_META:]
```

```python
import functools

import jax
import jax.numpy as jnp
from jax import lax
from jax.experimental import pallas as pl
from jax.experimental.pallas import tpu as pltpu

F32 = jnp.float32
BF16 = jnp.bfloat16

D_MODEL = 2048
DEPTH = 2
N_META = 16
POOL_WINDOWS = (2, 4, 8, 16)
POOL_WIDTH = D_MODEL // 2
POOL_GROUP = POOL_WIDTH // len(POOL_WINDOWS)
N_HEADS = 16
N_KV_HEADS = 4
HEAD_DIM = 64
Q_PER_KV = N_HEADS // N_KV_HEADS
WINDOW = 128
BLOCK = 128
NEG = -1e30
LRU_WIDTH = D_MODEL // 2
LRU_BLOCKS = 4
LRU_BLOCK = LRU_WIDTH // LRU_BLOCKS
CONV_WIDTH = 4
LRU_C = 8.0
N_GROUPS = 4
EXPERTS_PER_GROUP = 8
N_EXPERTS = N_GROUPS * EXPERTS_PER_GROUP
TOP_K = 2
D_EXPERT = D_MODEL // 4
LN_EPS = 1e-5
ALPHA = (2.0 * DEPTH) ** 0.25

PAD = BLOCK - N_META
ATT_W = N_HEADS * HEAD_DIM
KV_W = N_KV_HEADS * HEAD_DIM
OFF_POOL = 0
OFF_Q = OFF_POOL + POOL_WIDTH
OFF_K = OFF_Q + ATT_W
OFF_V = OFF_K + KV_W
OFF_LX = OFF_V + KV_W
OFF_LY = OFF_LX + LRU_WIDTH
OFF_GATE = OFF_LY + LRU_WIDTH
IN_COLS = OFF_GATE + 3 * D_MODEL

VMEM_LIMIT_BYTES = 56 * 1024 * 1024
SUBLANES = 8

SEQ_TILE = 3 * BLOCK
INPROJ_BM = 1536
INPROJ_BN = 768
MERGE_BM = 768
MERGE_BN = 1024
OUT_BM = 512
OUT_SPLIT = 2
MOE_BM = 256
DISPATCH_BM = 1536
COMBINE_BM = 512
ROUTE_W = 128


def _cparams(*sem):
    return pltpu.CompilerParams(dimension_semantics=sem, vmem_limit_bytes=VMEM_LIMIT_BYTES)


def _layer_norm(x, g, b):
    mu = jnp.mean(x, axis=-1, keepdims=True)
    xc = x - mu
    var = jnp.mean(xc * xc, axis=-1, keepdims=True)
    return xc * lax.rsqrt(var + LN_EPS) * g + b


def _flat_valid_rows(row0, n_rows, batch, lp):
    r = row0 + lax.broadcasted_iota(jnp.int32, (n_rows, 1), 0)
    pad_row = jnp.zeros((n_rows, 1), jnp.bool_)
    for b in range(batch):
        pad_row = pad_row | ((r >= b * lp) & (r < b * lp + PAD))
    return jnp.logical_not(pad_row)


def _embed_kernel(x_ref, meta_ref, g_ref, b_ref, hf_ref, hb_ref):
    i = pl.program_id(1)
    src = jnp.where(i == 0, meta_ref[...], x_ref[...])
    y = _layer_norm(src, g_ref[...], b_ref[...])
    row = lax.broadcasted_iota(jnp.int32, (BLOCK, 1), 0)
    y = jnp.where((i > 0) | (row >= PAD), y, 0.0)
    hf_ref[...] = y
    hb_ref[...] = y.astype(BF16)


def _embed(x, meta, g, b):
    batch, seq, d = x.shape
    nblk = seq // BLOCK
    m = batch * (nblk + 1) * BLOCK
    meta_tile = jnp.concatenate([jnp.zeros((PAD, d), F32), meta.astype(F32)], axis=0)
    row_spec = pl.BlockSpec((BLOCK, d), lambda bi, i: (bi * (nblk + 1) + i, 0))
    vec_spec = pl.BlockSpec((1, d), lambda bi, i: (0, 0))
    return pl.pallas_call(
        _embed_kernel,
        grid=(batch, nblk + 1),
        in_specs=[
            pl.BlockSpec((BLOCK, d), lambda bi, i: (bi * nblk + jnp.maximum(i - 1, 0), 0)),
            pl.BlockSpec((BLOCK, d), lambda bi, i: (0, 0)),
            vec_spec, vec_spec,
        ],
        out_specs=[row_spec, row_spec],
        out_shape=[jax.ShapeDtypeStruct((m, d), F32), jax.ShapeDtypeStruct((m, d), BF16)],
        compiler_params=_cparams("parallel", "arbitrary"),
        name="embed_ln",
    )(x.reshape(batch * seq, d), meta_tile, g.reshape(1, d), b.reshape(1, d))


def _sigmoid(x):
    return 0.5 * jnp.tanh(0.5 * x) + 0.5


def _inproj_kernel(x_ref, w_ref, o_ref, *, gates):
    acc = jnp.dot(x_ref[...], w_ref[...], preferred_element_type=F32)
    o_ref[...] = (_sigmoid(acc) if gates else acc).astype(o_ref.dtype)


def _inproj(hb, w_in_bf16, layer, gates):
    m, d = hb.shape
    bm, bn = INPROJ_BM, INPROJ_BN
    col0, width = (OFF_GATE, IN_COLS - OFF_GATE) if gates else (0, OFF_GATE)
    assert m % bm == 0 and width % bn == 0 and col0 % bn == 0
    return pl.pallas_call(
        functools.partial(_inproj_kernel, gates=gates),
        grid=(m // bm, width // bn),
        in_specs=[
            pl.BlockSpec((bm, d), lambda i, j: (i, 0)),
            pl.BlockSpec((None, d, bn), lambda i, j: (layer, 0, col0 // bn + j)),
        ],
        out_specs=pl.BlockSpec((bm, bn), lambda i, j: (i, j)),
        out_shape=jax.ShapeDtypeStruct((m, width), BF16),
        compiler_params=_cparams("parallel", "arbitrary"),
        name="in_proj_gates" if gates else "in_proj",
    )(hb, w_in_bf16)


def _pool_kernel(u_ref, w_ref, scale_ref, o_ref, ext_ref):
    t = pl.program_id(1)
    tile = SEQ_TILE
    maxw = max(POOL_WINDOWS)

    @pl.when(t == 0)
    def _():
        ext_ref[pl.ds(0, maxw), :] = jnp.zeros((maxw, POOL_WIDTH), F32)

    @pl.when(t > 0)
    def _():
        ext_ref[pl.ds(0, maxw), :] = ext_ref[pl.ds(tile, maxw), :]

    ext_ref[pl.ds(maxw, tile), :] = u_ref[...].astype(F32)

    pos = t * tile + lax.broadcasted_iota(jnp.int32, (tile, 1), 0) - PAD
    for gi, w in enumerate(POOL_WINDOWS):
        cols = pl.ds(gi * POOL_GROUP, POOL_GROUP)
        u = ext_ref[pl.ds(maxw, tile), cols]
        win = u
        for k in range(1, w):
            win = win + ext_ref[pl.ds(maxw - k, tile), cols]
        cnt = jnp.clip(pos + 1, 1, w).astype(F32)
        delta = win / cnt - u
        mixed = jnp.dot(delta.astype(BF16), w_ref[gi], preferred_element_type=F32)
        o_ref[:, cols] = (mixed * scale_ref[:, cols]).astype(o_ref.dtype)


def _pool(cols, pool_w_bf16, pool_scale, batch, lp):
    m = cols.shape[0]
    nt = lp // SEQ_TILE
    maxw = max(POOL_WINDOWS)
    return pl.pallas_call(
        _pool_kernel,
        grid=(batch, nt),
        in_specs=[
            pl.BlockSpec((SEQ_TILE, POOL_WIDTH), lambda b, t: (b * nt + t, OFF_POOL // POOL_WIDTH)),
            pl.BlockSpec((len(POOL_WINDOWS), POOL_GROUP, POOL_GROUP), lambda b, t: (0, 0, 0)),
            pl.BlockSpec((1, POOL_WIDTH), lambda b, t: (0, 0)),
        ],
        out_specs=pl.BlockSpec((SEQ_TILE, POOL_WIDTH), lambda b, t: (b * nt + t, 0)),
        out_shape=jax.ShapeDtypeStruct((m, POOL_WIDTH), BF16),
        scratch_shapes=[pltpu.VMEM((SEQ_TILE + maxw, POOL_WIDTH), F32)],
        compiler_params=_cparams("parallel", "arbitrary"),
        name="pool_mixer",
    )(cols, pool_w_bf16, pool_scale.reshape(1, POOL_WIDTH))


def _attn_bias():
    kj = jnp.arange(2 * BLOCK, dtype=jnp.int32)[:, None]
    qi = jnp.arange(BLOCK, dtype=jnp.int32)[None, :]
    dist = BLOCK + qi - kj
    in_window = (dist >= 0) & (dist < WINDOW)
    slopes = 2.0 ** (-8.0 * jnp.arange(1, N_HEADS + 1, dtype=F32) / N_HEADS)
    alibi = -slopes[:, None, None] * dist.astype(F32)[None]
    return jnp.where(in_window[None], alibi, NEG)


def _attn_kernel(q_ref, kp_ref, kc_ref, vp_ref, vc_ref, bias_ref, sink_ref, o_ref):
    n = pl.program_id(1)
    q = q_ref[...] * (HEAD_DIM ** -0.5)

    def heads(early_keys):
        for kh in range(N_KV_HEADS):
            hs = pl.ds(kh * HEAD_DIM, HEAD_DIM)
            k2 = jnp.concatenate([kp_ref[:, hs], kc_ref[:, hs]], axis=0)
            v2 = jnp.concatenate([vp_ref[:, hs], vc_ref[:, hs]], axis=0)
            for g in range(Q_PER_KV):
                h = kh * Q_PER_KV + g
                qh = q[:, h * HEAD_DIM:(h + 1) * HEAD_DIM]
                s = lax.dot_general(k2, qh, (((1,), (1,)), ((), ())), preferred_element_type=F32)
                s = s + bias_ref[h]
                if early_keys is not None:
                    s = s + early_keys
                sk = sink_ref[0, h]
                mx = jnp.maximum(jnp.max(s, axis=0, keepdims=True), sk)
                p = jnp.exp(s - mx)
                den = jnp.sum(p, axis=0, keepdims=True) + jnp.exp(sk - mx)
                pn = (p * (1.0 / den)).astype(BF16)
                o = lax.dot_general(pn, v2, (((0,), (0,)), ((), ())), preferred_element_type=F32)
                o_ref[:, pl.ds(h * HEAD_DIM, HEAD_DIM)] = o.astype(o_ref.dtype)

    @pl.when(n < 2)
    def _():
        k_pos = (n - 1) * BLOCK + lax.broadcasted_iota(jnp.int32, (2 * BLOCK, 1), 0)
        heads(jnp.where(k_pos < PAD, NEG, 0.0))

    @pl.when(n >= 2)
    def _():
        heads(None)


def _attention(cols, sink, batch, lp):
    m = cols.shape[0]
    nb = lp // BLOCK
    cur = lambda cb: (lambda b, n: (b * nb + n, cb))
    prev = lambda cb: (lambda b, n: (b * nb + jnp.maximum(n - 1, 0), cb))
    return pl.pallas_call(
        _attn_kernel,
        grid=(batch, nb),
        in_specs=[
            pl.BlockSpec((BLOCK, ATT_W), cur(OFF_Q // ATT_W)),
            pl.BlockSpec((BLOCK, KV_W), prev(OFF_K // KV_W)),
            pl.BlockSpec((BLOCK, KV_W), cur(OFF_K // KV_W)),
            pl.BlockSpec((BLOCK, KV_W), prev(OFF_V // KV_W)),
            pl.BlockSpec((BLOCK, KV_W), cur(OFF_V // KV_W)),
            pl.BlockSpec((N_HEADS, 2 * BLOCK, BLOCK), lambda b, n: (0, 0, 0)),
            pl.BlockSpec(memory_space=pltpu.SMEM),
        ],
        out_specs=pl.BlockSpec((BLOCK, ATT_W), lambda b, n: (b * nb + n, 0)),
        out_shape=jax.ShapeDtypeStruct((m, ATT_W), BF16),
        compiler_params=_cparams("parallel", "arbitrary"),
        name="swa_attention",
    )(cols, cols, cols, cols, cols, _attn_bias(), sink.reshape(1, N_HEADS).astype(F32))


LRU_HALF = LRU_WIDTH // 2
LRU_HALO = 8
LRU_SCAN_UNROLL = 6


def _gelu_tanh(x):
    return 0.5 * x * (1.0 + jnp.tanh(0.7978845608028654 * (x + 0.044715 * (x * x * x))))


def _lru_kernel(x_ref, y_ref, cw_ref, cb_ref, wa_ref, ba_ref, wx_ref, bx_ref, lam_ref, o_ref,
                ext_ref, a_ref, b_ref, carry_ref):
    t = pl.program_id(2)
    tile = SEQ_TILE
    width = LRU_HALF

    @pl.when(t == 0)
    def _():
        ext_ref[pl.ds(0, LRU_HALO), :] = jnp.zeros((LRU_HALO, width), F32)
        carry_ref[...] = jnp.zeros((1, width), F32)

    @pl.when(t > 0)
    def _():
        ext_ref[pl.ds(0, LRU_HALO), :] = ext_ref[pl.ds(tile, LRU_HALO), :]

    ext_ref[pl.ds(LRU_HALO, tile), :] = x_ref[...].astype(F32)

    xc = cb_ref[...] + cw_ref[pl.ds(CONV_WIDTH - 1, 1), :] * ext_ref[pl.ds(LRU_HALO, tile), :]
    for j in range(CONV_WIDTH - 1):
        shift = CONV_WIDTH - 1 - j
        xc = xc + cw_ref[pl.ds(j, 1), :] * ext_ref[pl.ds(LRU_HALO - shift, tile), :]

    xcb = xc.astype(BF16)
    ga_parts, gx_parts = [], []
    for blk in range(width // LRU_BLOCK):
        xb = xcb[:, blk * LRU_BLOCK:(blk + 1) * LRU_BLOCK]
        ga_parts.append(jnp.dot(xb, wa_ref[blk], preferred_element_type=F32))
        gx_parts.append(jnp.dot(xb, wx_ref[blk], preferred_element_type=F32))
    gate_a = _sigmoid(jnp.concatenate(ga_parts, axis=1) + ba_ref[...])
    gate_x = _sigmoid(jnp.concatenate(gx_parts, axis=1) + bx_ref[...])

    neg_lam = -lam_ref[...]
    softplus = jnp.maximum(neg_lam, 0.0) + jnp.log1p(jnp.exp(-jnp.abs(neg_lam)))
    log_a = (-LRU_C) * gate_a * softplus
    a = jnp.exp(log_a)
    b_in = jnp.sqrt(1.0 - a * a) * gate_x * xc
    pos = t * tile + lax.broadcasted_iota(jnp.int32, (tile, 1), 0)
    b_in = jnp.where(pos >= PAD, b_in, 0.0)
    a_ref[...] = a
    b_ref[...] = b_in

    row = lax.broadcasted_iota(jnp.int32, (8, width), 0)

    def group(r, carry):
        rows = pl.ds(pl.multiple_of(r * 8, 8), 8)
        av = a_ref[rows, :]
        bv = b_ref[rows, :]
        for k in (1, 2, 4):
            a_sh = jnp.where(row >= k, pltpu.roll(av, k, 0), 1.0)
            b_sh = jnp.where(row >= k, pltpu.roll(bv, k, 0), 0.0)
            bv = av * b_sh + bv
            av = av * a_sh
        hv = av * carry + bv
        b_ref[rows, :] = hv
        return hv[7:8, :]

    carry_ref[...] = lax.fori_loop(0, tile // 8, group, carry_ref[...], unroll=LRU_SCAN_UNROLL)
    o_ref[...] = (b_ref[...] * _gelu_tanh(y_ref[...].astype(F32))).astype(o_ref.dtype)


def _lru(cols, conv_w, conv_b, wa_bf16, ba, wx_bf16, bx, lam, batch, lp):
    m = cols.shape[0]
    nt = lp // SEQ_TILE
    nhalf = LRU_WIDTH // LRU_HALF
    blocks_per_half = LRU_HALF // LRU_BLOCK
    vec = lambda v: v.reshape(1, LRU_WIDTH).astype(F32)
    vec_spec = pl.BlockSpec((1, LRU_HALF), lambda b, c, t: (0, c))
    w_spec = pl.BlockSpec((blocks_per_half, LRU_BLOCK, LRU_BLOCK), lambda b, c, t: (c, 0, 0))
    return pl.pallas_call(
        _lru_kernel,
        grid=(batch, nhalf, nt),
        in_specs=[
            pl.BlockSpec((SEQ_TILE, LRU_HALF), lambda b, c, t: (b * nt + t, OFF_LX // LRU_HALF + c)),
            pl.BlockSpec((SEQ_TILE, LRU_HALF), lambda b, c, t: (b * nt + t, OFF_LY // LRU_HALF + c)),
            pl.BlockSpec((CONV_WIDTH, LRU_HALF), lambda b, c, t: (0, c)),
            vec_spec, w_spec, vec_spec, w_spec, vec_spec, vec_spec,
        ],
        out_specs=pl.BlockSpec((SEQ_TILE, LRU_HALF), lambda b, c, t: (b * nt + t, c)),
        out_shape=jax.ShapeDtypeStruct((m, LRU_WIDTH), BF16),
        scratch_shapes=[
            pltpu.VMEM((SEQ_TILE + LRU_HALO, LRU_HALF), F32),
            pltpu.VMEM((SEQ_TILE, LRU_HALF), F32),
            pltpu.VMEM((SEQ_TILE, LRU_HALF), F32),
            pltpu.VMEM((1, LRU_HALF), F32),
        ],
        compiler_params=_cparams("parallel", "parallel", "arbitrary"),
        name="rglru",
    )(cols, cols, conv_w.astype(F32), vec(conv_b), wa_bf16, vec(ba), wx_bf16, vec(bx), vec(lam))


def _merge_kernel(p_ref, a_ref, r_ref, gp_ref, ga_ref, gr_ref, wp_ref, wa_ref, wr_ref, o_ref):
    acc = gp_ref[...].astype(F32) * jnp.dot(p_ref[...], wp_ref[...], preferred_element_type=F32)
    acc += ga_ref[...].astype(F32) * jnp.dot(a_ref[...], wa_ref[...], preferred_element_type=F32)
    acc += gr_ref[...].astype(F32) * jnp.dot(r_ref[...], wr_ref[...], preferred_element_type=F32)
    o_ref[...] = acc.astype(o_ref.dtype)


def _merge(pool_o, attn_o, lru_o, gates, wp, wa, wr, layer):
    m = pool_o.shape[0]
    bm, bn = MERGE_BM, MERGE_BN
    assert m % bm == 0 and D_MODEL % bn == 0
    x_spec = pl.BlockSpec((bm, POOL_WIDTH), lambda i, j: (i, 0))
    gate_spec = lambda k: pl.BlockSpec((bm, bn), lambda i, j: (i, k * D_MODEL // bn + j))
    w_spec = pl.BlockSpec((None, POOL_WIDTH, bn), lambda i, j: (layer, 0, j))
    return pl.pallas_call(
        _merge_kernel,
        grid=(m // bm, D_MODEL // bn),
        in_specs=[x_spec, x_spec, x_spec, gate_spec(0), gate_spec(1), gate_spec(2), w_spec, w_spec, w_spec],
        out_specs=pl.BlockSpec((bm, bn), lambda i, j: (i, j)),
        out_shape=jax.ShapeDtypeStruct((m, D_MODEL), BF16),
        compiler_params=_cparams("parallel", "arbitrary"),
        name="gated_merge",
    )(pool_o, attn_o, lru_o, gates, gates, gates, wp, wa, wr)


def _pack_bf16_pairs(yb):
    c = yb.shape[1] // 2
    lo = lax.bitcast_convert_type(yb[:, :c].astype(F32), jnp.uint32)
    hi = lax.bitcast_convert_type(yb[:, c:].astype(F32), jnp.uint32)
    return (hi & jnp.uint32(0xFFFF0000)) | (lo >> 16)


def _unpack_bf16_pairs(words):
    lo = lax.bitcast_convert_type(words << 16, F32)
    hi = lax.bitcast_convert_type(words & jnp.uint32(0xFFFF0000), F32)
    return jnp.concatenate([lo, hi], axis=1).astype(BF16)


def _route_tile(logits, valid, base):
    bm = logits.shape[0]
    lane = lax.broadcasted_iota(jnp.int32, (bm, ROUTE_W), 1)
    lane_f = lane.astype(F32)
    ninf = -jnp.inf
    big = float(ROUTE_W)

    gl = jnp.where(lane < N_GROUPS, logits, ninf)
    gmax = jnp.max(gl, axis=-1, keepdims=True)
    g = jnp.min(jnp.where(gl == gmax, lane_f, big), axis=-1, keepdims=True)
    p_g = 1.0 / jnp.sum(jnp.exp(gl - gmax), axis=-1, keepdims=True)

    first = N_GROUPS + g * EXPERTS_PER_GROUP
    sl = jnp.where((lane_f >= first) & (lane_f < first + EXPERTS_PER_GROUP), logits, ninf)
    m1 = jnp.max(sl, axis=-1, keepdims=True)
    i1 = jnp.min(jnp.where(sl == m1, lane_f, big), axis=-1, keepdims=True)
    ssum = jnp.sum(jnp.exp(sl - m1), axis=-1, keepdims=True)
    sl2 = jnp.where(lane_f == i1, ninf, sl)
    m2 = jnp.max(sl2, axis=-1, keepdims=True)
    i2 = jnp.min(jnp.where(sl2 == m2, lane_f, big), axis=-1, keepdims=True)
    p1 = 1.0 / ssum
    p2 = jnp.exp(m2 - m1) / ssum
    w1 = p_g * p1 / (p1 + p2)
    w2 = p_g * p2 / (p1 + p2)
    e1 = i1 - N_GROUPS
    e2 = i2 - N_GROUPS

    oh1 = (lane_f == e1) & valid
    oh2 = (lane_f == e2) & valid
    both = (oh1 | oh2).astype(F32)
    earlier = (lax.broadcasted_iota(jnp.int32, (bm, bm), 0) > lax.broadcasted_iota(jnp.int32, (bm, bm), 1))
    prefix = jnp.dot(earlier.astype(BF16), both.astype(BF16), preferred_element_type=F32) + base
    r1 = jnp.sum(jnp.where(oh1, prefix, 0.0), axis=-1, keepdims=True)
    r2 = jnp.sum(jnp.where(oh2, prefix, 0.0), axis=-1, keepdims=True)

    ri = jnp.where(lane == 0, e1, jnp.where(lane == 1, e2, jnp.where(lane == 2, r1, jnp.where(lane == 3, r2, 0.0))))
    rw = jnp.where(lane == 0, w1, jnp.where(lane == 1, w2, 0.0))
    return ri.astype(jnp.int32), rw, jnp.sum(both, axis=0, keepdims=True)


def _outproj_kernel(x_ref, w_ref, h_ref, g_ref, b_ref, rw_ref, rb_ref, hf_ref, hp_ref, ri_ref, rwt_ref, cnt_ref,
                    *, batch, lp):
    i = pl.program_id(0)
    bm = x_ref.shape[0]
    sub = bm // OUT_SPLIT

    @pl.when(i == 0)
    def _():
        cnt_ref[...] = jnp.zeros(cnt_ref.shape, F32)

    for s in range(OUT_SPLIT):
        rows = pl.ds(s * sub, sub)
        t = jnp.dot(x_ref[rows, :], w_ref[...], preferred_element_type=F32)
        y = _layer_norm(ALPHA * h_ref[rows, :] + t, g_ref[...], b_ref[...])
        valid = _flat_valid_rows(i * bm + s * sub, sub, batch, lp)
        y = jnp.where(valid, y, 0.0)
        yb = y.astype(BF16)
        hf_ref[rows, :] = y
        hp_ref[rows, :] = _pack_bf16_pairs(yb)
        logits = jnp.dot(yb, rw_ref[...], preferred_element_type=F32) + rb_ref[...]
        ri, rw, tile_cnt = _route_tile(logits, valid, cnt_ref[...])
        ri_ref[rows, :] = ri
        rwt_ref[rows, :] = rw
        cnt_ref[...] += tile_cnt


def _outproj_ln(merged, w_out_bf16, h, g, b, route_w, route_b, layer, batch, lp):
    m, d = h.shape
    bm = OUT_BM
    assert m % bm == 0
    row_spec = pl.BlockSpec((bm, d), lambda i: (i, 0))
    vec_spec = pl.BlockSpec((1, d), lambda i: (0, 0))
    route_spec = pl.BlockSpec((bm, ROUTE_W), lambda i: (i, 0))
    return pl.pallas_call(
        functools.partial(_outproj_kernel, batch=batch, lp=lp),
        grid=(m // bm,),
        in_specs=[
            row_spec,
            pl.BlockSpec((None, d, d), lambda i: (layer, 0, 0)),
            row_spec, vec_spec, vec_spec,
            pl.BlockSpec((d, ROUTE_W), lambda i: (0, 0)),
            pl.BlockSpec((1, ROUTE_W), lambda i: (0, 0)),
        ],
        out_specs=[row_spec, pl.BlockSpec((bm, d // 2), lambda i: (i, 0)), route_spec, route_spec,
                   pl.BlockSpec((1, ROUTE_W), lambda i: (0, 0))],
        out_shape=[jax.ShapeDtypeStruct((m, d), F32), jax.ShapeDtypeStruct((m, d // 2), jnp.uint32),
                   jax.ShapeDtypeStruct((m, ROUTE_W), jnp.int32), jax.ShapeDtypeStruct((m, ROUTE_W), F32),
                   jax.ShapeDtypeStruct((1, ROUTE_W), F32)],
        compiler_params=_cparams("arbitrary"),
        name="out_proj_ln",
    )(merged, w_out_bf16, h, g.reshape(1, d), b.reshape(1, d), route_w, route_b)


def _dispatch_kernel(dest_hbm, hp_ref, xs_in_ref, xs_ref, idx_ref, idx_sem, row_sem):
    del xs_in_ref
    i = pl.program_id(0)
    groups = hp_ref.shape[0]
    bm = groups * SUBLANES
    n_idx = TOP_K * bm
    idx_copy = pltpu.make_async_copy(dest_hbm.at[pl.ds(pl.multiple_of(i * n_idx, n_idx), n_idx)], idx_ref, idx_sem)
    idx_copy.start()
    idx_copy.wait()

    def issue(g, carry):
        for s in range(SUBLANES):
            for k in range(TOP_K):
                slot = idx_ref[TOP_K * (g * SUBLANES + s) + k]
                pltpu.make_async_copy(hp_ref.at[g, pl.ds(s, 1)], xs_ref.at[pl.ds(slot, 1)],
                                      row_sem).start(priority=k % 2)
        return carry

    lax.fori_loop(0, groups, issue, 0)
    for k in range(TOP_K):
        pltpu.make_async_copy(xs_ref.at[pl.ds(0, bm)], xs_ref.at[pl.ds(0, bm)], row_sem).wait()


def _dispatch(hp, dest, n_slots):
    m, c = hp.shape
    bm = DISPATCH_BM
    assert m % bm == 0 and bm % SUBLANES == 0
    slots = jnp.zeros((n_slots, c), jnp.uint32)
    hp = hp.reshape(m // SUBLANES, SUBLANES, c)
    return pl.pallas_call(
        _dispatch_kernel,
        grid=(m // bm,),
        in_specs=[
            pl.BlockSpec(memory_space=pl.ANY),
            pl.BlockSpec((bm // SUBLANES, SUBLANES, c), lambda i: (i, 0, 0)),
            pl.BlockSpec(memory_space=pl.ANY),
        ],
        out_specs=pl.BlockSpec(memory_space=pl.ANY),
        out_shape=jax.ShapeDtypeStruct((n_slots, c), jnp.uint32),
        scratch_shapes=[pltpu.SMEM((TOP_K * bm,), jnp.int32), pltpu.SemaphoreType.DMA(()),
                        pltpu.SemaphoreType.DMA(())],
        input_output_aliases={2: 0},
        compiler_params=_cparams("arbitrary"),
        name="moe_dispatch",
    )(dest.reshape(m * TOP_K), hp, slots)


def _expert_kernel(start_ref, nchunk_ref, xs_hbm, wg_ref, wu_ref, wd_ref, yb_hbm, wg_s, wu_s, wd_s, x_buf, o_buf,
                   in_sem, out_sem):
    e = pl.program_id(0)
    n = nchunk_ref[e]
    row0 = start_ref[e]
    bm = MOE_BM

    def rows(j):
        return pl.ds(pl.multiple_of(row0 + j * bm, bm), bm)

    def x_copy(j, slot):
        return pltpu.make_async_copy(xs_hbm.at[rows(j)], x_buf.at[slot], in_sem.at[slot])

    def o_copy(j, slot):
        return pltpu.make_async_copy(o_buf.at[slot], yb_hbm.at[rows(j)], out_sem.at[slot])

    @pl.when(n > 0)
    def _():
        x_copy(0, 0).start(priority=1)
        wg_s[...] = wg_ref[...].astype(BF16)
        wu_s[...] = wu_ref[...].astype(BF16)
        wd_s[...] = wd_ref[...].astype(BF16)

        def chunk(j, carry):
            slot = j % 2
            x_copy(j, slot).wait()

            @pl.when(j + 1 < n)
            def _():
                x_copy(j + 1, 1 - slot).start(priority=1)

            @pl.when(j >= 2)
            def _():
                o_copy(j - 2, slot).wait()

            x = _unpack_bf16_pairs(x_buf[slot])
            gate = jnp.dot(x, wg_s[...], preferred_element_type=F32)
            up = jnp.dot(x, wu_s[...], preferred_element_type=F32)
            hdn = (gate * _sigmoid(gate)) * up
            o_buf[slot] = jnp.dot(hdn.astype(BF16), wd_s[...], preferred_element_type=F32)
            o_copy(j, slot).start(priority=1)
            return carry

        lax.fori_loop(0, n, chunk, 0)

        @pl.when(n >= 2)
        def _():
            o_copy(n - 2, n % 2).wait()

        o_copy(n - 1, (n - 1) % 2).wait()

    @pl.when(e == N_EXPERTS - 1)
    def _():
        o_buf[0] = jnp.zeros((bm, D_MODEL), F32)

        def fill(j, carry):
            o_copy(j, 0).start()
            o_copy(j, 0).wait()
            return carry

        lax.fori_loop(n, (yb_hbm.shape[0] - row0) // bm, fill, 0)


def _experts(xs, n_rows, chunk_start, n_chunks, w_gate, w_up, w_down, layer):
    d = D_MODEL
    bm = MOE_BM
    w_map = lambda e, st, nc: (layer, e, 0, 0)
    grid_spec = pltpu.PrefetchScalarGridSpec(
        num_scalar_prefetch=2,
        grid=(N_EXPERTS,),
        in_specs=[
            pl.BlockSpec(memory_space=pl.ANY),
            pl.BlockSpec((None, None, d, D_EXPERT), w_map),
            pl.BlockSpec((None, None, d, D_EXPERT), w_map),
            pl.BlockSpec((None, None, D_EXPERT, d), w_map),
        ],
        out_specs=pl.BlockSpec(memory_space=pl.ANY),
        scratch_shapes=[
            pltpu.VMEM((d, D_EXPERT), BF16),
            pltpu.VMEM((d, D_EXPERT), BF16),
            pltpu.VMEM((D_EXPERT, d), BF16),
            pltpu.VMEM((2, bm, d // 2), jnp.uint32),
            pltpu.VMEM((2, bm, d), F32),
            pltpu.SemaphoreType.DMA((2,)),
            pltpu.SemaphoreType.DMA((2,)),
        ],
    )
    return pl.pallas_call(
        _expert_kernel,
        grid_spec=grid_spec,
        out_shape=jax.ShapeDtypeStruct((n_rows, d), F32),
        compiler_params=_cparams("arbitrary"),
        name="expert_mlp",
    )(chunk_start, n_chunks, xs, w_gate, w_up, w_down)


def _combine_ln_kernel(dest_hbm, h_ref, rw_ref, g_ref, b_ref, yb_hbm, *rest, batch, lp, final):
    if final:
        out_hbm, idx_ref, rows_ref, y_buf, idx_sem, row_sem, out_sem = rest
    else:
        hf_ref, hb_ref, idx_ref, rows_ref, idx_sem, row_sem = rest
    i = pl.program_id(0)
    n_tiles = pl.num_programs(0)
    bm = h_ref.shape[0]
    n_idx = TOP_K * bm
    buf = i % 2

    def start_gather(tile, into):
        idx_copy = pltpu.make_async_copy(
            dest_hbm.at[pl.ds(pl.multiple_of(tile * n_idx, n_idx), n_idx)], idx_ref, idx_sem)
        idx_copy.start()
        idx_copy.wait()

        def issue(g, carry):
            for s in range(SUBLANES):
                for k in range(TOP_K):
                    slot = idx_ref[TOP_K * (g * SUBLANES + s) + k]
                    pltpu.make_async_copy(yb_hbm.at[pl.ds(slot, 1)], rows_ref.at[into, k, g, pl.ds(s, 1)],
                                          row_sem.at[into]).start(priority=k % 2)
            return carry

        lax.fori_loop(0, bm // SUBLANES, issue, 0)

    @pl.when(i == 0)
    def _():
        start_gather(0, 0)

    @pl.when(i + 1 < n_tiles)
    def _():
        start_gather(i + 1, 1 - buf)

    for k in range(TOP_K):
        pltpu.make_async_copy(yb_hbm.at[pl.ds(0, bm)], yb_hbm.at[pl.ds(0, bm)], row_sem.at[buf]).wait()

    d = h_ref.shape[1]
    ffn = rw_ref[:, 0:1] * rows_ref[buf, 0].reshape(bm, d)
    for k in range(1, TOP_K):
        ffn = ffn + rw_ref[:, k:k + 1] * rows_ref[buf, k].reshape(bm, d)
    y = _layer_norm(ALPHA * h_ref[...] + ffn, g_ref[...], b_ref[...])
    if not final:
        y = jnp.where(_flat_valid_rows(i * bm, bm, batch, lp), y, 0.0)
        hf_ref[...] = y
        hb_ref[...] = y.astype(BF16)
        return

    nb = lp // BLOCK

    def for_each_out_block(step, fn):
        for j in range(bm // BLOCK):
            blk = step * (bm // BLOCK) + j
            seq_blk = blk % nb

            @pl.when(seq_blk >= 1)
            def _():
                dst = pl.multiple_of(((blk // nb) * (nb - 1) + seq_blk - 1) * BLOCK, BLOCK)
                fn(pltpu.make_async_copy(y_buf.at[pl.ds(j * BLOCK, BLOCK)], out_hbm.at[pl.ds(dst, BLOCK)], out_sem))

    @pl.when(i > 0)
    def _():
        for_each_out_block(i - 1, lambda c: c.wait())

    y_buf[...] = y
    for_each_out_block(i, lambda c: c.start(priority=1))

    @pl.when(i == n_tiles - 1)
    def _():
        for_each_out_block(i, lambda c: c.wait())


def _combine_ln(h, yb, dest, rw, g, b, batch, lp, final):
    m, d = h.shape
    bm = COMBINE_BM
    assert m % bm == 0 and bm % BLOCK == 0
    row_spec = pl.BlockSpec((bm, d), lambda i: (i, 0))
    vec_spec = pl.BlockSpec((1, d), lambda i: (0, 0))
    scratch = [pltpu.SMEM((TOP_K * bm,), jnp.int32), pltpu.VMEM((2, TOP_K, bm // SUBLANES, SUBLANES, d), F32)]
    sems = [pltpu.SemaphoreType.DMA(()), pltpu.SemaphoreType.DMA((2,))]
    if final:
        out_specs = pl.BlockSpec(memory_space=pl.ANY)
        out_shape = jax.ShapeDtypeStruct((batch * (lp - BLOCK), d), F32)
        scratch = scratch + [pltpu.VMEM((bm, d), F32)] + sems + [pltpu.SemaphoreType.DMA(())]
    else:
        out_specs = [row_spec, row_spec]
        out_shape = [jax.ShapeDtypeStruct((m, d), F32), jax.ShapeDtypeStruct((m, d), BF16)]
        scratch = scratch + sems
    return pl.pallas_call(
        functools.partial(_combine_ln_kernel, batch=batch, lp=lp, final=final),
        grid=(m // bm,),
        in_specs=[
            pl.BlockSpec(memory_space=pl.ANY),
            row_spec,
            pl.BlockSpec((bm, ROUTE_W), lambda i: (i, 0)),
            vec_spec, vec_spec,
            pl.BlockSpec(memory_space=pl.ANY),
        ],
        out_specs=out_specs,
        out_shape=out_shape,
        scratch_shapes=scratch,
        compiler_params=_cparams("arbitrary"),
        name="moe_combine_out" if final else "moe_combine_ln",
    )(dest.reshape(m * TOP_K), h, rw, g.reshape(1, d), b.reshape(1, d), yb)


def _slot_tables(ri, cnt, batch, lp):
    m = ri.shape[0]
    bm = MOE_BM
    eid = ri[:, 0:TOP_K]
    rank = ri[:, TOP_K:2 * TOP_K]
    counts = cnt[0, :N_EXPERTS].astype(jnp.int32)
    padded = (counts + bm - 1) // bm * bm
    pad_end = jnp.cumsum(padded)
    pad_start = pad_end - padded
    n_real = batch * (lp - PAD) * TOP_K
    nblk = -(-(n_real + N_EXPERTS * (bm - 1)) // bm)
    cap = nblk * bm
    experts = jnp.arange(N_EXPERTS, dtype=jnp.int32)
    start = jnp.sum(jnp.where(eid[:, :, None] == experts[None, None, :], pad_start[None, None, :], 0), axis=-1)
    row = jnp.arange(m, dtype=jnp.int32)
    pos = row % lp
    spare = cap + ((row // lp) * PAD + pos)[:, None] * TOP_K + jnp.arange(TOP_K, dtype=jnp.int32)[None, :]
    valid = (pos >= PAD)[:, None]
    dest = jnp.where(valid, start + rank, spare)
    dest_read = jnp.where(valid, dest, 0)
    n_slots = cap + batch * PAD * TOP_K
    return dest, dest_read, pad_start.astype(jnp.int32), (padded // bm).astype(jnp.int32), cap, n_slots


def kernel(x, meta, ln_emb_g, ln_emb_b, w_in, pool_w, pool_scale, attn_sink, conv_w, conv_b, lru_wa, lru_ba,
           lru_wx, lru_bx, lru_lambda, proj_pool, proj_attn, proj_lru, w_out, ln1_g, ln1_b, router_grp_w,
           router_grp_b, router_exp_w, router_exp_b, exp_w_gate, exp_w_up, exp_w_down, ln2_g, ln2_b):
    batch, seq, d = x.shape
    lp = PAD + N_META + seq
    m = batch * lp

    hf, hb = _embed(x, meta, ln_emb_g, ln_emb_b)
    w_in_b = w_in.astype(BF16)
    pool_w_b = pool_w.astype(BF16)
    wa_b = lru_wa.astype(BF16)
    wx_b = lru_wx.astype(BF16)
    wp_b = proj_pool.astype(BF16)
    wat_b = proj_attn.astype(BF16)
    wl_b = proj_lru.astype(BF16)
    w_out_b = w_out.astype(BF16)
    route_pad = ROUTE_W - N_GROUPS - N_EXPERTS

    for l in range(DEPTH):
        cols = _inproj(hb, w_in_b, l, gates=False)
        gates = _inproj(hb, w_in_b, l, gates=True)
        pool_o = _pool(cols, pool_w_b[l], pool_scale[l], batch, lp)
        attn_o = _attention(cols, attn_sink[l], batch, lp)
        lru_o = _lru(cols, conv_w[l], conv_b[l], wa_b[l], lru_ba[l], wx_b[l], lru_bx[l], lru_lambda[l],
                     batch, lp)
        merged = _merge(pool_o, attn_o, lru_o, gates, wp_b, wat_b, wl_b, l)
        route_w = jnp.concatenate(
            [router_grp_w[l], router_exp_w[l], jnp.zeros((d, route_pad), F32)], axis=1).astype(BF16)
        route_b = jnp.concatenate(
            [router_grp_b[l], router_exp_b[l], jnp.zeros((route_pad,), F32)]).reshape(1, ROUTE_W)
        h1f, h1p, ri, rw, cnt = _outproj_ln(merged, w_out_b, hf, ln1_g[l], ln1_b[l], route_w, route_b, l,
                                            batch, lp)
        dest, dest_read, chunk_start, n_chunks, cap, n_slots = _slot_tables(ri, cnt, batch, lp)
        xs = _dispatch(h1p, dest, n_slots)
        yb = _experts(xs, cap, chunk_start, n_chunks, exp_w_gate, exp_w_up, exp_w_down, l)
        if l + 1 < DEPTH:
            hf, hb = _combine_ln(h1f, yb, dest_read, rw, ln2_g[l], ln2_b[l], batch, lp, final=False)
        else:
            out = _combine_ln(h1f, yb, dest_read, rw, ln2_g[l], ln2_b[l], batch, lp, final=True)

    return out.reshape(batch, seq, d)
```

```python
import functools

import jax
import jax.numpy as jnp
from jax import lax
from jax.experimental import pallas as pl
from jax.experimental.pallas import tpu as pltpu

F32 = jnp.float32
BF16 = jnp.bfloat16

D_MODEL = 2048
DEPTH = 2
N_META = 16
POOL_WINDOWS = (2, 4, 8, 16)
POOL_WIDTH = D_MODEL // 2
POOL_GROUP = POOL_WIDTH // len(POOL_WINDOWS)
N_HEADS = 16
N_KV_HEADS = 4
HEAD_DIM = 64
Q_PER_KV = N_HEADS // N_KV_HEADS
WINDOW = 128
BLOCK = 128
NEG = -1e30
LRU_WIDTH = D_MODEL // 2
LRU_BLOCKS = 4
LRU_BLOCK = LRU_WIDTH // LRU_BLOCKS
CONV_WIDTH = 4
LRU_C = 8.0
N_GROUPS = 4
EXPERTS_PER_GROUP = 8
N_EXPERTS = N_GROUPS * EXPERTS_PER_GROUP
TOP_K = 2
D_EXPERT = D_MODEL // 4
LN_EPS = 1e-5
ALPHA = (2.0 * DEPTH) ** 0.25

PAD = BLOCK - N_META
ATT_W = N_HEADS * HEAD_DIM
KV_W = N_KV_HEADS * HEAD_DIM
OFF_POOL = 0
OFF_Q = OFF_POOL + POOL_WIDTH
OFF_K = OFF_Q + ATT_W
OFF_V = OFF_K + KV_W
OFF_LX = OFF_V + KV_W
OFF_LY = OFF_LX + LRU_WIDTH
OFF_GATE = OFF_LY + LRU_WIDTH
IN_COLS = OFF_GATE + 3 * D_MODEL

VMEM_LIMIT_BYTES = 56 * 1024 * 1024
SUBLANES = 8

SEQ_TILE = 3 * BLOCK
INPROJ_BM = 1536
INPROJ_BN = 1536
MERGE_BM = 768
MERGE_BN = 1024
OUT_BM = 512
OUT_SPLIT = 2
MOE_BM = 256
DISPATCH_BM = 1536
COMBINE_BM = 512
EXPERT_X_BUFFERS = 3
ROUTE_W = 128


def _cparams(*sem):
    return pltpu.CompilerParams(dimension_semantics=sem, vmem_limit_bytes=VMEM_LIMIT_BYTES)


def _layer_norm(x, g, b):
    mu = jnp.mean(x, axis=-1, keepdims=True)
    xc = x - mu
    var = jnp.mean(xc * xc, axis=-1, keepdims=True)
    return xc * lax.rsqrt(var + LN_EPS) * g + b


def _flat_valid_rows(row0, n_rows, batch, lp):
    r = row0 + lax.broadcasted_iota(jnp.int32, (n_rows, 1), 0)
    pad_row = jnp.zeros((n_rows, 1), jnp.bool_)
    for b in range(batch):
        pad_row = pad_row | ((r >= b * lp) & (r < b * lp + PAD))
    return jnp.logical_not(pad_row)


EMBED_BLOCKS = SEQ_TILE // BLOCK


def _embed_kernel(*refs):
    x_refs, (meta_ref, g_ref, b_ref, hf_ref, hb_ref) = refs[:EMBED_BLOCKS], refs[EMBED_BLOCKS:]
    t = pl.program_id(1)
    for j in range(EMBED_BLOCKS):
        src = x_refs[j][...]
        if j == 0:
            src = jnp.where(t == 0, meta_ref[...], src)
        y = _layer_norm(src, g_ref[...], b_ref[...])
        if j == 0:
            row = lax.broadcasted_iota(jnp.int32, (BLOCK, 1), 0)
            y = jnp.where((t > 0) | (row >= PAD), y, 0.0)
        hf_ref[pl.ds(j * BLOCK, BLOCK), :] = y
        hb_ref[pl.ds(j * BLOCK, BLOCK), :] = y.astype(BF16)


def _embed(x, meta, g, b):
    batch, seq, d = x.shape
    nblk = seq // BLOCK
    lp = (nblk + 1) * BLOCK
    nt = lp // SEQ_TILE
    m = batch * lp
    meta_tile = jnp.concatenate([jnp.zeros((PAD, d), F32), meta.astype(F32)], axis=0)
    row_spec = pl.BlockSpec((SEQ_TILE, d), lambda bi, t: (bi * nt + t, 0))
    vec_spec = pl.BlockSpec((1, d), lambda bi, t: (0, 0))
    x_spec = lambda j: pl.BlockSpec(
        (BLOCK, d), lambda bi, t: (bi * nblk + jnp.maximum(EMBED_BLOCKS * t + j - 1, 0), 0))
    x2 = x.reshape(batch * seq, d)
    return pl.pallas_call(
        _embed_kernel,
        grid=(batch, nt),
        in_specs=[x_spec(j) for j in range(EMBED_BLOCKS)]
        + [pl.BlockSpec((BLOCK, d), lambda bi, t: (0, 0)), vec_spec, vec_spec],
        out_specs=[row_spec, row_spec],
        out_shape=[jax.ShapeDtypeStruct((m, d), F32), jax.ShapeDtypeStruct((m, d), BF16)],
        compiler_params=_cparams("parallel", "arbitrary"),
        name="embed_ln",
    )(*([x2] * EMBED_BLOCKS), meta_tile, g.reshape(1, d), b.reshape(1, d))


def _sigmoid(x):
    return 0.5 * jnp.tanh(0.5 * x) + 0.5


def _inproj_kernel(x_ref, w_ref, o_ref, *, gates):
    acc = jnp.dot(x_ref[...], w_ref[...], preferred_element_type=F32)
    o_ref[...] = (_sigmoid(acc) if gates else acc).astype(o_ref.dtype)


def _inproj(hb, w_in_bf16, layer, gates):
    m, d = hb.shape
    bm, bn = INPROJ_BM, INPROJ_BN
    col0, width = (OFF_GATE, IN_COLS - OFF_GATE) if gates else (0, OFF_GATE)
    assert m % bm == 0 and width % bn == 0 and col0 % bn == 0
    return pl.pallas_call(
        functools.partial(_inproj_kernel, gates=gates),
        grid=(m // bm, width // bn),
        in_specs=[
            pl.BlockSpec((bm, d), lambda i, j: (i, 0)),
            pl.BlockSpec((None, d, bn), lambda i, j: (layer, 0, col0 // bn + j)),
        ],
        out_specs=pl.BlockSpec((bm, bn), lambda i, j: (i, j)),
        out_shape=jax.ShapeDtypeStruct((m, width), BF16),
        compiler_params=_cparams("parallel", "arbitrary"),
        name="in_proj_gates" if gates else "in_proj",
    )(hb, w_in_bf16)


def _pool_kernel(u_ref, w_ref, scale_ref, o_ref, ext_ref):
    t = pl.program_id(1)
    tile = SEQ_TILE
    maxw = max(POOL_WINDOWS)

    @pl.when(t == 0)
    def _():
        ext_ref[pl.ds(0, maxw), :] = jnp.zeros((maxw, POOL_WIDTH), F32)

    @pl.when(t > 0)
    def _():
        ext_ref[pl.ds(0, maxw), :] = ext_ref[pl.ds(tile, maxw), :]

    ext_ref[pl.ds(maxw, tile), :] = u_ref[...].astype(F32)

    pos = t * tile + lax.broadcasted_iota(jnp.int32, (tile, 1), 0) - PAD
    for gi, w in enumerate(POOL_WINDOWS):
        cols = pl.ds(gi * POOL_GROUP, POOL_GROUP)
        u = ext_ref[pl.ds(maxw, tile), cols]
        win = u
        for k in range(1, w):
            win = win + ext_ref[pl.ds(maxw - k, tile), cols]
        cnt = jnp.clip(pos + 1, 1, w).astype(F32)
        delta = win / cnt - u
        mixed = jnp.dot(delta.astype(BF16), w_ref[gi], preferred_element_type=F32)
        o_ref[:, cols] = (mixed * scale_ref[:, cols]).astype(o_ref.dtype)


def _pool(cols, pool_w_bf16, pool_scale, batch, lp):
    m = cols.shape[0]
    nt = lp // SEQ_TILE
    maxw = max(POOL_WINDOWS)
    return pl.pallas_call(
        _pool_kernel,
        grid=(batch, nt),
        in_specs=[
            pl.BlockSpec((SEQ_TILE, POOL_WIDTH), lambda b, t: (b * nt + t, OFF_POOL // POOL_WIDTH)),
            pl.BlockSpec((len(POOL_WINDOWS), POOL_GROUP, POOL_GROUP), lambda b, t: (0, 0, 0)),
            pl.BlockSpec((1, POOL_WIDTH), lambda b, t: (0, 0)),
        ],
        out_specs=pl.BlockSpec((SEQ_TILE, POOL_WIDTH), lambda b, t: (b * nt + t, 0)),
        out_shape=jax.ShapeDtypeStruct((m, POOL_WIDTH), BF16),
        scratch_shapes=[pltpu.VMEM((SEQ_TILE + maxw, POOL_WIDTH), F32)],
        compiler_params=_cparams("parallel", "arbitrary"),
        name="pool_mixer",
    )(cols, pool_w_bf16, pool_scale.reshape(1, POOL_WIDTH))


def _attn_bias():
    kj = jnp.arange(2 * BLOCK, dtype=jnp.int32)[:, None]
    qi = jnp.arange(BLOCK, dtype=jnp.int32)[None, :]
    dist = BLOCK + qi - kj
    in_window = (dist >= 0) & (dist < WINDOW)
    slopes = 2.0 ** (-8.0 * jnp.arange(1, N_HEADS + 1, dtype=F32) / N_HEADS)
    alibi = -slopes[:, None, None] * dist.astype(F32)[None]
    return jnp.where(in_window[None], alibi, NEG)


def _attn_kernel(q_ref, kp_ref, kc_ref, vp_ref, vc_ref, bias_ref, sink_ref, o_ref):
    n = pl.program_id(1)
    q = q_ref[...] * (HEAD_DIM ** -0.5)

    def heads(early_keys):
        for kh in range(N_KV_HEADS):
            hs = pl.ds(kh * HEAD_DIM, HEAD_DIM)
            k2 = jnp.concatenate([kp_ref[:, hs], kc_ref[:, hs]], axis=0)
            v2 = jnp.concatenate([vp_ref[:, hs], vc_ref[:, hs]], axis=0)
            for g in range(Q_PER_KV):
                h = kh * Q_PER_KV + g
                qh = q[:, h * HEAD_DIM:(h + 1) * HEAD_DIM]
                s = lax.dot_general(k2, qh, (((1,), (1,)), ((), ())), preferred_element_type=F32)
                s = s + bias_ref[h]
                if early_keys is not None:
                    s = s + early_keys
                sk = sink_ref[0, h]
                mx = jnp.maximum(jnp.max(s, axis=0, keepdims=True), sk)
                p = jnp.exp(s - mx)
                den = jnp.sum(p, axis=0, keepdims=True) + jnp.exp(sk - mx)
                pn = (p * (1.0 / den)).astype(BF16)
                o = lax.dot_general(pn, v2, (((0,), (0,)), ((), ())), preferred_element_type=F32)
                o_ref[:, pl.ds(h * HEAD_DIM, HEAD_DIM)] = o.astype(o_ref.dtype)

    @pl.when(n < 2)
    def _():
        k_pos = (n - 1) * BLOCK + lax.broadcasted_iota(jnp.int32, (2 * BLOCK, 1), 0)
        heads(jnp.where(k_pos < PAD, NEG, 0.0))

    @pl.when(n >= 2)
    def _():
        heads(None)


def _attention(cols, sink, batch, lp):
    m = cols.shape[0]
    nb = lp // BLOCK
    cur = lambda cb: (lambda b, n: (b * nb + n, cb))
    prev = lambda cb: (lambda b, n: (b * nb + jnp.maximum(n - 1, 0), cb))
    return pl.pallas_call(
        _attn_kernel,
        grid=(batch, nb),
        in_specs=[
            pl.BlockSpec((BLOCK, ATT_W), cur(OFF_Q // ATT_W)),
            pl.BlockSpec((BLOCK, KV_W), prev(OFF_K // KV_W)),
            pl.BlockSpec((BLOCK, KV_W), cur(OFF_K // KV_W)),
            pl.BlockSpec((BLOCK, KV_W), prev(OFF_V // KV_W)),
            pl.BlockSpec((BLOCK, KV_W), cur(OFF_V // KV_W)),
            pl.BlockSpec((N_HEADS, 2 * BLOCK, BLOCK), lambda b, n: (0, 0, 0)),
            pl.BlockSpec(memory_space=pltpu.SMEM),
        ],
        out_specs=pl.BlockSpec((BLOCK, ATT_W), lambda b, n: (b * nb + n, 0)),
        out_shape=jax.ShapeDtypeStruct((m, ATT_W), BF16),
        compiler_params=_cparams("parallel", "arbitrary"),
        name="swa_attention",
    )(cols, cols, cols, cols, cols, _attn_bias(), sink.reshape(1, N_HEADS).astype(F32))


LRU_HALF = LRU_WIDTH // 2
LRU_HALO = 8
LRU_SCAN_UNROLL = 6


def _gelu_tanh(x):
    return 0.5 * x * (1.0 + jnp.tanh(0.7978845608028654 * (x + 0.044715 * (x * x * x))))


def _lru_kernel(x_ref, y_ref, cw_ref, cb_ref, wa_ref, ba_ref, wx_ref, bx_ref, lam_ref, o_ref,
                ext_ref, a_ref, b_ref, carry_ref):
    t = pl.program_id(2)
    tile = SEQ_TILE
    width = LRU_HALF

    @pl.when(t == 0)
    def _():
        ext_ref[pl.ds(0, LRU_HALO), :] = jnp.zeros((LRU_HALO, width), F32)
        carry_ref[...] = jnp.zeros((1, width), F32)

    @pl.when(t > 0)
    def _():
        ext_ref[pl.ds(0, LRU_HALO), :] = ext_ref[pl.ds(tile, LRU_HALO), :]

    ext_ref[pl.ds(LRU_HALO, tile), :] = x_ref[...].astype(F32)

    xc = cb_ref[...] + cw_ref[pl.ds(CONV_WIDTH - 1, 1), :] * ext_ref[pl.ds(LRU_HALO, tile), :]
    for j in range(CONV_WIDTH - 1):
        shift = CONV_WIDTH - 1 - j
        xc = xc + cw_ref[pl.ds(j, 1), :] * ext_ref[pl.ds(LRU_HALO - shift, tile), :]

    xcb = xc.astype(BF16)
    ga_parts, gx_parts = [], []
    for blk in range(width // LRU_BLOCK):
        xb = xcb[:, blk * LRU_BLOCK:(blk + 1) * LRU_BLOCK]
        ga_parts.append(jnp.dot(xb, wa_ref[blk], preferred_element_type=F32))
        gx_parts.append(jnp.dot(xb, wx_ref[blk], preferred_element_type=F32))
    gate_a = _sigmoid(jnp.concatenate(ga_parts, axis=1) + ba_ref[...])
    gate_x = _sigmoid(jnp.concatenate(gx_parts, axis=1) + bx_ref[...])

    neg_lam = -lam_ref[...]
    softplus = jnp.maximum(neg_lam, 0.0) + jnp.log1p(jnp.exp(-jnp.abs(neg_lam)))
    log_a = (-LRU_C) * gate_a * softplus
    a = jnp.exp(log_a)
    b_in = jnp.sqrt(1.0 - a * a) * gate_x * xc
    pos = t * tile + lax.broadcasted_iota(jnp.int32, (tile, 1), 0)
    b_in = jnp.where(pos >= PAD, b_in, 0.0)
    a_ref[...] = a
    b_ref[...] = b_in

    row = lax.broadcasted_iota(jnp.int32, (8, width), 0)

    def group(r, carry):
        rows = pl.ds(pl.multiple_of(r * 8, 8), 8)
        av = a_ref[rows, :]
        bv = b_ref[rows, :]
        for k in (1, 2, 4):
            a_sh = jnp.where(row >= k, pltpu.roll(av, k, 0), 1.0)
            b_sh = jnp.where(row >= k, pltpu.roll(bv, k, 0), 0.0)
            bv = av * b_sh + bv
            av = av * a_sh
        hv = av * carry + bv
        b_ref[rows, :] = hv
        return hv[7:8, :]

    carry_ref[...] = lax.fori_loop(0, tile // 8, group, carry_ref[...], unroll=LRU_SCAN_UNROLL)
    o_ref[...] = (b_ref[...] * _gelu_tanh(y_ref[...].astype(F32))).astype(o_ref.dtype)


def _lru(cols, conv_w, conv_b, wa_bf16, ba, wx_bf16, bx, lam, batch, lp):
    m = cols.shape[0]
    nt = lp // SEQ_TILE
    nhalf = LRU_WIDTH // LRU_HALF
    blocks_per_half = LRU_HALF // LRU_BLOCK
    vec = lambda v: v.reshape(1, LRU_WIDTH).astype(F32)
    vec_spec = pl.BlockSpec((1, LRU_HALF), lambda b, c, t: (0, c))
    w_spec = pl.BlockSpec((blocks_per_half, LRU_BLOCK, LRU_BLOCK), lambda b, c, t: (c, 0, 0))
    return pl.pallas_call(
        _lru_kernel,
        grid=(batch, nhalf, nt),
        in_specs=[
            pl.BlockSpec((SEQ_TILE, LRU_HALF), lambda b, c, t: (b * nt + t, OFF_LX // LRU_HALF + c)),
            pl.BlockSpec((SEQ_TILE, LRU_HALF), lambda b, c, t: (b * nt + t, OFF_LY // LRU_HALF + c)),
            pl.BlockSpec((CONV_WIDTH, LRU_HALF), lambda b, c, t: (0, c)),
            vec_spec, w_spec, vec_spec, w_spec, vec_spec, vec_spec,
        ],
        out_specs=pl.BlockSpec((SEQ_TILE, LRU_HALF), lambda b, c, t: (b * nt + t, c)),
        out_shape=jax.ShapeDtypeStruct((m, LRU_WIDTH), BF16),
        scratch_shapes=[
            pltpu.VMEM((SEQ_TILE + LRU_HALO, LRU_HALF), F32),
            pltpu.VMEM((SEQ_TILE, LRU_HALF), F32),
            pltpu.VMEM((SEQ_TILE, LRU_HALF), F32),
            pltpu.VMEM((1, LRU_HALF), F32),
        ],
        compiler_params=_cparams("parallel", "parallel", "arbitrary"),
        name="rglru",
    )(cols, cols, conv_w.astype(F32), vec(conv_b), wa_bf16, vec(ba), wx_bf16, vec(bx), vec(lam))


def _merge_kernel(p_ref, a_ref, r_ref, gp_ref, ga_ref, gr_ref, wp_ref, wa_ref, wr_ref, o_ref):
    acc = gp_ref[...].astype(F32) * jnp.dot(p_ref[...], wp_ref[...], preferred_element_type=F32)
    acc += ga_ref[...].astype(F32) * jnp.dot(a_ref[...], wa_ref[...], preferred_element_type=F32)
    acc += gr_ref[...].astype(F32) * jnp.dot(r_ref[...], wr_ref[...], preferred_element_type=F32)
    o_ref[...] = acc.astype(o_ref.dtype)


def _merge(pool_o, attn_o, lru_o, gates, wp, wa, wr, layer):
    m = pool_o.shape[0]
    bm, bn = MERGE_BM, MERGE_BN
    assert m % bm == 0 and D_MODEL % bn == 0
    x_spec = pl.BlockSpec((bm, POOL_WIDTH), lambda i, j: (i, 0))
    gate_spec = lambda k: pl.BlockSpec((bm, bn), lambda i, j: (i, k * D_MODEL // bn + j))
    w_spec = pl.BlockSpec((None, POOL_WIDTH, bn), lambda i, j: (layer, 0, j))
    return pl.pallas_call(
        _merge_kernel,
        grid=(m // bm, D_MODEL // bn),
        in_specs=[x_spec, x_spec, x_spec, gate_spec(0), gate_spec(1), gate_spec(2), w_spec, w_spec, w_spec],
        out_specs=pl.BlockSpec((bm, bn), lambda i, j: (i, j)),
        out_shape=jax.ShapeDtypeStruct((m, D_MODEL), BF16),
        compiler_params=_cparams("parallel", "arbitrary"),
        name="gated_merge",
    )(pool_o, attn_o, lru_o, gates, gates, gates, wp, wa, wr)


def _pack_bf16_pairs(yb):
    c = yb.shape[1] // 2
    lo = lax.bitcast_convert_type(yb[:, :c].astype(F32), jnp.uint32)
    hi = lax.bitcast_convert_type(yb[:, c:].astype(F32), jnp.uint32)
    return (hi & jnp.uint32(0xFFFF0000)) | (lo >> 16)


def _unpack_bf16_pairs(words, dtype=BF16):
    lo = lax.bitcast_convert_type(words << 16, F32)
    hi = lax.bitcast_convert_type(words & jnp.uint32(0xFFFF0000), F32)
    return jnp.concatenate([lo, hi], axis=1).astype(dtype)


def _route_tile(logits, valid, base):
    bm = logits.shape[0]
    lane = lax.broadcasted_iota(jnp.int32, (bm, ROUTE_W), 1)
    lane_f = lane.astype(F32)
    ninf = -jnp.inf
    big = float(ROUTE_W)

    gl = jnp.where(lane < N_GROUPS, logits, ninf)
    gmax = jnp.max(gl, axis=-1, keepdims=True)
    g = jnp.min(jnp.where(gl == gmax, lane_f, big), axis=-1, keepdims=True)
    p_g = 1.0 / jnp.sum(jnp.exp(gl - gmax), axis=-1, keepdims=True)

    first = N_GROUPS + g * EXPERTS_PER_GROUP
    sl = jnp.where((lane_f >= first) & (lane_f < first + EXPERTS_PER_GROUP), logits, ninf)
    m1 = jnp.max(sl, axis=-1, keepdims=True)
    i1 = jnp.min(jnp.where(sl == m1, lane_f, big), axis=-1, keepdims=True)
    ssum = jnp.sum(jnp.exp(sl - m1), axis=-1, keepdims=True)
    sl2 = jnp.where(lane_f == i1, ninf, sl)
    m2 = jnp.max(sl2, axis=-1, keepdims=True)
    i2 = jnp.min(jnp.where(sl2 == m2, lane_f, big), axis=-1, keepdims=True)
    p1 = 1.0 / ssum
    p2 = jnp.exp(m2 - m1) / ssum
    w1 = p_g * p1 / (p1 + p2)
    w2 = p_g * p2 / (p1 + p2)
    e1 = i1 - N_GROUPS
    e2 = i2 - N_GROUPS

    oh1 = (lane_f == e1) & valid
    oh2 = (lane_f == e2) & valid
    both = (oh1 | oh2).astype(F32)
    earlier = (lax.broadcasted_iota(jnp.int32, (bm, bm), 0) > lax.broadcasted_iota(jnp.int32, (bm, bm), 1))
    prefix = jnp.dot(earlier.astype(BF16), both.astype(BF16), preferred_element_type=F32) + base
    r1 = jnp.sum(jnp.where(oh1, prefix, 0.0), axis=-1, keepdims=True)
    r2 = jnp.sum(jnp.where(oh2, prefix, 0.0), axis=-1, keepdims=True)

    ri = jnp.where(lane == 0, e1, jnp.where(lane == 1, e2, jnp.where(lane == 2, r1, jnp.where(lane == 3, r2, 0.0))))
    rw = jnp.where(lane == 0, w1, jnp.where(lane == 1, w2, 0.0))
    return ri.astype(jnp.int32), rw, jnp.sum(both, axis=0, keepdims=True)


def _outproj_kernel(x_ref, w_ref, h_ref, g_ref, b_ref, rw_ref, rb_ref, hf_ref, hp_ref, ri_ref, rwt_ref, cnt_ref,
                    *, batch, lp):
    i = pl.program_id(0)
    bm = x_ref.shape[0]
    sub = bm // OUT_SPLIT

    @pl.when(i == 0)
    def _():
        cnt_ref[...] = jnp.zeros(cnt_ref.shape, F32)

    for s in range(OUT_SPLIT):
        rows = pl.ds(s * sub, sub)
        t = jnp.dot(x_ref[rows, :], w_ref[...], preferred_element_type=F32)
        y = _layer_norm(ALPHA * h_ref[rows, :] + t, g_ref[...], b_ref[...])
        valid = _flat_valid_rows(i * bm + s * sub, sub, batch, lp)
        y = jnp.where(valid, y, 0.0)
        yb = y.astype(BF16)
        hf_ref[rows, :] = y
        hp_ref[rows, :] = _pack_bf16_pairs(yb)
        logits = jnp.dot(yb, rw_ref[...], preferred_element_type=F32) + rb_ref[...]
        ri, rw, tile_cnt = _route_tile(logits, valid, cnt_ref[...])
        ri_ref[rows, :] = ri
        rwt_ref[rows, :] = rw
        cnt_ref[...] += tile_cnt


def _outproj_ln(merged, w_out_bf16, h, g, b, route_w, route_b, layer, batch, lp):
    m, d = h.shape
    bm = OUT_BM
    assert m % bm == 0
    row_spec = pl.BlockSpec((bm, d), lambda i: (i, 0))
    vec_spec = pl.BlockSpec((1, d), lambda i: (0, 0))
    route_spec = pl.BlockSpec((bm, ROUTE_W), lambda i: (i, 0))
    return pl.pallas_call(
        functools.partial(_outproj_kernel, batch=batch, lp=lp),
        grid=(m // bm,),
        in_specs=[
            row_spec,
            pl.BlockSpec((None, d, d), lambda i: (layer, 0, 0)),
            row_spec, vec_spec, vec_spec,
            pl.BlockSpec((d, ROUTE_W), lambda i: (0, 0)),
            pl.BlockSpec((1, ROUTE_W), lambda i: (0, 0)),
        ],
        out_specs=[row_spec, pl.BlockSpec((bm, d // 2), lambda i: (i, 0)), route_spec, route_spec,
                   pl.BlockSpec((1, ROUTE_W), lambda i: (0, 0))],
        out_shape=[jax.ShapeDtypeStruct((m, d), F32), jax.ShapeDtypeStruct((m, d // 2), jnp.uint32),
                   jax.ShapeDtypeStruct((m, ROUTE_W), jnp.int32), jax.ShapeDtypeStruct((m, ROUTE_W), F32),
                   jax.ShapeDtypeStruct((1, ROUTE_W), F32)],
        compiler_params=_cparams("arbitrary"),
        name="out_proj_ln",
    )(merged, w_out_bf16, h, g.reshape(1, d), b.reshape(1, d), route_w, route_b)


def _dispatch_kernel(start_ref, nchunk_ref, dest_hbm, hp_ref, xs_ref, idx_ref, zero_ref, idx_sem, row_sem, zero_sem,
                     *, n_expert_rows):
    i = pl.program_id(0)
    groups = hp_ref.shape[0]
    bm = groups * SUBLANES
    n_idx = TOP_K * bm

    @pl.when(i == 0)
    def _():
        zero_ref[...] = jnp.zeros(zero_ref.shape, zero_ref.dtype)

        def last_chunk(e):
            first = pl.multiple_of(start_ref[e] + (nchunk_ref[e] - 1) * MOE_BM, MOE_BM)
            return pltpu.make_async_copy(zero_ref, xs_ref.at[pl.ds(first, MOE_BM)], zero_sem)

        for e in range(N_EXPERTS):
            @pl.when(nchunk_ref[e] > 0)
            def _():
                last_chunk(e).start()

        for e in range(N_EXPERTS):
            @pl.when(nchunk_ref[e] > 0)
            def _():
                last_chunk(e).wait()

        used = start_ref[N_EXPERTS - 1] + nchunk_ref[N_EXPERTS - 1] * MOE_BM

        def tail_chunk(j):
            return pltpu.make_async_copy(
                zero_ref, xs_ref.at[pl.ds(pl.multiple_of(used + j * MOE_BM, MOE_BM), MOE_BM)], zero_sem)

        n_tail = (n_expert_rows - used) // MOE_BM
        lax.fori_loop(0, n_tail, lambda j, c: (tail_chunk(j).start(), c)[1], 0)
        lax.fori_loop(0, n_tail, lambda j, c: (tail_chunk(j).wait(), c)[1], 0)

    idx_copy = pltpu.make_async_copy(dest_hbm.at[pl.ds(pl.multiple_of(i * n_idx, n_idx), n_idx)], idx_ref, idx_sem)
    idx_copy.start()
    idx_copy.wait()

    def issue(g, carry):
        for s in range(SUBLANES):
            for k in range(TOP_K):
                slot = idx_ref[TOP_K * (g * SUBLANES + s) + k]
                pltpu.make_async_copy(hp_ref.at[g, pl.ds(s, 1)], xs_ref.at[pl.ds(slot, 1)],
                                      row_sem).start(priority=k % 2)
        return carry

    lax.fori_loop(0, groups, issue, 0)
    for k in range(TOP_K):
        pltpu.make_async_copy(xs_ref.at[pl.ds(0, bm)], xs_ref.at[pl.ds(0, bm)], row_sem).wait()


def _dispatch(hp, dest, chunk_start, n_chunks, n_expert_rows, n_slots):
    m, c = hp.shape
    bm = DISPATCH_BM
    assert m % bm == 0 and bm % SUBLANES == 0
    hp = hp.reshape(m // SUBLANES, SUBLANES, c)
    grid_spec = pltpu.PrefetchScalarGridSpec(
        num_scalar_prefetch=2,
        grid=(m // bm,),
        in_specs=[
            pl.BlockSpec(memory_space=pl.ANY),
            pl.BlockSpec((bm // SUBLANES, SUBLANES, c), lambda i, st, nc: (i, 0, 0)),
        ],
        out_specs=pl.BlockSpec(memory_space=pl.ANY),
        scratch_shapes=[pltpu.SMEM((TOP_K * bm,), jnp.int32), pltpu.VMEM((MOE_BM, c), jnp.uint32),
                        pltpu.SemaphoreType.DMA(()), pltpu.SemaphoreType.DMA(()), pltpu.SemaphoreType.DMA(())],
    )
    return pl.pallas_call(
        functools.partial(_dispatch_kernel, n_expert_rows=n_expert_rows),
        grid_spec=grid_spec,
        out_shape=jax.ShapeDtypeStruct((n_slots, c), jnp.uint32),
        compiler_params=_cparams("arbitrary"),
        name="moe_dispatch",
    )(chunk_start, n_chunks, dest.reshape(m * TOP_K), hp)


def _expert_kernel(start_ref, nchunk_ref, xs_hbm, wg_ref, wu_ref, wd_ref, yb_hbm, wg_s, wu_s, wd_s, x_buf, o_buf,
                   in_sem, out_sem):
    e = pl.program_id(0)
    n = nchunk_ref[e]
    row0 = start_ref[e]
    bm = MOE_BM

    def rows(j):
        return pl.ds(pl.multiple_of(row0 + j * bm, bm), bm)

    def x_copy(j):
        slot = j % EXPERT_X_BUFFERS
        return pltpu.make_async_copy(xs_hbm.at[rows(j)], x_buf.at[slot], in_sem.at[slot])

    def o_copy(j, slot):
        return pltpu.make_async_copy(o_buf.at[slot], yb_hbm.at[rows(j)], out_sem.at[slot])

    @pl.when(n > 0)
    def _():
        for ahead in range(EXPERT_X_BUFFERS - 1):
            @pl.when(ahead < n)
            def _():
                x_copy(ahead).start(priority=1)

        wg_s[...] = wg_ref[...].astype(BF16)
        wu_s[...] = wu_ref[...].astype(BF16)
        wd_s[...] = wd_ref[...].astype(BF16)

        def chunk(j, carry):
            slot = j % 2
            x_copy(j).wait()

            @pl.when(j + EXPERT_X_BUFFERS - 1 < n)
            def _():
                x_copy(j + EXPERT_X_BUFFERS - 1).start(priority=1)

            @pl.when(j >= 2)
            def _():
                o_copy(j - 2, slot).wait()

            x = _unpack_bf16_pairs(x_buf[j % EXPERT_X_BUFFERS])
            gate = jnp.dot(x, wg_s[...], preferred_element_type=F32)
            up = jnp.dot(x, wu_s[...], preferred_element_type=F32)
            hdn = (gate * _sigmoid(gate)) * up
            y = jnp.dot(hdn.astype(BF16), wd_s[...], preferred_element_type=F32)
            o_buf[slot] = _pack_bf16_pairs(y.astype(BF16))
            o_copy(j, slot).start(priority=1)
            return carry

        lax.fori_loop(0, n, chunk, 0)

        @pl.when(n >= 2)
        def _():
            o_copy(n - 2, n % 2).wait()

        o_copy(n - 1, (n - 1) % 2).wait()

    @pl.when(e == N_EXPERTS - 1)
    def _():
        o_buf[0] = jnp.zeros(o_buf.shape[1:], o_buf.dtype)

        def fill(j, carry):
            o_copy(j, 0).start()
            o_copy(j, 0).wait()
            return carry

        lax.fori_loop(n, (yb_hbm.shape[0] - row0) // bm, fill, 0)


def _experts(xs, n_rows, chunk_start, n_chunks, w_gate, w_up, w_down, layer):
    d = D_MODEL
    bm = MOE_BM
    w_map = lambda e, st, nc: (layer, e, 0, 0)
    grid_spec = pltpu.PrefetchScalarGridSpec(
        num_scalar_prefetch=2,
        grid=(N_EXPERTS,),
        in_specs=[
            pl.BlockSpec(memory_space=pl.ANY),
            pl.BlockSpec((None, None, d, D_EXPERT), w_map),
            pl.BlockSpec((None, None, d, D_EXPERT), w_map),
            pl.BlockSpec((None, None, D_EXPERT, d), w_map),
        ],
        out_specs=pl.BlockSpec(memory_space=pl.ANY),
        scratch_shapes=[
            pltpu.VMEM((d, D_EXPERT), BF16),
            pltpu.VMEM((d, D_EXPERT), BF16),
            pltpu.VMEM((D_EXPERT, d), BF16),
            pltpu.VMEM((EXPERT_X_BUFFERS, bm, d // 2), jnp.uint32),
            pltpu.VMEM((2, bm, d // 2), jnp.uint32),
            pltpu.SemaphoreType.DMA((EXPERT_X_BUFFERS,)),
            pltpu.SemaphoreType.DMA((2,)),
        ],
    )
    return pl.pallas_call(
        _expert_kernel,
        grid_spec=grid_spec,
        out_shape=jax.ShapeDtypeStruct((n_rows, d // 2), jnp.uint32),
        compiler_params=_cparams("arbitrary"),
        name="expert_mlp",
    )(chunk_start, n_chunks, xs, w_gate, w_up, w_down)


def _combine_ln_kernel(dest_hbm, h_ref, rw_ref, g_ref, b_ref, yb_hbm, *rest, batch, lp, final):
    if final:
        out_hbm, idx_ref, rows_ref, y_buf, idx_sem, row_sem, out_sem = rest
    else:
        hf_ref, hb_ref, idx_ref, rows_ref, idx_sem, row_sem = rest
    i = pl.program_id(0)
    n_tiles = pl.num_programs(0)
    bm = h_ref.shape[0]
    n_idx = TOP_K * bm
    buf = i % 2

    def start_gather(tile, into):
        idx_copy = pltpu.make_async_copy(
            dest_hbm.at[pl.ds(pl.multiple_of(tile * n_idx, n_idx), n_idx)], idx_ref, idx_sem)
        idx_copy.start()
        idx_copy.wait()

        def issue(g, carry):
            for s in range(SUBLANES):
                for k in range(TOP_K):
                    slot = idx_ref[TOP_K * (g * SUBLANES + s) + k]
                    pltpu.make_async_copy(yb_hbm.at[pl.ds(slot, 1)], rows_ref.at[into, k, g, pl.ds(s, 1)],
                                          row_sem.at[into]).start(priority=k % 2)
            return carry

        lax.fori_loop(0, bm // SUBLANES, issue, 0)

    @pl.when(i == 0)
    def _():
        start_gather(0, 0)

    @pl.when(i + 1 < n_tiles)
    def _():
        start_gather(i + 1, 1 - buf)

    for k in range(TOP_K):
        pltpu.make_async_copy(yb_hbm.at[pl.ds(0, bm)], yb_hbm.at[pl.ds(0, bm)], row_sem.at[buf]).wait()

    d = h_ref.shape[1]

    def expert_rows(k):
        return _unpack_bf16_pairs(rows_ref[buf, k].reshape(bm, d // 2), F32)

    ffn = rw_ref[:, 0:1] * expert_rows(0)
    for k in range(1, TOP_K):
        ffn = ffn + rw_ref[:, k:k + 1] * expert_rows(k)
    y = _layer_norm(ALPHA * h_ref[...] + ffn, g_ref[...], b_ref[...])
    if not final:
        y = jnp.where(_flat_valid_rows(i * bm, bm, batch, lp), y, 0.0)
        hf_ref[...] = y
        hb_ref[...] = y.astype(BF16)
        return

    nb = lp // BLOCK

    def for_each_out_block(step, fn):
        for j in range(bm // BLOCK):
            blk = step * (bm // BLOCK) + j
            seq_blk = blk % nb

            @pl.when(seq_blk >= 1)
            def _():
                dst = pl.multiple_of(((blk // nb) * (nb - 1) + seq_blk - 1) * BLOCK, BLOCK)
                fn(pltpu.make_async_copy(y_buf.at[pl.ds(j * BLOCK, BLOCK)], out_hbm.at[pl.ds(dst, BLOCK)], out_sem))

    @pl.when(i > 0)
    def _():
        for_each_out_block(i - 1, lambda c: c.wait())

    y_buf[...] = y
    for_each_out_block(i, lambda c: c.start(priority=1))

    @pl.when(i == n_tiles - 1)
    def _():
        for_each_out_block(i, lambda c: c.wait())


def _combine_ln(h, yb, dest, rw, g, b, batch, lp, final):
    m, d = h.shape
    bm = COMBINE_BM
    assert m % bm == 0 and bm % BLOCK == 0
    row_spec = pl.BlockSpec((bm, d), lambda i: (i, 0))
    vec_spec = pl.BlockSpec((1, d), lambda i: (0, 0))
    scratch = [pltpu.SMEM((TOP_K * bm,), jnp.int32),
               pltpu.VMEM((2, TOP_K, bm // SUBLANES, SUBLANES, d // 2), jnp.uint32)]
    sems = [pltpu.SemaphoreType.DMA(()), pltpu.SemaphoreType.DMA((2,))]
    if final:
        out_specs = pl.BlockSpec(memory_space=pl.ANY)
        out_shape = jax.ShapeDtypeStruct((batch * (lp - BLOCK), d), F32)
        scratch = scratch + [pltpu.VMEM((bm, d), F32)] + sems + [pltpu.SemaphoreType.DMA(())]
    else:
        out_specs = [row_spec, row_spec]
        out_shape = [jax.ShapeDtypeStruct((m, d), F32), jax.ShapeDtypeStruct((m, d), BF16)]
        scratch = scratch + sems
    return pl.pallas_call(
        functools.partial(_combine_ln_kernel, batch=batch, lp=lp, final=final),
        grid=(m // bm,),
        in_specs=[
            pl.BlockSpec(memory_space=pl.ANY),
            row_spec,
            pl.BlockSpec((bm, ROUTE_W), lambda i: (i, 0)),
            vec_spec, vec_spec,
            pl.BlockSpec(memory_space=pl.ANY),
        ],
        out_specs=out_specs,
        out_shape=out_shape,
        scratch_shapes=scratch,
        compiler_params=_cparams("arbitrary"),
        name="moe_combine_out" if final else "moe_combine_ln",
    )(dest.reshape(m * TOP_K), h, rw, g.reshape(1, d), b.reshape(1, d), yb)


def _slot_tables(ri, cnt, batch, lp):
    m = ri.shape[0]
    bm = MOE_BM
    eid = ri[:, 0:TOP_K]
    rank = ri[:, TOP_K:2 * TOP_K]
    counts = cnt[0, :N_EXPERTS].astype(jnp.int32)
    padded = (counts + bm - 1) // bm * bm
    pad_end = jnp.cumsum(padded)
    pad_start = pad_end - padded
    n_real = batch * (lp - PAD) * TOP_K
    nblk = -(-(n_real + N_EXPERTS * (bm - 1)) // bm)
    cap = nblk * bm
    experts = jnp.arange(N_EXPERTS, dtype=jnp.int32)
    start = jnp.sum(jnp.where(eid[:, :, None] == experts[None, None, :], pad_start[None, None, :], 0), axis=-1)
    row = jnp.arange(m, dtype=jnp.int32)
    pos = row % lp
    spare = cap + ((row // lp) * PAD + pos)[:, None] * TOP_K + jnp.arange(TOP_K, dtype=jnp.int32)[None, :]
    valid = (pos >= PAD)[:, None]
    dest = jnp.where(valid, start + rank, spare)
    dest_read = jnp.where(valid, dest, 0)
    n_slots = cap + batch * PAD * TOP_K
    return dest, dest_read, pad_start.astype(jnp.int32), (padded // bm).astype(jnp.int32), cap, n_slots


def kernel(x, meta, ln_emb_g, ln_emb_b, w_in, pool_w, pool_scale, attn_sink, conv_w, conv_b, lru_wa, lru_ba,
           lru_wx, lru_bx, lru_lambda, proj_pool, proj_attn, proj_lru, w_out, ln1_g, ln1_b, router_grp_w,
           router_grp_b, router_exp_w, router_exp_b, exp_w_gate, exp_w_up, exp_w_down, ln2_g, ln2_b):
    batch, seq, d = x.shape
    lp = PAD + N_META + seq
    m = batch * lp

    hf, hb = _embed(x, meta, ln_emb_g, ln_emb_b)
    w_in_b = w_in.astype(BF16)
    pool_w_b = pool_w.astype(BF16)
    wa_b = lru_wa.astype(BF16)
    wx_b = lru_wx.astype(BF16)
    wp_b = proj_pool.astype(BF16)
    wat_b = proj_attn.astype(BF16)
    wl_b = proj_lru.astype(BF16)
    w_out_b = w_out.astype(BF16)
    route_pad = ROUTE_W - N_GROUPS - N_EXPERTS

    for l in range(DEPTH):
        cols = _inproj(hb, w_in_b, l, gates=False)
        gates = _inproj(hb, w_in_b, l, gates=True)
        pool_o = _pool(cols, pool_w_b[l], pool_scale[l], batch, lp)
        attn_o = _attention(cols, attn_sink[l], batch, lp)
        lru_o = _lru(cols, conv_w[l], conv_b[l], wa_b[l], lru_ba[l], wx_b[l], lru_bx[l], lru_lambda[l],
                     batch, lp)
        merged = _merge(pool_o, attn_o, lru_o, gates, wp_b, wat_b, wl_b, l)
        route_w = jnp.concatenate(
            [router_grp_w[l], router_exp_w[l], jnp.zeros((d, route_pad), F32)], axis=1).astype(BF16)
        route_b = jnp.concatenate(
            [router_grp_b[l], router_exp_b[l], jnp.zeros((route_pad,), F32)]).reshape(1, ROUTE_W)
        h1f, h1p, ri, rw, cnt = _outproj_ln(merged, w_out_b, hf, ln1_g[l], ln1_b[l], route_w, route_b, l,
                                            batch, lp)
        dest, dest_read, chunk_start, n_chunks, cap, n_slots = _slot_tables(ri, cnt, batch, lp)
        xs = _dispatch(h1p, dest, chunk_start, n_chunks, cap, n_slots)
        yb = _experts(xs, cap, chunk_start, n_chunks, exp_w_gate, exp_w_up, exp_w_down, l)
        if l + 1 < DEPTH:
            hf, hb = _combine_ln(h1f, yb, dest_read, rw, ln2_g[l], ln2_b[l], batch, lp, final=False)
        else:
            out = _combine_ln(h1f, yb, dest_read, rw, ln2_g[l], ln2_b[l], batch, lp, final=True)

    return out.reshape(batch, seq, d)
```

```python
import functools

import jax
import jax.numpy as jnp
from jax import lax
from jax.experimental import pallas as pl
from jax.experimental.pallas import tpu as pltpu

F32 = jnp.float32
BF16 = jnp.bfloat16

D_MODEL = 2048
DEPTH = 2
N_META = 16
POOL_WINDOWS = (2, 4, 8, 16)
POOL_WIDTH = D_MODEL // 2
POOL_GROUP = POOL_WIDTH // len(POOL_WINDOWS)
N_HEADS = 16
N_KV_HEADS = 4
HEAD_DIM = 64
Q_PER_KV = N_HEADS // N_KV_HEADS
WINDOW = 128
BLOCK = 128
NEG = -1e30
LRU_WIDTH = D_MODEL // 2
LRU_BLOCKS = 4
LRU_BLOCK = LRU_WIDTH // LRU_BLOCKS
CONV_WIDTH = 4
LRU_C = 8.0
N_GROUPS = 4
EXPERTS_PER_GROUP = 8
N_EXPERTS = N_GROUPS * EXPERTS_PER_GROUP
TOP_K = 2
D_EXPERT = D_MODEL // 4
LN_EPS = 1e-5
ALPHA = (2.0 * DEPTH) ** 0.25

PAD = BLOCK - N_META
ATT_W = N_HEADS * HEAD_DIM
KV_W = N_KV_HEADS * HEAD_DIM
OFF_POOL = 0
OFF_Q = OFF_POOL + POOL_WIDTH
OFF_K = OFF_Q + ATT_W
OFF_V = OFF_K + KV_W
OFF_LX = OFF_V + KV_W
OFF_LY = OFF_LX + LRU_WIDTH
OFF_GATE = OFF_LY + LRU_WIDTH
IN_COLS = OFF_GATE + 3 * D_MODEL

VMEM_LIMIT_BYTES = 56 * 1024 * 1024
SUBLANES = 8

SEQ_TILE = 3 * BLOCK
INPROJ_BM = 1536
INPROJ_BN = 1536
MERGE_BM = 768
MERGE_BN = 1024
OUT_BM = 512
OUT_SPLIT = 2
MOE_BM = 256
DISPATCH_BM = 1536
COMBINE_BM = 512
EXPERT_X_BUFFERS = 3
ROUTE_W = 128


def _cparams(*sem):
    return pltpu.CompilerParams(dimension_semantics=sem, vmem_limit_bytes=VMEM_LIMIT_BYTES)


def _layer_norm(x, g, b):
    mu = jnp.mean(x, axis=-1, keepdims=True)
    xc = x - mu
    var = jnp.mean(xc * xc, axis=-1, keepdims=True)
    return xc * lax.rsqrt(var + LN_EPS) * g + b


def _flat_valid_rows(row0, n_rows, batch, lp):
    r = row0 + lax.broadcasted_iota(jnp.int32, (n_rows, 1), 0)
    pad_row = jnp.zeros((n_rows, 1), jnp.bool_)
    for b in range(batch):
        pad_row = pad_row | ((r >= b * lp) & (r < b * lp + PAD))
    return jnp.logical_not(pad_row)


EMBED_BLOCKS = SEQ_TILE // BLOCK


def _embed_kernel(*refs):
    x_refs, (meta_ref, g_ref, b_ref, hf_ref, hb_ref) = refs[:EMBED_BLOCKS], refs[EMBED_BLOCKS:]
    t = pl.program_id(1)
    for j in range(EMBED_BLOCKS):
        src = x_refs[j][...]
        if j == 0:
            src = jnp.where(t == 0, meta_ref[...], src)
        y = _layer_norm(src, g_ref[...], b_ref[...])
        if j == 0:
            row = lax.broadcasted_iota(jnp.int32, (BLOCK, 1), 0)
            y = jnp.where((t > 0) | (row >= PAD), y, 0.0)
        hf_ref[pl.ds(j * BLOCK, BLOCK), :] = y
        hb_ref[pl.ds(j * BLOCK, BLOCK), :] = y.astype(BF16)


def _embed(x, meta, g, b):
    batch, seq, d = x.shape
    nblk = seq // BLOCK
    lp = (nblk + 1) * BLOCK
    nt = lp // SEQ_TILE
    m = batch * lp
    meta_tile = jnp.concatenate([jnp.zeros((PAD, d), F32), meta.astype(F32)], axis=0)
    row_spec = pl.BlockSpec((SEQ_TILE, d), lambda bi, t: (bi * nt + t, 0))
    vec_spec = pl.BlockSpec((1, d), lambda bi, t: (0, 0))
    x_spec = lambda j: pl.BlockSpec(
        (BLOCK, d), lambda bi, t: (bi * nblk + jnp.maximum(EMBED_BLOCKS * t + j - 1, 0), 0))
    x2 = x.reshape(batch * seq, d)
    return pl.pallas_call(
        _embed_kernel,
        grid=(batch, nt),
        in_specs=[x_spec(j) for j in range(EMBED_BLOCKS)]
        + [pl.BlockSpec((BLOCK, d), lambda bi, t: (0, 0)), vec_spec, vec_spec],
        out_specs=[row_spec, row_spec],
        out_shape=[jax.ShapeDtypeStruct((m, d), F32), jax.ShapeDtypeStruct((m, d), BF16)],
        compiler_params=_cparams("parallel", "arbitrary"),
        name="embed_ln",
    )(*([x2] * EMBED_BLOCKS), meta_tile, g.reshape(1, d), b.reshape(1, d))


def _sigmoid(x):
    return 0.5 * jnp.tanh(0.5 * x) + 0.5


def _inproj_kernel(x_ref, w_ref, o_ref, *, gates):
    acc = jnp.dot(x_ref[...], w_ref[...], preferred_element_type=F32)
    o_ref[...] = (_sigmoid(acc) if gates else acc).astype(o_ref.dtype)


def _inproj(hb, w_in_bf16, layer, gates):
    m, d = hb.shape
    bm, bn = INPROJ_BM, INPROJ_BN
    col0, width = (OFF_GATE, IN_COLS - OFF_GATE) if gates else (0, OFF_GATE)
    assert m % bm == 0 and width % bn == 0 and col0 % bn == 0
    return pl.pallas_call(
        functools.partial(_inproj_kernel, gates=gates),
        grid=(m // bm, width // bn),
        in_specs=[
            pl.BlockSpec((bm, d), lambda i, j: (i, 0)),
            pl.BlockSpec((None, d, bn), lambda i, j: (layer, 0, col0 // bn + j)),
        ],
        out_specs=pl.BlockSpec((bm, bn), lambda i, j: (i, j)),
        out_shape=jax.ShapeDtypeStruct((m, width), BF16),
        compiler_params=_cparams("parallel", "arbitrary"),
        name="in_proj_gates" if gates else "in_proj",
    )(hb, w_in_bf16)


def _pool_kernel(u_ref, w_ref, scale_ref, o_ref, ext_ref):
    t = pl.program_id(1)
    tile = SEQ_TILE
    maxw = max(POOL_WINDOWS)

    @pl.when(t == 0)
    def _():
        ext_ref[pl.ds(0, maxw), :] = jnp.zeros((maxw, POOL_WIDTH), F32)

    @pl.when(t > 0)
    def _():
        ext_ref[pl.ds(0, maxw), :] = ext_ref[pl.ds(tile, maxw), :]

    ext_ref[pl.ds(maxw, tile), :] = u_ref[...].astype(F32)

    pos = t * tile + lax.broadcasted_iota(jnp.int32, (tile, 1), 0) - PAD
    for gi, w in enumerate(POOL_WINDOWS):
        cols = pl.ds(gi * POOL_GROUP, POOL_GROUP)
        u = ext_ref[pl.ds(maxw, tile), cols]
        win = u
        for k in range(1, w):
            win = win + ext_ref[pl.ds(maxw - k, tile), cols]
        cnt = jnp.clip(pos + 1, 1, w).astype(F32)
        delta = win / cnt - u
        mixed = jnp.dot(delta.astype(BF16), w_ref[gi], preferred_element_type=F32)
        o_ref[:, cols] = (mixed * scale_ref[:, cols]).astype(o_ref.dtype)


def _pool(cols, pool_w_bf16, pool_scale, batch, lp):
    m = cols.shape[0]
    nt = lp // SEQ_TILE
    maxw = max(POOL_WINDOWS)
    return pl.pallas_call(
        _pool_kernel,
        grid=(batch, nt),
        in_specs=[
            pl.BlockSpec((SEQ_TILE, POOL_WIDTH), lambda b, t: (b * nt + t, OFF_POOL // POOL_WIDTH)),
            pl.BlockSpec((len(POOL_WINDOWS), POOL_GROUP, POOL_GROUP), lambda b, t: (0, 0, 0)),
            pl.BlockSpec((1, POOL_WIDTH), lambda b, t: (0, 0)),
        ],
        out_specs=pl.BlockSpec((SEQ_TILE, POOL_WIDTH), lambda b, t: (b * nt + t, 0)),
        out_shape=jax.ShapeDtypeStruct((m, POOL_WIDTH), BF16),
        scratch_shapes=[pltpu.VMEM((SEQ_TILE + maxw, POOL_WIDTH), F32)],
        compiler_params=_cparams("parallel", "arbitrary"),
        name="pool_mixer",
    )(cols, pool_w_bf16, pool_scale.reshape(1, POOL_WIDTH))


def _attn_bias():
    kj = jnp.arange(2 * BLOCK, dtype=jnp.int32)[:, None]
    qi = jnp.arange(BLOCK, dtype=jnp.int32)[None, :]
    dist = BLOCK + qi - kj
    in_window = (dist >= 0) & (dist < WINDOW)
    slopes = 2.0 ** (-8.0 * jnp.arange(1, N_HEADS + 1, dtype=F32) / N_HEADS)
    alibi = -slopes[:, None, None] * dist.astype(F32)[None]
    return jnp.where(in_window[None], alibi, NEG)


def _attn_kernel(q_ref, kp_ref, kc_ref, vp_ref, vc_ref, bias_ref, sink_ref, o_ref):
    n = pl.program_id(1)
    q = q_ref[...] * (HEAD_DIM ** -0.5)

    def heads(early_keys):
        for kh in range(N_KV_HEADS):
            hs = pl.ds(kh * HEAD_DIM, HEAD_DIM)
            k2 = jnp.concatenate([kp_ref[:, hs], kc_ref[:, hs]], axis=0)
            v2 = jnp.concatenate([vp_ref[:, hs], vc_ref[:, hs]], axis=0)
            for g in range(Q_PER_KV):
                h = kh * Q_PER_KV + g
                qh = q[:, h * HEAD_DIM:(h + 1) * HEAD_DIM]
                s = lax.dot_general(k2, qh, (((1,), (1,)), ((), ())), preferred_element_type=F32)
                s = s + bias_ref[h]
                if early_keys is not None:
                    s = s + early_keys
                sk = sink_ref[0, h]
                mx = jnp.maximum(jnp.max(s, axis=0, keepdims=True), sk)
                p = jnp.exp(s - mx)
                den = jnp.sum(p, axis=0, keepdims=True) + jnp.exp(sk - mx)
                pn = (p * (1.0 / den)).astype(BF16)
                o = lax.dot_general(pn, v2, (((0,), (0,)), ((), ())), preferred_element_type=F32)
                o_ref[:, pl.ds(h * HEAD_DIM, HEAD_DIM)] = o.astype(o_ref.dtype)

    @pl.when(n < 2)
    def _():
        k_pos = (n - 1) * BLOCK + lax.broadcasted_iota(jnp.int32, (2 * BLOCK, 1), 0)
        heads(jnp.where(k_pos < PAD, NEG, 0.0))

    @pl.when(n >= 2)
    def _():
        heads(None)


def _attention(cols, sink, batch, lp):
    m = cols.shape[0]
    nb = lp // BLOCK
    cur = lambda cb: (lambda b, n: (b * nb + n, cb))
    prev = lambda cb: (lambda b, n: (b * nb + jnp.maximum(n - 1, 0), cb))
    return pl.pallas_call(
        _attn_kernel,
        grid=(batch, nb),
        in_specs=[
            pl.BlockSpec((BLOCK, ATT_W), cur(OFF_Q // ATT_W)),
            pl.BlockSpec((BLOCK, KV_W), prev(OFF_K // KV_W)),
            pl.BlockSpec((BLOCK, KV_W), cur(OFF_K // KV_W)),
            pl.BlockSpec((BLOCK, KV_W), prev(OFF_V // KV_W)),
            pl.BlockSpec((BLOCK, KV_W), cur(OFF_V // KV_W)),
            pl.BlockSpec((N_HEADS, 2 * BLOCK, BLOCK), lambda b, n: (0, 0, 0)),
            pl.BlockSpec(memory_space=pltpu.SMEM),
        ],
        out_specs=pl.BlockSpec((BLOCK, ATT_W), lambda b, n: (b * nb + n, 0)),
        out_shape=jax.ShapeDtypeStruct((m, ATT_W), BF16),
        compiler_params=_cparams("parallel", "arbitrary"),
        name="swa_attention",
    )(cols, cols, cols, cols, cols, _attn_bias(), sink.reshape(1, N_HEADS).astype(F32))


LRU_HALF = LRU_WIDTH // 2
LRU_HALO = 8
LRU_SCAN_UNROLL = 6


def _gelu_tanh(x):
    return 0.5 * x * (1.0 + jnp.tanh(0.7978845608028654 * (x + 0.044715 * (x * x * x))))


def _lru_kernel(x_ref, y_ref, cw_ref, cb_ref, wa_ref, ba_ref, wx_ref, bx_ref, lam_ref, o_ref,
                ext_ref, a_ref, b_ref, carry_ref):
    t = pl.program_id(2)
    tile = SEQ_TILE
    width = LRU_HALF

    @pl.when(t == 0)
    def _():
        ext_ref[pl.ds(0, LRU_HALO), :] = jnp.zeros((LRU_HALO, width), F32)
        carry_ref[...] = jnp.zeros((1, width), F32)

    @pl.when(t > 0)
    def _():
        ext_ref[pl.ds(0, LRU_HALO), :] = ext_ref[pl.ds(tile, LRU_HALO), :]

    ext_ref[pl.ds(LRU_HALO, tile), :] = x_ref[...].astype(F32)

    xc = cb_ref[...] + cw_ref[pl.ds(CONV_WIDTH - 1, 1), :] * ext_ref[pl.ds(LRU_HALO, tile), :]
    for j in range(CONV_WIDTH - 1):
        shift = CONV_WIDTH - 1 - j
        xc = xc + cw_ref[pl.ds(j, 1), :] * ext_ref[pl.ds(LRU_HALO - shift, tile), :]

    xcb = xc.astype(BF16)
    ga_parts, gx_parts = [], []
    for blk in range(width // LRU_BLOCK):
        xb = xcb[:, blk * LRU_BLOCK:(blk + 1) * LRU_BLOCK]
        ga_parts.append(jnp.dot(xb, wa_ref[blk], preferred_element_type=F32))
        gx_parts.append(jnp.dot(xb, wx_ref[blk], preferred_element_type=F32))
    gate_a = _sigmoid(jnp.concatenate(ga_parts, axis=1) + ba_ref[...])
    gate_x = _sigmoid(jnp.concatenate(gx_parts, axis=1) + bx_ref[...])

    neg_lam = -lam_ref[...]
    softplus = jnp.maximum(neg_lam, 0.0) + jnp.log1p(jnp.exp(-jnp.abs(neg_lam)))
    log_a = (-LRU_C) * gate_a * softplus
    a = jnp.exp(log_a)
    b_in = jnp.sqrt(1.0 - a * a) * gate_x * xc
    pos = t * tile + lax.broadcasted_iota(jnp.int32, (tile, 1), 0)
    b_in = jnp.where(pos >= PAD, b_in, 0.0)
    a_ref[...] = a
    b_ref[...] = b_in

    row = lax.broadcasted_iota(jnp.int32, (8, width), 0)

    def group(r, carry):
        rows = pl.ds(pl.multiple_of(r * 8, 8), 8)
        av = a_ref[rows, :]
        bv = b_ref[rows, :]
        for k in (1, 2, 4):
            a_sh = jnp.where(row >= k, pltpu.roll(av, k, 0), 1.0)
            b_sh = jnp.where(row >= k, pltpu.roll(bv, k, 0), 0.0)
            bv = av * b_sh + bv
            av = av * a_sh
        hv = av * carry + bv
        b_ref[rows, :] = hv
        return hv[7:8, :]

    carry_ref[...] = lax.fori_loop(0, tile // 8, group, carry_ref[...], unroll=LRU_SCAN_UNROLL)
    o_ref[...] = (b_ref[...] * _gelu_tanh(y_ref[...].astype(F32))).astype(o_ref.dtype)


def _lru(cols, conv_w, conv_b, wa_bf16, ba, wx_bf16, bx, lam, batch, lp):
    m = cols.shape[0]
    nt = lp // SEQ_TILE
    nhalf = LRU_WIDTH // LRU_HALF
    blocks_per_half = LRU_HALF // LRU_BLOCK
    vec = lambda v: v.reshape(1, LRU_WIDTH).astype(F32)
    vec_spec = pl.BlockSpec((1, LRU_HALF), lambda b, c, t: (0, c))
    w_spec = pl.BlockSpec((blocks_per_half, LRU_BLOCK, LRU_BLOCK), lambda b, c, t: (c, 0, 0))
    return pl.pallas_call(
        _lru_kernel,
        grid=(batch, nhalf, nt),
        in_specs=[
            pl.BlockSpec((SEQ_TILE, LRU_HALF), lambda b, c, t: (b * nt + t, OFF_LX // LRU_HALF + c)),
            pl.BlockSpec((SEQ_TILE, LRU_HALF), lambda b, c, t: (b * nt + t, OFF_LY // LRU_HALF + c)),
            pl.BlockSpec((CONV_WIDTH, LRU_HALF), lambda b, c, t: (0, c)),
            vec_spec, w_spec, vec_spec, w_spec, vec_spec, vec_spec,
        ],
        out_specs=pl.BlockSpec((SEQ_TILE, LRU_HALF), lambda b, c, t: (b * nt + t, c)),
        out_shape=jax.ShapeDtypeStruct((m, LRU_WIDTH), BF16),
        scratch_shapes=[
            pltpu.VMEM((SEQ_TILE + LRU_HALO, LRU_HALF), F32),
            pltpu.VMEM((SEQ_TILE, LRU_HALF), F32),
            pltpu.VMEM((SEQ_TILE, LRU_HALF), F32),
            pltpu.VMEM((1, LRU_HALF), F32),
        ],
        compiler_params=_cparams("parallel", "parallel", "arbitrary"),
        name="rglru",
    )(cols, cols, conv_w.astype(F32), vec(conv_b), wa_bf16, vec(ba), wx_bf16, vec(bx), vec(lam))


def _merge_kernel(p_ref, a_ref, r_ref, gp_ref, ga_ref, gr_ref, wp_ref, wa_ref, wr_ref, o_ref):
    acc = gp_ref[...].astype(F32) * jnp.dot(p_ref[...], wp_ref[...], preferred_element_type=F32)
    acc += ga_ref[...].astype(F32) * jnp.dot(a_ref[...], wa_ref[...], preferred_element_type=F32)
    acc += gr_ref[...].astype(F32) * jnp.dot(r_ref[...], wr_ref[...], preferred_element_type=F32)
    o_ref[...] = acc.astype(o_ref.dtype)


def _merge(pool_o, attn_o, lru_o, gates, wp, wa, wr, layer):
    m = pool_o.shape[0]
    bm, bn = MERGE_BM, MERGE_BN
    assert m % bm == 0 and D_MODEL % bn == 0
    x_spec = pl.BlockSpec((bm, POOL_WIDTH), lambda i, j: (i, 0))
    gate_spec = lambda k: pl.BlockSpec((bm, bn), lambda i, j: (i, k * D_MODEL // bn + j))
    w_spec = pl.BlockSpec((None, POOL_WIDTH, bn), lambda i, j: (layer, 0, j))
    return pl.pallas_call(
        _merge_kernel,
        grid=(m // bm, D_MODEL // bn),
        in_specs=[x_spec, x_spec, x_spec, gate_spec(0), gate_spec(1), gate_spec(2), w_spec, w_spec, w_spec],
        out_specs=pl.BlockSpec((bm, bn), lambda i, j: (i, j)),
        out_shape=jax.ShapeDtypeStruct((m, D_MODEL), BF16),
        compiler_params=_cparams("parallel", "arbitrary"),
        name="gated_merge",
    )(pool_o, attn_o, lru_o, gates, gates, gates, wp, wa, wr)


def _pack_bf16_pairs(yb):
    c = yb.shape[1] // 2
    lo = lax.bitcast_convert_type(yb[:, :c].astype(F32), jnp.uint32)
    hi = lax.bitcast_convert_type(yb[:, c:].astype(F32), jnp.uint32)
    return (hi & jnp.uint32(0xFFFF0000)) | (lo >> 16)


def _unpack_bf16_pairs(words, dtype=BF16):
    lo = lax.bitcast_convert_type(words << 16, F32)
    hi = lax.bitcast_convert_type(words & jnp.uint32(0xFFFF0000), F32)
    return jnp.concatenate([lo, hi], axis=1).astype(dtype)


def _route_tile(logits, valid, base):
    bm = logits.shape[0]
    lane = lax.broadcasted_iota(jnp.int32, (bm, ROUTE_W), 1)
    lane_f = lane.astype(F32)
    ninf = -jnp.inf
    big = float(ROUTE_W)

    gl = jnp.where(lane < N_GROUPS, logits, ninf)
    gmax = jnp.max(gl, axis=-1, keepdims=True)
    g = jnp.min(jnp.where(gl == gmax, lane_f, big), axis=-1, keepdims=True)
    p_g = 1.0 / jnp.sum(jnp.exp(gl - gmax), axis=-1, keepdims=True)

    first = N_GROUPS + g * EXPERTS_PER_GROUP
    sl = jnp.where((lane_f >= first) & (lane_f < first + EXPERTS_PER_GROUP), logits, ninf)
    m1 = jnp.max(sl, axis=-1, keepdims=True)
    i1 = jnp.min(jnp.where(sl == m1, lane_f, big), axis=-1, keepdims=True)
    ssum = jnp.sum(jnp.exp(sl - m1), axis=-1, keepdims=True)
    sl2 = jnp.where(lane_f == i1, ninf, sl)
    m2 = jnp.max(sl2, axis=-1, keepdims=True)
    i2 = jnp.min(jnp.where(sl2 == m2, lane_f, big), axis=-1, keepdims=True)
    p1 = 1.0 / ssum
    p2 = jnp.exp(m2 - m1) / ssum
    w1 = p_g * p1 / (p1 + p2)
    w2 = p_g * p2 / (p1 + p2)
    e1 = i1 - N_GROUPS
    e2 = i2 - N_GROUPS

    oh1 = (lane_f == e1) & valid
    oh2 = (lane_f == e2) & valid
    both = (oh1 | oh2).astype(F32)
    earlier = (lax.broadcasted_iota(jnp.int32, (bm, bm), 0) > lax.broadcasted_iota(jnp.int32, (bm, bm), 1))
    prefix = jnp.dot(earlier.astype(BF16), both.astype(BF16), preferred_element_type=F32) + base
    r1 = jnp.sum(jnp.where(oh1, prefix, 0.0), axis=-1, keepdims=True)
    r2 = jnp.sum(jnp.where(oh2, prefix, 0.0), axis=-1, keepdims=True)

    ri = jnp.where(lane == 0, e1, jnp.where(lane == 1, e2, jnp.where(lane == 2, r1, jnp.where(lane == 3, r2, 0.0))))
    rw = jnp.where(lane == 0, w1, jnp.where(lane == 1, w2, 0.0))
    return ri.astype(jnp.int32), rw, jnp.sum(both, axis=0, keepdims=True)


def _outproj_kernel(x_ref, w_ref, h_ref, g_ref, b_ref, rw_ref, rb_ref, hf_ref, hp_ref, ri_ref, rwt_ref, cnt_ref,
                    *, batch, lp):
    i = pl.program_id(0)
    bm = x_ref.shape[0]
    sub = bm // OUT_SPLIT

    @pl.when(i == 0)
    def _():
        cnt_ref[...] = jnp.zeros(cnt_ref.shape, F32)

    for s in range(OUT_SPLIT):
        rows = pl.ds(s * sub, sub)
        t = jnp.dot(x_ref[rows, :], w_ref[...], preferred_element_type=F32)
        y = _layer_norm(ALPHA * h_ref[rows, :] + t, g_ref[...], b_ref[...])
        valid = _flat_valid_rows(i * bm + s * sub, sub, batch, lp)
        y = jnp.where(valid, y, 0.0)
        yb = y.astype(BF16)
        hf_ref[rows, :] = y
        hp_ref[rows, :] = _pack_bf16_pairs(yb)
        logits = jnp.dot(yb, rw_ref[...], preferred_element_type=F32) + rb_ref[...]
        ri, rw, tile_cnt = _route_tile(logits, valid, cnt_ref[...])
        ri_ref[rows, :] = ri
        rwt_ref[rows, :] = rw
        cnt_ref[...] += tile_cnt


def _outproj_ln(merged, w_out_bf16, h, g, b, route_w, route_b, layer, batch, lp):
    m, d = h.shape
    bm = OUT_BM
    assert m % bm == 0
    row_spec = pl.BlockSpec((bm, d), lambda i: (i, 0))
    vec_spec = pl.BlockSpec((1, d), lambda i: (0, 0))
    route_spec = pl.BlockSpec((bm, ROUTE_W), lambda i: (i, 0))
    return pl.pallas_call(
        functools.partial(_outproj_kernel, batch=batch, lp=lp),
        grid=(m // bm,),
        in_specs=[
            row_spec,
            pl.BlockSpec((None, d, d), lambda i: (layer, 0, 0)),
            row_spec, vec_spec, vec_spec,
            pl.BlockSpec((d, ROUTE_W), lambda i: (0, 0)),
            pl.BlockSpec((1, ROUTE_W), lambda i: (0, 0)),
        ],
        out_specs=[row_spec, pl.BlockSpec((bm, d // 2), lambda i: (i, 0)), route_spec, route_spec,
                   pl.BlockSpec((1, ROUTE_W), lambda i: (0, 0))],
        out_shape=[jax.ShapeDtypeStruct((m, d), F32), jax.ShapeDtypeStruct((m, d // 2), jnp.uint32),
                   jax.ShapeDtypeStruct((m, ROUTE_W), jnp.int32), jax.ShapeDtypeStruct((m, ROUTE_W), F32),
                   jax.ShapeDtypeStruct((1, ROUTE_W), F32)],
        compiler_params=_cparams("arbitrary"),
        name="out_proj_ln",
    )(merged, w_out_bf16, h, g.reshape(1, d), b.reshape(1, d), route_w, route_b)


def _dispatch_kernel(start_ref, nchunk_ref, dest_hbm, hp_ref, xs_ref, idx_ref, zero_ref, idx_sem, row_sem, zero_sem,
                     *, n_expert_rows):
    i = pl.program_id(0)
    groups = hp_ref.shape[0]
    bm = groups * SUBLANES
    n_idx = TOP_K * bm

    @pl.when(i == 0)
    def _():
        zero_ref[...] = jnp.zeros(zero_ref.shape, zero_ref.dtype)

        def last_chunk(e):
            first = pl.multiple_of(start_ref[e] + (nchunk_ref[e] - 1) * MOE_BM, MOE_BM)
            return pltpu.make_async_copy(zero_ref, xs_ref.at[pl.ds(first, MOE_BM)], zero_sem)

        for e in range(N_EXPERTS):
            @pl.when(nchunk_ref[e] > 0)
            def _():
                last_chunk(e).start()

        for e in range(N_EXPERTS):
            @pl.when(nchunk_ref[e] > 0)
            def _():
                last_chunk(e).wait()

        used = start_ref[N_EXPERTS - 1] + nchunk_ref[N_EXPERTS - 1] * MOE_BM

        def tail_chunk(j):
            return pltpu.make_async_copy(
                zero_ref, xs_ref.at[pl.ds(pl.multiple_of(used + j * MOE_BM, MOE_BM), MOE_BM)], zero_sem)

        n_tail = (n_expert_rows - used) // MOE_BM
        lax.fori_loop(0, n_tail, lambda j, c: (tail_chunk(j).start(), c)[1], 0)
        lax.fori_loop(0, n_tail, lambda j, c: (tail_chunk(j).wait(), c)[1], 0)

    idx_copy = pltpu.make_async_copy(dest_hbm.at[pl.ds(pl.multiple_of(i * n_idx, n_idx), n_idx)], idx_ref, idx_sem)
    idx_copy.start()
    idx_copy.wait()

    def issue(g, carry):
        for s in range(SUBLANES):
            for k in range(TOP_K):
                slot = idx_ref[TOP_K * (g * SUBLANES + s) + k]
                pltpu.make_async_copy(hp_ref.at[g, pl.ds(s, 1)], xs_ref.at[pl.ds(slot, 1)],
                                      row_sem).start(priority=k % 2)
        return carry

    lax.fori_loop(0, groups, issue, 0)
    for k in range(TOP_K):
        pltpu.make_async_copy(xs_ref.at[pl.ds(0, bm)], xs_ref.at[pl.ds(0, bm)], row_sem).wait()


def _dispatch(hp, dest, chunk_start, n_chunks, n_expert_rows, n_slots):
    m, c = hp.shape
    bm = DISPATCH_BM
    assert m % bm == 0 and bm % SUBLANES == 0
    hp = hp.reshape(m // SUBLANES, SUBLANES, c)
    grid_spec = pltpu.PrefetchScalarGridSpec(
        num_scalar_prefetch=2,
        grid=(m // bm,),
        in_specs=[
            pl.BlockSpec(memory_space=pl.ANY),
            pl.BlockSpec((bm // SUBLANES, SUBLANES, c), lambda i, st, nc: (i, 0, 0)),
        ],
        out_specs=pl.BlockSpec(memory_space=pl.ANY),
        scratch_shapes=[pltpu.SMEM((TOP_K * bm,), jnp.int32), pltpu.VMEM((MOE_BM, c), jnp.uint32),
                        pltpu.SemaphoreType.DMA(()), pltpu.SemaphoreType.DMA(()), pltpu.SemaphoreType.DMA(())],
    )
    return pl.pallas_call(
        functools.partial(_dispatch_kernel, n_expert_rows=n_expert_rows),
        grid_spec=grid_spec,
        out_shape=jax.ShapeDtypeStruct((n_slots, c), jnp.uint32),
        compiler_params=_cparams("arbitrary"),
        name="moe_dispatch",
    )(chunk_start, n_chunks, dest.reshape(m * TOP_K), hp)


def _expert_kernel(start_ref, nchunk_ref, xs_hbm, wg_hbm, wu_hbm, wd_hbm, yb_hbm, wg_f, wu_f, wd_f, wg_s, wu_s, wd_s,
                   x_buf, o_buf, w_sem, in_sem, out_sem, *, layer):
    e = pl.program_id(0)
    n = nchunk_ref[e]
    row0 = start_ref[e]
    bm = MOE_BM
    wslot = e % 2

    def w_copies(expert, slot):
        return [pltpu.make_async_copy(src.at[layer, expert], dst.at[slot], w_sem.at[slot])
                for src, dst in ((wg_hbm, wg_f), (wu_hbm, wu_f), (wd_hbm, wd_f))]

    def rows(j):
        return pl.ds(pl.multiple_of(row0 + j * bm, bm), bm)

    def x_copy(j):
        slot = j % EXPERT_X_BUFFERS
        return pltpu.make_async_copy(xs_hbm.at[rows(j)], x_buf.at[slot], in_sem.at[slot])

    def o_copy(j, slot):
        return pltpu.make_async_copy(o_buf.at[slot], yb_hbm.at[rows(j)], out_sem.at[slot])

    for ahead in range(EXPERT_X_BUFFERS - 1):
        @pl.when(ahead < n)
        def _():
            x_copy(ahead).start()

    @pl.when(e == 0)
    def _():
        for c in w_copies(0, 0):
            c.start(priority=1)

    @pl.when(e + 1 < N_EXPERTS)
    def _():
        for c in w_copies(e + 1, 1 - wslot):
            c.start(priority=1)

    for c in w_copies(e, wslot):
        c.wait()

    @pl.when(n > 0)
    def _():
        wg_s[...] = wg_f[wslot].astype(BF16)
        wu_s[...] = wu_f[wslot].astype(BF16)
        wd_s[...] = wd_f[wslot].astype(BF16)

        def chunk(j, carry):
            slot = j % 2
            x_copy(j).wait()

            @pl.when(j + EXPERT_X_BUFFERS - 1 < n)
            def _():
                x_copy(j + EXPERT_X_BUFFERS - 1).start()

            @pl.when(j >= 2)
            def _():
                o_copy(j - 2, slot).wait()

            x = _unpack_bf16_pairs(x_buf[j % EXPERT_X_BUFFERS])
            gate = jnp.dot(x, wg_s[...], preferred_element_type=F32)
            up = jnp.dot(x, wu_s[...], preferred_element_type=F32)
            hdn = (gate * _sigmoid(gate)) * up
            y = jnp.dot(hdn.astype(BF16), wd_s[...], preferred_element_type=F32)
            o_buf[slot] = _pack_bf16_pairs(y.astype(BF16))
            o_copy(j, slot).start()
            return carry

        lax.fori_loop(0, n, chunk, 0)

        @pl.when(n >= 2)
        def _():
            o_copy(n - 2, n % 2).wait()

        o_copy(n - 1, (n - 1) % 2).wait()

    @pl.when(e == N_EXPERTS - 1)
    def _():
        o_buf[0] = jnp.zeros(o_buf.shape[1:], o_buf.dtype)

        def fill(j, carry):
            o_copy(j, 0).start()
            o_copy(j, 0).wait()
            return carry

        lax.fori_loop(n, (yb_hbm.shape[0] - row0) // bm, fill, 0)


def _experts(xs, n_rows, chunk_start, n_chunks, w_gate, w_up, w_down, layer):
    d = D_MODEL
    bm = MOE_BM
    any_spec = pl.BlockSpec(memory_space=pl.ANY)
    grid_spec = pltpu.PrefetchScalarGridSpec(
        num_scalar_prefetch=2,
        grid=(N_EXPERTS,),
        in_specs=[any_spec, any_spec, any_spec, any_spec],
        out_specs=any_spec,
        scratch_shapes=[
            pltpu.VMEM((2, d, D_EXPERT), F32),
            pltpu.VMEM((2, d, D_EXPERT), F32),
            pltpu.VMEM((2, D_EXPERT, d), F32),
            pltpu.VMEM((d, D_EXPERT), BF16),
            pltpu.VMEM((d, D_EXPERT), BF16),
            pltpu.VMEM((D_EXPERT, d), BF16),
            pltpu.VMEM((EXPERT_X_BUFFERS, bm, d // 2), jnp.uint32),
            pltpu.VMEM((2, bm, d // 2), jnp.uint32),
            pltpu.SemaphoreType.DMA((2,)),
            pltpu.SemaphoreType.DMA((EXPERT_X_BUFFERS,)),
            pltpu.SemaphoreType.DMA((2,)),
        ],
    )
    return pl.pallas_call(
        functools.partial(_expert_kernel, layer=layer),
        grid_spec=grid_spec,
        out_shape=jax.ShapeDtypeStruct((n_rows, d // 2), jnp.uint32),
        compiler_params=_cparams("arbitrary"),
        name="expert_mlp",
    )(chunk_start, n_chunks, xs, w_gate, w_up, w_down)


def _combine_ln_kernel(dest_hbm, h_ref, rw_ref, g_ref, b_ref, yb_hbm, *rest, batch, lp, final):
    if final:
        out_hbm, idx_ref, rows_ref, y_buf, idx_sem, row_sem, out_sem = rest
    else:
        hf_ref, hb_ref, idx_ref, rows_ref, idx_sem, row_sem = rest
    i = pl.program_id(0)
    n_tiles = pl.num_programs(0)
    bm = h_ref.shape[0]
    n_idx = TOP_K * bm
    buf = i % 2

    def start_gather(tile, into):
        idx_copy = pltpu.make_async_copy(
            dest_hbm.at[pl.ds(pl.multiple_of(tile * n_idx, n_idx), n_idx)], idx_ref, idx_sem)
        idx_copy.start()
        idx_copy.wait()

        def issue(g, carry):
            for s in range(SUBLANES):
                for k in range(TOP_K):
                    slot = idx_ref[TOP_K * (g * SUBLANES + s) + k]
                    pltpu.make_async_copy(yb_hbm.at[pl.ds(slot, 1)], rows_ref.at[into, k, g, pl.ds(s, 1)],
                                          row_sem.at[into]).start(priority=k % 2)
            return carry

        lax.fori_loop(0, bm // SUBLANES, issue, 0)

    @pl.when(i == 0)
    def _():
        start_gather(0, 0)

    @pl.when(i + 1 < n_tiles)
    def _():
        start_gather(i + 1, 1 - buf)

    for k in range(TOP_K):
        pltpu.make_async_copy(yb_hbm.at[pl.ds(0, bm)], yb_hbm.at[pl.ds(0, bm)], row_sem.at[buf]).wait()

    d = h_ref.shape[1]

    def expert_rows(k):
        return _unpack_bf16_pairs(rows_ref[buf, k].reshape(bm, d // 2), F32)

    ffn = rw_ref[:, 0:1] * expert_rows(0)
    for k in range(1, TOP_K):
        ffn = ffn + rw_ref[:, k:k + 1] * expert_rows(k)
    y = _layer_norm(ALPHA * h_ref[...] + ffn, g_ref[...], b_ref[...])
    if not final:
        y = jnp.where(_flat_valid_rows(i * bm, bm, batch, lp), y, 0.0)
        hf_ref[...] = y
        hb_ref[...] = y.astype(BF16)
        return

    nb = lp // BLOCK

    def for_each_out_block(step, fn):
        for j in range(bm // BLOCK):
            blk = step * (bm // BLOCK) + j
            seq_blk = blk % nb

            @pl.when(seq_blk >= 1)
            def _():
                dst = pl.multiple_of(((blk // nb) * (nb - 1) + seq_blk - 1) * BLOCK, BLOCK)
                fn(pltpu.make_async_copy(y_buf.at[pl.ds(j * BLOCK, BLOCK)], out_hbm.at[pl.ds(dst, BLOCK)], out_sem))

    @pl.when(i > 0)
    def _():
        for_each_out_block(i - 1, lambda c: c.wait())

    y_buf[...] = y
    for_each_out_block(i, lambda c: c.start(priority=1))

    @pl.when(i == n_tiles - 1)
    def _():
        for_each_out_block(i, lambda c: c.wait())


def _combine_ln(h, yb, dest, rw, g, b, batch, lp, final):
    m, d = h.shape
    bm = COMBINE_BM
    assert m % bm == 0 and bm % BLOCK == 0
    row_spec = pl.BlockSpec((bm, d), lambda i: (i, 0))
    vec_spec = pl.BlockSpec((1, d), lambda i: (0, 0))
    scratch = [pltpu.SMEM((TOP_K * bm,), jnp.int32),
               pltpu.VMEM((2, TOP_K, bm // SUBLANES, SUBLANES, d // 2), jnp.uint32)]
    sems = [pltpu.SemaphoreType.DMA(()), pltpu.SemaphoreType.DMA((2,))]
    if final:
        out_specs = pl.BlockSpec(memory_space=pl.ANY)
        out_shape = jax.ShapeDtypeStruct((batch * (lp - BLOCK), d), F32)
        scratch = scratch + [pltpu.VMEM((bm, d), F32)] + sems + [pltpu.SemaphoreType.DMA(())]
    else:
        out_specs = [row_spec, row_spec]
        out_shape = [jax.ShapeDtypeStruct((m, d), F32), jax.ShapeDtypeStruct((m, d), BF16)]
        scratch = scratch + sems
    return pl.pallas_call(
        functools.partial(_combine_ln_kernel, batch=batch, lp=lp, final=final),
        grid=(m // bm,),
        in_specs=[
            pl.BlockSpec(memory_space=pl.ANY),
            row_spec,
            pl.BlockSpec((bm, ROUTE_W), lambda i: (i, 0)),
            vec_spec, vec_spec,
            pl.BlockSpec(memory_space=pl.ANY),
        ],
        out_specs=out_specs,
        out_shape=out_shape,
        scratch_shapes=scratch,
        compiler_params=_cparams("arbitrary"),
        name="moe_combine_out" if final else "moe_combine_ln",
    )(dest.reshape(m * TOP_K), h, rw, g.reshape(1, d), b.reshape(1, d), yb)


def _slot_tables(ri, cnt, batch, lp):
    m = ri.shape[0]
    bm = MOE_BM
    eid = ri[:, 0:TOP_K]
    rank = ri[:, TOP_K:2 * TOP_K]
    counts = cnt[0, :N_EXPERTS].astype(jnp.int32)
    padded = (counts + bm - 1) // bm * bm
    pad_end = jnp.cumsum(padded)
    pad_start = pad_end - padded
    n_real = batch * (lp - PAD) * TOP_K
    nblk = -(-(n_real + N_EXPERTS * (bm - 1)) // bm)
    cap = nblk * bm
    experts = jnp.arange(N_EXPERTS, dtype=jnp.int32)
    start = jnp.sum(jnp.where(eid[:, :, None] == experts[None, None, :], pad_start[None, None, :], 0), axis=-1)
    row = jnp.arange(m, dtype=jnp.int32)
    pos = row % lp
    spare = cap + ((row // lp) * PAD + pos)[:, None] * TOP_K + jnp.arange(TOP_K, dtype=jnp.int32)[None, :]
    valid = (pos >= PAD)[:, None]
    dest = jnp.where(valid, start + rank, spare)
    dest_read = jnp.where(valid, dest, 0)
    n_slots = cap + batch * PAD * TOP_K
    return dest, dest_read, pad_start.astype(jnp.int32), (padded // bm).astype(jnp.int32), cap, n_slots


def kernel(x, meta, ln_emb_g, ln_emb_b, w_in, pool_w, pool_scale, attn_sink, conv_w, conv_b, lru_wa, lru_ba,
           lru_wx, lru_bx, lru_lambda, proj_pool, proj_attn, proj_lru, w_out, ln1_g, ln1_b, router_grp_w,
           router_grp_b, router_exp_w, router_exp_b, exp_w_gate, exp_w_up, exp_w_down, ln2_g, ln2_b):
    batch, seq, d = x.shape
    lp = PAD + N_META + seq
    m = batch * lp

    hf, hb = _embed(x, meta, ln_emb_g, ln_emb_b)
    w_in_b = w_in.astype(BF16)
    pool_w_b = pool_w.astype(BF16)
    wa_b = lru_wa.astype(BF16)
    wx_b = lru_wx.astype(BF16)
    wp_b = proj_pool.astype(BF16)
    wat_b = proj_attn.astype(BF16)
    wl_b = proj_lru.astype(BF16)
    w_out_b = w_out.astype(BF16)
    route_pad = ROUTE_W - N_GROUPS - N_EXPERTS

    for l in range(DEPTH):
        cols = _inproj(hb, w_in_b, l, gates=False)
        gates = _inproj(hb, w_in_b, l, gates=True)
        pool_o = _pool(cols, pool_w_b[l], pool_scale[l], batch, lp)
        attn_o = _attention(cols, attn_sink[l], batch, lp)
        lru_o = _lru(cols, conv_w[l], conv_b[l], wa_b[l], lru_ba[l], wx_b[l], lru_bx[l], lru_lambda[l],
                     batch, lp)
        merged = _merge(pool_o, attn_o, lru_o, gates, wp_b, wat_b, wl_b, l)
        route_w = jnp.concatenate(
            [router_grp_w[l], router_exp_w[l], jnp.zeros((d, route_pad), F32)], axis=1).astype(BF16)
        route_b = jnp.concatenate(
            [router_grp_b[l], router_exp_b[l], jnp.zeros((route_pad,), F32)]).reshape(1, ROUTE_W)
        h1f, h1p, ri, rw, cnt = _outproj_ln(merged, w_out_b, hf, ln1_g[l], ln1_b[l], route_w, route_b, l,
                                            batch, lp)
        dest, dest_read, chunk_start, n_chunks, cap, n_slots = _slot_tables(ri, cnt, batch, lp)
        xs = _dispatch(h1p, dest, chunk_start, n_chunks, cap, n_slots)
        yb = _experts(xs, cap, chunk_start, n_chunks, exp_w_gate, exp_w_up, exp_w_down, l)
        if l + 1 < DEPTH:
            hf, hb = _combine_ln(h1f, yb, dest_read, rw, ln2_g[l], ln2_b[l], batch, lp, final=False)
        else:
            out = _combine_ln(h1f, yb, dest_read, rw, ln2_g[l], ln2_b[l], batch, lp, final=True)

    return out.reshape(batch, seq, d)
```

```python
import functools

import jax
import jax.numpy as jnp
from jax import lax
from jax.experimental import pallas as pl
from jax.experimental.pallas import tpu as pltpu

F32 = jnp.float32
BF16 = jnp.bfloat16

D_MODEL = 2048
DEPTH = 2
N_META = 16
POOL_WINDOWS = (2, 4, 8, 16)
POOL_WIDTH = D_MODEL // 2
POOL_GROUP = POOL_WIDTH // len(POOL_WINDOWS)
N_HEADS = 16
N_KV_HEADS = 4
HEAD_DIM = 64
Q_PER_KV = N_HEADS // N_KV_HEADS
WINDOW = 128
BLOCK = 128
NEG = -1e30
LRU_WIDTH = D_MODEL // 2
LRU_BLOCKS = 4
LRU_BLOCK = LRU_WIDTH // LRU_BLOCKS
CONV_WIDTH = 4
LRU_C = 8.0
N_GROUPS = 4
EXPERTS_PER_GROUP = 8
N_EXPERTS = N_GROUPS * EXPERTS_PER_GROUP
TOP_K = 2
D_EXPERT = D_MODEL // 4
LN_EPS = 1e-5
ALPHA = (2.0 * DEPTH) ** 0.25

PAD = BLOCK - N_META
ATT_W = N_HEADS * HEAD_DIM
KV_W = N_KV_HEADS * HEAD_DIM
OFF_POOL = 0
OFF_Q = OFF_POOL + POOL_WIDTH
OFF_K = OFF_Q + ATT_W
OFF_V = OFF_K + KV_W
OFF_LX = OFF_V + KV_W
OFF_LY = OFF_LX + LRU_WIDTH
OFF_GATE = OFF_LY + LRU_WIDTH
IN_COLS = OFF_GATE + 3 * D_MODEL

VMEM_LIMIT_BYTES = 56 * 1024 * 1024
SUBLANES = 8

SEQ_TILE = 3 * BLOCK
INPROJ_BM = 1536
INPROJ_BN = 1536
MERGE_BM = 768
MERGE_BN = 1024
OUT_BM = 512
OUT_SPLIT = 2
MOE_BM = 256
DISPATCH_BM = 1536
COMBINE_BM = 512
EXPERT_X_BUFFERS = 3
ROUTE_W = 128


def _cparams(*sem):
    return pltpu.CompilerParams(dimension_semantics=sem, vmem_limit_bytes=VMEM_LIMIT_BYTES)


def _layer_norm(x, g, b):
    mu = jnp.mean(x, axis=-1, keepdims=True)
    xc = x - mu
    var = jnp.mean(xc * xc, axis=-1, keepdims=True)
    return xc * lax.rsqrt(var + LN_EPS) * g + b


def _flat_valid_rows(row0, n_rows, batch, lp):
    r = row0 + lax.broadcasted_iota(jnp.int32, (n_rows, 1), 0)
    pad_row = jnp.zeros((n_rows, 1), jnp.bool_)
    for b in range(batch):
        pad_row = pad_row | ((r >= b * lp) & (r < b * lp + PAD))
    return jnp.logical_not(pad_row)


EMBED_BLOCKS = SEQ_TILE // BLOCK


def _embed_kernel(*refs):
    x_refs, (meta_ref, g_ref, b_ref, hf_ref, hb_ref) = refs[:EMBED_BLOCKS], refs[EMBED_BLOCKS:]
    t = pl.program_id(1)
    for j in range(EMBED_BLOCKS):
        src = x_refs[j][...]
        if j == 0:
            src = jnp.where(t == 0, meta_ref[...], src)
        y = _layer_norm(src, g_ref[...], b_ref[...])
        if j == 0:
            row = lax.broadcasted_iota(jnp.int32, (BLOCK, 1), 0)
            y = jnp.where((t > 0) | (row >= PAD), y, 0.0)
        hf_ref[pl.ds(j * BLOCK, BLOCK), :] = y
        hb_ref[pl.ds(j * BLOCK, BLOCK), :] = y.astype(BF16)


def _embed(x, meta, g, b):
    batch, seq, d = x.shape
    nblk = seq // BLOCK
    lp = (nblk + 1) * BLOCK
    nt = lp // SEQ_TILE
    m = batch * lp
    meta_tile = jnp.concatenate([jnp.zeros((PAD, d), F32), meta.astype(F32)], axis=0)
    row_spec = pl.BlockSpec((SEQ_TILE, d), lambda bi, t: (bi * nt + t, 0))
    vec_spec = pl.BlockSpec((1, d), lambda bi, t: (0, 0))
    x_spec = lambda j: pl.BlockSpec(
        (BLOCK, d), lambda bi, t: (bi * nblk + jnp.maximum(EMBED_BLOCKS * t + j - 1, 0), 0))
    x2 = x.reshape(batch * seq, d)
    return pl.pallas_call(
        _embed_kernel,
        grid=(batch, nt),
        in_specs=[x_spec(j) for j in range(EMBED_BLOCKS)]
        + [pl.BlockSpec((BLOCK, d), lambda bi, t: (0, 0)), vec_spec, vec_spec],
        out_specs=[row_spec, row_spec],
        out_shape=[jax.ShapeDtypeStruct((m, d), F32), jax.ShapeDtypeStruct((m, d), BF16)],
        compiler_params=_cparams("parallel", "arbitrary"),
        name="embed_ln",
    )(*([x2] * EMBED_BLOCKS), meta_tile, g.reshape(1, d), b.reshape(1, d))


def _sigmoid(x):
    return 0.5 * jnp.tanh(0.5 * x) + 0.5


def _inproj_kernel(x_ref, w_ref, o_ref, *, gates):
    acc = jnp.dot(x_ref[...], w_ref[...], preferred_element_type=F32)
    o_ref[...] = (_sigmoid(acc) if gates else acc).astype(o_ref.dtype)


def _inproj(hb, w_in_bf16, layer, gates):
    m, d = hb.shape
    bm, bn = INPROJ_BM, INPROJ_BN
    col0, width = (OFF_GATE, IN_COLS - OFF_GATE) if gates else (0, OFF_GATE)
    assert m % bm == 0 and width % bn == 0 and col0 % bn == 0
    return pl.pallas_call(
        functools.partial(_inproj_kernel, gates=gates),
        grid=(m // bm, width // bn),
        in_specs=[
            pl.BlockSpec((bm, d), lambda i, j: (i, 0)),
            pl.BlockSpec((None, d, bn), lambda i, j: (layer, 0, col0 // bn + j)),
        ],
        out_specs=pl.BlockSpec((bm, bn), lambda i, j: (i, j)),
        out_shape=jax.ShapeDtypeStruct((m, width), BF16),
        compiler_params=_cparams("parallel", "arbitrary"),
        name="in_proj_gates" if gates else "in_proj",
    )(hb, w_in_bf16)


POOL_HALO = 2 * max(POOL_WINDOWS)
assert POOL_WINDOWS == tuple(2 ** (g + 1) for g in range(len(POOL_WINDOWS)))


def _pool_kernel(u_ref, w_ref, scale_ref, o_ref, ext_ref, lvl_ref):
    t = pl.program_id(1)
    tile = SEQ_TILE
    halo = POOL_HALO

    @pl.when(t == 0)
    def _():
        ext_ref[pl.ds(0, halo), :] = jnp.zeros((halo, POOL_WIDTH), F32)

    @pl.when(t > 0)
    def _():
        ext_ref[pl.ds(0, halo), :] = ext_ref[pl.ds(tile, halo), :]

    ext_ref[pl.ds(halo, tile), :] = u_ref[...].astype(F32)

    pos = t * tile + lax.broadcasted_iota(jnp.int32, (tile, 1), 0) - PAD
    src, first = ext_ref, 0
    for gi, w in enumerate(POOL_WINDOWS):
        lane0 = gi * POOL_GROUP
        lanes = pl.ds(lane0, POOL_WIDTH - lane0)
        new_first = -(-(first + w // 2) // SUBLANES) * SUBLANES
        n_rows = tile + halo - new_first
        level = src[pl.ds(new_first, n_rows), lanes] + src[pl.ds(new_first - w // 2, n_rows), lanes]
        win = level[halo - new_first:, :POOL_GROUP]
        cols = pl.ds(lane0, POOL_GROUP)
        u = ext_ref[pl.ds(halo, tile), cols]
        cnt = jnp.clip(pos + 1, 1, w).astype(F32)
        delta = win / cnt - u
        mixed = jnp.dot(delta.astype(BF16), w_ref[gi], preferred_element_type=F32)
        o_ref[:, cols] = (mixed * scale_ref[:, cols]).astype(o_ref.dtype)
        if gi + 1 < len(POOL_WINDOWS):
            lvl_ref[gi, pl.ds(new_first, n_rows), lanes] = level
            src, first = lvl_ref.at[gi], new_first


def _pool(cols, pool_w_bf16, pool_scale, batch, lp):
    m = cols.shape[0]
    nt = lp // SEQ_TILE
    maxw = POOL_HALO
    return pl.pallas_call(
        _pool_kernel,
        grid=(batch, nt),
        in_specs=[
            pl.BlockSpec((SEQ_TILE, POOL_WIDTH), lambda b, t: (b * nt + t, OFF_POOL // POOL_WIDTH)),
            pl.BlockSpec((len(POOL_WINDOWS), POOL_GROUP, POOL_GROUP), lambda b, t: (0, 0, 0)),
            pl.BlockSpec((1, POOL_WIDTH), lambda b, t: (0, 0)),
        ],
        out_specs=pl.BlockSpec((SEQ_TILE, POOL_WIDTH), lambda b, t: (b * nt + t, 0)),
        out_shape=jax.ShapeDtypeStruct((m, POOL_WIDTH), BF16),
        scratch_shapes=[pltpu.VMEM((SEQ_TILE + maxw, POOL_WIDTH), F32),
                        pltpu.VMEM((len(POOL_WINDOWS) - 1, SEQ_TILE + maxw, POOL_WIDTH), F32)],
        compiler_params=_cparams("parallel", "arbitrary"),
        name="pool_mixer",
    )(cols, pool_w_bf16, pool_scale.reshape(1, POOL_WIDTH))


def _attn_bias():
    kj = jnp.arange(2 * BLOCK, dtype=jnp.int32)[:, None]
    qi = jnp.arange(BLOCK, dtype=jnp.int32)[None, :]
    dist = BLOCK + qi - kj
    in_window = (dist >= 0) & (dist < WINDOW)
    slopes = 2.0 ** (-8.0 * jnp.arange(1, N_HEADS + 1, dtype=F32) / N_HEADS)
    alibi = -slopes[:, None, None] * dist.astype(F32)[None]
    return jnp.where(in_window[None], alibi, NEG)


def _attn_kernel(q_ref, kp_ref, kc_ref, vp_ref, vc_ref, bias_ref, sink_ref, o_ref):
    n = pl.program_id(1)
    q = q_ref[...] * (HEAD_DIM ** -0.5)

    def heads(early_keys):
        for kh in range(N_KV_HEADS):
            hs = pl.ds(kh * HEAD_DIM, HEAD_DIM)
            k2 = jnp.concatenate([kp_ref[:, hs], kc_ref[:, hs]], axis=0)
            v2 = jnp.concatenate([vp_ref[:, hs], vc_ref[:, hs]], axis=0)
            for g in range(Q_PER_KV):
                h = kh * Q_PER_KV + g
                qh = q[:, h * HEAD_DIM:(h + 1) * HEAD_DIM]
                s = lax.dot_general(k2, qh, (((1,), (1,)), ((), ())), preferred_element_type=F32)
                s = s + bias_ref[h]
                if early_keys is not None:
                    s = s + early_keys
                sk = sink_ref[0, h]
                mx = jnp.maximum(jnp.max(s, axis=0, keepdims=True), sk)
                p = jnp.exp(s - mx)
                den = jnp.sum(p, axis=0, keepdims=True) + jnp.exp(sk - mx)
                pn = (p * (1.0 / den)).astype(BF16)
                o = lax.dot_general(pn, v2, (((0,), (0,)), ((), ())), preferred_element_type=F32)
                o_ref[:, pl.ds(h * HEAD_DIM, HEAD_DIM)] = o.astype(o_ref.dtype)

    @pl.when(n < 2)
    def _():
        k_pos = (n - 1) * BLOCK + lax.broadcasted_iota(jnp.int32, (2 * BLOCK, 1), 0)
        heads(jnp.where(k_pos < PAD, NEG, 0.0))

    @pl.when(n >= 2)
    def _():
        heads(None)


def _attention(cols, sink, batch, lp):
    m = cols.shape[0]
    nb = lp // BLOCK
    cur = lambda cb: (lambda b, n: (b * nb + n, cb))
    prev = lambda cb: (lambda b, n: (b * nb + jnp.maximum(n - 1, 0), cb))
    return pl.pallas_call(
        _attn_kernel,
        grid=(batch, nb),
        in_specs=[
            pl.BlockSpec((BLOCK, ATT_W), cur(OFF_Q // ATT_W)),
            pl.BlockSpec((BLOCK, KV_W), prev(OFF_K // KV_W)),
            pl.BlockSpec((BLOCK, KV_W), cur(OFF_K // KV_W)),
            pl.BlockSpec((BLOCK, KV_W), prev(OFF_V // KV_W)),
            pl.BlockSpec((BLOCK, KV_W), cur(OFF_V // KV_W)),
            pl.BlockSpec((N_HEADS, 2 * BLOCK, BLOCK), lambda b, n: (0, 0, 0)),
            pl.BlockSpec(memory_space=pltpu.SMEM),
        ],
        out_specs=pl.BlockSpec((BLOCK, ATT_W), lambda b, n: (b * nb + n, 0)),
        out_shape=jax.ShapeDtypeStruct((m, ATT_W), BF16),
        compiler_params=_cparams("parallel", "arbitrary"),
        name="swa_attention",
    )(cols, cols, cols, cols, cols, _attn_bias(), sink.reshape(1, N_HEADS).astype(F32))


LRU_HALF = LRU_WIDTH // 2
LRU_HALO = 8
LRU_SCAN_UNROLL = 6


def _gelu_tanh(x):
    return 0.5 * x * (1.0 + jnp.tanh(0.7978845608028654 * (x + 0.044715 * (x * x * x))))


def _lru_kernel(x_ref, y_ref, cw_ref, cb_ref, wa_ref, ba_ref, wx_ref, bx_ref, lam_ref, o_ref,
                ext_ref, a_ref, b_ref, carry_ref):
    t = pl.program_id(2)
    tile = SEQ_TILE
    width = LRU_HALF

    @pl.when(t == 0)
    def _():
        ext_ref[pl.ds(0, LRU_HALO), :] = jnp.zeros((LRU_HALO, width), F32)
        carry_ref[...] = jnp.zeros((1, width), F32)

    @pl.when(t > 0)
    def _():
        ext_ref[pl.ds(0, LRU_HALO), :] = ext_ref[pl.ds(tile, LRU_HALO), :]

    ext_ref[pl.ds(LRU_HALO, tile), :] = x_ref[...].astype(F32)

    xc = cb_ref[...] + cw_ref[pl.ds(CONV_WIDTH - 1, 1), :] * ext_ref[pl.ds(LRU_HALO, tile), :]
    for j in range(CONV_WIDTH - 1):
        shift = CONV_WIDTH - 1 - j
        xc = xc + cw_ref[pl.ds(j, 1), :] * ext_ref[pl.ds(LRU_HALO - shift, tile), :]

    xcb = xc.astype(BF16)
    ga_parts, gx_parts = [], []
    for blk in range(width // LRU_BLOCK):
        xb = xcb[:, blk * LRU_BLOCK:(blk + 1) * LRU_BLOCK]
        ga_parts.append(jnp.dot(xb, wa_ref[blk], preferred_element_type=F32))
        gx_parts.append(jnp.dot(xb, wx_ref[blk], preferred_element_type=F32))
    gate_a = _sigmoid(jnp.concatenate(ga_parts, axis=1) + ba_ref[...])
    gate_x = _sigmoid(jnp.concatenate(gx_parts, axis=1) + bx_ref[...])

    neg_lam = -lam_ref[...]
    softplus = jnp.maximum(neg_lam, 0.0) + jnp.log1p(jnp.exp(-jnp.abs(neg_lam)))
    log_a = (-LRU_C) * gate_a * softplus
    a = jnp.exp(log_a)
    b_in = jnp.sqrt(1.0 - a * a) * gate_x * xc
    pos = t * tile + lax.broadcasted_iota(jnp.int32, (tile, 1), 0)
    b_in = jnp.where(pos >= PAD, b_in, 0.0)
    a_ref[...] = a
    b_ref[...] = b_in

    row = lax.broadcasted_iota(jnp.int32, (8, width), 0)

    def group(r, carry):
        rows = pl.ds(pl.multiple_of(r * 8, 8), 8)
        av = a_ref[rows, :]
        bv = b_ref[rows, :]
        for k in (1, 2, 4):
            a_sh = jnp.where(row >= k, pltpu.roll(av, k, 0), 1.0)
            b_sh = jnp.where(row >= k, pltpu.roll(bv, k, 0), 0.0)
            bv = av * b_sh + bv
            av = av * a_sh
        hv = av * carry + bv
        b_ref[rows, :] = hv
        return hv[7:8, :]

    carry_ref[...] = lax.fori_loop(0, tile // 8, group, carry_ref[...], unroll=LRU_SCAN_UNROLL)
    o_ref[...] = (b_ref[...] * _gelu_tanh(y_ref[...].astype(F32))).astype(o_ref.dtype)


def _lru(cols, conv_w, conv_b, wa_bf16, ba, wx_bf16, bx, lam, batch, lp):
    m = cols.shape[0]
    nt = lp // SEQ_TILE
    nhalf = LRU_WIDTH // LRU_HALF
    blocks_per_half = LRU_HALF // LRU_BLOCK
    vec = lambda v: v.reshape(1, LRU_WIDTH).astype(F32)
    vec_spec = pl.BlockSpec((1, LRU_HALF), lambda b, c, t: (0, c))
    w_spec = pl.BlockSpec((blocks_per_half, LRU_BLOCK, LRU_BLOCK), lambda b, c, t: (c, 0, 0))
    return pl.pallas_call(
        _lru_kernel,
        grid=(batch, nhalf, nt),
        in_specs=[
            pl.BlockSpec((SEQ_TILE, LRU_HALF), lambda b, c, t: (b * nt + t, OFF_LX // LRU_HALF + c)),
            pl.BlockSpec((SEQ_TILE, LRU_HALF), lambda b, c, t: (b * nt + t, OFF_LY // LRU_HALF + c)),
            pl.BlockSpec((CONV_WIDTH, LRU_HALF), lambda b, c, t: (0, c)),
            vec_spec, w_spec, vec_spec, w_spec, vec_spec, vec_spec,
        ],
        out_specs=pl.BlockSpec((SEQ_TILE, LRU_HALF), lambda b, c, t: (b * nt + t, c)),
        out_shape=jax.ShapeDtypeStruct((m, LRU_WIDTH), BF16),
        scratch_shapes=[
            pltpu.VMEM((SEQ_TILE + LRU_HALO, LRU_HALF), F32),
            pltpu.VMEM((SEQ_TILE, LRU_HALF), F32),
            pltpu.VMEM((SEQ_TILE, LRU_HALF), F32),
            pltpu.VMEM((1, LRU_HALF), F32),
        ],
        compiler_params=_cparams("parallel", "parallel", "arbitrary"),
        name="rglru",
    )(cols, cols, conv_w.astype(F32), vec(conv_b), wa_bf16, vec(ba), wx_bf16, vec(bx), vec(lam))


def _merge_kernel(p_ref, a_ref, r_ref, gp_ref, ga_ref, gr_ref, wp_ref, wa_ref, wr_ref, o_ref):
    acc = gp_ref[...].astype(F32) * jnp.dot(p_ref[...], wp_ref[...], preferred_element_type=F32)
    acc += ga_ref[...].astype(F32) * jnp.dot(a_ref[...], wa_ref[...], preferred_element_type=F32)
    acc += gr_ref[...].astype(F32) * jnp.dot(r_ref[...], wr_ref[...], preferred_element_type=F32)
    o_ref[...] = acc.astype(o_ref.dtype)


def _merge(pool_o, attn_o, lru_o, gates, wp, wa, wr, layer):
    m = pool_o.shape[0]
    bm, bn = MERGE_BM, MERGE_BN
    assert m % bm == 0 and D_MODEL % bn == 0
    x_spec = pl.BlockSpec((bm, POOL_WIDTH), lambda i, j: (i, 0))
    gate_spec = lambda k: pl.BlockSpec((bm, bn), lambda i, j: (i, k * D_MODEL // bn + j))
    w_spec = pl.BlockSpec((None, POOL_WIDTH, bn), lambda i, j: (layer, 0, j))
    return pl.pallas_call(
        _merge_kernel,
        grid=(m // bm, D_MODEL // bn),
        in_specs=[x_spec, x_spec, x_spec, gate_spec(0), gate_spec(1), gate_spec(2), w_spec, w_spec, w_spec],
        out_specs=pl.BlockSpec((bm, bn), lambda i, j: (i, j)),
        out_shape=jax.ShapeDtypeStruct((m, D_MODEL), BF16),
        compiler_params=_cparams("parallel", "arbitrary"),
        name="gated_merge",
    )(pool_o, attn_o, lru_o, gates, gates, gates, wp, wa, wr)


def _pack_bf16_pairs(yb):
    c = yb.shape[1] // 2
    lo = lax.bitcast_convert_type(yb[:, :c].astype(F32), jnp.uint32)
    hi = lax.bitcast_convert_type(yb[:, c:].astype(F32), jnp.uint32)
    return (hi & jnp.uint32(0xFFFF0000)) | (lo >> 16)


def _unpack_bf16_pairs(words, dtype=BF16):
    lo = lax.bitcast_convert_type(words << 16, F32)
    hi = lax.bitcast_convert_type(words & jnp.uint32(0xFFFF0000), F32)
    return jnp.concatenate([lo, hi], axis=1).astype(dtype)


def _route_tile(logits, valid, base):
    bm = logits.shape[0]
    lane = lax.broadcasted_iota(jnp.int32, (bm, ROUTE_W), 1)
    lane_f = lane.astype(F32)
    ninf = -jnp.inf
    big = float(ROUTE_W)

    gl = jnp.where(lane < N_GROUPS, logits, ninf)
    gmax = jnp.max(gl, axis=-1, keepdims=True)
    g = jnp.min(jnp.where(gl == gmax, lane_f, big), axis=-1, keepdims=True)
    p_g = 1.0 / jnp.sum(jnp.exp(gl - gmax), axis=-1, keepdims=True)

    first = N_GROUPS + g * EXPERTS_PER_GROUP
    sl = jnp.where((lane_f >= first) & (lane_f < first + EXPERTS_PER_GROUP), logits, ninf)
    m1 = jnp.max(sl, axis=-1, keepdims=True)
    i1 = jnp.min(jnp.where(sl == m1, lane_f, big), axis=-1, keepdims=True)
    ssum = jnp.sum(jnp.exp(sl - m1), axis=-1, keepdims=True)
    sl2 = jnp.where(lane_f == i1, ninf, sl)
    m2 = jnp.max(sl2, axis=-1, keepdims=True)
    i2 = jnp.min(jnp.where(sl2 == m2, lane_f, big), axis=-1, keepdims=True)
    p1 = 1.0 / ssum
    p2 = jnp.exp(m2 - m1) / ssum
    w1 = p_g * p1 / (p1 + p2)
    w2 = p_g * p2 / (p1 + p2)
    e1 = i1 - N_GROUPS
    e2 = i2 - N_GROUPS

    oh1 = (lane_f == e1) & valid
    oh2 = (lane_f == e2) & valid
    both = (oh1 | oh2).astype(F32)
    earlier = (lax.broadcasted_iota(jnp.int32, (bm, bm), 0) > lax.broadcasted_iota(jnp.int32, (bm, bm), 1))
    prefix = jnp.dot(earlier.astype(BF16), both.astype(BF16), preferred_element_type=F32) + base
    r1 = jnp.sum(jnp.where(oh1, prefix, 0.0), axis=-1, keepdims=True)
    r2 = jnp.sum(jnp.where(oh2, prefix, 0.0), axis=-1, keepdims=True)

    ri = jnp.where(lane == 0, e1, jnp.where(lane == 1, e2, jnp.where(lane == 2, r1, jnp.where(lane == 3, r2, 0.0))))
    rw = jnp.where(lane == 0, w1, jnp.where(lane == 1, w2, 0.0))
    return ri.astype(jnp.int32), rw, jnp.sum(both, axis=0, keepdims=True)


def _outproj_kernel(x_ref, w_ref, h_ref, g_ref, b_ref, rw_ref, rb_ref, hf_ref, hp_ref, ri_ref, rwt_ref, cnt_ref,
                    *, batch, lp):
    i = pl.program_id(0)
    bm = x_ref.shape[0]
    sub = bm // OUT_SPLIT

    @pl.when(i == 0)
    def _():
        cnt_ref[...] = jnp.zeros(cnt_ref.shape, F32)

    for s in range(OUT_SPLIT):
        rows = pl.ds(s * sub, sub)
        t = jnp.dot(x_ref[rows, :], w_ref[...], preferred_element_type=F32)
        y = _layer_norm(ALPHA * h_ref[rows, :] + t, g_ref[...], b_ref[...])
        valid = _flat_valid_rows(i * bm + s * sub, sub, batch, lp)
        y = jnp.where(valid, y, 0.0)
        yb = y.astype(BF16)
        hf_ref[rows, :] = y
        hp_ref[rows, :] = _pack_bf16_pairs(yb)
        logits = jnp.dot(yb, rw_ref[...], preferred_element_type=F32) + rb_ref[...]
        ri, rw, tile_cnt = _route_tile(logits, valid, cnt_ref[...])
        ri_ref[rows, :] = ri
        rwt_ref[rows, :] = rw
        cnt_ref[...] += tile_cnt


def _outproj_ln(merged, w_out_bf16, h, g, b, route_w, route_b, layer, batch, lp):
    m, d = h.shape
    bm = OUT_BM
    assert m % bm == 0
    row_spec = pl.BlockSpec((bm, d), lambda i: (i, 0))
    vec_spec = pl.BlockSpec((1, d), lambda i: (0, 0))
    route_spec = pl.BlockSpec((bm, ROUTE_W), lambda i: (i, 0))
    return pl.pallas_call(
        functools.partial(_outproj_kernel, batch=batch, lp=lp),
        grid=(m // bm,),
        in_specs=[
            row_spec,
            pl.BlockSpec((None, d, d), lambda i: (layer, 0, 0)),
            row_spec, vec_spec, vec_spec,
            pl.BlockSpec((d, ROUTE_W), lambda i: (0, 0)),
            pl.BlockSpec((1, ROUTE_W), lambda i: (0, 0)),
        ],
        out_specs=[row_spec, pl.BlockSpec((bm, d // 2), lambda i: (i, 0)), route_spec, route_spec,
                   pl.BlockSpec((1, ROUTE_W), lambda i: (0, 0))],
        out_shape=[jax.ShapeDtypeStruct((m, d), F32), jax.ShapeDtypeStruct((m, d // 2), jnp.uint32),
                   jax.ShapeDtypeStruct((m, ROUTE_W), jnp.int32), jax.ShapeDtypeStruct((m, ROUTE_W), F32),
                   jax.ShapeDtypeStruct((1, ROUTE_W), F32)],
        compiler_params=_cparams("arbitrary"),
        name="out_proj_ln",
    )(merged, w_out_bf16, h, g.reshape(1, d), b.reshape(1, d), route_w, route_b)


def _dispatch_kernel(start_ref, nchunk_ref, dest_hbm, hp_ref, xs_ref, idx_ref, zero_ref, idx_sem, row_sem, zero_sem,
                     *, n_expert_rows):
    i = pl.program_id(0)
    groups = hp_ref.shape[0]
    bm = groups * SUBLANES
    n_idx = TOP_K * bm

    @pl.when(i == 0)
    def _():
        zero_ref[...] = jnp.zeros(zero_ref.shape, zero_ref.dtype)

        def last_chunk(e):
            first = pl.multiple_of(start_ref[e] + (nchunk_ref[e] - 1) * MOE_BM, MOE_BM)
            return pltpu.make_async_copy(zero_ref, xs_ref.at[pl.ds(first, MOE_BM)], zero_sem)

        for e in range(N_EXPERTS):
            @pl.when(nchunk_ref[e] > 0)
            def _():
                last_chunk(e).start()

        for e in range(N_EXPERTS):
            @pl.when(nchunk_ref[e] > 0)
            def _():
                last_chunk(e).wait()

        used = start_ref[N_EXPERTS - 1] + nchunk_ref[N_EXPERTS - 1] * MOE_BM

        def tail_chunk(j):
            return pltpu.make_async_copy(
                zero_ref, xs_ref.at[pl.ds(pl.multiple_of(used + j * MOE_BM, MOE_BM), MOE_BM)], zero_sem)

        n_tail = (n_expert_rows - used) // MOE_BM
        lax.fori_loop(0, n_tail, lambda j, c: (tail_chunk(j).start(), c)[1], 0)
        lax.fori_loop(0, n_tail, lambda j, c: (tail_chunk(j).wait(), c)[1], 0)

    idx_copy = pltpu.make_async_copy(dest_hbm.at[pl.ds(pl.multiple_of(i * n_idx, n_idx), n_idx)], idx_ref, idx_sem)
    idx_copy.start()
    idx_copy.wait()

    def issue(g, carry):
        for s in range(SUBLANES):
            for k in range(TOP_K):
                slot = idx_ref[TOP_K * (g * SUBLANES + s) + k]
                pltpu.make_async_copy(hp_ref.at[g, pl.ds(s, 1)], xs_ref.at[pl.ds(slot, 1)],
                                      row_sem).start(priority=k % 2)
        return carry

    lax.fori_loop(0, groups, issue, 0)
    for k in range(TOP_K):
        pltpu.make_async_copy(xs_ref.at[pl.ds(0, bm)], xs_ref.at[pl.ds(0, bm)], row_sem).wait()


def _dispatch(hp, dest, chunk_start, n_chunks, n_expert_rows, n_slots):
    m, c = hp.shape
    bm = DISPATCH_BM
    assert m % bm == 0 and bm % SUBLANES == 0
    hp = hp.reshape(m // SUBLANES, SUBLANES, c)
    grid_spec = pltpu.PrefetchScalarGridSpec(
        num_scalar_prefetch=2,
        grid=(m // bm,),
        in_specs=[
            pl.BlockSpec(memory_space=pl.ANY),
            pl.BlockSpec((bm // SUBLANES, SUBLANES, c), lambda i, st, nc: (i, 0, 0)),
        ],
        out_specs=pl.BlockSpec(memory_space=pl.ANY),
        scratch_shapes=[pltpu.SMEM((TOP_K * bm,), jnp.int32), pltpu.VMEM((MOE_BM, c), jnp.uint32),
                        pltpu.SemaphoreType.DMA(()), pltpu.SemaphoreType.DMA(()), pltpu.SemaphoreType.DMA(())],
    )
    return pl.pallas_call(
        functools.partial(_dispatch_kernel, n_expert_rows=n_expert_rows),
        grid_spec=grid_spec,
        out_shape=jax.ShapeDtypeStruct((n_slots, c), jnp.uint32),
        compiler_params=_cparams("arbitrary"),
        name="moe_dispatch",
    )(chunk_start, n_chunks, dest.reshape(m * TOP_K), hp)


def _expert_kernel(start_ref, nchunk_ref, xs_hbm, wg_hbm, wu_hbm, wd_hbm, yb_hbm, wg_f, wu_f, wd_f,
                   x_buf, o_buf, w_sem, in_sem, out_sem, *, layer):
    e = pl.program_id(0)
    n = nchunk_ref[e]
    row0 = start_ref[e]
    bm = MOE_BM
    wslot = e % 2

    def w_copies(expert, slot):
        return [pltpu.make_async_copy(src.at[layer, expert], dst.at[slot], w_sem.at[slot])
                for src, dst in ((wg_hbm, wg_f), (wu_hbm, wu_f), (wd_hbm, wd_f))]

    def rows(j):
        return pl.ds(pl.multiple_of(row0 + j * bm, bm), bm)

    def x_copy(j):
        slot = j % EXPERT_X_BUFFERS
        return pltpu.make_async_copy(xs_hbm.at[rows(j)], x_buf.at[slot], in_sem.at[slot])

    def o_copy(j, slot):
        return pltpu.make_async_copy(o_buf.at[slot], yb_hbm.at[rows(j)], out_sem.at[slot])

    for ahead in range(EXPERT_X_BUFFERS - 1):
        @pl.when(ahead < n)
        def _():
            x_copy(ahead).start()

    @pl.when(e == 0)
    def _():
        for c in w_copies(0, 0):
            c.start(priority=1)

    @pl.when(e + 1 < N_EXPERTS)
    def _():
        for c in w_copies(e + 1, 1 - wslot):
            c.start(priority=1)

    for c in w_copies(e, wslot):
        c.wait()

    @pl.when(n > 0)
    def _():
        def chunk(j, carry):
            slot = j % 2
            x_copy(j).wait()

            @pl.when(j + EXPERT_X_BUFFERS - 1 < n)
            def _():
                x_copy(j + EXPERT_X_BUFFERS - 1).start()

            @pl.when(j >= 2)
            def _():
                o_copy(j - 2, slot).wait()

            x = _unpack_bf16_pairs(x_buf[j % EXPERT_X_BUFFERS])
            gate = jnp.dot(x, wg_f[wslot].astype(BF16), preferred_element_type=F32)
            up = jnp.dot(x, wu_f[wslot].astype(BF16), preferred_element_type=F32)
            hdn = (gate * _sigmoid(gate)) * up
            y = jnp.dot(hdn.astype(BF16), wd_f[wslot].astype(BF16), preferred_element_type=F32)
            o_buf[slot] = _pack_bf16_pairs(y.astype(BF16))
            o_copy(j, slot).start()
            return carry

        lax.fori_loop(0, n, chunk, 0)

        @pl.when(n >= 2)
        def _():
            o_copy(n - 2, n % 2).wait()

        o_copy(n - 1, (n - 1) % 2).wait()

    @pl.when(e == N_EXPERTS - 1)
    def _():
        o_buf[0] = jnp.zeros(o_buf.shape[1:], o_buf.dtype)

        def fill(j, carry):
            o_copy(j, 0).start()
            o_copy(j, 0).wait()
            return carry

        lax.fori_loop(n, (yb_hbm.shape[0] - row0) // bm, fill, 0)


def _experts(xs, n_rows, chunk_start, n_chunks, w_gate, w_up, w_down, layer):
    d = D_MODEL
    bm = MOE_BM
    any_spec = pl.BlockSpec(memory_space=pl.ANY)
    grid_spec = pltpu.PrefetchScalarGridSpec(
        num_scalar_prefetch=2,
        grid=(N_EXPERTS,),
        in_specs=[any_spec, any_spec, any_spec, any_spec],
        out_specs=any_spec,
        scratch_shapes=[
            pltpu.VMEM((2, d, D_EXPERT), F32),
            pltpu.VMEM((2, d, D_EXPERT), F32),
            pltpu.VMEM((2, D_EXPERT, d), F32),
            pltpu.VMEM((EXPERT_X_BUFFERS, bm, d // 2), jnp.uint32),
            pltpu.VMEM((2, bm, d // 2), jnp.uint32),
            pltpu.SemaphoreType.DMA((2,)),
            pltpu.SemaphoreType.DMA((EXPERT_X_BUFFERS,)),
            pltpu.SemaphoreType.DMA((2,)),
        ],
    )
    return pl.pallas_call(
        functools.partial(_expert_kernel, layer=layer),
        grid_spec=grid_spec,
        out_shape=jax.ShapeDtypeStruct((n_rows, d // 2), jnp.uint32),
        compiler_params=_cparams("arbitrary"),
        name="expert_mlp",
    )(chunk_start, n_chunks, xs, w_gate, w_up, w_down)


def _combine_ln_kernel(dest_hbm, h_ref, rw_ref, g_ref, b_ref, yb_hbm, *rest, batch, lp, final):
    if final:
        out_hbm, idx_ref, rows_ref, y_buf, idx_sem, row_sem, out_sem = rest
    else:
        hf_ref, hb_ref, idx_ref, rows_ref, idx_sem, row_sem = rest
    i = pl.program_id(0)
    n_tiles = pl.num_programs(0)
    bm = h_ref.shape[0]
    n_idx = TOP_K * bm
    buf = i % 2

    def start_gather(tile, into):
        idx_copy = pltpu.make_async_copy(
            dest_hbm.at[pl.ds(pl.multiple_of(tile * n_idx, n_idx), n_idx)], idx_ref, idx_sem)
        idx_copy.start()
        idx_copy.wait()

        def issue(g, carry):
            for s in range(SUBLANES):
                for k in range(TOP_K):
                    slot = idx_ref[TOP_K * (g * SUBLANES + s) + k]
                    pltpu.make_async_copy(yb_hbm.at[pl.ds(slot, 1)], rows_ref.at[into, k, g, pl.ds(s, 1)],
                                          row_sem.at[into]).start(priority=k % 2)
            return carry

        lax.fori_loop(0, bm // SUBLANES, issue, 0)

    @pl.when(i == 0)
    def _():
        start_gather(0, 0)

    @pl.when(i + 1 < n_tiles)
    def _():
        start_gather(i + 1, 1 - buf)

    for k in range(TOP_K):
        pltpu.make_async_copy(yb_hbm.at[pl.ds(0, bm)], yb_hbm.at[pl.ds(0, bm)], row_sem.at[buf]).wait()

    d = h_ref.shape[1]

    def expert_rows(k):
        return _unpack_bf16_pairs(rows_ref[buf, k].reshape(bm, d // 2), F32)

    ffn = rw_ref[:, 0:1] * expert_rows(0)
    for k in range(1, TOP_K):
        ffn = ffn + rw_ref[:, k:k + 1] * expert_rows(k)
    y = _layer_norm(ALPHA * h_ref[...] + ffn, g_ref[...], b_ref[...])
    if not final:
        y = jnp.where(_flat_valid_rows(i * bm, bm, batch, lp), y, 0.0)
        hf_ref[...] = y
        hb_ref[...] = y.astype(BF16)
        return

    nb = lp // BLOCK

    def for_each_out_block(step, fn):
        for j in range(bm // BLOCK):
            blk = step * (bm // BLOCK) + j
            seq_blk = blk % nb

            @pl.when(seq_blk >= 1)
            def _():
                dst = pl.multiple_of(((blk // nb) * (nb - 1) + seq_blk - 1) * BLOCK, BLOCK)
                fn(pltpu.make_async_copy(y_buf.at[pl.ds(j * BLOCK, BLOCK)], out_hbm.at[pl.ds(dst, BLOCK)], out_sem))

    @pl.when(i > 0)
    def _():
        for_each_out_block(i - 1, lambda c: c.wait())

    y_buf[...] = y
    for_each_out_block(i, lambda c: c.start(priority=1))

    @pl.when(i == n_tiles - 1)
    def _():
        for_each_out_block(i, lambda c: c.wait())


def _combine_ln(h, yb, dest, rw, g, b, batch, lp, final):
    m, d = h.shape
    bm = COMBINE_BM
    assert m % bm == 0 and bm % BLOCK == 0
    row_spec = pl.BlockSpec((bm, d), lambda i: (i, 0))
    vec_spec = pl.BlockSpec((1, d), lambda i: (0, 0))
    scratch = [pltpu.SMEM((TOP_K * bm,), jnp.int32),
               pltpu.VMEM((2, TOP_K, bm // SUBLANES, SUBLANES, d // 2), jnp.uint32)]
    sems = [pltpu.SemaphoreType.DMA(()), pltpu.SemaphoreType.DMA((2,))]
    if final:
        out_specs = pl.BlockSpec(memory_space=pl.ANY)
        out_shape = jax.ShapeDtypeStruct((batch * (lp - BLOCK), d), F32)
        scratch = scratch + [pltpu.VMEM((bm, d), F32)] + sems + [pltpu.SemaphoreType.DMA(())]
    else:
        out_specs = [row_spec, row_spec]
        out_shape = [jax.ShapeDtypeStruct((m, d), F32), jax.ShapeDtypeStruct((m, d), BF16)]
        scratch = scratch + sems
    return pl.pallas_call(
        functools.partial(_combine_ln_kernel, batch=batch, lp=lp, final=final),
        grid=(m // bm,),
        in_specs=[
            pl.BlockSpec(memory_space=pl.ANY),
            row_spec,
            pl.BlockSpec((bm, ROUTE_W), lambda i: (i, 0)),
            vec_spec, vec_spec,
            pl.BlockSpec(memory_space=pl.ANY),
        ],
        out_specs=out_specs,
        out_shape=out_shape,
        scratch_shapes=scratch,
        compiler_params=_cparams("arbitrary"),
        name="moe_combine_out" if final else "moe_combine_ln",
    )(dest.reshape(m * TOP_K), h, rw, g.reshape(1, d), b.reshape(1, d), yb)


def _slot_tables(ri, cnt, batch, lp):
    m = ri.shape[0]
    bm = MOE_BM
    eid = ri[:, 0:TOP_K]
    rank = ri[:, TOP_K:2 * TOP_K]
    counts = cnt[0, :N_EXPERTS].astype(jnp.int32)
    padded = (counts + bm - 1) // bm * bm
    pad_end = jnp.cumsum(padded)
    pad_start = pad_end - padded
    n_real = batch * (lp - PAD) * TOP_K
    nblk = -(-(n_real + N_EXPERTS * (bm - 1)) // bm)
    cap = nblk * bm
    experts = jnp.arange(N_EXPERTS, dtype=jnp.int32)
    start = jnp.sum(jnp.where(eid[:, :, None] == experts[None, None, :], pad_start[None, None, :], 0), axis=-1)
    row = jnp.arange(m, dtype=jnp.int32)
    pos = row % lp
    spare = cap + ((row // lp) * PAD + pos)[:, None] * TOP_K + jnp.arange(TOP_K, dtype=jnp.int32)[None, :]
    valid = (pos >= PAD)[:, None]
    dest = jnp.where(valid, start + rank, spare)
    dest_read = jnp.where(valid, dest, 0)
    n_slots = cap + batch * PAD * TOP_K
    return dest, dest_read, pad_start.astype(jnp.int32), (padded // bm).astype(jnp.int32), cap, n_slots


def kernel(x, meta, ln_emb_g, ln_emb_b, w_in, pool_w, pool_scale, attn_sink, conv_w, conv_b, lru_wa, lru_ba,
           lru_wx, lru_bx, lru_lambda, proj_pool, proj_attn, proj_lru, w_out, ln1_g, ln1_b, router_grp_w,
           router_grp_b, router_exp_w, router_exp_b, exp_w_gate, exp_w_up, exp_w_down, ln2_g, ln2_b):
    batch, seq, d = x.shape
    lp = PAD + N_META + seq
    m = batch * lp

    hf, hb = _embed(x, meta, ln_emb_g, ln_emb_b)
    w_in_b = w_in.astype(BF16)
    pool_w_b = pool_w.astype(BF16)
    wa_b = lru_wa.astype(BF16)
    wx_b = lru_wx.astype(BF16)
    wp_b = proj_pool.astype(BF16)
    wat_b = proj_attn.astype(BF16)
    wl_b = proj_lru.astype(BF16)
    w_out_b = w_out.astype(BF16)
    route_pad = ROUTE_W - N_GROUPS - N_EXPERTS

    for l in range(DEPTH):
        cols = _inproj(hb, w_in_b, l, gates=False)
        gates = _inproj(hb, w_in_b, l, gates=True)
        pool_o = _pool(cols, pool_w_b[l], pool_scale[l], batch, lp)
        attn_o = _attention(cols, attn_sink[l], batch, lp)
        lru_o = _lru(cols, conv_w[l], conv_b[l], wa_b[l], lru_ba[l], wx_b[l], lru_bx[l], lru_lambda[l],
                     batch, lp)
        merged = _merge(pool_o, attn_o, lru_o, gates, wp_b, wat_b, wl_b, l)
        route_w = jnp.concatenate(
            [router_grp_w[l], router_exp_w[l], jnp.zeros((d, route_pad), F32)], axis=1).astype(BF16)
        route_b = jnp.concatenate(
            [router_grp_b[l], router_exp_b[l], jnp.zeros((route_pad,), F32)]).reshape(1, ROUTE_W)
        h1f, h1p, ri, rw, cnt = _outproj_ln(merged, w_out_b, hf, ln1_g[l], ln1_b[l], route_w, route_b, l,
                                            batch, lp)
        dest, dest_read, chunk_start, n_chunks, cap, n_slots = _slot_tables(ri, cnt, batch, lp)
        xs = _dispatch(h1p, dest, chunk_start, n_chunks, cap, n_slots)
        yb = _experts(xs, cap, chunk_start, n_chunks, exp_w_gate, exp_w_up, exp_w_down, l)
        if l + 1 < DEPTH:
            hf, hb = _combine_ln(h1f, yb, dest_read, rw, ln2_g[l], ln2_b[l], batch, lp, final=False)
        else:
            out = _combine_ln(h1f, yb, dest_read, rw, ln2_g[l], ln2_b[l], batch, lp, final=True)

    return out.reshape(batch, seq, d)
```

```python
import functools

import jax
import jax.numpy as jnp
from jax import lax
from jax.experimental import pallas as pl
from jax.experimental.pallas import tpu as pltpu

F32 = jnp.float32
BF16 = jnp.bfloat16

D_MODEL = 2048
DEPTH = 2
N_META = 16
POOL_WINDOWS = (2, 4, 8, 16)
POOL_WIDTH = D_MODEL // 2
POOL_GROUP = POOL_WIDTH // len(POOL_WINDOWS)
N_HEADS = 16
N_KV_HEADS = 4
HEAD_DIM = 64
Q_PER_KV = N_HEADS // N_KV_HEADS
WINDOW = 128
BLOCK = 128
NEG = -1e30
LRU_WIDTH = D_MODEL // 2
LRU_BLOCKS = 4
LRU_BLOCK = LRU_WIDTH // LRU_BLOCKS
CONV_WIDTH = 4
LRU_C = 8.0
N_GROUPS = 4
EXPERTS_PER_GROUP = 8
N_EXPERTS = N_GROUPS * EXPERTS_PER_GROUP
TOP_K = 2
D_EXPERT = D_MODEL // 4
LN_EPS = 1e-5
ALPHA = (2.0 * DEPTH) ** 0.25

PAD = BLOCK - N_META
ATT_W = N_HEADS * HEAD_DIM
KV_W = N_KV_HEADS * HEAD_DIM
OFF_POOL = 0
OFF_Q = OFF_POOL + POOL_WIDTH
OFF_K = OFF_Q + ATT_W
OFF_V = OFF_K + KV_W
OFF_LX = OFF_V + KV_W
OFF_LY = OFF_LX + LRU_WIDTH
OFF_GATE = OFF_LY + LRU_WIDTH
IN_COLS = OFF_GATE + 3 * D_MODEL

VMEM_LIMIT_BYTES = 56 * 1024 * 1024
SUBLANES = 8

SEQ_TILE = 3 * BLOCK
INPROJ_BM = 1536
INPROJ_BN = 1536
MERGE_BM = 768
MERGE_BN = 1024
OUT_BM = 512
OUT_SPLIT = 2
MOE_BM = 256
DISPATCH_BM = 1536
COMBINE_BM = 512
EXPERT_X_BUFFERS = 3
ROUTE_W = 128


def _cparams(*sem):
    return pltpu.CompilerParams(dimension_semantics=sem, vmem_limit_bytes=VMEM_LIMIT_BYTES)


def _layer_norm(x, g, b):
    mu = jnp.mean(x, axis=-1, keepdims=True)
    xc = x - mu
    var = jnp.mean(xc * xc, axis=-1, keepdims=True)
    return xc * lax.rsqrt(var + LN_EPS) * g + b


def _flat_valid_rows(row0, n_rows, batch, lp):
    r = row0 + lax.broadcasted_iota(jnp.int32, (n_rows, 1), 0)
    pad_row = jnp.zeros((n_rows, 1), jnp.bool_)
    for b in range(batch):
        pad_row = pad_row | ((r >= b * lp) & (r < b * lp + PAD))
    return jnp.logical_not(pad_row)


EMBED_BLOCKS = SEQ_TILE // BLOCK


def _embed_kernel(*refs):
    x_refs, (meta_ref, g_ref, b_ref, hf_ref, hb_ref) = refs[:EMBED_BLOCKS], refs[EMBED_BLOCKS:]
    t = pl.program_id(1)
    for j in range(EMBED_BLOCKS):
        src = x_refs[j][...]
        if j == 0:
            src = jnp.where(t == 0, meta_ref[...], src)
        y = _layer_norm(src, g_ref[...], b_ref[...])
        if j == 0:
            row = lax.broadcasted_iota(jnp.int32, (BLOCK, 1), 0)
            y = jnp.where((t > 0) | (row >= PAD), y, 0.0)
        hf_ref[pl.ds(j * BLOCK, BLOCK), :] = y
        hb_ref[pl.ds(j * BLOCK, BLOCK), :] = y.astype(BF16)


def _embed(x, meta, g, b):
    batch, seq, d = x.shape
    nblk = seq // BLOCK
    lp = (nblk + 1) * BLOCK
    nt = lp // SEQ_TILE
    m = batch * lp
    meta_tile = jnp.concatenate([jnp.zeros((PAD, d), F32), meta.astype(F32)], axis=0)
    row_spec = pl.BlockSpec((SEQ_TILE, d), lambda bi, t: (bi * nt + t, 0))
    vec_spec = pl.BlockSpec((1, d), lambda bi, t: (0, 0))
    x_spec = lambda j: pl.BlockSpec(
        (BLOCK, d), lambda bi, t: (bi * nblk + jnp.maximum(EMBED_BLOCKS * t + j - 1, 0), 0))
    x2 = x.reshape(batch * seq, d)
    return pl.pallas_call(
        _embed_kernel,
        grid=(batch, nt),
        in_specs=[x_spec(j) for j in range(EMBED_BLOCKS)]
        + [pl.BlockSpec((BLOCK, d), lambda bi, t: (0, 0)), vec_spec, vec_spec],
        out_specs=[row_spec, row_spec],
        out_shape=[jax.ShapeDtypeStruct((m, d), F32), jax.ShapeDtypeStruct((m, d), BF16)],
        compiler_params=_cparams("parallel", "arbitrary"),
        name="embed_ln",
    )(*([x2] * EMBED_BLOCKS), meta_tile, g.reshape(1, d), b.reshape(1, d))


def _sigmoid(x):
    return 0.5 * jnp.tanh(0.5 * x) + 0.5


def _inproj_kernel(x_ref, w_ref, o_ref, *, gates):
    acc = jnp.dot(x_ref[...], w_ref[...], preferred_element_type=F32)
    o_ref[...] = (_sigmoid(acc) if gates else acc).astype(o_ref.dtype)


def _inproj(hb, w_in_bf16, layer, gates):
    m, d = hb.shape
    bm, bn = INPROJ_BM, INPROJ_BN
    col0, width = (OFF_GATE, IN_COLS - OFF_GATE) if gates else (0, OFF_GATE)
    assert m % bm == 0 and width % bn == 0 and col0 % bn == 0
    return pl.pallas_call(
        functools.partial(_inproj_kernel, gates=gates),
        grid=(m // bm, width // bn),
        in_specs=[
            pl.BlockSpec((bm, d), lambda i, j: (i, 0)),
            pl.BlockSpec((None, d, bn), lambda i, j: (layer, 0, col0 // bn + j)),
        ],
        out_specs=pl.BlockSpec((bm, bn), lambda i, j: (i, j)),
        out_shape=jax.ShapeDtypeStruct((m, width), BF16),
        compiler_params=_cparams("parallel", "arbitrary"),
        name="in_proj_gates" if gates else "in_proj",
    )(hb, w_in_bf16)


POOL_HALO = 2 * max(POOL_WINDOWS)
assert POOL_WINDOWS == tuple(2 ** (g + 1) for g in range(len(POOL_WINDOWS)))


def _pool_kernel(u_ref, w_ref, scale_ref, o_ref, ext_ref, lvl_ref):
    t = pl.program_id(1)
    tile = SEQ_TILE
    halo = POOL_HALO

    @pl.when(t == 0)
    def _():
        ext_ref[pl.ds(0, halo), :] = jnp.zeros((halo, POOL_WIDTH), F32)

    @pl.when(t > 0)
    def _():
        ext_ref[pl.ds(0, halo), :] = ext_ref[pl.ds(tile, halo), :]

    ext_ref[pl.ds(halo, tile), :] = u_ref[...].astype(F32)

    pos = t * tile + lax.broadcasted_iota(jnp.int32, (tile, 1), 0) - PAD
    src, first = ext_ref, 0
    for gi, w in enumerate(POOL_WINDOWS):
        lane0 = gi * POOL_GROUP
        lanes = pl.ds(lane0, POOL_WIDTH - lane0)
        new_first = -(-(first + w // 2) // SUBLANES) * SUBLANES
        n_rows = tile + halo - new_first
        level = src[pl.ds(new_first, n_rows), lanes] + src[pl.ds(new_first - w // 2, n_rows), lanes]
        win = level[halo - new_first:, :POOL_GROUP]
        cols = pl.ds(lane0, POOL_GROUP)
        u = ext_ref[pl.ds(halo, tile), cols]
        cnt = jnp.clip(pos + 1, 1, w).astype(F32)
        delta = win / cnt - u
        mixed = jnp.dot(delta.astype(BF16), w_ref[gi], preferred_element_type=F32)
        o_ref[:, cols] = (mixed * scale_ref[:, cols]).astype(o_ref.dtype)
        if gi + 1 < len(POOL_WINDOWS):
            lvl_ref[gi, pl.ds(new_first, n_rows), lanes] = level
            src, first = lvl_ref.at[gi], new_first


def _pool(cols, pool_w_bf16, pool_scale, batch, lp):
    m = cols.shape[0]
    nt = lp // SEQ_TILE
    maxw = POOL_HALO
    return pl.pallas_call(
        _pool_kernel,
        grid=(batch, nt),
        in_specs=[
            pl.BlockSpec((SEQ_TILE, POOL_WIDTH), lambda b, t: (b * nt + t, OFF_POOL // POOL_WIDTH)),
            pl.BlockSpec((len(POOL_WINDOWS), POOL_GROUP, POOL_GROUP), lambda b, t: (0, 0, 0)),
            pl.BlockSpec((1, POOL_WIDTH), lambda b, t: (0, 0)),
        ],
        out_specs=pl.BlockSpec((SEQ_TILE, POOL_WIDTH), lambda b, t: (b * nt + t, 0)),
        out_shape=jax.ShapeDtypeStruct((m, POOL_WIDTH), BF16),
        scratch_shapes=[pltpu.VMEM((SEQ_TILE + maxw, POOL_WIDTH), F32),
                        pltpu.VMEM((len(POOL_WINDOWS) - 1, SEQ_TILE + maxw, POOL_WIDTH), F32)],
        compiler_params=_cparams("parallel", "arbitrary"),
        name="pool_mixer",
    )(cols, pool_w_bf16, pool_scale.reshape(1, POOL_WIDTH))


def _attn_bias():
    kj = jnp.arange(2 * BLOCK, dtype=jnp.int32)[:, None]
    qi = jnp.arange(BLOCK, dtype=jnp.int32)[None, :]
    dist = BLOCK + qi - kj
    in_window = (dist >= 0) & (dist < WINDOW)
    slopes = 2.0 ** (-8.0 * jnp.arange(1, N_HEADS + 1, dtype=F32) / N_HEADS)
    alibi = -slopes[:, None, None] * dist.astype(F32)[None]
    return jnp.where(in_window[None], alibi, NEG)


def _attn_kernel(q_ref, kp_ref, kc_ref, vp_ref, vc_ref, bias_ref, sink_ref, o_ref):
    n = pl.program_id(1)
    q = q_ref[...] * (HEAD_DIM ** -0.5)

    def heads(early_keys):
        for kh in range(N_KV_HEADS):
            hs = pl.ds(kh * HEAD_DIM, HEAD_DIM)
            k2 = jnp.concatenate([kp_ref[:, hs], kc_ref[:, hs]], axis=0)
            v2 = jnp.concatenate([vp_ref[:, hs], vc_ref[:, hs]], axis=0)
            for g in range(Q_PER_KV):
                h = kh * Q_PER_KV + g
                qh = q[:, h * HEAD_DIM:(h + 1) * HEAD_DIM]
                s = lax.dot_general(k2, qh, (((1,), (1,)), ((), ())), preferred_element_type=F32)
                s = s + bias_ref[h]
                if early_keys is not None:
                    s = s + early_keys
                sk = sink_ref[0, h]
                mx = jnp.maximum(jnp.max(s, axis=0, keepdims=True), sk)
                p = jnp.exp(s - mx)
                den = jnp.sum(p, axis=0, keepdims=True) + jnp.exp(sk - mx)
                pn = (p * (1.0 / den)).astype(BF16)
                o = lax.dot_general(pn, v2, (((0,), (0,)), ((), ())), preferred_element_type=F32)
                o_ref[:, pl.ds(h * HEAD_DIM, HEAD_DIM)] = o.astype(o_ref.dtype)

    @pl.when(n < 2)
    def _():
        k_pos = (n - 1) * BLOCK + lax.broadcasted_iota(jnp.int32, (2 * BLOCK, 1), 0)
        heads(jnp.where(k_pos < PAD, NEG, 0.0))

    @pl.when(n >= 2)
    def _():
        heads(None)


def _attention(cols, sink, batch, lp):
    m = cols.shape[0]
    nb = lp // BLOCK
    cur = lambda cb: (lambda b, n: (b * nb + n, cb))
    prev = lambda cb: (lambda b, n: (b * nb + jnp.maximum(n - 1, 0), cb))
    return pl.pallas_call(
        _attn_kernel,
        grid=(batch, nb),
        in_specs=[
            pl.BlockSpec((BLOCK, ATT_W), cur(OFF_Q // ATT_W)),
            pl.BlockSpec((BLOCK, KV_W), prev(OFF_K // KV_W)),
            pl.BlockSpec((BLOCK, KV_W), cur(OFF_K // KV_W)),
            pl.BlockSpec((BLOCK, KV_W), prev(OFF_V // KV_W)),
            pl.BlockSpec((BLOCK, KV_W), cur(OFF_V // KV_W)),
            pl.BlockSpec((N_HEADS, 2 * BLOCK, BLOCK), lambda b, n: (0, 0, 0)),
            pl.BlockSpec(memory_space=pltpu.SMEM),
        ],
        out_specs=pl.BlockSpec((BLOCK, ATT_W), lambda b, n: (b * nb + n, 0)),
        out_shape=jax.ShapeDtypeStruct((m, ATT_W), BF16),
        compiler_params=_cparams("parallel", "arbitrary"),
        name="swa_attention",
    )(cols, cols, cols, cols, cols, _attn_bias(), sink.reshape(1, N_HEADS).astype(F32))


LRU_HALF = LRU_WIDTH // 2
LRU_HALO = 8
LRU_SCAN_UNROLL = 6


def _gelu_tanh(x):
    return 0.5 * x * (1.0 + jnp.tanh(0.7978845608028654 * (x + 0.044715 * (x * x * x))))


def _lru_kernel(x_ref, y_ref, cw_ref, cb_ref, wa_ref, ba_ref, wx_ref, bx_ref, lam_ref, o_ref,
                ext_ref, a_ref, b_ref, carry_ref):
    t = pl.program_id(2)
    tile = SEQ_TILE
    width = LRU_HALF

    @pl.when(t == 0)
    def _():
        ext_ref[pl.ds(0, LRU_HALO), :] = jnp.zeros((LRU_HALO, width), F32)
        carry_ref[...] = jnp.zeros((1, width), F32)

    @pl.when(t > 0)
    def _():
        ext_ref[pl.ds(0, LRU_HALO), :] = ext_ref[pl.ds(tile, LRU_HALO), :]

    ext_ref[pl.ds(LRU_HALO, tile), :] = x_ref[...].astype(F32)

    xc = cb_ref[...] + cw_ref[pl.ds(CONV_WIDTH - 1, 1), :] * ext_ref[pl.ds(LRU_HALO, tile), :]
    for j in range(CONV_WIDTH - 1):
        shift = CONV_WIDTH - 1 - j
        xc = xc + cw_ref[pl.ds(j, 1), :] * ext_ref[pl.ds(LRU_HALO - shift, tile), :]

    xcb = xc.astype(BF16)
    ga_parts, gx_parts = [], []
    for blk in range(width // LRU_BLOCK):
        xb = xcb[:, blk * LRU_BLOCK:(blk + 1) * LRU_BLOCK]
        ga_parts.append(jnp.dot(xb, wa_ref[blk], preferred_element_type=F32))
        gx_parts.append(jnp.dot(xb, wx_ref[blk], preferred_element_type=F32))
    gate_a = _sigmoid(jnp.concatenate(ga_parts, axis=1) + ba_ref[...])
    gate_x = _sigmoid(jnp.concatenate(gx_parts, axis=1) + bx_ref[...])

    neg_lam = -lam_ref[...]
    softplus = jnp.maximum(neg_lam, 0.0) + jnp.log1p(jnp.exp(-jnp.abs(neg_lam)))
    log_a = (-LRU_C) * gate_a * softplus
    a = jnp.exp(log_a)
    b_in = jnp.sqrt(1.0 - a * a) * gate_x * xc
    pos = t * tile + lax.broadcasted_iota(jnp.int32, (tile, 1), 0)
    b_in = jnp.where(pos >= PAD, b_in, 0.0)
    a_ref[...] = a
    b_ref[...] = b_in

    row = lax.broadcasted_iota(jnp.int32, (8, width), 0)

    def group(r, carry):
        rows = pl.ds(pl.multiple_of(r * 8, 8), 8)
        av = a_ref[rows, :]
        bv = b_ref[rows, :]
        for k in (1, 2, 4):
            a_sh = jnp.where(row >= k, pltpu.roll(av, k, 0), 1.0)
            b_sh = jnp.where(row >= k, pltpu.roll(bv, k, 0), 0.0)
            bv = av * b_sh + bv
            av = av * a_sh
        hv = av * carry + bv
        b_ref[rows, :] = hv
        return hv[7:8, :]

    carry_ref[...] = lax.fori_loop(0, tile // 8, group, carry_ref[...], unroll=LRU_SCAN_UNROLL)
    o_ref[...] = (b_ref[...] * _gelu_tanh(y_ref[...].astype(F32))).astype(o_ref.dtype)


def _lru(cols, conv_w, conv_b, wa_bf16, ba, wx_bf16, bx, lam, batch, lp):
    m = cols.shape[0]
    nt = lp // SEQ_TILE
    nhalf = LRU_WIDTH // LRU_HALF
    blocks_per_half = LRU_HALF // LRU_BLOCK
    vec = lambda v: v.reshape(1, LRU_WIDTH).astype(F32)
    vec_spec = pl.BlockSpec((1, LRU_HALF), lambda b, c, t: (0, c))
    w_spec = pl.BlockSpec((blocks_per_half, LRU_BLOCK, LRU_BLOCK), lambda b, c, t: (c, 0, 0))
    return pl.pallas_call(
        _lru_kernel,
        grid=(batch, nhalf, nt),
        in_specs=[
            pl.BlockSpec((SEQ_TILE, LRU_HALF), lambda b, c, t: (b * nt + t, OFF_LX // LRU_HALF + c)),
            pl.BlockSpec((SEQ_TILE, LRU_HALF), lambda b, c, t: (b * nt + t, OFF_LY // LRU_HALF + c)),
            pl.BlockSpec((CONV_WIDTH, LRU_HALF), lambda b, c, t: (0, c)),
            vec_spec, w_spec, vec_spec, w_spec, vec_spec, vec_spec,
        ],
        out_specs=pl.BlockSpec((SEQ_TILE, LRU_HALF), lambda b, c, t: (b * nt + t, c)),
        out_shape=jax.ShapeDtypeStruct((m, LRU_WIDTH), BF16),
        scratch_shapes=[
            pltpu.VMEM((SEQ_TILE + LRU_HALO, LRU_HALF), F32),
            pltpu.VMEM((SEQ_TILE, LRU_HALF), F32),
            pltpu.VMEM((SEQ_TILE, LRU_HALF), F32),
            pltpu.VMEM((1, LRU_HALF), F32),
        ],
        compiler_params=_cparams("parallel", "parallel", "arbitrary"),
        name="rglru",
    )(cols, cols, conv_w.astype(F32), vec(conv_b), wa_bf16, vec(ba), wx_bf16, vec(bx), vec(lam))


def _merge_kernel(p_ref, a_ref, r_ref, gp_ref, ga_ref, gr_ref, wp_ref, wa_ref, wr_ref, o_ref):
    acc = gp_ref[...].astype(F32) * jnp.dot(p_ref[...], wp_ref[...], preferred_element_type=F32)
    acc += ga_ref[...].astype(F32) * jnp.dot(a_ref[...], wa_ref[...], preferred_element_type=F32)
    acc += gr_ref[...].astype(F32) * jnp.dot(r_ref[...], wr_ref[...], preferred_element_type=F32)
    o_ref[...] = acc.astype(o_ref.dtype)


def _merge(pool_o, attn_o, lru_o, gates, wp, wa, wr, layer):
    m = pool_o.shape[0]
    bm, bn = MERGE_BM, MERGE_BN
    assert m % bm == 0 and D_MODEL % bn == 0
    x_spec = pl.BlockSpec((bm, POOL_WIDTH), lambda i, j: (i, 0))
    gate_spec = lambda k: pl.BlockSpec((bm, bn), lambda i, j: (i, k * D_MODEL // bn + j))
    w_spec = pl.BlockSpec((None, POOL_WIDTH, bn), lambda i, j: (layer, 0, j))
    return pl.pallas_call(
        _merge_kernel,
        grid=(m // bm, D_MODEL // bn),
        in_specs=[x_spec, x_spec, x_spec, gate_spec(0), gate_spec(1), gate_spec(2), w_spec, w_spec, w_spec],
        out_specs=pl.BlockSpec((bm, bn), lambda i, j: (i, j)),
        out_shape=jax.ShapeDtypeStruct((m, D_MODEL), BF16),
        compiler_params=_cparams("parallel", "arbitrary"),
        name="gated_merge",
    )(pool_o, attn_o, lru_o, gates, gates, gates, wp, wa, wr)


def _pack_bf16_pairs(yb):
    c = yb.shape[1] // 2
    lo = lax.bitcast_convert_type(yb[:, :c].astype(F32), jnp.uint32)
    hi = lax.bitcast_convert_type(yb[:, c:].astype(F32), jnp.uint32)
    return (hi & jnp.uint32(0xFFFF0000)) | (lo >> 16)


def _unpack_bf16_pairs(words, dtype=BF16):
    lo = lax.bitcast_convert_type(words << 16, F32)
    hi = lax.bitcast_convert_type(words & jnp.uint32(0xFFFF0000), F32)
    return jnp.concatenate([lo, hi], axis=1).astype(dtype)


def _route_tile(logits, valid, base):
    bm = logits.shape[0]
    lane = lax.broadcasted_iota(jnp.int32, (bm, ROUTE_W), 1)
    lane_f = lane.astype(F32)
    ninf = -jnp.inf
    big = float(ROUTE_W)

    gl = jnp.where(lane < N_GROUPS, logits, ninf)
    gmax = jnp.max(gl, axis=-1, keepdims=True)
    g = jnp.min(jnp.where(gl == gmax, lane_f, big), axis=-1, keepdims=True)
    p_g = 1.0 / jnp.sum(jnp.exp(gl - gmax), axis=-1, keepdims=True)

    first = N_GROUPS + g * EXPERTS_PER_GROUP
    sl = jnp.where((lane_f >= first) & (lane_f < first + EXPERTS_PER_GROUP), logits, ninf)
    m1 = jnp.max(sl, axis=-1, keepdims=True)
    i1 = jnp.min(jnp.where(sl == m1, lane_f, big), axis=-1, keepdims=True)
    ssum = jnp.sum(jnp.exp(sl - m1), axis=-1, keepdims=True)
    sl2 = jnp.where(lane_f == i1, ninf, sl)
    m2 = jnp.max(sl2, axis=-1, keepdims=True)
    i2 = jnp.min(jnp.where(sl2 == m2, lane_f, big), axis=-1, keepdims=True)
    p1 = 1.0 / ssum
    p2 = jnp.exp(m2 - m1) / ssum
    w1 = p_g * p1 / (p1 + p2)
    w2 = p_g * p2 / (p1 + p2)
    e1 = i1 - N_GROUPS
    e2 = i2 - N_GROUPS

    oh1 = (lane_f == e1) & valid
    oh2 = (lane_f == e2) & valid
    both = (oh1 | oh2).astype(F32)
    earlier = (lax.broadcasted_iota(jnp.int32, (bm, bm), 0) > lax.broadcasted_iota(jnp.int32, (bm, bm), 1))
    prefix = jnp.dot(earlier.astype(BF16), both.astype(BF16), preferred_element_type=F32) + base
    r1 = jnp.sum(jnp.where(oh1, prefix, 0.0), axis=-1, keepdims=True)
    r2 = jnp.sum(jnp.where(oh2, prefix, 0.0), axis=-1, keepdims=True)

    ri = jnp.where(lane == 0, e1, jnp.where(lane == 1, e2, jnp.where(lane == 2, r1, jnp.where(lane == 3, r2, 0.0))))
    rw = jnp.where(lane == 0, w1, jnp.where(lane == 1, w2, 0.0))
    return ri.astype(jnp.int32), rw, jnp.sum(both, axis=0, keepdims=True)


def _outproj_kernel(x_ref, w_ref, h_ref, g_ref, b_ref, rw_ref, rb_ref, hf_ref, hp_ref, ri_ref, rwt_ref, cnt_ref,
                    *, batch, lp):
    i = pl.program_id(0)
    bm = x_ref.shape[0]
    sub = bm // OUT_SPLIT

    @pl.when(i == 0)
    def _():
        cnt_ref[...] = jnp.zeros(cnt_ref.shape, F32)

    for s in range(OUT_SPLIT):
        rows = pl.ds(s * sub, sub)
        t = jnp.dot(x_ref[rows, :], w_ref[...], preferred_element_type=F32)
        y = _layer_norm(ALPHA * h_ref[rows, :] + t, g_ref[...], b_ref[...])
        valid = _flat_valid_rows(i * bm + s * sub, sub, batch, lp)
        y = jnp.where(valid, y, 0.0)
        yb = y.astype(BF16)
        hf_ref[rows, :] = y
        hp_ref[rows, :] = _pack_bf16_pairs(yb)
        logits = jnp.dot(yb, rw_ref[...], preferred_element_type=F32) + rb_ref[...]
        ri, rw, tile_cnt = _route_tile(logits, valid, cnt_ref[...])
        ri_ref[rows, :] = ri
        rwt_ref[rows, :] = rw
        cnt_ref[...] += tile_cnt


def _outproj_ln(merged, w_out_bf16, h, g, b, route_w, route_b, layer, batch, lp):
    m, d = h.shape
    bm = OUT_BM
    assert m % bm == 0
    row_spec = pl.BlockSpec((bm, d), lambda i: (i, 0))
    vec_spec = pl.BlockSpec((1, d), lambda i: (0, 0))
    route_spec = pl.BlockSpec((bm, ROUTE_W), lambda i: (i, 0))
    return pl.pallas_call(
        functools.partial(_outproj_kernel, batch=batch, lp=lp),
        grid=(m // bm,),
        in_specs=[
            row_spec,
            pl.BlockSpec((None, d, d), lambda i: (layer, 0, 0)),
            row_spec, vec_spec, vec_spec,
            pl.BlockSpec((d, ROUTE_W), lambda i: (0, 0)),
            pl.BlockSpec((1, ROUTE_W), lambda i: (0, 0)),
        ],
        out_specs=[row_spec, pl.BlockSpec((bm, d // 2), lambda i: (i, 0)), route_spec, route_spec,
                   pl.BlockSpec((1, ROUTE_W), lambda i: (0, 0))],
        out_shape=[jax.ShapeDtypeStruct((m, d), F32), jax.ShapeDtypeStruct((m, d // 2), jnp.uint32),
                   jax.ShapeDtypeStruct((m, ROUTE_W), jnp.int32), jax.ShapeDtypeStruct((m, ROUTE_W), F32),
                   jax.ShapeDtypeStruct((1, ROUTE_W), F32)],
        compiler_params=_cparams("arbitrary"),
        name="out_proj_ln",
    )(merged, w_out_bf16, h, g.reshape(1, d), b.reshape(1, d), route_w, route_b)


def _dispatch_kernel(start_ref, nchunk_ref, dest_hbm, hp_ref, xs_ref, idx_ref, zero_ref, idx_sem, row_sem, zero_sem,
                     *, n_expert_rows):
    i = pl.program_id(0)
    groups = hp_ref.shape[0]
    bm = groups * SUBLANES
    n_idx = TOP_K * bm

    @pl.when(i == 0)
    def _():
        zero_ref[...] = jnp.zeros(zero_ref.shape, zero_ref.dtype)

        def last_chunk(e):
            first = pl.multiple_of(start_ref[e] + (nchunk_ref[e] - 1) * MOE_BM, MOE_BM)
            return pltpu.make_async_copy(zero_ref, xs_ref.at[pl.ds(first, MOE_BM)], zero_sem)

        for e in range(N_EXPERTS):
            @pl.when(nchunk_ref[e] > 0)
            def _():
                last_chunk(e).start()

        for e in range(N_EXPERTS):
            @pl.when(nchunk_ref[e] > 0)
            def _():
                last_chunk(e).wait()

        used = start_ref[N_EXPERTS - 1] + nchunk_ref[N_EXPERTS - 1] * MOE_BM

        def tail_chunk(j):
            return pltpu.make_async_copy(
                zero_ref, xs_ref.at[pl.ds(pl.multiple_of(used + j * MOE_BM, MOE_BM), MOE_BM)], zero_sem)

        n_tail = (n_expert_rows - used) // MOE_BM
        lax.fori_loop(0, n_tail, lambda j, c: (tail_chunk(j).start(), c)[1], 0)
        lax.fori_loop(0, n_tail, lambda j, c: (tail_chunk(j).wait(), c)[1], 0)

    idx_copy = pltpu.make_async_copy(dest_hbm.at[pl.ds(pl.multiple_of(i * n_idx, n_idx), n_idx)], idx_ref, idx_sem)
    idx_copy.start()
    idx_copy.wait()

    def issue(g, carry):
        for s in range(SUBLANES):
            for k in range(TOP_K):
                slot = idx_ref[TOP_K * (g * SUBLANES + s) + k]
                pltpu.make_async_copy(hp_ref.at[g, pl.ds(s, 1)], xs_ref.at[pl.ds(slot, 1)],
                                      row_sem).start(priority=k % 2)
        return carry

    lax.fori_loop(0, groups, issue, 0)
    for k in range(TOP_K):
        pltpu.make_async_copy(xs_ref.at[pl.ds(0, bm)], xs_ref.at[pl.ds(0, bm)], row_sem).wait()


def _dispatch(hp, dest, chunk_start, n_chunks, n_expert_rows, n_slots):
    m, c = hp.shape
    bm = DISPATCH_BM
    assert m % bm == 0 and bm % SUBLANES == 0
    hp = hp.reshape(m // SUBLANES, SUBLANES, c)
    grid_spec = pltpu.PrefetchScalarGridSpec(
        num_scalar_prefetch=2,
        grid=(m // bm,),
        in_specs=[
            pl.BlockSpec(memory_space=pl.ANY),
            pl.BlockSpec((bm // SUBLANES, SUBLANES, c), lambda i, st, nc: (i, 0, 0)),
        ],
        out_specs=pl.BlockSpec(memory_space=pl.ANY),
        scratch_shapes=[pltpu.SMEM((TOP_K * bm,), jnp.int32), pltpu.VMEM((MOE_BM, c), jnp.uint32),
                        pltpu.SemaphoreType.DMA(()), pltpu.SemaphoreType.DMA(()), pltpu.SemaphoreType.DMA(())],
    )
    return pl.pallas_call(
        functools.partial(_dispatch_kernel, n_expert_rows=n_expert_rows),
        grid_spec=grid_spec,
        out_shape=jax.ShapeDtypeStruct((n_slots, c), jnp.uint32),
        compiler_params=_cparams("arbitrary"),
        name="moe_dispatch",
    )(chunk_start, n_chunks, dest.reshape(m * TOP_K), hp)


def _expert_kernel(start_ref, nchunk_ref, xs_hbm, wg_hbm, wu_hbm, wd_hbm, yb_hbm, wg_f, wu_f, wd_f, wg_s, wu_s, wd_s,
                   x_buf, o_buf, w_sem, in_sem, out_sem, *, layer):
    e = pl.program_id(0)
    n = nchunk_ref[e]
    bm = MOE_BM
    first = start_ref[e] // bm
    total = start_ref[N_EXPERTS - 1] // bm + nchunk_ref[N_EXPERTS - 1]
    wslot = e % 2

    def w_copies(expert, slot):
        return [pltpu.make_async_copy(src.at[layer, expert], dst.at[slot], w_sem.at[slot])
                for src, dst in ((wg_hbm, wg_f), (wu_hbm, wu_f), (wd_hbm, wd_f))]

    def rows(g):
        return pl.ds(pl.multiple_of(g * bm, bm), bm)

    def x_copy(g):
        slot = g % EXPERT_X_BUFFERS
        return pltpu.make_async_copy(xs_hbm.at[rows(g)], x_buf.at[slot], in_sem.at[slot])

    def o_copy(g):
        slot = g % 2
        return pltpu.make_async_copy(o_buf.at[slot], yb_hbm.at[rows(g)], out_sem.at[slot])

    @pl.when(e == 0)
    def _():
        for ahead in range(EXPERT_X_BUFFERS - 1):
            @pl.when(ahead < total)
            def _():
                x_copy(ahead).start()

        for c in w_copies(0, 0):
            c.start(priority=1)

    @pl.when(e + 1 < N_EXPERTS)
    def _():
        for c in w_copies(e + 1, 1 - wslot):
            c.start(priority=1)

    for c in w_copies(e, wslot):
        c.wait()

    @pl.when(n > 0)
    def _():
        wg_s[...] = wg_f[wslot].astype(BF16)
        wu_s[...] = wu_f[wslot].astype(BF16)
        wd_s[...] = wd_f[wslot].astype(BF16)

        def chunk(j, carry):
            g = first + j
            x_copy(g).wait()

            @pl.when(g + EXPERT_X_BUFFERS - 1 < total)
            def _():
                x_copy(g + EXPERT_X_BUFFERS - 1).start()

            @pl.when(g >= 2)
            def _():
                o_copy(g - 2).wait()

            x = _unpack_bf16_pairs(x_buf[g % EXPERT_X_BUFFERS])
            gate = jnp.dot(x, wg_s[...], preferred_element_type=F32)
            up = jnp.dot(x, wu_s[...], preferred_element_type=F32)
            hdn = (gate * _sigmoid(gate)) * up
            y = jnp.dot(hdn.astype(BF16), wd_s[...], preferred_element_type=F32)
            o_buf[g % 2] = _pack_bf16_pairs(y.astype(BF16))
            o_copy(g).start()
            return carry

        lax.fori_loop(0, n, chunk, 0)

    @pl.when(e == N_EXPERTS - 1)
    def _():
        for back in (2, 1):
            @pl.when(total >= back)
            def _():
                o_copy(total - back).wait()

        o_buf[0] = jnp.zeros(o_buf.shape[1:], o_buf.dtype)

        def fill(g, carry):
            tail = pltpu.make_async_copy(o_buf.at[0], yb_hbm.at[rows(g)], out_sem.at[0])
            tail.start()
            tail.wait()
            return carry

        lax.fori_loop(total, yb_hbm.shape[0] // bm, fill, 0)


def _experts(xs, n_rows, chunk_start, n_chunks, w_gate, w_up, w_down, layer):
    d = D_MODEL
    bm = MOE_BM
    any_spec = pl.BlockSpec(memory_space=pl.ANY)
    grid_spec = pltpu.PrefetchScalarGridSpec(
        num_scalar_prefetch=2,
        grid=(N_EXPERTS,),
        in_specs=[any_spec, any_spec, any_spec, any_spec],
        out_specs=any_spec,
        scratch_shapes=[
            pltpu.VMEM((2, d, D_EXPERT), F32),
            pltpu.VMEM((2, d, D_EXPERT), F32),
            pltpu.VMEM((2, D_EXPERT, d), F32),
            pltpu.VMEM((d, D_EXPERT), BF16),
            pltpu.VMEM((d, D_EXPERT), BF16),
            pltpu.VMEM((D_EXPERT, d), BF16),
            pltpu.VMEM((EXPERT_X_BUFFERS, bm, d // 2), jnp.uint32),
            pltpu.VMEM((2, bm, d // 2), jnp.uint32),
            pltpu.SemaphoreType.DMA((2,)),
            pltpu.SemaphoreType.DMA((EXPERT_X_BUFFERS,)),
            pltpu.SemaphoreType.DMA((2,)),
        ],
    )
    return pl.pallas_call(
        functools.partial(_expert_kernel, layer=layer),
        grid_spec=grid_spec,
        out_shape=jax.ShapeDtypeStruct((n_rows, d // 2), jnp.uint32),
        compiler_params=_cparams("arbitrary"),
        name="expert_mlp",
    )(chunk_start, n_chunks, xs, w_gate, w_up, w_down)


def _combine_ln_kernel(dest_hbm, h_ref, rw_ref, g_ref, b_ref, yb_hbm, *rest, batch, lp, final):
    if final:
        out_hbm, idx_ref, rows_ref, y_buf, idx_sem, row_sem, out_sem = rest
    else:
        hf_ref, hb_ref, idx_ref, rows_ref, idx_sem, row_sem = rest
    i = pl.program_id(0)
    n_tiles = pl.num_programs(0)
    bm = h_ref.shape[0]
    n_idx = TOP_K * bm
    buf = i % 2

    def start_gather(tile, into):
        idx_copy = pltpu.make_async_copy(
            dest_hbm.at[pl.ds(pl.multiple_of(tile * n_idx, n_idx), n_idx)], idx_ref, idx_sem)
        idx_copy.start()
        idx_copy.wait()

        def issue(g, carry):
            for s in range(SUBLANES):
                for k in range(TOP_K):
                    slot = idx_ref[TOP_K * (g * SUBLANES + s) + k]
                    pltpu.make_async_copy(yb_hbm.at[pl.ds(slot, 1)], rows_ref.at[into, k, g, pl.ds(s, 1)],
                                          row_sem.at[into]).start(priority=k % 2)
            return carry

        lax.fori_loop(0, bm // SUBLANES, issue, 0)

    @pl.when(i == 0)
    def _():
        start_gather(0, 0)

    @pl.when(i + 1 < n_tiles)
    def _():
        start_gather(i + 1, 1 - buf)

    for k in range(TOP_K):
        pltpu.make_async_copy(yb_hbm.at[pl.ds(0, bm)], yb_hbm.at[pl.ds(0, bm)], row_sem.at[buf]).wait()

    d = h_ref.shape[1]

    def expert_rows(k):
        return _unpack_bf16_pairs(rows_ref[buf, k].reshape(bm, d // 2), F32)

    ffn = rw_ref[:, 0:1] * expert_rows(0)
    for k in range(1, TOP_K):
        ffn = ffn + rw_ref[:, k:k + 1] * expert_rows(k)
    y = _layer_norm(ALPHA * h_ref[...] + ffn, g_ref[...], b_ref[...])
    if not final:
        y = jnp.where(_flat_valid_rows(i * bm, bm, batch, lp), y, 0.0)
        hf_ref[...] = y
        hb_ref[...] = y.astype(BF16)
        return

    nb = lp // BLOCK

    def for_each_out_block(step, fn):
        for j in range(bm // BLOCK):
            blk = step * (bm // BLOCK) + j
            seq_blk = blk % nb

            @pl.when(seq_blk >= 1)
            def _():
                dst = pl.multiple_of(((blk // nb) * (nb - 1) + seq_blk - 1) * BLOCK, BLOCK)
                fn(pltpu.make_async_copy(y_buf.at[pl.ds(j * BLOCK, BLOCK)], out_hbm.at[pl.ds(dst, BLOCK)], out_sem))

    @pl.when(i > 0)
    def _():
        for_each_out_block(i - 1, lambda c: c.wait())

    y_buf[...] = y
    for_each_out_block(i, lambda c: c.start(priority=1))

    @pl.when(i == n_tiles - 1)
    def _():
        for_each_out_block(i, lambda c: c.wait())


def _combine_ln(h, yb, dest, rw, g, b, batch, lp, final):
    m, d = h.shape
    bm = COMBINE_BM
    assert m % bm == 0 and bm % BLOCK == 0
    row_spec = pl.BlockSpec((bm, d), lambda i: (i, 0))
    vec_spec = pl.BlockSpec((1, d), lambda i: (0, 0))
    scratch = [pltpu.SMEM((TOP_K * bm,), jnp.int32),
               pltpu.VMEM((2, TOP_K, bm // SUBLANES, SUBLANES, d // 2), jnp.uint32)]
    sems = [pltpu.SemaphoreType.DMA(()), pltpu.SemaphoreType.DMA((2,))]
    if final:
        out_specs = pl.BlockSpec(memory_space=pl.ANY)
        out_shape = jax.ShapeDtypeStruct((batch * (lp - BLOCK), d), F32)
        scratch = scratch + [pltpu.VMEM((bm, d), F32)] + sems + [pltpu.SemaphoreType.DMA(())]
    else:
        out_specs = [row_spec, row_spec]
        out_shape = [jax.ShapeDtypeStruct((m, d), F32), jax.ShapeDtypeStruct((m, d), BF16)]
        scratch = scratch + sems
    return pl.pallas_call(
        functools.partial(_combine_ln_kernel, batch=batch, lp=lp, final=final),
        grid=(m // bm,),
        in_specs=[
            pl.BlockSpec(memory_space=pl.ANY),
            row_spec,
            pl.BlockSpec((bm, ROUTE_W), lambda i: (i, 0)),
            vec_spec, vec_spec,
            pl.BlockSpec(memory_space=pl.ANY),
        ],
        out_specs=out_specs,
        out_shape=out_shape,
        scratch_shapes=scratch,
        compiler_params=_cparams("arbitrary"),
        name="moe_combine_out" if final else "moe_combine_ln",
    )(dest.reshape(m * TOP_K), h, rw, g.reshape(1, d), b.reshape(1, d), yb)


def _slot_tables(ri, cnt, batch, lp):
    m = ri.shape[0]
    bm = MOE_BM
    eid = ri[:, 0:TOP_K]
    rank = ri[:, TOP_K:2 * TOP_K]
    counts = cnt[0, :N_EXPERTS].astype(jnp.int32)
    padded = (counts + bm - 1) // bm * bm
    pad_end = jnp.cumsum(padded)
    pad_start = pad_end - padded
    n_real = batch * (lp - PAD) * TOP_K
    nblk = -(-(n_real + N_EXPERTS * (bm - 1)) // bm)
    cap = nblk * bm
    experts = jnp.arange(N_EXPERTS, dtype=jnp.int32)
    start = jnp.sum(jnp.where(eid[:, :, None] == experts[None, None, :], pad_start[None, None, :], 0), axis=-1)
    row = jnp.arange(m, dtype=jnp.int32)
    pos = row % lp
    spare = cap + ((row // lp) * PAD + pos)[:, None] * TOP_K + jnp.arange(TOP_K, dtype=jnp.int32)[None, :]
    valid = (pos >= PAD)[:, None]
    dest = jnp.where(valid, start + rank, spare)
    dest_read = jnp.where(valid, dest, 0)
    n_slots = cap + batch * PAD * TOP_K
    return dest, dest_read, pad_start.astype(jnp.int32), (padded // bm).astype(jnp.int32), cap, n_slots


def kernel(x, meta, ln_emb_g, ln_emb_b, w_in, pool_w, pool_scale, attn_sink, conv_w, conv_b, lru_wa, lru_ba,
           lru_wx, lru_bx, lru_lambda, proj_pool, proj_attn, proj_lru, w_out, ln1_g, ln1_b, router_grp_w,
           router_grp_b, router_exp_w, router_exp_b, exp_w_gate, exp_w_up, exp_w_down, ln2_g, ln2_b):
    batch, seq, d = x.shape
    lp = PAD + N_META + seq
    m = batch * lp

    hf, hb = _embed(x, meta, ln_emb_g, ln_emb_b)
    w_in_b = w_in.astype(BF16)
    pool_w_b = pool_w.astype(BF16)
    wa_b = lru_wa.astype(BF16)
    wx_b = lru_wx.astype(BF16)
    wp_b = proj_pool.astype(BF16)
    wat_b = proj_attn.astype(BF16)
    wl_b = proj_lru.astype(BF16)
    w_out_b = w_out.astype(BF16)
    route_pad = ROUTE_W - N_GROUPS - N_EXPERTS

    for l in range(DEPTH):
        cols = _inproj(hb, w_in_b, l, gates=False)
        gates = _inproj(hb, w_in_b, l, gates=True)
        pool_o = _pool(cols, pool_w_b[l], pool_scale[l], batch, lp)
        attn_o = _attention(cols, attn_sink[l], batch, lp)
        lru_o = _lru(cols, conv_w[l], conv_b[l], wa_b[l], lru_ba[l], wx_b[l], lru_bx[l], lru_lambda[l],
                     batch, lp)
        merged = _merge(pool_o, attn_o, lru_o, gates, wp_b, wat_b, wl_b, l)
        route_w = jnp.concatenate(
            [router_grp_w[l], router_exp_w[l], jnp.zeros((d, route_pad), F32)], axis=1).astype(BF16)
        route_b = jnp.concatenate(
            [router_grp_b[l], router_exp_b[l], jnp.zeros((route_pad,), F32)]).reshape(1, ROUTE_W)
        h1f, h1p, ri, rw, cnt = _outproj_ln(merged, w_out_b, hf, ln1_g[l], ln1_b[l], route_w, route_b, l,
                                            batch, lp)
        dest, dest_read, chunk_start, n_chunks, cap, n_slots = _slot_tables(ri, cnt, batch, lp)
        xs = _dispatch(h1p, dest, chunk_start, n_chunks, cap, n_slots)
        yb = _experts(xs, cap, chunk_start, n_chunks, exp_w_gate, exp_w_up, exp_w_down, l)
        if l + 1 < DEPTH:
            hf, hb = _combine_ln(h1f, yb, dest_read, rw, ln2_g[l], ln2_b[l], batch, lp, final=False)
        else:
            out = _combine_ln(h1f, yb, dest_read, rw, ln2_g[l], ln2_b[l], batch, lp, final=True)

    return out.reshape(batch, seq, d)
```

```python
import functools

import jax
import jax.numpy as jnp
from jax import lax
from jax.experimental import pallas as pl
from jax.experimental.pallas import tpu as pltpu

F32 = jnp.float32
BF16 = jnp.bfloat16

D_MODEL = 2048
DEPTH = 2
N_META = 16
POOL_WINDOWS = (2, 4, 8, 16)
POOL_WIDTH = D_MODEL // 2
POOL_GROUP = POOL_WIDTH // len(POOL_WINDOWS)
N_HEADS = 16
N_KV_HEADS = 4
HEAD_DIM = 64
Q_PER_KV = N_HEADS // N_KV_HEADS
WINDOW = 128
BLOCK = 128
NEG = -1e30
LRU_WIDTH = D_MODEL // 2
LRU_BLOCKS = 4
LRU_BLOCK = LRU_WIDTH // LRU_BLOCKS
CONV_WIDTH = 4
LRU_C = 8.0
N_GROUPS = 4
EXPERTS_PER_GROUP = 8
N_EXPERTS = N_GROUPS * EXPERTS_PER_GROUP
TOP_K = 2
D_EXPERT = D_MODEL // 4
LN_EPS = 1e-5
ALPHA = (2.0 * DEPTH) ** 0.25

PAD = BLOCK - N_META
ATT_W = N_HEADS * HEAD_DIM
KV_W = N_KV_HEADS * HEAD_DIM
OFF_POOL = 0
OFF_Q = OFF_POOL + POOL_WIDTH
OFF_K = OFF_Q + ATT_W
OFF_V = OFF_K + KV_W
OFF_LX = OFF_V + KV_W
OFF_LY = OFF_LX + LRU_WIDTH
OFF_GATE = OFF_LY + LRU_WIDTH
IN_COLS = OFF_GATE + 3 * D_MODEL

VMEM_LIMIT_BYTES = 56 * 1024 * 1024
SUBLANES = 8

SEQ_TILE = 3 * BLOCK
INPROJ_BM = 1536
INPROJ_BN = 1536
MERGE_BM = 768
MERGE_BN = 1024
OUT_BM = 512
OUT_SPLIT = 2
MOE_BM = 256
DISPATCH_BM = 1536
COMBINE_BM = 512
EXPERT_X_BUFFERS = 3
ROUTE_W = 128


def _cparams(*sem):
    return pltpu.CompilerParams(dimension_semantics=sem, vmem_limit_bytes=VMEM_LIMIT_BYTES)


def _layer_norm(x, g, b):
    mu = jnp.mean(x, axis=-1, keepdims=True)
    xc = x - mu
    var = jnp.mean(xc * xc, axis=-1, keepdims=True)
    return xc * lax.rsqrt(var + LN_EPS) * g + b


def _flat_valid_rows(row0, n_rows, batch, lp):
    r = row0 + lax.broadcasted_iota(jnp.int32, (n_rows, 1), 0)
    pad_row = jnp.zeros((n_rows, 1), jnp.bool_)
    for b in range(batch):
        pad_row = pad_row | ((r >= b * lp) & (r < b * lp + PAD))
    return jnp.logical_not(pad_row)


EMBED_BLOCKS = SEQ_TILE // BLOCK


def _embed_kernel(*refs):
    x_refs, (meta_ref, g_ref, b_ref, hf_ref, hb_ref) = refs[:EMBED_BLOCKS], refs[EMBED_BLOCKS:]
    t = pl.program_id(1)
    for j in range(EMBED_BLOCKS):
        src = x_refs[j][...]
        if j == 0:
            src = jnp.where(t == 0, meta_ref[...], src)
        y = _layer_norm(src, g_ref[...], b_ref[...])
        if j == 0:
            row = lax.broadcasted_iota(jnp.int32, (BLOCK, 1), 0)
            y = jnp.where((t > 0) | (row >= PAD), y, 0.0)
        hf_ref[pl.ds(j * BLOCK, BLOCK), :] = y
        hb_ref[pl.ds(j * BLOCK, BLOCK), :] = y.astype(BF16)


def _embed(x, meta, g, b):
    batch, seq, d = x.shape
    nblk = seq // BLOCK
    lp = (nblk + 1) * BLOCK
    nt = lp // SEQ_TILE
    m = batch * lp
    meta_tile = jnp.concatenate([jnp.zeros((PAD, d), F32), meta.astype(F32)], axis=0)
    row_spec = pl.BlockSpec((SEQ_TILE, d), lambda bi, t: (bi * nt + t, 0))
    vec_spec = pl.BlockSpec((1, d), lambda bi, t: (0, 0))
    x_spec = lambda j: pl.BlockSpec(
        (BLOCK, d), lambda bi, t: (bi * nblk + jnp.maximum(EMBED_BLOCKS * t + j - 1, 0), 0))
    x2 = x.reshape(batch * seq, d)
    return pl.pallas_call(
        _embed_kernel,
        grid=(batch, nt),
        in_specs=[x_spec(j) for j in range(EMBED_BLOCKS)]
        + [pl.BlockSpec((BLOCK, d), lambda bi, t: (0, 0)), vec_spec, vec_spec],
        out_specs=[row_spec, row_spec],
        out_shape=[jax.ShapeDtypeStruct((m, d), F32), jax.ShapeDtypeStruct((m, d), BF16)],
        compiler_params=_cparams("parallel", "arbitrary"),
        name="embed_ln",
    )(*([x2] * EMBED_BLOCKS), meta_tile, g.reshape(1, d), b.reshape(1, d))


def _sigmoid(x):
    return 0.5 * jnp.tanh(0.5 * x) + 0.5


def _inproj_kernel(x_ref, w_ref, o_ref, *, gates):
    acc = jnp.dot(x_ref[...], w_ref[...], preferred_element_type=F32)
    o_ref[...] = (_sigmoid(acc) if gates else acc).astype(o_ref.dtype)


def _inproj(hb, w_in_bf16, layer, gates):
    m, d = hb.shape
    bm, bn = INPROJ_BM, INPROJ_BN
    col0, width = (OFF_GATE, IN_COLS - OFF_GATE) if gates else (0, OFF_GATE)
    assert m % bm == 0 and width % bn == 0 and col0 % bn == 0
    return pl.pallas_call(
        functools.partial(_inproj_kernel, gates=gates),
        grid=(m // bm, width // bn),
        in_specs=[
            pl.BlockSpec((bm, d), lambda i, j: (i, 0)),
            pl.BlockSpec((None, d, bn), lambda i, j: (layer, 0, col0 // bn + j)),
        ],
        out_specs=pl.BlockSpec((bm, bn), lambda i, j: (i, j)),
        out_shape=jax.ShapeDtypeStruct((m, width), BF16),
        compiler_params=_cparams("parallel", "arbitrary"),
        name="in_proj_gates" if gates else "in_proj",
    )(hb, w_in_bf16)


POOL_HALO = 2 * max(POOL_WINDOWS)
assert POOL_WINDOWS == tuple(2 ** (g + 1) for g in range(len(POOL_WINDOWS)))


def _pool_kernel(u_ref, w_ref, scale_ref, o_ref, ext_ref, lvl_ref):
    t = pl.program_id(1)
    tile = SEQ_TILE
    halo = POOL_HALO

    @pl.when(t == 0)
    def _():
        ext_ref[pl.ds(0, halo), :] = jnp.zeros((halo, POOL_WIDTH), F32)

    @pl.when(t > 0)
    def _():
        ext_ref[pl.ds(0, halo), :] = ext_ref[pl.ds(tile, halo), :]

    ext_ref[pl.ds(halo, tile), :] = u_ref[...].astype(F32)

    pos = t * tile + lax.broadcasted_iota(jnp.int32, (tile, 1), 0) - PAD
    src, first = ext_ref, 0
    for gi, w in enumerate(POOL_WINDOWS):
        lane0 = gi * POOL_GROUP
        lanes = pl.ds(lane0, POOL_WIDTH - lane0)
        new_first = -(-(first + w // 2) // SUBLANES) * SUBLANES
        n_rows = tile + halo - new_first
        level = src[pl.ds(new_first, n_rows), lanes] + src[pl.ds(new_first - w // 2, n_rows), lanes]
        win = level[halo - new_first:, :POOL_GROUP]
        cols = pl.ds(lane0, POOL_GROUP)
        u = ext_ref[pl.ds(halo, tile), cols]
        cnt = jnp.clip(pos + 1, 1, w).astype(F32)
        delta = win / cnt - u
        mixed = jnp.dot(delta.astype(BF16), w_ref[gi], preferred_element_type=F32)
        o_ref[:, cols] = (mixed * scale_ref[:, cols]).astype(o_ref.dtype)
        if gi + 1 < len(POOL_WINDOWS):
            lvl_ref[gi, pl.ds(new_first, n_rows), lanes] = level
            src, first = lvl_ref.at[gi], new_first


def _pool(cols, pool_w_bf16, pool_scale, batch, lp):
    m = cols.shape[0]
    nt = lp // SEQ_TILE
    maxw = POOL_HALO
    return pl.pallas_call(
        _pool_kernel,
        grid=(batch, nt),
        in_specs=[
            pl.BlockSpec((SEQ_TILE, POOL_WIDTH), lambda b, t: (b * nt + t, OFF_POOL // POOL_WIDTH)),
            pl.BlockSpec((len(POOL_WINDOWS), POOL_GROUP, POOL_GROUP), lambda b, t: (0, 0, 0)),
            pl.BlockSpec((1, POOL_WIDTH), lambda b, t: (0, 0)),
        ],
        out_specs=pl.BlockSpec((SEQ_TILE, POOL_WIDTH), lambda b, t: (b * nt + t, 0)),
        out_shape=jax.ShapeDtypeStruct((m, POOL_WIDTH), BF16),
        scratch_shapes=[pltpu.VMEM((SEQ_TILE + maxw, POOL_WIDTH), F32),
                        pltpu.VMEM((len(POOL_WINDOWS) - 1, SEQ_TILE + maxw, POOL_WIDTH), F32)],
        compiler_params=_cparams("parallel", "arbitrary"),
        name="pool_mixer",
    )(cols, pool_w_bf16, pool_scale.reshape(1, POOL_WIDTH))


def _attn_bias():
    kj = jnp.arange(2 * BLOCK, dtype=jnp.int32)[:, None]
    qi = jnp.arange(BLOCK, dtype=jnp.int32)[None, :]
    dist = BLOCK + qi - kj
    in_window = (dist >= 0) & (dist < WINDOW)
    slopes = 2.0 ** (-8.0 * jnp.arange(1, N_HEADS + 1, dtype=F32) / N_HEADS)
    alibi = -slopes[:, None, None] * dist.astype(F32)[None]
    return jnp.where(in_window[None], alibi, NEG)


def _attn_kernel(q_ref, kp_ref, kc_ref, vp_ref, vc_ref, bias_ref, sink_ref, o_ref):
    n = pl.program_id(1)
    q = q_ref[...] * (HEAD_DIM ** -0.5)

    def heads(early_keys):
        for kh in range(N_KV_HEADS):
            hs = pl.ds(kh * HEAD_DIM, HEAD_DIM)
            k2 = jnp.concatenate([kp_ref[:, hs], kc_ref[:, hs]], axis=0)
            v2 = jnp.concatenate([vp_ref[:, hs], vc_ref[:, hs]], axis=0)
            for g in range(Q_PER_KV):
                h = kh * Q_PER_KV + g
                qh = q[:, h * HEAD_DIM:(h + 1) * HEAD_DIM]
                s = lax.dot_general(k2, qh, (((1,), (1,)), ((), ())), preferred_element_type=F32)
                s = s + bias_ref[h]
                if early_keys is not None:
                    s = s + early_keys
                sk = sink_ref[0, h]
                mx = jnp.maximum(jnp.max(s, axis=0, keepdims=True), sk)
                p = jnp.exp(s - mx)
                den = jnp.sum(p, axis=0, keepdims=True) + jnp.exp(sk - mx)
                pn = (p * (1.0 / den)).astype(BF16)
                o = lax.dot_general(pn, v2, (((0,), (0,)), ((), ())), preferred_element_type=F32)
                o_ref[:, pl.ds(h * HEAD_DIM, HEAD_DIM)] = o.astype(o_ref.dtype)

    @pl.when(n < 2)
    def _():
        k_pos = (n - 1) * BLOCK + lax.broadcasted_iota(jnp.int32, (2 * BLOCK, 1), 0)
        heads(jnp.where(k_pos < PAD, NEG, 0.0))

    @pl.when(n >= 2)
    def _():
        heads(None)


def _attention(cols, sink, batch, lp):
    m = cols.shape[0]
    nb = lp // BLOCK
    cur = lambda cb: (lambda b, n: (b * nb + n, cb))
    prev = lambda cb: (lambda b, n: (b * nb + jnp.maximum(n - 1, 0), cb))
    return pl.pallas_call(
        _attn_kernel,
        grid=(batch, nb),
        in_specs=[
            pl.BlockSpec((BLOCK, ATT_W), cur(OFF_Q // ATT_W)),
            pl.BlockSpec((BLOCK, KV_W), prev(OFF_K // KV_W)),
            pl.BlockSpec((BLOCK, KV_W), cur(OFF_K // KV_W)),
            pl.BlockSpec((BLOCK, KV_W), prev(OFF_V // KV_W)),
            pl.BlockSpec((BLOCK, KV_W), cur(OFF_V // KV_W)),
            pl.BlockSpec((N_HEADS, 2 * BLOCK, BLOCK), lambda b, n: (0, 0, 0)),
            pl.BlockSpec(memory_space=pltpu.SMEM),
        ],
        out_specs=pl.BlockSpec((BLOCK, ATT_W), lambda b, n: (b * nb + n, 0)),
        out_shape=jax.ShapeDtypeStruct((m, ATT_W), BF16),
        compiler_params=_cparams("parallel", "arbitrary"),
        name="swa_attention",
    )(cols, cols, cols, cols, cols, _attn_bias(), sink.reshape(1, N_HEADS).astype(F32))


LRU_HALF = LRU_WIDTH // 2
LRU_HALO = 8
LRU_SCAN_UNROLL = 6


LOG2_E = 1.4426950408889634
GELU_C = 0.7978845608028654


def _gelu_tanh(x):
    half = 0.5 * x
    return half + half * jnp.tanh(x * (GELU_C + (GELU_C * 0.044715) * (x * x)))


def _lru_kernel(x_ref, y_ref, cw_ref, cb_ref, wa_ref, ba_ref, wx_ref, bx_ref, lam_ref, o_ref,
                ext_ref, a_ref, b_ref, carry_ref):
    t = pl.program_id(2)
    tile = SEQ_TILE
    width = LRU_HALF

    @pl.when(t == 0)
    def _():
        ext_ref[pl.ds(0, LRU_HALO), :] = jnp.zeros((LRU_HALO, width), F32)
        carry_ref[...] = jnp.zeros((1, width), F32)

    @pl.when(t > 0)
    def _():
        ext_ref[pl.ds(0, LRU_HALO), :] = ext_ref[pl.ds(tile, LRU_HALO), :]

    ext_ref[pl.ds(LRU_HALO, tile), :] = x_ref[...].astype(F32)

    xc = cb_ref[...] + cw_ref[pl.ds(CONV_WIDTH - 1, 1), :] * ext_ref[pl.ds(LRU_HALO, tile), :]
    for j in range(CONV_WIDTH - 1):
        shift = CONV_WIDTH - 1 - j
        xc = xc + cw_ref[pl.ds(j, 1), :] * ext_ref[pl.ds(LRU_HALO - shift, tile), :]

    xcb = xc.astype(BF16)
    ga_parts, gx_parts = [], []
    for blk in range(width // LRU_BLOCK):
        xb = xcb[:, blk * LRU_BLOCK:(blk + 1) * LRU_BLOCK]
        ga_parts.append(jnp.dot(xb, wa_ref[blk], preferred_element_type=F32))
        gx_parts.append(jnp.dot(xb, wx_ref[blk], preferred_element_type=F32))
    gate_a = _sigmoid(jnp.concatenate(ga_parts, axis=1) + ba_ref[...])
    gate_x = _sigmoid(jnp.concatenate(gx_parts, axis=1) + bx_ref[...])

    neg_lam = -lam_ref[...]
    softplus = jnp.maximum(neg_lam, 0.0) + jnp.log1p(jnp.exp(-jnp.abs(neg_lam)))
    a = jnp.exp2(gate_a * ((-LRU_C * LOG2_E) * softplus))
    b_in = jnp.sqrt(1.0 - a * a) * gate_x * xc
    pos = t * tile + lax.broadcasted_iota(jnp.int32, (tile, 1), 0)
    b_in = jnp.where(pos >= PAD, b_in, 0.0)
    a_ref[...] = a
    b_ref[...] = b_in

    row = lax.broadcasted_iota(jnp.int32, (8, width), 0)

    def group(r, carry):
        rows = pl.ds(pl.multiple_of(r * 8, 8), 8)
        av = a_ref[rows, :]
        bv = b_ref[rows, :]
        for k in (1, 2, 4):
            a_sh = jnp.where(row >= k, pltpu.roll(av, k, 0), 1.0)
            b_sh = jnp.where(row >= k, pltpu.roll(bv, k, 0), 0.0)
            bv = av * b_sh + bv
            av = av * a_sh
        hv = av * carry + bv
        b_ref[rows, :] = hv
        return hv[7:8, :]

    carry_ref[...] = lax.fori_loop(0, tile // 8, group, carry_ref[...], unroll=LRU_SCAN_UNROLL)
    o_ref[...] = (b_ref[...] * _gelu_tanh(y_ref[...].astype(F32))).astype(o_ref.dtype)


def _lru(cols, conv_w, conv_b, wa_bf16, ba, wx_bf16, bx, lam, batch, lp):
    m = cols.shape[0]
    nt = lp // SEQ_TILE
    nhalf = LRU_WIDTH // LRU_HALF
    blocks_per_half = LRU_HALF // LRU_BLOCK
    vec = lambda v: v.reshape(1, LRU_WIDTH).astype(F32)
    vec_spec = pl.BlockSpec((1, LRU_HALF), lambda b, c, t: (0, c))
    w_spec = pl.BlockSpec((blocks_per_half, LRU_BLOCK, LRU_BLOCK), lambda b, c, t: (c, 0, 0))
    return pl.pallas_call(
        _lru_kernel,
        grid=(batch, nhalf, nt),
        in_specs=[
            pl.BlockSpec((SEQ_TILE, LRU_HALF), lambda b, c, t: (b * nt + t, OFF_LX // LRU_HALF + c)),
            pl.BlockSpec((SEQ_TILE, LRU_HALF), lambda b, c, t: (b * nt + t, OFF_LY // LRU_HALF + c)),
            pl.BlockSpec((CONV_WIDTH, LRU_HALF), lambda b, c, t: (0, c)),
            vec_spec, w_spec, vec_spec, w_spec, vec_spec, vec_spec,
        ],
        out_specs=pl.BlockSpec((SEQ_TILE, LRU_HALF), lambda b, c, t: (b * nt + t, c)),
        out_shape=jax.ShapeDtypeStruct((m, LRU_WIDTH), BF16),
        scratch_shapes=[
            pltpu.VMEM((SEQ_TILE + LRU_HALO, LRU_HALF), F32),
            pltpu.VMEM((SEQ_TILE, LRU_HALF), F32),
            pltpu.VMEM((SEQ_TILE, LRU_HALF), F32),
            pltpu.VMEM((1, LRU_HALF), F32),
        ],
        compiler_params=_cparams("parallel", "parallel", "arbitrary"),
        name="rglru",
    )(cols, cols, conv_w.astype(F32), vec(conv_b), wa_bf16, vec(ba), wx_bf16, vec(bx), vec(lam))


def _merge_kernel(p_ref, a_ref, r_ref, gp_ref, ga_ref, gr_ref, wp_ref, wa_ref, wr_ref, o_ref):
    acc = gp_ref[...].astype(F32) * jnp.dot(p_ref[...], wp_ref[...], preferred_element_type=F32)
    acc += ga_ref[...].astype(F32) * jnp.dot(a_ref[...], wa_ref[...], preferred_element_type=F32)
    acc += gr_ref[...].astype(F32) * jnp.dot(r_ref[...], wr_ref[...], preferred_element_type=F32)
    o_ref[...] = acc.astype(o_ref.dtype)


def _merge(pool_o, attn_o, lru_o, gates, wp, wa, wr, layer):
    m = pool_o.shape[0]
    bm, bn = MERGE_BM, MERGE_BN
    assert m % bm == 0 and D_MODEL % bn == 0
    x_spec = pl.BlockSpec((bm, POOL_WIDTH), lambda i, j: (i, 0))
    gate_spec = lambda k: pl.BlockSpec((bm, bn), lambda i, j: (i, k * D_MODEL // bn + j))
    w_spec = pl.BlockSpec((None, POOL_WIDTH, bn), lambda i, j: (layer, 0, j))
    return pl.pallas_call(
        _merge_kernel,
        grid=(m // bm, D_MODEL // bn),
        in_specs=[x_spec, x_spec, x_spec, gate_spec(0), gate_spec(1), gate_spec(2), w_spec, w_spec, w_spec],
        out_specs=pl.BlockSpec((bm, bn), lambda i, j: (i, j)),
        out_shape=jax.ShapeDtypeStruct((m, D_MODEL), BF16),
        compiler_params=_cparams("parallel", "arbitrary"),
        name="gated_merge",
    )(pool_o, attn_o, lru_o, gates, gates, gates, wp, wa, wr)


def _pack_bf16_pairs(yb):
    c = yb.shape[1] // 2
    lo = lax.bitcast_convert_type(yb[:, :c].astype(F32), jnp.uint32)
    hi = lax.bitcast_convert_type(yb[:, c:].astype(F32), jnp.uint32)
    return (hi & jnp.uint32(0xFFFF0000)) | (lo >> 16)


def _unpack_bf16_pairs(words, dtype=BF16):
    lo = lax.bitcast_convert_type(words << 16, F32)
    hi = lax.bitcast_convert_type(words & jnp.uint32(0xFFFF0000), F32)
    return jnp.concatenate([lo, hi], axis=1).astype(dtype)


def _route_tile(logits, valid, base):
    bm = logits.shape[0]
    lane = lax.broadcasted_iota(jnp.int32, (bm, ROUTE_W), 1)
    lane_f = lane.astype(F32)
    ninf = -jnp.inf
    big = float(ROUTE_W)

    gl = jnp.where(lane < N_GROUPS, logits, ninf)
    gmax = jnp.max(gl, axis=-1, keepdims=True)
    g = jnp.min(jnp.where(gl == gmax, lane_f, big), axis=-1, keepdims=True)
    p_g = 1.0 / jnp.sum(jnp.exp(gl - gmax), axis=-1, keepdims=True)

    first = N_GROUPS + g * EXPERTS_PER_GROUP
    sl = jnp.where((lane_f >= first) & (lane_f < first + EXPERTS_PER_GROUP), logits, ninf)
    m1 = jnp.max(sl, axis=-1, keepdims=True)
    i1 = jnp.min(jnp.where(sl == m1, lane_f, big), axis=-1, keepdims=True)
    ssum = jnp.sum(jnp.exp(sl - m1), axis=-1, keepdims=True)
    sl2 = jnp.where(lane_f == i1, ninf, sl)
    m2 = jnp.max(sl2, axis=-1, keepdims=True)
    i2 = jnp.min(jnp.where(sl2 == m2, lane_f, big), axis=-1, keepdims=True)
    p1 = 1.0 / ssum
    p2 = jnp.exp(m2 - m1) / ssum
    w1 = p_g * p1 / (p1 + p2)
    w2 = p_g * p2 / (p1 + p2)
    e1 = i1 - N_GROUPS
    e2 = i2 - N_GROUPS

    oh1 = (lane_f == e1) & valid
    oh2 = (lane_f == e2) & valid
    both = (oh1 | oh2).astype(F32)
    earlier = (lax.broadcasted_iota(jnp.int32, (bm, bm), 0) > lax.broadcasted_iota(jnp.int32, (bm, bm), 1))
    prefix = jnp.dot(earlier.astype(BF16), both.astype(BF16), preferred_element_type=F32) + base
    r1 = jnp.sum(jnp.where(oh1, prefix, 0.0), axis=-1, keepdims=True)
    r2 = jnp.sum(jnp.where(oh2, prefix, 0.0), axis=-1, keepdims=True)

    ri = jnp.where(lane == 0, e1, jnp.where(lane == 1, e2, jnp.where(lane == 2, r1, jnp.where(lane == 3, r2, 0.0))))
    rw = jnp.where(lane == 0, w1, jnp.where(lane == 1, w2, 0.0))
    return ri.astype(jnp.int32), rw, jnp.sum(both, axis=0, keepdims=True)


def _outproj_kernel(x_ref, w_ref, h_ref, g_ref, b_ref, rw_ref, rb_ref, hf_ref, hp_ref, ri_ref, rwt_ref, cnt_ref,
                    *, batch, lp):
    i = pl.program_id(0)
    bm = x_ref.shape[0]
    sub = bm // OUT_SPLIT

    @pl.when(i == 0)
    def _():
        cnt_ref[...] = jnp.zeros(cnt_ref.shape, F32)

    for s in range(OUT_SPLIT):
        rows = pl.ds(s * sub, sub)
        t = jnp.dot(x_ref[rows, :], w_ref[...], preferred_element_type=F32)
        y = _layer_norm(ALPHA * h_ref[rows, :] + t, g_ref[...], b_ref[...])
        valid = _flat_valid_rows(i * bm + s * sub, sub, batch, lp)
        y = jnp.where(valid, y, 0.0)
        yb = y.astype(BF16)
        hf_ref[rows, :] = y
        hp_ref[rows, :] = _pack_bf16_pairs(yb)
        logits = jnp.dot(yb, rw_ref[...], preferred_element_type=F32) + rb_ref[...]
        ri, rw, tile_cnt = _route_tile(logits, valid, cnt_ref[...])
        ri_ref[rows, :] = ri
        rwt_ref[rows, :] = rw
        cnt_ref[...] += tile_cnt


def _outproj_ln(merged, w_out_bf16, h, g, b, route_w, route_b, layer, batch, lp):
    m, d = h.shape
    bm = OUT_BM
    assert m % bm == 0
    row_spec = pl.BlockSpec((bm, d), lambda i: (i, 0))
    vec_spec = pl.BlockSpec((1, d), lambda i: (0, 0))
    route_spec = pl.BlockSpec((bm, ROUTE_W), lambda i: (i, 0))
    return pl.pallas_call(
        functools.partial(_outproj_kernel, batch=batch, lp=lp),
        grid=(m // bm,),
        in_specs=[
            row_spec,
            pl.BlockSpec((None, d, d), lambda i: (layer, 0, 0)),
            row_spec, vec_spec, vec_spec,
            pl.BlockSpec((d, ROUTE_W), lambda i: (0, 0)),
            pl.BlockSpec((1, ROUTE_W), lambda i: (0, 0)),
        ],
        out_specs=[row_spec, pl.BlockSpec((bm, d // 2), lambda i: (i, 0)), route_spec, route_spec,
                   pl.BlockSpec((1, ROUTE_W), lambda i: (0, 0))],
        out_shape=[jax.ShapeDtypeStruct((m, d), F32), jax.ShapeDtypeStruct((m, d // 2), jnp.uint32),
                   jax.ShapeDtypeStruct((m, ROUTE_W), jnp.int32), jax.ShapeDtypeStruct((m, ROUTE_W), F32),
                   jax.ShapeDtypeStruct((1, ROUTE_W), F32)],
        compiler_params=_cparams("arbitrary"),
        name="out_proj_ln",
    )(merged, w_out_bf16, h, g.reshape(1, d), b.reshape(1, d), route_w, route_b)


def _dispatch_kernel(start_ref, nchunk_ref, dest_hbm, hp_ref, xs_ref, idx_ref, zero_ref, idx_sem, row_sem, zero_sem,
                     *, n_expert_rows):
    i = pl.program_id(0)
    groups = hp_ref.shape[0]
    bm = groups * SUBLANES
    n_idx = TOP_K * bm

    @pl.when(i == 0)
    def _():
        zero_ref[...] = jnp.zeros(zero_ref.shape, zero_ref.dtype)

        def last_chunk(e):
            first = pl.multiple_of(start_ref[e] + (nchunk_ref[e] - 1) * MOE_BM, MOE_BM)
            return pltpu.make_async_copy(zero_ref, xs_ref.at[pl.ds(first, MOE_BM)], zero_sem)

        for e in range(N_EXPERTS):
            @pl.when(nchunk_ref[e] > 0)
            def _():
                last_chunk(e).start()

        for e in range(N_EXPERTS):
            @pl.when(nchunk_ref[e] > 0)
            def _():
                last_chunk(e).wait()

        used = start_ref[N_EXPERTS - 1] + nchunk_ref[N_EXPERTS - 1] * MOE_BM

        def tail_chunk(j):
            return pltpu.make_async_copy(
                zero_ref, xs_ref.at[pl.ds(pl.multiple_of(used + j * MOE_BM, MOE_BM), MOE_BM)], zero_sem)

        n_tail = (n_expert_rows - used) // MOE_BM
        lax.fori_loop(0, n_tail, lambda j, c: (tail_chunk(j).start(), c)[1], 0)
        lax.fori_loop(0, n_tail, lambda j, c: (tail_chunk(j).wait(), c)[1], 0)

    idx_copy = pltpu.make_async_copy(dest_hbm.at[pl.ds(pl.multiple_of(i * n_idx, n_idx), n_idx)], idx_ref, idx_sem)
    idx_copy.start()
    idx_copy.wait()

    def issue(g, carry):
        for s in range(SUBLANES):
            for k in range(TOP_K):
                slot = idx_ref[TOP_K * (g * SUBLANES + s) + k]
                pltpu.make_async_copy(hp_ref.at[g, pl.ds(s, 1)], xs_ref.at[pl.ds(slot, 1)],
                                      row_sem).start(priority=k % 2)
        return carry

    lax.fori_loop(0, groups, issue, 0)
    for k in range(TOP_K):
        pltpu.make_async_copy(xs_ref.at[pl.ds(0, bm)], xs_ref.at[pl.ds(0, bm)], row_sem).wait()


def _dispatch(hp, dest, chunk_start, n_chunks, n_expert_rows, n_slots):
    m, c = hp.shape
    bm = DISPATCH_BM
    assert m % bm == 0 and bm % SUBLANES == 0
    hp = hp.reshape(m // SUBLANES, SUBLANES, c)
    grid_spec = pltpu.PrefetchScalarGridSpec(
        num_scalar_prefetch=2,
        grid=(m // bm,),
        in_specs=[
            pl.BlockSpec(memory_space=pl.ANY),
            pl.BlockSpec((bm // SUBLANES, SUBLANES, c), lambda i, st, nc: (i, 0, 0)),
        ],
        out_specs=pl.BlockSpec(memory_space=pl.ANY),
        scratch_shapes=[pltpu.SMEM((TOP_K * bm,), jnp.int32), pltpu.VMEM((MOE_BM, c), jnp.uint32),
                        pltpu.SemaphoreType.DMA(()), pltpu.SemaphoreType.DMA(()), pltpu.SemaphoreType.DMA(())],
    )
    return pl.pallas_call(
        functools.partial(_dispatch_kernel, n_expert_rows=n_expert_rows),
        grid_spec=grid_spec,
        out_shape=jax.ShapeDtypeStruct((n_slots, c), jnp.uint32),
        compiler_params=_cparams("arbitrary"),
        name="moe_dispatch",
    )(chunk_start, n_chunks, dest.reshape(m * TOP_K), hp)


def _expert_kernel(start_ref, nchunk_ref, xs_hbm, wg_hbm, wu_hbm, wd_hbm, yb_hbm, wg_f, wu_f, wd_f, wg_s, wu_s, wd_s,
                   x_buf, o_buf, w_sem, in_sem, out_sem, *, layer):
    e = pl.program_id(0)
    n = nchunk_ref[e]
    bm = MOE_BM
    first = start_ref[e] // bm
    total = start_ref[N_EXPERTS - 1] // bm + nchunk_ref[N_EXPERTS - 1]
    wslot = e % 2

    def w_copies(expert, slot):
        return [pltpu.make_async_copy(src.at[layer, expert], dst.at[slot], w_sem.at[slot])
                for src, dst in ((wg_hbm, wg_f), (wu_hbm, wu_f), (wd_hbm, wd_f))]

    def rows(g):
        return pl.ds(pl.multiple_of(g * bm, bm), bm)

    def x_copy(g):
        slot = g % EXPERT_X_BUFFERS
        return pltpu.make_async_copy(xs_hbm.at[rows(g)], x_buf.at[slot], in_sem.at[slot])

    def o_copy(g):
        slot = g % 2
        return pltpu.make_async_copy(o_buf.at[slot], yb_hbm.at[rows(g)], out_sem.at[slot])

    @pl.when(e == 0)
    def _():
        for ahead in range(EXPERT_X_BUFFERS - 1):
            @pl.when(ahead < total)
            def _():
                x_copy(ahead).start()

        for c in w_copies(0, 0):
            c.start(priority=1)

    @pl.when(e + 1 < N_EXPERTS)
    def _():
        for c in w_copies(e + 1, 1 - wslot):
            c.start(priority=1)

    for c in w_copies(e, wslot):
        c.wait()

    @pl.when(n > 0)
    def _():
        wg_s[...] = wg_f[wslot].astype(BF16)
        wu_s[...] = wu_f[wslot].astype(BF16)
        wd_s[...] = wd_f[wslot].astype(BF16)

        def chunk(j, carry):
            g = first + j
            x_copy(g).wait()

            @pl.when(g + EXPERT_X_BUFFERS - 1 < total)
            def _():
                x_copy(g + EXPERT_X_BUFFERS - 1).start()

            @pl.when(g >= 2)
            def _():
                o_copy(g - 2).wait()

            x = _unpack_bf16_pairs(x_buf[g % EXPERT_X_BUFFERS])
            gate = jnp.dot(x, wg_s[...], preferred_element_type=F32)
            up = jnp.dot(x, wu_s[...], preferred_element_type=F32)
            hdn = (gate * _sigmoid(gate)) * up
            y = jnp.dot(hdn.astype(BF16), wd_s[...], preferred_element_type=F32)
            o_buf[g % 2] = _pack_bf16_pairs(y.astype(BF16))
            o_copy(g).start()
            return carry

        lax.fori_loop(0, n, chunk, 0)

    @pl.when(e == N_EXPERTS - 1)
    def _():
        for back in (2, 1):
            @pl.when(total >= back)
            def _():
                o_copy(total - back).wait()

        o_buf[0] = jnp.zeros(o_buf.shape[1:], o_buf.dtype)

        def fill(g, carry):
            tail = pltpu.make_async_copy(o_buf.at[0], yb_hbm.at[rows(g)], out_sem.at[0])
            tail.start()
            tail.wait()
            return carry

        lax.fori_loop(total, yb_hbm.shape[0] // bm, fill, 0)


def _experts(xs, n_rows, chunk_start, n_chunks, w_gate, w_up, w_down, layer):
    d = D_MODEL
    bm = MOE_BM
    any_spec = pl.BlockSpec(memory_space=pl.ANY)
    grid_spec = pltpu.PrefetchScalarGridSpec(
        num_scalar_prefetch=2,
        grid=(N_EXPERTS,),
        in_specs=[any_spec, any_spec, any_spec, any_spec],
        out_specs=any_spec,
        scratch_shapes=[
            pltpu.VMEM((2, d, D_EXPERT), F32),
            pltpu.VMEM((2, d, D_EXPERT), F32),
            pltpu.VMEM((2, D_EXPERT, d), F32),
            pltpu.VMEM((d, D_EXPERT), BF16),
            pltpu.VMEM((d, D_EXPERT), BF16),
            pltpu.VMEM((D_EXPERT, d), BF16),
            pltpu.VMEM((EXPERT_X_BUFFERS, bm, d // 2), jnp.uint32),
            pltpu.VMEM((2, bm, d // 2), jnp.uint32),
            pltpu.SemaphoreType.DMA((2,)),
            pltpu.SemaphoreType.DMA((EXPERT_X_BUFFERS,)),
            pltpu.SemaphoreType.DMA((2,)),
        ],
    )
    return pl.pallas_call(
        functools.partial(_expert_kernel, layer=layer),
        grid_spec=grid_spec,
        out_shape=jax.ShapeDtypeStruct((n_rows, d // 2), jnp.uint32),
        compiler_params=_cparams("arbitrary"),
        name="expert_mlp",
    )(chunk_start, n_chunks, xs, w_gate, w_up, w_down)


def _combine_ln_kernel(dest_hbm, h_ref, rw_ref, g_ref, b_ref, yb_hbm, *rest, batch, lp, final):
    if final:
        out_hbm, idx_ref, rows_ref, y_buf, idx_sem, row_sem, out_sem = rest
    else:
        hf_ref, hb_ref, idx_ref, rows_ref, idx_sem, row_sem = rest
    i = pl.program_id(0)
    n_tiles = pl.num_programs(0)
    bm = h_ref.shape[0]
    n_idx = TOP_K * bm
    buf = i % 2

    def idx_copy(tile):
        half = pl.ds(pl.multiple_of((tile % 2) * n_idx, n_idx), n_idx)
        return pltpu.make_async_copy(
            dest_hbm.at[pl.ds(pl.multiple_of(tile * n_idx, n_idx), n_idx)], idx_ref.at[half], idx_sem.at[tile % 2])

    def start_gather(tile):
        into = tile % 2
        idx_copy(tile).wait()

        def issue(g, carry):
            base = into * n_idx + g * (TOP_K * SUBLANES)
            for s in range(SUBLANES):
                for k in range(TOP_K):
                    slot = idx_ref[base + (TOP_K * s + k)]
                    pltpu.make_async_copy(yb_hbm.at[pl.ds(slot, 1)], rows_ref.at[into, k, g, pl.ds(s, 1)],
                                          row_sem.at[into]).start(priority=k % 2)
            return carry

        lax.fori_loop(0, bm // SUBLANES, issue, 0)

        @pl.when(tile + 2 < n_tiles)
        def _():
            idx_copy(tile + 2).start()

    @pl.when(i == 0)
    def _():
        idx_copy(0).start()

        @pl.when(1 < n_tiles)
        def _():
            idx_copy(1).start()

        start_gather(0)

    @pl.when(i + 1 < n_tiles)
    def _():
        start_gather(i + 1)

    for k in range(TOP_K):
        pltpu.make_async_copy(yb_hbm.at[pl.ds(0, bm)], yb_hbm.at[pl.ds(0, bm)], row_sem.at[buf]).wait()

    d = h_ref.shape[1]

    def expert_rows(k):
        return _unpack_bf16_pairs(rows_ref[buf, k].reshape(bm, d // 2), F32)

    ffn = rw_ref[:, 0:1] * expert_rows(0)
    for k in range(1, TOP_K):
        ffn = ffn + rw_ref[:, k:k + 1] * expert_rows(k)
    y = _layer_norm(ALPHA * h_ref[...] + ffn, g_ref[...], b_ref[...])
    if not final:
        y = jnp.where(_flat_valid_rows(i * bm, bm, batch, lp), y, 0.0)
        hf_ref[...] = y
        hb_ref[...] = y.astype(BF16)
        return

    nb = lp // BLOCK

    def for_each_out_block(step, fn):
        for j in range(bm // BLOCK):
            blk = step * (bm // BLOCK) + j
            seq_blk = blk % nb

            @pl.when(seq_blk >= 1)
            def _():
                dst = pl.multiple_of(((blk // nb) * (nb - 1) + seq_blk - 1) * BLOCK, BLOCK)
                fn(pltpu.make_async_copy(y_buf.at[pl.ds(j * BLOCK, BLOCK)], out_hbm.at[pl.ds(dst, BLOCK)], out_sem))

    @pl.when(i > 0)
    def _():
        for_each_out_block(i - 1, lambda c: c.wait())

    y_buf[...] = y
    for_each_out_block(i, lambda c: c.start(priority=1))

    @pl.when(i == n_tiles - 1)
    def _():
        for_each_out_block(i, lambda c: c.wait())


def _combine_ln(h, yb, dest, rw, g, b, batch, lp, final):
    m, d = h.shape
    bm = COMBINE_BM
    assert m % bm == 0 and bm % BLOCK == 0
    row_spec = pl.BlockSpec((bm, d), lambda i: (i, 0))
    vec_spec = pl.BlockSpec((1, d), lambda i: (0, 0))
    scratch = [pltpu.SMEM((2 * TOP_K * bm,), jnp.int32),
               pltpu.VMEM((2, TOP_K, bm // SUBLANES, SUBLANES, d // 2), jnp.uint32)]
    sems = [pltpu.SemaphoreType.DMA((2,)), pltpu.SemaphoreType.DMA((2,))]
    if final:
        out_specs = pl.BlockSpec(memory_space=pl.ANY)
        out_shape = jax.ShapeDtypeStruct((batch * (lp - BLOCK), d), F32)
        scratch = scratch + [pltpu.VMEM((bm, d), F32)] + sems + [pltpu.SemaphoreType.DMA(())]
    else:
        out_specs = [row_spec, row_spec]
        out_shape = [jax.ShapeDtypeStruct((m, d), F32), jax.ShapeDtypeStruct((m, d), BF16)]
        scratch = scratch + sems
    return pl.pallas_call(
        functools.partial(_combine_ln_kernel, batch=batch, lp=lp, final=final),
        grid=(m // bm,),
        in_specs=[
            pl.BlockSpec(memory_space=pl.ANY),
            row_spec,
            pl.BlockSpec((bm, ROUTE_W), lambda i: (i, 0)),
            vec_spec, vec_spec,
            pl.BlockSpec(memory_space=pl.ANY),
        ],
        out_specs=out_specs,
        out_shape=out_shape,
        scratch_shapes=scratch,
        compiler_params=_cparams("arbitrary"),
        name="moe_combine_out" if final else "moe_combine_ln",
    )(dest.reshape(m * TOP_K), h, rw, g.reshape(1, d), b.reshape(1, d), yb)


def _slot_tables(ri, cnt, batch, lp):
    m = ri.shape[0]
    bm = MOE_BM
    eid = ri[:, 0:TOP_K]
    rank = ri[:, TOP_K:2 * TOP_K]
    counts = cnt[0, :N_EXPERTS].astype(jnp.int32)
    padded = (counts + bm - 1) // bm * bm
    pad_end = jnp.cumsum(padded)
    pad_start = pad_end - padded
    n_real = batch * (lp - PAD) * TOP_K
    nblk = -(-(n_real + N_EXPERTS * (bm - 1)) // bm)
    cap = nblk * bm
    experts = jnp.arange(N_EXPERTS, dtype=jnp.int32)
    start = jnp.sum(jnp.where(eid[:, :, None] == experts[None, None, :], pad_start[None, None, :], 0), axis=-1)
    row = jnp.arange(m, dtype=jnp.int32)
    pos = row % lp
    spare = cap + ((row // lp) * PAD + pos)[:, None] * TOP_K + jnp.arange(TOP_K, dtype=jnp.int32)[None, :]
    valid = (pos >= PAD)[:, None]
    dest = jnp.where(valid, start + rank, spare)
    dest_read = jnp.where(valid, dest, 0)
    n_slots = cap + batch * PAD * TOP_K
    return dest, dest_read, pad_start.astype(jnp.int32), (padded // bm).astype(jnp.int32), cap, n_slots


def kernel(x, meta, ln_emb_g, ln_emb_b, w_in, pool_w, pool_scale, attn_sink, conv_w, conv_b, lru_wa, lru_ba,
           lru_wx, lru_bx, lru_lambda, proj_pool, proj_attn, proj_lru, w_out, ln1_g, ln1_b, router_grp_w,
           router_grp_b, router_exp_w, router_exp_b, exp_w_gate, exp_w_up, exp_w_down, ln2_g, ln2_b):
    batch, seq, d = x.shape
    lp = PAD + N_META + seq
    m = batch * lp

    hf, hb = _embed(x, meta, ln_emb_g, ln_emb_b)
    w_in_b = w_in.astype(BF16)
    pool_w_b = pool_w.astype(BF16)
    wa_b = lru_wa.astype(BF16)
    wx_b = lru_wx.astype(BF16)
    wp_b = proj_pool.astype(BF16)
    wat_b = proj_attn.astype(BF16)
    wl_b = proj_lru.astype(BF16)
    w_out_b = w_out.astype(BF16)
    route_pad = ROUTE_W - N_GROUPS - N_EXPERTS

    for l in range(DEPTH):
        cols = _inproj(hb, w_in_b, l, gates=False)
        gates = _inproj(hb, w_in_b, l, gates=True)
        pool_o = _pool(cols, pool_w_b[l], pool_scale[l], batch, lp)
        attn_o = _attention(cols, attn_sink[l], batch, lp)
        lru_o = _lru(cols, conv_w[l], conv_b[l], wa_b[l], lru_ba[l], wx_b[l], lru_bx[l], lru_lambda[l],
                     batch, lp)
        merged = _merge(pool_o, attn_o, lru_o, gates, wp_b, wat_b, wl_b, l)
        route_w = jnp.concatenate(
            [router_grp_w[l], router_exp_w[l], jnp.zeros((d, route_pad), F32)], axis=1).astype(BF16)
        route_b = jnp.concatenate(
            [router_grp_b[l], router_exp_b[l], jnp.zeros((route_pad,), F32)]).reshape(1, ROUTE_W)
        h1f, h1p, ri, rw, cnt = _outproj_ln(merged, w_out_b, hf, ln1_g[l], ln1_b[l], route_w, route_b, l,
                                            batch, lp)
        dest, dest_read, chunk_start, n_chunks, cap, n_slots = _slot_tables(ri, cnt, batch, lp)
        xs = _dispatch(h1p, dest, chunk_start, n_chunks, cap, n_slots)
        yb = _experts(xs, cap, chunk_start, n_chunks, exp_w_gate, exp_w_up, exp_w_down, l)
        if l + 1 < DEPTH:
            hf, hb = _combine_ln(h1f, yb, dest_read, rw, ln2_g[l], ln2_b[l], batch, lp, final=False)
        else:
            out = _combine_ln(h1f, yb, dest_read, rw, ln2_g[l], ln2_b[l], batch, lp, final=True)

    return out.reshape(batch, seq, d)
```

```python
import functools

import jax
import jax.numpy as jnp
from jax import lax
from jax.experimental import pallas as pl
from jax.experimental.pallas import tpu as pltpu

F32 = jnp.float32
BF16 = jnp.bfloat16

D_MODEL = 2048
DEPTH = 2
N_META = 16
POOL_WINDOWS = (2, 4, 8, 16)
POOL_WIDTH = D_MODEL // 2
POOL_GROUP = POOL_WIDTH // len(POOL_WINDOWS)
N_HEADS = 16
N_KV_HEADS = 4
HEAD_DIM = 64
Q_PER_KV = N_HEADS // N_KV_HEADS
WINDOW = 128
BLOCK = 128
NEG = -1e30
LRU_WIDTH = D_MODEL // 2
LRU_BLOCKS = 4
LRU_BLOCK = LRU_WIDTH // LRU_BLOCKS
CONV_WIDTH = 4
LRU_C = 8.0
N_GROUPS = 4
EXPERTS_PER_GROUP = 8
N_EXPERTS = N_GROUPS * EXPERTS_PER_GROUP
TOP_K = 2
D_EXPERT = D_MODEL // 4
LN_EPS = 1e-5
ALPHA = (2.0 * DEPTH) ** 0.25

PAD = BLOCK - N_META
ATT_W = N_HEADS * HEAD_DIM
KV_W = N_KV_HEADS * HEAD_DIM
OFF_POOL = 0
OFF_Q = OFF_POOL + POOL_WIDTH
OFF_K = OFF_Q + ATT_W
OFF_V = OFF_K + KV_W
OFF_LX = OFF_V + KV_W
OFF_LY = OFF_LX + LRU_WIDTH
OFF_GATE = OFF_LY + LRU_WIDTH
IN_COLS = OFF_GATE + 3 * D_MODEL

VMEM_LIMIT_BYTES = 56 * 1024 * 1024
SUBLANES = 8

SEQ_TILE = 3 * BLOCK
INPROJ_BM = 1536
INPROJ_BN = 1536
MERGE_BM = 768
MERGE_BN = 1024
OUT_BM = 512
OUT_SPLIT = 2
MOE_BM = 256
DISPATCH_BM = 1536
COMBINE_BM = 512
EXPERT_X_BUFFERS = 3
ROUTE_W = 128


def _cparams(*sem):
    return pltpu.CompilerParams(dimension_semantics=sem, vmem_limit_bytes=VMEM_LIMIT_BYTES)


def _layer_norm(x, g, b):
    mu = jnp.mean(x, axis=-1, keepdims=True)
    xc = x - mu
    var = jnp.mean(xc * xc, axis=-1, keepdims=True)
    return xc * lax.rsqrt(var + LN_EPS) * g + b


def _flat_valid_rows(row0, n_rows, batch, lp):
    r = row0 + lax.broadcasted_iota(jnp.int32, (n_rows, 1), 0)
    pad_row = jnp.zeros((n_rows, 1), jnp.bool_)
    for b in range(batch):
        pad_row = pad_row | ((r >= b * lp) & (r < b * lp + PAD))
    return jnp.logical_not(pad_row)


EMBED_BLOCKS = SEQ_TILE // BLOCK


def _embed_kernel(*refs):
    x_refs, (meta_ref, g_ref, b_ref, hf_ref, hb_ref) = refs[:EMBED_BLOCKS], refs[EMBED_BLOCKS:]
    t = pl.program_id(1)
    for j in range(EMBED_BLOCKS):
        src = x_refs[j][...]
        if j == 0:
            src = jnp.where(t == 0, meta_ref[...], src)
        y = _layer_norm(src, g_ref[...], b_ref[...])
        if j == 0:
            row = lax.broadcasted_iota(jnp.int32, (BLOCK, 1), 0)
            y = jnp.where((t > 0) | (row >= PAD), y, 0.0)
        hf_ref[pl.ds(j * BLOCK, BLOCK), :] = y
        hb_ref[pl.ds(j * BLOCK, BLOCK), :] = y.astype(BF16)


def _embed(x, meta, g, b):
    batch, seq, d = x.shape
    nblk = seq // BLOCK
    lp = (nblk + 1) * BLOCK
    nt = lp // SEQ_TILE
    m = batch * lp
    meta_tile = jnp.concatenate([jnp.zeros((PAD, d), F32), meta.astype(F32)], axis=0)
    row_spec = pl.BlockSpec((SEQ_TILE, d), lambda bi, t: (bi * nt + t, 0))
    vec_spec = pl.BlockSpec((1, d), lambda bi, t: (0, 0))
    x_spec = lambda j: pl.BlockSpec(
        (BLOCK, d), lambda bi, t: (bi * nblk + jnp.maximum(EMBED_BLOCKS * t + j - 1, 0), 0))
    x2 = x.reshape(batch * seq, d)
    return pl.pallas_call(
        _embed_kernel,
        grid=(batch, nt),
        in_specs=[x_spec(j) for j in range(EMBED_BLOCKS)]
        + [pl.BlockSpec((BLOCK, d), lambda bi, t: (0, 0)), vec_spec, vec_spec],
        out_specs=[row_spec, row_spec],
        out_shape=[jax.ShapeDtypeStruct((m, d), F32), jax.ShapeDtypeStruct((m, d), BF16)],
        compiler_params=_cparams("parallel", "arbitrary"),
        name="embed_ln",
    )(*([x2] * EMBED_BLOCKS), meta_tile, g.reshape(1, d), b.reshape(1, d))


def _sigmoid(x):
    return 0.5 * jnp.tanh(0.5 * x) + 0.5


def _inproj_kernel(x_ref, w_ref, o_ref, *, gates):
    acc = jnp.dot(x_ref[...], w_ref[...], preferred_element_type=F32)
    o_ref[...] = (_sigmoid(acc) if gates else acc).astype(o_ref.dtype)


def _inproj(hb, w_in_bf16, layer, gates):
    m, d = hb.shape
    bm, bn = INPROJ_BM, INPROJ_BN
    col0, width = (OFF_GATE, IN_COLS - OFF_GATE) if gates else (0, OFF_GATE)
    assert m % bm == 0 and width % bn == 0 and col0 % bn == 0
    return pl.pallas_call(
        functools.partial(_inproj_kernel, gates=gates),
        grid=(m // bm, width // bn),
        in_specs=[
            pl.BlockSpec((bm, d), lambda i, j: (i, 0)),
            pl.BlockSpec((None, d, bn), lambda i, j: (layer, 0, col0 // bn + j)),
        ],
        out_specs=pl.BlockSpec((bm, bn), lambda i, j: (i, j)),
        out_shape=jax.ShapeDtypeStruct((m, width), BF16),
        compiler_params=_cparams("parallel", "arbitrary"),
        name="in_proj_gates" if gates else "in_proj",
    )(hb, w_in_bf16)


POOL_HALO = 2 * max(POOL_WINDOWS)
assert POOL_WINDOWS == tuple(2 ** (g + 1) for g in range(len(POOL_WINDOWS)))


def _pool_kernel(u_ref, w_ref, scale_ref, o_ref, ext_ref, lvl_ref):
    t = pl.program_id(1)
    tile = SEQ_TILE
    halo = POOL_HALO

    @pl.when(t == 0)
    def _():
        ext_ref[pl.ds(0, halo), :] = jnp.zeros((halo, POOL_WIDTH), F32)

    @pl.when(t > 0)
    def _():
        ext_ref[pl.ds(0, halo), :] = ext_ref[pl.ds(tile, halo), :]

    ext_ref[pl.ds(halo, tile), :] = u_ref[...].astype(F32)

    pos = t * tile + lax.broadcasted_iota(jnp.int32, (tile, 1), 0) - PAD
    src, first = ext_ref, 0
    for gi, w in enumerate(POOL_WINDOWS):
        lane0 = gi * POOL_GROUP
        lanes = pl.ds(lane0, POOL_WIDTH - lane0)
        new_first = -(-(first + w // 2) // SUBLANES) * SUBLANES
        n_rows = tile + halo - new_first
        level = src[pl.ds(new_first, n_rows), lanes] + src[pl.ds(new_first - w // 2, n_rows), lanes]
        win = level[halo - new_first:, :POOL_GROUP]
        cols = pl.ds(lane0, POOL_GROUP)
        u = ext_ref[pl.ds(halo, tile), cols]
        cnt = jnp.clip(pos + 1, 1, w).astype(F32)
        delta = win / cnt - u
        mixed = jnp.dot(delta.astype(BF16), w_ref[gi], preferred_element_type=F32)
        o_ref[:, cols] = (mixed * scale_ref[:, cols]).astype(o_ref.dtype)
        if gi + 1 < len(POOL_WINDOWS):
            lvl_ref[gi, pl.ds(new_first, n_rows), lanes] = level
            src, first = lvl_ref.at[gi], new_first


def _pool(cols, pool_w_bf16, pool_scale, batch, lp):
    m = cols.shape[0]
    nt = lp // SEQ_TILE
    maxw = POOL_HALO
    return pl.pallas_call(
        _pool_kernel,
        grid=(batch, nt),
        in_specs=[
            pl.BlockSpec((SEQ_TILE, POOL_WIDTH), lambda b, t: (b * nt + t, OFF_POOL // POOL_WIDTH)),
            pl.BlockSpec((len(POOL_WINDOWS), POOL_GROUP, POOL_GROUP), lambda b, t: (0, 0, 0)),
            pl.BlockSpec((1, POOL_WIDTH), lambda b, t: (0, 0)),
        ],
        out_specs=pl.BlockSpec((SEQ_TILE, POOL_WIDTH), lambda b, t: (b * nt + t, 0)),
        out_shape=jax.ShapeDtypeStruct((m, POOL_WIDTH), BF16),
        scratch_shapes=[pltpu.VMEM((SEQ_TILE + maxw, POOL_WIDTH), F32),
                        pltpu.VMEM((len(POOL_WINDOWS) - 1, SEQ_TILE + maxw, POOL_WIDTH), F32)],
        compiler_params=_cparams("parallel", "arbitrary"),
        name="pool_mixer",
    )(cols, pool_w_bf16, pool_scale.reshape(1, POOL_WIDTH))


def _attn_bias():
    kj = jnp.arange(2 * BLOCK, dtype=jnp.int32)[:, None]
    qi = jnp.arange(BLOCK, dtype=jnp.int32)[None, :]
    dist = BLOCK + qi - kj
    in_window = (dist >= 0) & (dist < WINDOW)
    slopes = 2.0 ** (-8.0 * jnp.arange(1, N_HEADS + 1, dtype=F32) / N_HEADS)
    alibi = -slopes[:, None, None] * dist.astype(F32)[None]
    bias = jnp.where(in_window[None], alibi, NEG).reshape(N_KV_HEADS, Q_PER_KV, 2 * BLOCK, BLOCK)
    return bias.transpose(0, 2, 1, 3).reshape(N_KV_HEADS, 2 * BLOCK, Q_PER_KV * BLOCK)


def _attn_kernel(q_ref, kp_ref, kc_ref, vp_ref, vc_ref, bias_ref, sink_ref, o_ref):
    n = pl.program_id(1)
    q = q_ref[...] * (HEAD_DIM ** -0.5)

    def heads(early_keys):
        for kh in range(N_KV_HEADS):
            hs = pl.ds(kh * HEAD_DIM, HEAD_DIM)
            k2 = jnp.concatenate([kp_ref[:, hs], kc_ref[:, hs]], axis=0)
            v2 = jnp.concatenate([vp_ref[:, hs], vc_ref[:, hs]], axis=0)
            first = kh * Q_PER_KV
            qg = jnp.concatenate([q[:, (first + g) * HEAD_DIM:(first + g + 1) * HEAD_DIM]
                                  for g in range(Q_PER_KV)], axis=0)
            s = lax.dot_general(k2, qg, (((1,), (1,)), ((), ())), preferred_element_type=F32)
            s = s + bias_ref[kh]
            if early_keys is not None:
                s = s + early_keys
            sk = sink_ref[pl.ds(kh, 1), :]
            mx = jnp.maximum(jnp.max(s, axis=0, keepdims=True), sk)
            p = jnp.exp(s - mx)
            den = jnp.sum(p, axis=0, keepdims=True) + jnp.exp(sk - mx)
            pn = (p * (1.0 / den)).astype(BF16)
            o = lax.dot_general(pn, v2, (((0,), (0,)), ((), ())), preferred_element_type=F32)
            for g in range(Q_PER_KV):
                o_ref[:, pl.ds((first + g) * HEAD_DIM, HEAD_DIM)] = o[g * BLOCK:(g + 1) * BLOCK].astype(o_ref.dtype)

    @pl.when(n < 2)
    def _():
        k_pos = (n - 1) * BLOCK + lax.broadcasted_iota(jnp.int32, (2 * BLOCK, 1), 0)
        heads(jnp.where(k_pos < PAD, NEG, 0.0))

    @pl.when(n >= 2)
    def _():
        heads(None)


def _attention(cols, sink, batch, lp):
    m = cols.shape[0]
    nb = lp // BLOCK
    cur = lambda cb: (lambda b, n: (b * nb + n, cb))
    prev = lambda cb: (lambda b, n: (b * nb + jnp.maximum(n - 1, 0), cb))
    return pl.pallas_call(
        _attn_kernel,
        grid=(batch, nb),
        in_specs=[
            pl.BlockSpec((BLOCK, ATT_W), cur(OFF_Q // ATT_W)),
            pl.BlockSpec((BLOCK, KV_W), prev(OFF_K // KV_W)),
            pl.BlockSpec((BLOCK, KV_W), cur(OFF_K // KV_W)),
            pl.BlockSpec((BLOCK, KV_W), prev(OFF_V // KV_W)),
            pl.BlockSpec((BLOCK, KV_W), cur(OFF_V // KV_W)),
            pl.BlockSpec((N_KV_HEADS, 2 * BLOCK, Q_PER_KV * BLOCK), lambda b, n: (0, 0, 0)),
            pl.BlockSpec((N_KV_HEADS, Q_PER_KV * BLOCK), lambda b, n: (0, 0)),
        ],
        out_specs=pl.BlockSpec((BLOCK, ATT_W), lambda b, n: (b * nb + n, 0)),
        out_shape=jax.ShapeDtypeStruct((m, ATT_W), BF16),
        compiler_params=_cparams("parallel", "arbitrary"),
        name="swa_attention",
    )(cols, cols, cols, cols, cols, _attn_bias(),
      jnp.repeat(sink.astype(F32).reshape(N_KV_HEADS, Q_PER_KV), BLOCK, axis=1))


LRU_HALF = LRU_WIDTH // 2
LRU_HALO = 8
LRU_SCAN_UNROLL = 6


LOG2_E = 1.4426950408889634
GELU_C = 0.7978845608028654


def _gelu_tanh(x):
    half = 0.5 * x
    return half + half * jnp.tanh(x * (GELU_C + (GELU_C * 0.044715) * (x * x)))


def _lru_kernel(*refs):
    nh = LRU_WIDTH // LRU_HALF
    x_refs, y_refs = refs[:nh], refs[nh:2 * nh]
    (cw_ref, cb_ref, wa_ref, ba_ref, wx_ref, bx_ref, lam_ref, o_ref,
     ext_ref, a_ref, b_ref, carry_ref) = refs[2 * nh:]
    t = pl.program_id(1)
    tile = SEQ_TILE
    width = LRU_HALF
    pos = t * tile + lax.broadcasted_iota(jnp.int32, (tile, 1), 0)
    row = lax.broadcasted_iota(jnp.int32, (8, width), 0)

    for c in range(nh):
        lanes = pl.ds(c * width, width)
        ext, a_s, b_s, carry = ext_ref.at[c], a_ref.at[c], b_ref.at[c], carry_ref.at[c]

        @pl.when(t == 0)
        def _():
            ext[pl.ds(0, LRU_HALO), :] = jnp.zeros((LRU_HALO, width), F32)
            carry[...] = jnp.zeros((1, width), F32)

        @pl.when(t > 0)
        def _():
            ext[pl.ds(0, LRU_HALO), :] = ext[pl.ds(tile, LRU_HALO), :]

        ext[pl.ds(LRU_HALO, tile), :] = x_refs[c][...].astype(F32)

        xc = cb_ref[:, lanes] + cw_ref[pl.ds(CONV_WIDTH - 1, 1), lanes] * ext[pl.ds(LRU_HALO, tile), :]
        for j in range(CONV_WIDTH - 1):
            shift = CONV_WIDTH - 1 - j
            xc = xc + cw_ref[pl.ds(j, 1), lanes] * ext[pl.ds(LRU_HALO - shift, tile), :]

        xcb = xc.astype(BF16)
        ga_parts, gx_parts = [], []
        for blk in range(width // LRU_BLOCK):
            xb = xcb[:, blk * LRU_BLOCK:(blk + 1) * LRU_BLOCK]
            w_idx = c * (width // LRU_BLOCK) + blk
            ga_parts.append(jnp.dot(xb, wa_ref[w_idx], preferred_element_type=F32))
            gx_parts.append(jnp.dot(xb, wx_ref[w_idx], preferred_element_type=F32))
        gate_a = _sigmoid(jnp.concatenate(ga_parts, axis=1) + ba_ref[:, lanes])
        gate_x = _sigmoid(jnp.concatenate(gx_parts, axis=1) + bx_ref[:, lanes])

        neg_lam = -lam_ref[:, lanes]
        softplus = jnp.maximum(neg_lam, 0.0) + jnp.log1p(jnp.exp(-jnp.abs(neg_lam)))
        a = jnp.exp2(gate_a * ((-LRU_C * LOG2_E) * softplus))
        b_in = jnp.sqrt(1.0 - a * a) * gate_x * xc
        b_in = jnp.where(pos >= PAD, b_in, 0.0)
        a_s[...] = a
        b_s[...] = b_in

        def group(r, h_prev, a_s=a_s, b_s=b_s):
            rows = pl.ds(pl.multiple_of(r * 8, 8), 8)
            av = a_s[rows, :]
            bv = b_s[rows, :]
            for k in (1, 2, 4):
                a_sh = jnp.where(row >= k, pltpu.roll(av, k, 0), 1.0)
                b_sh = jnp.where(row >= k, pltpu.roll(bv, k, 0), 0.0)
                bv = av * b_sh + bv
                av = av * a_sh
            hv = av * h_prev + bv
            b_s[rows, :] = hv
            return hv[7:8, :]

        carry[...] = lax.fori_loop(0, tile // 8, group, carry[...], unroll=LRU_SCAN_UNROLL)
        o_ref[:, lanes] = (b_s[...] * _gelu_tanh(y_refs[c][...].astype(F32))).astype(o_ref.dtype)


def _lru(cols, conv_w, conv_b, wa_bf16, ba, wx_bf16, bx, lam, batch, lp):
    m = cols.shape[0]
    nt = lp // SEQ_TILE
    nh = LRU_WIDTH // LRU_HALF
    vec = lambda v: v.reshape(1, LRU_WIDTH).astype(F32)
    vec_spec = pl.BlockSpec((1, LRU_WIDTH), lambda b, t: (0, 0))
    w_spec = pl.BlockSpec((LRU_BLOCKS, LRU_BLOCK, LRU_BLOCK), lambda b, t: (0, 0, 0))
    half_spec = lambda off, c: pl.BlockSpec((SEQ_TILE, LRU_HALF), lambda b, t: (b * nt + t, off // LRU_HALF + c))
    return pl.pallas_call(
        _lru_kernel,
        grid=(batch, nt),
        in_specs=[half_spec(OFF_LX, c) for c in range(nh)] + [half_spec(OFF_LY, c) for c in range(nh)]
        + [pl.BlockSpec((CONV_WIDTH, LRU_WIDTH), lambda b, t: (0, 0)),
           vec_spec, w_spec, vec_spec, w_spec, vec_spec, vec_spec],
        out_specs=pl.BlockSpec((SEQ_TILE, LRU_WIDTH), lambda b, t: (b * nt + t, 0)),
        out_shape=jax.ShapeDtypeStruct((m, LRU_WIDTH), BF16),
        scratch_shapes=[
            pltpu.VMEM((nh, SEQ_TILE + LRU_HALO, LRU_HALF), F32),
            pltpu.VMEM((nh, SEQ_TILE, LRU_HALF), F32),
            pltpu.VMEM((nh, SEQ_TILE, LRU_HALF), F32),
            pltpu.VMEM((nh, 1, LRU_HALF), F32),
        ],
        compiler_params=_cparams("parallel", "arbitrary"),
        name="rglru",
    )(*([cols] * (2 * nh)), conv_w.astype(F32), vec(conv_b), wa_bf16, vec(ba), wx_bf16, vec(bx), vec(lam))


def _merge_kernel(p_ref, a_ref, r_ref, gp_ref, ga_ref, gr_ref, wp_ref, wa_ref, wr_ref, o_ref):
    acc = gp_ref[...].astype(F32) * jnp.dot(p_ref[...], wp_ref[...], preferred_element_type=F32)
    acc += ga_ref[...].astype(F32) * jnp.dot(a_ref[...], wa_ref[...], preferred_element_type=F32)
    acc += gr_ref[...].astype(F32) * jnp.dot(r_ref[...], wr_ref[...], preferred_element_type=F32)
    o_ref[...] = acc.astype(o_ref.dtype)


def _merge(pool_o, attn_o, lru_o, gates, wp, wa, wr, layer):
    m = pool_o.shape[0]
    bm, bn = MERGE_BM, MERGE_BN
    assert m % bm == 0 and D_MODEL % bn == 0
    x_spec = pl.BlockSpec((bm, POOL_WIDTH), lambda i, j: (i, 0))
    gate_spec = lambda k: pl.BlockSpec((bm, bn), lambda i, j: (i, k * D_MODEL // bn + j))
    w_spec = pl.BlockSpec((None, POOL_WIDTH, bn), lambda i, j: (layer, 0, j))
    return pl.pallas_call(
        _merge_kernel,
        grid=(m // bm, D_MODEL // bn),
        in_specs=[x_spec, x_spec, x_spec, gate_spec(0), gate_spec(1), gate_spec(2), w_spec, w_spec, w_spec],
        out_specs=pl.BlockSpec((bm, bn), lambda i, j: (i, j)),
        out_shape=jax.ShapeDtypeStruct((m, D_MODEL), BF16),
        compiler_params=_cparams("parallel", "arbitrary"),
        name="gated_merge",
    )(pool_o, attn_o, lru_o, gates, gates, gates, wp, wa, wr)


def _pack_bf16_pairs(yb):
    c = yb.shape[1] // 2
    lo = lax.bitcast_convert_type(yb[:, :c].astype(F32), jnp.uint32)
    hi = lax.bitcast_convert_type(yb[:, c:].astype(F32), jnp.uint32)
    return (hi & jnp.uint32(0xFFFF0000)) | (lo >> 16)


def _unpack_bf16_pairs(words, dtype=BF16):
    lo = lax.bitcast_convert_type(words << 16, F32)
    hi = lax.bitcast_convert_type(words & jnp.uint32(0xFFFF0000), F32)
    return jnp.concatenate([lo, hi], axis=1).astype(dtype)


def _route_tile(logits, valid, base):
    bm = logits.shape[0]
    lane = lax.broadcasted_iota(jnp.int32, (bm, ROUTE_W), 1)
    lane_f = lane.astype(F32)
    ninf = -jnp.inf
    big = float(ROUTE_W)

    gl = jnp.where(lane < N_GROUPS, logits, ninf)
    gmax = jnp.max(gl, axis=-1, keepdims=True)
    g = jnp.min(jnp.where(gl == gmax, lane_f, big), axis=-1, keepdims=True)
    p_g = 1.0 / jnp.sum(jnp.exp(gl - gmax), axis=-1, keepdims=True)

    first = N_GROUPS + g * EXPERTS_PER_GROUP
    sl = jnp.where((lane_f >= first) & (lane_f < first + EXPERTS_PER_GROUP), logits, ninf)
    m1 = jnp.max(sl, axis=-1, keepdims=True)
    i1 = jnp.min(jnp.where(sl == m1, lane_f, big), axis=-1, keepdims=True)
    ssum = jnp.sum(jnp.exp(sl - m1), axis=-1, keepdims=True)
    sl2 = jnp.where(lane_f == i1, ninf, sl)
    m2 = jnp.max(sl2, axis=-1, keepdims=True)
    i2 = jnp.min(jnp.where(sl2 == m2, lane_f, big), axis=-1, keepdims=True)
    p1 = 1.0 / ssum
    p2 = jnp.exp(m2 - m1) / ssum
    w1 = p_g * p1 / (p1 + p2)
    w2 = p_g * p2 / (p1 + p2)
    e1 = i1 - N_GROUPS
    e2 = i2 - N_GROUPS

    oh1 = (lane_f == e1) & valid
    oh2 = (lane_f == e2) & valid
    both = (oh1 | oh2).astype(F32)
    earlier = (lax.broadcasted_iota(jnp.int32, (bm, bm), 0) > lax.broadcasted_iota(jnp.int32, (bm, bm), 1))
    prefix = jnp.dot(earlier.astype(BF16), both.astype(BF16), preferred_element_type=F32) + base
    r1 = jnp.sum(jnp.where(oh1, prefix, 0.0), axis=-1, keepdims=True)
    r2 = jnp.sum(jnp.where(oh2, prefix, 0.0), axis=-1, keepdims=True)

    ri = jnp.where(lane == 0, e1, jnp.where(lane == 1, e2, jnp.where(lane == 2, r1, jnp.where(lane == 3, r2, 0.0))))
    rw = jnp.where(lane == 0, w1, jnp.where(lane == 1, w2, 0.0))
    return ri.astype(jnp.int32), rw, jnp.sum(both, axis=0, keepdims=True)


def _outproj_kernel(x_ref, w_ref, h_ref, g_ref, b_ref, rw_ref, rb_ref, hf_ref, hp_ref, ri_ref, rwt_ref, cnt_ref,
                    *, batch, lp):
    i = pl.program_id(0)
    bm = x_ref.shape[0]
    sub = bm // OUT_SPLIT

    @pl.when(i == 0)
    def _():
        cnt_ref[...] = jnp.zeros(cnt_ref.shape, F32)

    for s in range(OUT_SPLIT):
        rows = pl.ds(s * sub, sub)
        t = jnp.dot(x_ref[rows, :], w_ref[...], preferred_element_type=F32)
        y = _layer_norm(ALPHA * h_ref[rows, :] + t, g_ref[...], b_ref[...])
        valid = _flat_valid_rows(i * bm + s * sub, sub, batch, lp)
        y = jnp.where(valid, y, 0.0)
        yb = y.astype(BF16)
        hf_ref[rows, :] = y
        hp_ref[rows, :] = _pack_bf16_pairs(yb)
        logits = jnp.dot(yb, rw_ref[...], preferred_element_type=F32) + rb_ref[...]
        ri, rw, tile_cnt = _route_tile(logits, valid, cnt_ref[...])
        ri_ref[rows, :] = ri
        rwt_ref[rows, :] = rw
        cnt_ref[...] += tile_cnt


def _outproj_ln(merged, w_out_bf16, h, g, b, route_w, route_b, layer, batch, lp):
    m, d = h.shape
    bm = OUT_BM
    assert m % bm == 0
    row_spec = pl.BlockSpec((bm, d), lambda i: (i, 0))
    vec_spec = pl.BlockSpec((1, d), lambda i: (0, 0))
    route_spec = pl.BlockSpec((bm, ROUTE_W), lambda i: (i, 0))
    return pl.pallas_call(
        functools.partial(_outproj_kernel, batch=batch, lp=lp),
        grid=(m // bm,),
        in_specs=[
            row_spec,
            pl.BlockSpec((None, d, d), lambda i: (layer, 0, 0)),
            row_spec, vec_spec, vec_spec,
            pl.BlockSpec((d, ROUTE_W), lambda i: (0, 0)),
            pl.BlockSpec((1, ROUTE_W), lambda i: (0, 0)),
        ],
        out_specs=[row_spec, pl.BlockSpec((bm, d // 2), lambda i: (i, 0)), route_spec, route_spec,
                   pl.BlockSpec((1, ROUTE_W), lambda i: (0, 0))],
        out_shape=[jax.ShapeDtypeStruct((m, d), F32), jax.ShapeDtypeStruct((m, d // 2), jnp.uint32),
                   jax.ShapeDtypeStruct((m, ROUTE_W), jnp.int32), jax.ShapeDtypeStruct((m, ROUTE_W), F32),
                   jax.ShapeDtypeStruct((1, ROUTE_W), F32)],
        compiler_params=_cparams("arbitrary"),
        name="out_proj_ln",
    )(merged, w_out_bf16, h, g.reshape(1, d), b.reshape(1, d), route_w, route_b)


def _dispatch_kernel(start_ref, nchunk_ref, dest_hbm, hp_ref, xs_ref, idx_ref, zero_ref, idx_sem, row_sem, zero_sem,
                     *, n_expert_rows):
    i = pl.program_id(0)
    groups = hp_ref.shape[0]
    bm = groups * SUBLANES
    n_idx = TOP_K * bm

    @pl.when(i == 0)
    def _():
        zero_ref[...] = jnp.zeros(zero_ref.shape, zero_ref.dtype)

        def last_chunk(e):
            first = pl.multiple_of(start_ref[e] + (nchunk_ref[e] - 1) * MOE_BM, MOE_BM)
            return pltpu.make_async_copy(zero_ref, xs_ref.at[pl.ds(first, MOE_BM)], zero_sem)

        for e in range(N_EXPERTS):
            @pl.when(nchunk_ref[e] > 0)
            def _():
                last_chunk(e).start()

        for e in range(N_EXPERTS):
            @pl.when(nchunk_ref[e] > 0)
            def _():
                last_chunk(e).wait()

        used = start_ref[N_EXPERTS - 1] + nchunk_ref[N_EXPERTS - 1] * MOE_BM

        def tail_chunk(j):
            return pltpu.make_async_copy(
                zero_ref, xs_ref.at[pl.ds(pl.multiple_of(used + j * MOE_BM, MOE_BM), MOE_BM)], zero_sem)

        n_tail = (n_expert_rows - used) // MOE_BM
        lax.fori_loop(0, n_tail, lambda j, c: (tail_chunk(j).start(), c)[1], 0)
        lax.fori_loop(0, n_tail, lambda j, c: (tail_chunk(j).wait(), c)[1], 0)

    idx_copy = pltpu.make_async_copy(dest_hbm.at[pl.ds(pl.multiple_of(i * n_idx, n_idx), n_idx)], idx_ref, idx_sem)
    idx_copy.start()
    idx_copy.wait()

    def issue(g, carry):
        for s in range(SUBLANES):
            for k in range(TOP_K):
                slot = idx_ref[TOP_K * (g * SUBLANES + s) + k]
                pltpu.make_async_copy(hp_ref.at[g, pl.ds(s, 1)], xs_ref.at[pl.ds(slot, 1)],
                                      row_sem).start(priority=k % 2)
        return carry

    lax.fori_loop(0, groups, issue, 0)
    for k in range(TOP_K):
        pltpu.make_async_copy(xs_ref.at[pl.ds(0, bm)], xs_ref.at[pl.ds(0, bm)], row_sem).wait()


def _dispatch(hp, dest, chunk_start, n_chunks, n_expert_rows, n_slots):
    m, c = hp.shape
    bm = DISPATCH_BM
    assert m % bm == 0 and bm % SUBLANES == 0
    hp = hp.reshape(m // SUBLANES, SUBLANES, c)
    grid_spec = pltpu.PrefetchScalarGridSpec(
        num_scalar_prefetch=2,
        grid=(m // bm,),
        in_specs=[
            pl.BlockSpec(memory_space=pl.ANY),
            pl.BlockSpec((bm // SUBLANES, SUBLANES, c), lambda i, st, nc: (i, 0, 0)),
        ],
        out_specs=pl.BlockSpec(memory_space=pl.ANY),
        scratch_shapes=[pltpu.SMEM((TOP_K * bm,), jnp.int32), pltpu.VMEM((MOE_BM, c), jnp.uint32),
                        pltpu.SemaphoreType.DMA(()), pltpu.SemaphoreType.DMA(()), pltpu.SemaphoreType.DMA(())],
    )
    return pl.pallas_call(
        functools.partial(_dispatch_kernel, n_expert_rows=n_expert_rows),
        grid_spec=grid_spec,
        out_shape=jax.ShapeDtypeStruct((n_slots, c), jnp.uint32),
        compiler_params=_cparams("arbitrary"),
        name="moe_dispatch",
    )(chunk_start, n_chunks, dest.reshape(m * TOP_K), hp)


def _expert_kernel(start_ref, nchunk_ref, xs_hbm, wg_hbm, wu_hbm, wd_hbm, yb_hbm, wg_f, wu_f, wd_f, wg_s, wu_s, wd_s,
                   x_buf, o_buf, w_sem, in_sem, out_sem, *, layer):
    e = pl.program_id(0)
    n = nchunk_ref[e]
    bm = MOE_BM
    first = start_ref[e] // bm
    total = start_ref[N_EXPERTS - 1] // bm + nchunk_ref[N_EXPERTS - 1]
    wslot = e % 2

    def w_copies(expert, slot):
        return [pltpu.make_async_copy(src.at[layer, expert], dst.at[slot], w_sem.at[slot])
                for src, dst in ((wg_hbm, wg_f), (wu_hbm, wu_f), (wd_hbm, wd_f))]

    def rows(g):
        return pl.ds(pl.multiple_of(g * bm, bm), bm)

    def x_copy(g):
        slot = g % EXPERT_X_BUFFERS
        return pltpu.make_async_copy(xs_hbm.at[rows(g)], x_buf.at[slot], in_sem.at[slot])

    def o_copy(g):
        slot = g % 2
        return pltpu.make_async_copy(o_buf.at[slot], yb_hbm.at[rows(g)], out_sem.at[slot])

    @pl.when(e == 0)
    def _():
        for ahead in range(EXPERT_X_BUFFERS - 1):
            @pl.when(ahead < total)
            def _():
                x_copy(ahead).start()

        for c in w_copies(0, 0):
            c.start(priority=1)

    @pl.when(e + 1 < N_EXPERTS)
    def _():
        for c in w_copies(e + 1, 1 - wslot):
            c.start(priority=1)

    for c in w_copies(e, wslot):
        c.wait()

    @pl.when(n > 0)
    def _():
        wg_s[...] = wg_f[wslot].astype(BF16)
        wu_s[...] = wu_f[wslot].astype(BF16)
        wd_s[...] = wd_f[wslot].astype(BF16)

        def chunk(j, carry):
            g = first + j
            x_copy(g).wait()

            @pl.when(g + EXPERT_X_BUFFERS - 1 < total)
            def _():
                x_copy(g + EXPERT_X_BUFFERS - 1).start()

            @pl.when(g >= 2)
            def _():
                o_copy(g - 2).wait()

            x = _unpack_bf16_pairs(x_buf[g % EXPERT_X_BUFFERS])
            gate = jnp.dot(x, wg_s[...], preferred_element_type=F32)
            up = jnp.dot(x, wu_s[...], preferred_element_type=F32)
            hdn = (gate * _sigmoid(gate)) * up
            y = jnp.dot(hdn.astype(BF16), wd_s[...], preferred_element_type=F32)
            o_buf[g % 2] = _pack_bf16_pairs(y.astype(BF16))
            o_copy(g).start()
            return carry

        lax.fori_loop(0, n, chunk, 0)

    @pl.when(e == N_EXPERTS - 1)
    def _():
        for back in (2, 1):
            @pl.when(total >= back)
            def _():
                o_copy(total - back).wait()

        o_buf[0] = jnp.zeros(o_buf.shape[1:], o_buf.dtype)

        def fill(g, carry):
            tail = pltpu.make_async_copy(o_buf.at[0], yb_hbm.at[rows(g)], out_sem.at[0])
            tail.start()
            tail.wait()
            return carry

        lax.fori_loop(total, yb_hbm.shape[0] // bm, fill, 0)


def _experts(xs, n_rows, chunk_start, n_chunks, w_gate, w_up, w_down, layer):
    d = D_MODEL
    bm = MOE_BM
    any_spec = pl.BlockSpec(memory_space=pl.ANY)
    grid_spec = pltpu.PrefetchScalarGridSpec(
        num_scalar_prefetch=2,
        grid=(N_EXPERTS,),
        in_specs=[any_spec, any_spec, any_spec, any_spec],
        out_specs=any_spec,
        scratch_shapes=[
            pltpu.VMEM((2, d, D_EXPERT), F32),
            pltpu.VMEM((2, d, D_EXPERT), F32),
            pltpu.VMEM((2, D_EXPERT, d), F32),
            pltpu.VMEM((d, D_EXPERT), BF16),
            pltpu.VMEM((d, D_EXPERT), BF16),
            pltpu.VMEM((D_EXPERT, d), BF16),
            pltpu.VMEM((EXPERT_X_BUFFERS, bm, d // 2), jnp.uint32),
            pltpu.VMEM((2, bm, d // 2), jnp.uint32),
            pltpu.SemaphoreType.DMA((2,)),
            pltpu.SemaphoreType.DMA((EXPERT_X_BUFFERS,)),
            pltpu.SemaphoreType.DMA((2,)),
        ],
    )
    return pl.pallas_call(
        functools.partial(_expert_kernel, layer=layer),
        grid_spec=grid_spec,
        out_shape=jax.ShapeDtypeStruct((n_rows, d // 2), jnp.uint32),
        compiler_params=_cparams("arbitrary"),
        name="expert_mlp",
    )(chunk_start, n_chunks, xs, w_gate, w_up, w_down)


def _combine_ln_kernel(dest_hbm, h_ref, rw_ref, g_ref, b_ref, yb_hbm, *rest, batch, lp, final):
    if final:
        out_hbm, idx_ref, rows_ref, y_buf, idx_sem, row_sem, out_sem = rest
    else:
        hf_ref, hb_ref, idx_ref, rows_ref, idx_sem, row_sem = rest
    i = pl.program_id(0)
    n_tiles = pl.num_programs(0)
    bm = h_ref.shape[0]
    n_idx = TOP_K * bm
    buf = i % 2

    def idx_copy(tile):
        half = pl.ds(pl.multiple_of((tile % 2) * n_idx, n_idx), n_idx)
        return pltpu.make_async_copy(
            dest_hbm.at[pl.ds(pl.multiple_of(tile * n_idx, n_idx), n_idx)], idx_ref.at[half], idx_sem.at[tile % 2])

    def start_gather(tile):
        into = tile % 2
        idx_copy(tile).wait()

        def issue(g, carry):
            base = into * n_idx + g * (TOP_K * SUBLANES)
            for s in range(SUBLANES):
                for k in range(TOP_K):
                    slot = idx_ref[base + (TOP_K * s + k)]
                    pltpu.make_async_copy(yb_hbm.at[pl.ds(slot, 1)], rows_ref.at[into, k, g, pl.ds(s, 1)],
                                          row_sem.at[into]).start(priority=k % 2)
            return carry

        lax.fori_loop(0, bm // SUBLANES, issue, 0)

        @pl.when(tile + 2 < n_tiles)
        def _():
            idx_copy(tile + 2).start()

    @pl.when(i == 0)
    def _():
        idx_copy(0).start()

        @pl.when(1 < n_tiles)
        def _():
            idx_copy(1).start()

        start_gather(0)

    @pl.when(i + 1 < n_tiles)
    def _():
        start_gather(i + 1)

    for k in range(TOP_K):
        pltpu.make_async_copy(yb_hbm.at[pl.ds(0, bm)], yb_hbm.at[pl.ds(0, bm)], row_sem.at[buf]).wait()

    d = h_ref.shape[1]

    def expert_rows(k):
        return _unpack_bf16_pairs(rows_ref[buf, k].reshape(bm, d // 2), F32)

    ffn = rw_ref[:, 0:1] * expert_rows(0)
    for k in range(1, TOP_K):
        ffn = ffn + rw_ref[:, k:k + 1] * expert_rows(k)
    y = _layer_norm(ALPHA * h_ref[...] + ffn, g_ref[...], b_ref[...])
    if not final:
        y = jnp.where(_flat_valid_rows(i * bm, bm, batch, lp), y, 0.0)
        hf_ref[...] = y
        hb_ref[...] = y.astype(BF16)
        return

    nb = lp // BLOCK

    def for_each_out_block(step, fn):
        for j in range(bm // BLOCK):
            blk = step * (bm // BLOCK) + j
            seq_blk = blk % nb

            @pl.when(seq_blk >= 1)
            def _():
                dst = pl.multiple_of(((blk // nb) * (nb - 1) + seq_blk - 1) * BLOCK, BLOCK)
                fn(pltpu.make_async_copy(y_buf.at[pl.ds(j * BLOCK, BLOCK)], out_hbm.at[pl.ds(dst, BLOCK)], out_sem))

    @pl.when(i > 0)
    def _():
        for_each_out_block(i - 1, lambda c: c.wait())

    y_buf[...] = y
    for_each_out_block(i, lambda c: c.start(priority=1))

    @pl.when(i == n_tiles - 1)
    def _():
        for_each_out_block(i, lambda c: c.wait())


def _combine_ln(h, yb, dest, rw, g, b, batch, lp, final):
    m, d = h.shape
    bm = COMBINE_BM
    assert m % bm == 0 and bm % BLOCK == 0
    row_spec = pl.BlockSpec((bm, d), lambda i: (i, 0))
    vec_spec = pl.BlockSpec((1, d), lambda i: (0, 0))
    scratch = [pltpu.SMEM((2 * TOP_K * bm,), jnp.int32),
               pltpu.VMEM((2, TOP_K, bm // SUBLANES, SUBLANES, d // 2), jnp.uint32)]
    sems = [pltpu.SemaphoreType.DMA((2,)), pltpu.SemaphoreType.DMA((2,))]
    if final:
        out_specs = pl.BlockSpec(memory_space=pl.ANY)
        out_shape = jax.ShapeDtypeStruct((batch * (lp - BLOCK), d), F32)
        scratch = scratch + [pltpu.VMEM((bm, d), F32)] + sems + [pltpu.SemaphoreType.DMA(())]
    else:
        out_specs = [row_spec, row_spec]
        out_shape = [jax.ShapeDtypeStruct((m, d), F32), jax.ShapeDtypeStruct((m, d), BF16)]
        scratch = scratch + sems
    return pl.pallas_call(
        functools.partial(_combine_ln_kernel, batch=batch, lp=lp, final=final),
        grid=(m // bm,),
        in_specs=[
            pl.BlockSpec(memory_space=pl.ANY),
            row_spec,
            pl.BlockSpec((bm, ROUTE_W), lambda i: (i, 0)),
            vec_spec, vec_spec,
            pl.BlockSpec(memory_space=pl.ANY),
        ],
        out_specs=out_specs,
        out_shape=out_shape,
        scratch_shapes=scratch,
        compiler_params=_cparams("arbitrary"),
        name="moe_combine_out" if final else "moe_combine_ln",
    )(dest.reshape(m * TOP_K), h, rw, g.reshape(1, d), b.reshape(1, d), yb)


def _slot_tables(ri, cnt, batch, lp):
    m = ri.shape[0]
    bm = MOE_BM
    eid = ri[:, 0:TOP_K]
    rank = ri[:, TOP_K:2 * TOP_K]
    counts = cnt[0, :N_EXPERTS].astype(jnp.int32)
    padded = (counts + bm - 1) // bm * bm
    pad_end = jnp.cumsum(padded)
    pad_start = pad_end - padded
    n_real = batch * (lp - PAD) * TOP_K
    nblk = -(-(n_real + N_EXPERTS * (bm - 1)) // bm)
    cap = nblk * bm
    experts = jnp.arange(N_EXPERTS, dtype=jnp.int32)
    start = jnp.sum(jnp.where(eid[:, :, None] == experts[None, None, :], pad_start[None, None, :], 0), axis=-1)
    row = jnp.arange(m, dtype=jnp.int32)
    pos = row % lp
    spare = cap + ((row // lp) * PAD + pos)[:, None] * TOP_K + jnp.arange(TOP_K, dtype=jnp.int32)[None, :]
    valid = (pos >= PAD)[:, None]
    dest = jnp.where(valid, start + rank, spare)
    dest_read = jnp.where(valid, dest, 0)
    n_slots = cap + batch * PAD * TOP_K
    return dest, dest_read, pad_start.astype(jnp.int32), (padded // bm).astype(jnp.int32), cap, n_slots


def kernel(x, meta, ln_emb_g, ln_emb_b, w_in, pool_w, pool_scale, attn_sink, conv_w, conv_b, lru_wa, lru_ba,
           lru_wx, lru_bx, lru_lambda, proj_pool, proj_attn, proj_lru, w_out, ln1_g, ln1_b, router_grp_w,
           router_grp_b, router_exp_w, router_exp_b, exp_w_gate, exp_w_up, exp_w_down, ln2_g, ln2_b):
    batch, seq, d = x.shape
    lp = PAD + N_META + seq
    m = batch * lp

    hf, hb = _embed(x, meta, ln_emb_g, ln_emb_b)
    w_in_b = w_in.astype(BF16)
    pool_w_b = pool_w.astype(BF16)
    wa_b = lru_wa.astype(BF16)
    wx_b = lru_wx.astype(BF16)
    wp_b = proj_pool.astype(BF16)
    wat_b = proj_attn.astype(BF16)
    wl_b = proj_lru.astype(BF16)
    w_out_b = w_out.astype(BF16)
    route_pad = ROUTE_W - N_GROUPS - N_EXPERTS

    for l in range(DEPTH):
        cols = _inproj(hb, w_in_b, l, gates=False)
        gates = _inproj(hb, w_in_b, l, gates=True)
        pool_o = _pool(cols, pool_w_b[l], pool_scale[l], batch, lp)
        attn_o = _attention(cols, attn_sink[l], batch, lp)
        lru_o = _lru(cols, conv_w[l], conv_b[l], wa_b[l], lru_ba[l], wx_b[l], lru_bx[l], lru_lambda[l],
                     batch, lp)
        merged = _merge(pool_o, attn_o, lru_o, gates, wp_b, wat_b, wl_b, l)
        route_w = jnp.concatenate(
            [router_grp_w[l], router_exp_w[l], jnp.zeros((d, route_pad), F32)], axis=1).astype(BF16)
        route_b = jnp.concatenate(
            [router_grp_b[l], router_exp_b[l], jnp.zeros((route_pad,), F32)]).reshape(1, ROUTE_W)
        h1f, h1p, ri, rw, cnt = _outproj_ln(merged, w_out_b, hf, ln1_g[l], ln1_b[l], route_w, route_b, l,
                                            batch, lp)
        dest, dest_read, chunk_start, n_chunks, cap, n_slots = _slot_tables(ri, cnt, batch, lp)
        xs = _dispatch(h1p, dest, chunk_start, n_chunks, cap, n_slots)
        yb = _experts(xs, cap, chunk_start, n_chunks, exp_w_gate, exp_w_up, exp_w_down, l)
        if l + 1 < DEPTH:
            hf, hb = _combine_ln(h1f, yb, dest_read, rw, ln2_g[l], ln2_b[l], batch, lp, final=False)
        else:
            out = _combine_ln(h1f, yb, dest_read, rw, ln2_g[l], ln2_b[l], batch, lp, final=True)

    return out.reshape(batch, seq, d)
```

```python
import functools

import jax
import jax.numpy as jnp
from jax import lax
from jax.experimental import pallas as pl
from jax.experimental.pallas import tpu as pltpu

F32 = jnp.float32
BF16 = jnp.bfloat16

D_MODEL = 2048
DEPTH = 2
N_META = 16
POOL_WINDOWS = (2, 4, 8, 16)
POOL_WIDTH = D_MODEL // 2
POOL_GROUP = POOL_WIDTH // len(POOL_WINDOWS)
N_HEADS = 16
N_KV_HEADS = 4
HEAD_DIM = 64
Q_PER_KV = N_HEADS // N_KV_HEADS
WINDOW = 128
BLOCK = 128
NEG = -1e30
LRU_WIDTH = D_MODEL // 2
LRU_BLOCKS = 4
LRU_BLOCK = LRU_WIDTH // LRU_BLOCKS
CONV_WIDTH = 4
LRU_C = 8.0
N_GROUPS = 4
EXPERTS_PER_GROUP = 8
N_EXPERTS = N_GROUPS * EXPERTS_PER_GROUP
TOP_K = 2
D_EXPERT = D_MODEL // 4
LN_EPS = 1e-5
ALPHA = (2.0 * DEPTH) ** 0.25

PAD = BLOCK - N_META
ATT_W = N_HEADS * HEAD_DIM
KV_W = N_KV_HEADS * HEAD_DIM
OFF_POOL = 0
OFF_Q = OFF_POOL + POOL_WIDTH
OFF_K = OFF_Q + ATT_W
OFF_V = OFF_K + KV_W
OFF_LX = OFF_V + KV_W
OFF_LY = OFF_LX + LRU_WIDTH
OFF_GATE = OFF_LY + LRU_WIDTH
IN_COLS = OFF_GATE + 3 * D_MODEL

VMEM_LIMIT_BYTES = 56 * 1024 * 1024
SUBLANES = 8

SEQ_TILE = 3 * BLOCK
INPROJ_BM = 1536
INPROJ_BN = 1536
MERGE_BM = 768
MERGE_BN = 1024
OUT_BM = 512
OUT_SPLIT = 2
MOE_BM = 256
DISPATCH_BM = 1536
COMBINE_BM = 512
EXPERT_X_BUFFERS = 3
ROUTE_W = 128
ROUTE_ROWS = 8


def _cparams(*sem):
    return pltpu.CompilerParams(dimension_semantics=sem, vmem_limit_bytes=VMEM_LIMIT_BYTES)


def _layer_norm(x, g, b):
    mu = jnp.mean(x, axis=-1, keepdims=True)
    xc = x - mu
    var = jnp.mean(xc * xc, axis=-1, keepdims=True)
    return xc * lax.rsqrt(var + LN_EPS) * g + b


def _flat_valid_rows(row0, n_rows, batch, lp):
    r = row0 + lax.broadcasted_iota(jnp.int32, (n_rows, 1), 0)
    pad_row = jnp.zeros((n_rows, 1), jnp.bool_)
    for b in range(batch):
        pad_row = pad_row | ((r >= b * lp) & (r < b * lp + PAD))
    return jnp.logical_not(pad_row)


EMBED_BLOCKS = SEQ_TILE // BLOCK


def _embed_kernel(*refs):
    x_refs, (meta_ref, g_ref, b_ref, hf_ref, hb_ref) = refs[:EMBED_BLOCKS], refs[EMBED_BLOCKS:]
    t = pl.program_id(1)
    for j in range(EMBED_BLOCKS):
        src = x_refs[j][...]
        if j == 0:
            src = jnp.where(t == 0, meta_ref[...], src)
        y = _layer_norm(src, g_ref[...], b_ref[...])
        if j == 0:
            row = lax.broadcasted_iota(jnp.int32, (BLOCK, 1), 0)
            y = jnp.where((t > 0) | (row >= PAD), y, 0.0)
        hf_ref[pl.ds(j * BLOCK, BLOCK), :] = y
        hb_ref[pl.ds(j * BLOCK, BLOCK), :] = y.astype(BF16)


def _embed(x, meta, g, b):
    batch, seq, d = x.shape
    nblk = seq // BLOCK
    lp = (nblk + 1) * BLOCK
    nt = lp // SEQ_TILE
    m = batch * lp
    meta_tile = jnp.concatenate([jnp.zeros((PAD, d), F32), meta.astype(F32)], axis=0)
    row_spec = pl.BlockSpec((SEQ_TILE, d), lambda bi, t: (bi * nt + t, 0))
    vec_spec = pl.BlockSpec((1, d), lambda bi, t: (0, 0))
    x_spec = lambda j: pl.BlockSpec(
        (BLOCK, d), lambda bi, t: (bi * nblk + jnp.maximum(EMBED_BLOCKS * t + j - 1, 0), 0))
    x2 = x.reshape(batch * seq, d)
    return pl.pallas_call(
        _embed_kernel,
        grid=(batch, nt),
        in_specs=[x_spec(j) for j in range(EMBED_BLOCKS)]
        + [pl.BlockSpec((BLOCK, d), lambda bi, t: (0, 0)), vec_spec, vec_spec],
        out_specs=[row_spec, row_spec],
        out_shape=[jax.ShapeDtypeStruct((m, d), F32), jax.ShapeDtypeStruct((m, d), BF16)],
        compiler_params=_cparams("parallel", "arbitrary"),
        name="embed_ln",
    )(*([x2] * EMBED_BLOCKS), meta_tile, g.reshape(1, d), b.reshape(1, d))


def _sigmoid(x):
    return 0.5 * jnp.tanh(0.5 * x) + 0.5


def _inproj_kernel(x_ref, w_ref, o_ref, *, gates):
    acc = jnp.dot(x_ref[...], w_ref[...], preferred_element_type=F32)
    o_ref[...] = (_sigmoid(acc) if gates else acc).astype(o_ref.dtype)


def _inproj(hb, w_in_bf16, layer, gates):
    m, d = hb.shape
    bm, bn = INPROJ_BM, INPROJ_BN
    col0, width = (OFF_GATE, IN_COLS - OFF_GATE) if gates else (0, OFF_GATE)
    assert m % bm == 0 and width % bn == 0 and col0 % bn == 0
    return pl.pallas_call(
        functools.partial(_inproj_kernel, gates=gates),
        grid=(m // bm, width // bn),
        in_specs=[
            pl.BlockSpec((bm, d), lambda i, j: (i, 0)),
            pl.BlockSpec((None, d, bn), lambda i, j: (layer, 0, col0 // bn + j)),
        ],
        out_specs=pl.BlockSpec((bm, bn), lambda i, j: (i, j)),
        out_shape=jax.ShapeDtypeStruct((m, width), BF16),
        compiler_params=_cparams("parallel", "arbitrary"),
        name="in_proj_gates" if gates else "in_proj",
    )(hb, w_in_bf16)


POOL_HALO = 2 * max(POOL_WINDOWS)
assert POOL_WINDOWS == tuple(2 ** (g + 1) for g in range(len(POOL_WINDOWS)))


def _pool_kernel(u_ref, w_ref, scale_ref, o_ref, ext_ref, lvl_ref):
    t = pl.program_id(1)
    tile = SEQ_TILE
    halo = POOL_HALO

    @pl.when(t == 0)
    def _():
        ext_ref[pl.ds(0, halo), :] = jnp.zeros((halo, POOL_WIDTH), F32)

    @pl.when(t > 0)
    def _():
        ext_ref[pl.ds(0, halo), :] = ext_ref[pl.ds(tile, halo), :]

    ext_ref[pl.ds(halo, tile), :] = u_ref[...].astype(F32)

    pos = t * tile + lax.broadcasted_iota(jnp.int32, (tile, 1), 0) - PAD
    src, first = ext_ref, 0
    for gi, w in enumerate(POOL_WINDOWS):
        lane0 = gi * POOL_GROUP
        lanes = pl.ds(lane0, POOL_WIDTH - lane0)
        new_first = -(-(first + w // 2) // SUBLANES) * SUBLANES
        n_rows = tile + halo - new_first
        level = src[pl.ds(new_first, n_rows), lanes] + src[pl.ds(new_first - w // 2, n_rows), lanes]
        win = level[halo - new_first:, :POOL_GROUP]
        cols = pl.ds(lane0, POOL_GROUP)
        u = ext_ref[pl.ds(halo, tile), cols]
        cnt = jnp.clip(pos + 1, 1, w).astype(F32)
        delta = win / cnt - u
        mixed = jnp.dot(delta.astype(BF16), w_ref[gi], preferred_element_type=F32)
        o_ref[:, cols] = (mixed * scale_ref[:, cols]).astype(o_ref.dtype)
        if gi + 1 < len(POOL_WINDOWS):
            lvl_ref[gi, pl.ds(new_first, n_rows), lanes] = level
            src, first = lvl_ref.at[gi], new_first


def _pool(cols, pool_w_bf16, pool_scale, batch, lp):
    m = cols.shape[0]
    nt = lp // SEQ_TILE
    maxw = POOL_HALO
    return pl.pallas_call(
        _pool_kernel,
        grid=(batch, nt),
        in_specs=[
            pl.BlockSpec((SEQ_TILE, POOL_WIDTH), lambda b, t: (b * nt + t, OFF_POOL // POOL_WIDTH)),
            pl.BlockSpec((len(POOL_WINDOWS), POOL_GROUP, POOL_GROUP), lambda b, t: (0, 0, 0)),
            pl.BlockSpec((1, POOL_WIDTH), lambda b, t: (0, 0)),
        ],
        out_specs=pl.BlockSpec((SEQ_TILE, POOL_WIDTH), lambda b, t: (b * nt + t, 0)),
        out_shape=jax.ShapeDtypeStruct((m, POOL_WIDTH), BF16),
        scratch_shapes=[pltpu.VMEM((SEQ_TILE + maxw, POOL_WIDTH), F32),
                        pltpu.VMEM((len(POOL_WINDOWS) - 1, SEQ_TILE + maxw, POOL_WIDTH), F32)],
        compiler_params=_cparams("parallel", "arbitrary"),
        name="pool_mixer",
    )(cols, pool_w_bf16, pool_scale.reshape(1, POOL_WIDTH))


def _attn_bias():
    kj = jnp.arange(2 * BLOCK, dtype=jnp.int32)[:, None]
    qi = jnp.arange(BLOCK, dtype=jnp.int32)[None, :]
    dist = BLOCK + qi - kj
    in_window = (dist >= 0) & (dist < WINDOW)
    slopes = 2.0 ** (-8.0 * jnp.arange(1, N_HEADS + 1, dtype=F32) / N_HEADS)
    alibi = -slopes[:, None, None] * dist.astype(F32)[None]
    bias = jnp.where(in_window[None], alibi, NEG).reshape(N_KV_HEADS, Q_PER_KV, 2 * BLOCK, BLOCK)
    return bias.transpose(0, 2, 1, 3).reshape(N_KV_HEADS, 2 * BLOCK, Q_PER_KV * BLOCK)


def _attn_kernel(q_ref, kp_ref, kc_ref, vp_ref, vc_ref, bias_ref, sink_ref, o_ref):
    n = pl.program_id(1)
    q = q_ref[...] * (HEAD_DIM ** -0.5)

    def heads(early_keys):
        for kh in range(N_KV_HEADS):
            hs = pl.ds(kh * HEAD_DIM, HEAD_DIM)
            k2 = jnp.concatenate([kp_ref[:, hs], kc_ref[:, hs]], axis=0)
            v2 = jnp.concatenate([vp_ref[:, hs], vc_ref[:, hs]], axis=0)
            first = kh * Q_PER_KV
            qg = jnp.concatenate([q[:, (first + g) * HEAD_DIM:(first + g + 1) * HEAD_DIM]
                                  for g in range(Q_PER_KV)], axis=0)
            s = lax.dot_general(k2, qg, (((1,), (1,)), ((), ())), preferred_element_type=F32)
            s = s + bias_ref[kh]
            if early_keys is not None:
                s = s + early_keys
            sk = sink_ref[pl.ds(kh, 1), :]
            mx = jnp.maximum(jnp.max(s, axis=0, keepdims=True), sk)
            p = jnp.exp(s - mx)
            den = jnp.sum(p, axis=0, keepdims=True) + jnp.exp(sk - mx)
            pn = (p * (1.0 / den)).astype(BF16)
            o = lax.dot_general(pn, v2, (((0,), (0,)), ((), ())), preferred_element_type=F32)
            for g in range(Q_PER_KV):
                o_ref[:, pl.ds((first + g) * HEAD_DIM, HEAD_DIM)] = o[g * BLOCK:(g + 1) * BLOCK].astype(o_ref.dtype)

    @pl.when(n < 2)
    def _():
        k_pos = (n - 1) * BLOCK + lax.broadcasted_iota(jnp.int32, (2 * BLOCK, 1), 0)
        heads(jnp.where(k_pos < PAD, NEG, 0.0))

    @pl.when(n >= 2)
    def _():
        heads(None)


def _attention(cols, sink, batch, lp):
    m = cols.shape[0]
    nb = lp // BLOCK
    cur = lambda cb: (lambda b, n: (b * nb + n, cb))
    prev = lambda cb: (lambda b, n: (b * nb + jnp.maximum(n - 1, 0), cb))
    return pl.pallas_call(
        _attn_kernel,
        grid=(batch, nb),
        in_specs=[
            pl.BlockSpec((BLOCK, ATT_W), cur(OFF_Q // ATT_W)),
            pl.BlockSpec((BLOCK, KV_W), prev(OFF_K // KV_W)),
            pl.BlockSpec((BLOCK, KV_W), cur(OFF_K // KV_W)),
            pl.BlockSpec((BLOCK, KV_W), prev(OFF_V // KV_W)),
            pl.BlockSpec((BLOCK, KV_W), cur(OFF_V // KV_W)),
            pl.BlockSpec((N_KV_HEADS, 2 * BLOCK, Q_PER_KV * BLOCK), lambda b, n: (0, 0, 0)),
            pl.BlockSpec((N_KV_HEADS, Q_PER_KV * BLOCK), lambda b, n: (0, 0)),
        ],
        out_specs=pl.BlockSpec((BLOCK, ATT_W), lambda b, n: (b * nb + n, 0)),
        out_shape=jax.ShapeDtypeStruct((m, ATT_W), BF16),
        compiler_params=_cparams("parallel", "arbitrary"),
        name="swa_attention",
    )(cols, cols, cols, cols, cols, _attn_bias(),
      jnp.repeat(sink.astype(F32).reshape(N_KV_HEADS, Q_PER_KV), BLOCK, axis=1))


LRU_HALF = LRU_WIDTH // 2
LRU_HALO = 8
LRU_SCAN_UNROLL = 6


LOG2_E = 1.4426950408889634
GELU_C = 0.7978845608028654


def _gelu_tanh(x):
    half = 0.5 * x
    return half + half * jnp.tanh(x * (GELU_C + (GELU_C * 0.044715) * (x * x)))


def _lru_kernel(*refs):
    nh = LRU_WIDTH // LRU_HALF
    x_refs, y_refs = refs[:nh], refs[nh:2 * nh]
    (cw_ref, cb_ref, wa_ref, ba_ref, wx_ref, bx_ref, lam_ref, o_ref,
     ext_ref, a_ref, b_ref, carry_ref) = refs[2 * nh:]
    t = pl.program_id(1)
    tile = SEQ_TILE
    width = LRU_HALF
    pos = t * tile + lax.broadcasted_iota(jnp.int32, (tile, 1), 0)
    row = lax.broadcasted_iota(jnp.int32, (8, width), 0)

    for c in range(nh):
        lanes = pl.ds(c * width, width)
        ext, a_s, b_s, carry = ext_ref.at[c], a_ref.at[c], b_ref.at[c], carry_ref.at[c]

        @pl.when(t == 0)
        def _():
            ext[pl.ds(0, LRU_HALO), :] = jnp.zeros((LRU_HALO, width), F32)
            carry[...] = jnp.zeros((1, width), F32)

        @pl.when(t > 0)
        def _():
            ext[pl.ds(0, LRU_HALO), :] = ext[pl.ds(tile, LRU_HALO), :]

        ext[pl.ds(LRU_HALO, tile), :] = x_refs[c][...].astype(F32)

        xc = cb_ref[:, lanes] + cw_ref[pl.ds(CONV_WIDTH - 1, 1), lanes] * ext[pl.ds(LRU_HALO, tile), :]
        for j in range(CONV_WIDTH - 1):
            shift = CONV_WIDTH - 1 - j
            xc = xc + cw_ref[pl.ds(j, 1), lanes] * ext[pl.ds(LRU_HALO - shift, tile), :]

        xcb = xc.astype(BF16)
        ga_parts, gx_parts = [], []
        for blk in range(width // LRU_BLOCK):
            xb = xcb[:, blk * LRU_BLOCK:(blk + 1) * LRU_BLOCK]
            w_idx = c * (width // LRU_BLOCK) + blk
            ga_parts.append(jnp.dot(xb, wa_ref[w_idx], preferred_element_type=F32))
            gx_parts.append(jnp.dot(xb, wx_ref[w_idx], preferred_element_type=F32))
        gate_a = _sigmoid(jnp.concatenate(ga_parts, axis=1) + ba_ref[:, lanes])
        gate_x = _sigmoid(jnp.concatenate(gx_parts, axis=1) + bx_ref[:, lanes])

        neg_lam = -lam_ref[:, lanes]
        softplus = jnp.maximum(neg_lam, 0.0) + jnp.log1p(jnp.exp(-jnp.abs(neg_lam)))
        a = jnp.exp2(gate_a * ((-LRU_C * LOG2_E) * softplus))
        b_in = jnp.sqrt(1.0 - a * a) * gate_x * xc
        b_in = jnp.where(pos >= PAD, b_in, 0.0)
        a_s[...] = a
        b_s[...] = b_in

        def group(r, h_prev, a_s=a_s, b_s=b_s):
            rows = pl.ds(pl.multiple_of(r * 8, 8), 8)
            av = a_s[rows, :]
            bv = b_s[rows, :]
            for k in (1, 2, 4):
                a_sh = jnp.where(row >= k, pltpu.roll(av, k, 0), 1.0)
                b_sh = jnp.where(row >= k, pltpu.roll(bv, k, 0), 0.0)
                bv = av * b_sh + bv
                av = av * a_sh
            hv = av * h_prev + bv
            b_s[rows, :] = hv
            return hv[7:8, :]

        carry[...] = lax.fori_loop(0, tile // 8, group, carry[...], unroll=LRU_SCAN_UNROLL)
        o_ref[:, lanes] = (b_s[...] * _gelu_tanh(y_refs[c][...].astype(F32))).astype(o_ref.dtype)


def _lru(cols, conv_w, conv_b, wa_bf16, ba, wx_bf16, bx, lam, batch, lp):
    m = cols.shape[0]
    nt = lp // SEQ_TILE
    nh = LRU_WIDTH // LRU_HALF
    vec = lambda v: v.reshape(1, LRU_WIDTH).astype(F32)
    vec_spec = pl.BlockSpec((1, LRU_WIDTH), lambda b, t: (0, 0))
    w_spec = pl.BlockSpec((LRU_BLOCKS, LRU_BLOCK, LRU_BLOCK), lambda b, t: (0, 0, 0))
    half_spec = lambda off, c: pl.BlockSpec((SEQ_TILE, LRU_HALF), lambda b, t: (b * nt + t, off // LRU_HALF + c))
    return pl.pallas_call(
        _lru_kernel,
        grid=(batch, nt),
        in_specs=[half_spec(OFF_LX, c) for c in range(nh)] + [half_spec(OFF_LY, c) for c in range(nh)]
        + [pl.BlockSpec((CONV_WIDTH, LRU_WIDTH), lambda b, t: (0, 0)),
           vec_spec, w_spec, vec_spec, w_spec, vec_spec, vec_spec],
        out_specs=pl.BlockSpec((SEQ_TILE, LRU_WIDTH), lambda b, t: (b * nt + t, 0)),
        out_shape=jax.ShapeDtypeStruct((m, LRU_WIDTH), BF16),
        scratch_shapes=[
            pltpu.VMEM((nh, SEQ_TILE + LRU_HALO, LRU_HALF), F32),
            pltpu.VMEM((nh, SEQ_TILE, LRU_HALF), F32),
            pltpu.VMEM((nh, SEQ_TILE, LRU_HALF), F32),
            pltpu.VMEM((nh, 1, LRU_HALF), F32),
        ],
        compiler_params=_cparams("parallel", "arbitrary"),
        name="rglru",
    )(*([cols] * (2 * nh)), conv_w.astype(F32), vec(conv_b), wa_bf16, vec(ba), wx_bf16, vec(bx), vec(lam))


def _merge_kernel(p_ref, a_ref, r_ref, gp_ref, ga_ref, gr_ref, wp_ref, wa_ref, wr_ref, o_ref):
    acc = gp_ref[...].astype(F32) * jnp.dot(p_ref[...], wp_ref[...], preferred_element_type=F32)
    acc += ga_ref[...].astype(F32) * jnp.dot(a_ref[...], wa_ref[...], preferred_element_type=F32)
    acc += gr_ref[...].astype(F32) * jnp.dot(r_ref[...], wr_ref[...], preferred_element_type=F32)
    o_ref[...] = acc.astype(o_ref.dtype)


def _merge(pool_o, attn_o, lru_o, gates, wp, wa, wr, layer):
    m = pool_o.shape[0]
    bm, bn = MERGE_BM, MERGE_BN
    assert m % bm == 0 and D_MODEL % bn == 0
    x_spec = pl.BlockSpec((bm, POOL_WIDTH), lambda i, j: (i, 0))
    gate_spec = lambda k: pl.BlockSpec((bm, bn), lambda i, j: (i, k * D_MODEL // bn + j))
    w_spec = pl.BlockSpec((None, POOL_WIDTH, bn), lambda i, j: (layer, 0, j))
    return pl.pallas_call(
        _merge_kernel,
        grid=(m // bm, D_MODEL // bn),
        in_specs=[x_spec, x_spec, x_spec, gate_spec(0), gate_spec(1), gate_spec(2), w_spec, w_spec, w_spec],
        out_specs=pl.BlockSpec((bm, bn), lambda i, j: (i, j)),
        out_shape=jax.ShapeDtypeStruct((m, D_MODEL), BF16),
        compiler_params=_cparams("parallel", "arbitrary"),
        name="gated_merge",
    )(pool_o, attn_o, lru_o, gates, gates, gates, wp, wa, wr)


def _pack_bf16_pairs(yb):
    c = yb.shape[1] // 2
    lo = lax.bitcast_convert_type(yb[:, :c].astype(F32), jnp.uint32)
    hi = lax.bitcast_convert_type(yb[:, c:].astype(F32), jnp.uint32)
    return (hi & jnp.uint32(0xFFFF0000)) | (lo >> 16)


def _unpack_bf16_pairs(words, dtype=BF16):
    lo = lax.bitcast_convert_type(words << 16, F32)
    hi = lax.bitcast_convert_type(words & jnp.uint32(0xFFFF0000), F32)
    return jnp.concatenate([lo, hi], axis=1).astype(dtype)


def _route_tile(logits, valid, base):
    bm = logits.shape[0]
    lane = lax.broadcasted_iota(jnp.int32, (bm, ROUTE_W), 1)
    lane_f = lane.astype(F32)
    ninf = -jnp.inf
    big = float(ROUTE_W)

    gl = jnp.where(lane < N_GROUPS, logits, ninf)
    gmax = jnp.max(gl, axis=-1, keepdims=True)
    g = jnp.min(jnp.where(gl == gmax, lane_f, big), axis=-1, keepdims=True)
    p_g = 1.0 / jnp.sum(jnp.exp(gl - gmax), axis=-1, keepdims=True)

    first = N_GROUPS + g * EXPERTS_PER_GROUP
    sl = jnp.where((lane_f >= first) & (lane_f < first + EXPERTS_PER_GROUP), logits, ninf)
    m1 = jnp.max(sl, axis=-1, keepdims=True)
    i1 = jnp.min(jnp.where(sl == m1, lane_f, big), axis=-1, keepdims=True)
    ssum = jnp.sum(jnp.exp(sl - m1), axis=-1, keepdims=True)
    sl2 = jnp.where(lane_f == i1, ninf, sl)
    m2 = jnp.max(sl2, axis=-1, keepdims=True)
    i2 = jnp.min(jnp.where(sl2 == m2, lane_f, big), axis=-1, keepdims=True)
    p1 = 1.0 / ssum
    p2 = jnp.exp(m2 - m1) / ssum
    w1 = p_g * p1 / (p1 + p2)
    w2 = p_g * p2 / (p1 + p2)
    e1 = i1 - N_GROUPS
    e2 = i2 - N_GROUPS

    oh1 = (lane_f == e1) & valid
    oh2 = (lane_f == e2) & valid
    both = (oh1 | oh2).astype(F32)
    earlier = (lax.broadcasted_iota(jnp.int32, (bm, bm), 0) > lax.broadcasted_iota(jnp.int32, (bm, bm), 1))
    prefix = jnp.dot(earlier.astype(BF16), both.astype(BF16), preferred_element_type=F32) + base
    r1 = jnp.sum(jnp.where(oh1, prefix, 0.0), axis=-1, keepdims=True)
    r2 = jnp.sum(jnp.where(oh2, prefix, 0.0), axis=-1, keepdims=True)

    ri = jnp.where(lane == 0, e1, jnp.where(lane == 1, e2, jnp.where(lane == 2, r1, jnp.where(lane == 3, r2, 0.0))))
    rw = jnp.where(lane == 0, w1, jnp.where(lane == 1, w2, 0.0))
    ri_rows = jnp.transpose(ri)[0:ROUTE_ROWS, :].astype(jnp.int32)
    return ri_rows, rw, jnp.sum(both, axis=0, keepdims=True)


def _outproj_kernel(x_ref, w_ref, h_ref, g_ref, b_ref, rw_ref, rb_ref, hf_ref, hp_ref, ri_ref, rwt_ref, cnt_ref,
                    *, batch, lp):
    i = pl.program_id(0)
    bm = x_ref.shape[0]
    sub = bm // OUT_SPLIT

    @pl.when(i == 0)
    def _():
        cnt_ref[...] = jnp.zeros(cnt_ref.shape, F32)

    for s in range(OUT_SPLIT):
        rows = pl.ds(s * sub, sub)
        t = jnp.dot(x_ref[rows, :], w_ref[...], preferred_element_type=F32)
        y = _layer_norm(ALPHA * h_ref[rows, :] + t, g_ref[...], b_ref[...])
        valid = _flat_valid_rows(i * bm + s * sub, sub, batch, lp)
        y = jnp.where(valid, y, 0.0)
        yb = y.astype(BF16)
        hf_ref[rows, :] = y
        hp_ref[rows, :] = _pack_bf16_pairs(yb)
        logits = jnp.dot(yb, rw_ref[...], preferred_element_type=F32) + rb_ref[...]
        ri, rw, tile_cnt = _route_tile(logits, valid, cnt_ref[...])
        ri_ref[:, rows] = ri
        rwt_ref[rows, :] = rw
        cnt_ref[...] += tile_cnt


def _outproj_ln(merged, w_out_bf16, h, g, b, route_w, route_b, layer, batch, lp):
    m, d = h.shape
    bm = OUT_BM
    assert m % bm == 0
    row_spec = pl.BlockSpec((bm, d), lambda i: (i, 0))
    vec_spec = pl.BlockSpec((1, d), lambda i: (0, 0))
    route_spec = pl.BlockSpec((bm, ROUTE_W), lambda i: (i, 0))
    return pl.pallas_call(
        functools.partial(_outproj_kernel, batch=batch, lp=lp),
        grid=(m // bm,),
        in_specs=[
            row_spec,
            pl.BlockSpec((None, d, d), lambda i: (layer, 0, 0)),
            row_spec, vec_spec, vec_spec,
            pl.BlockSpec((d, ROUTE_W), lambda i: (0, 0)),
            pl.BlockSpec((1, ROUTE_W), lambda i: (0, 0)),
        ],
        out_specs=[row_spec, pl.BlockSpec((bm, d // 2), lambda i: (i, 0)),
                   pl.BlockSpec((ROUTE_ROWS, bm), lambda i: (0, i)), route_spec,
                   pl.BlockSpec((1, ROUTE_W), lambda i: (0, 0))],
        out_shape=[jax.ShapeDtypeStruct((m, d), F32), jax.ShapeDtypeStruct((m, d // 2), jnp.uint32),
                   jax.ShapeDtypeStruct((ROUTE_ROWS, m), jnp.int32), jax.ShapeDtypeStruct((m, ROUTE_W), F32),
                   jax.ShapeDtypeStruct((1, ROUTE_W), F32)],
        compiler_params=_cparams("arbitrary"),
        name="out_proj_ln",
    )(merged, w_out_bf16, h, g.reshape(1, d), b.reshape(1, d), route_w, route_b)


def _dispatch_kernel(start_ref, nchunk_ref, dest_hbm, hp_ref, xs_ref, idx_ref, zero_ref, idx_sem, row_sem, zero_sem,
                     *, n_expert_rows):
    i = pl.program_id(0)
    groups = hp_ref.shape[0]
    bm = groups * SUBLANES
    n_idx = TOP_K * bm

    @pl.when(i == 0)
    def _():
        zero_ref[...] = jnp.zeros(zero_ref.shape, zero_ref.dtype)

        def last_chunk(e):
            first = pl.multiple_of(start_ref[e] + (nchunk_ref[e] - 1) * MOE_BM, MOE_BM)
            return pltpu.make_async_copy(zero_ref, xs_ref.at[pl.ds(first, MOE_BM)], zero_sem)

        for e in range(N_EXPERTS):
            @pl.when(nchunk_ref[e] > 0)
            def _():
                last_chunk(e).start()

        for e in range(N_EXPERTS):
            @pl.when(nchunk_ref[e] > 0)
            def _():
                last_chunk(e).wait()

        used = start_ref[N_EXPERTS - 1] + nchunk_ref[N_EXPERTS - 1] * MOE_BM

        def tail_chunk(j):
            return pltpu.make_async_copy(
                zero_ref, xs_ref.at[pl.ds(pl.multiple_of(used + j * MOE_BM, MOE_BM), MOE_BM)], zero_sem)

        n_tail = (n_expert_rows - used) // MOE_BM
        lax.fori_loop(0, n_tail, lambda j, c: (tail_chunk(j).start(), c)[1], 0)
        lax.fori_loop(0, n_tail, lambda j, c: (tail_chunk(j).wait(), c)[1], 0)

    idx_copy = pltpu.make_async_copy(dest_hbm.at[pl.ds(pl.multiple_of(i * n_idx, n_idx), n_idx)], idx_ref, idx_sem)
    idx_copy.start()
    idx_copy.wait()

    def issue(g, carry):
        for s in range(SUBLANES):
            for k in range(TOP_K):
                slot = idx_ref[g * SUBLANES + (k * bm + s)]
                pltpu.make_async_copy(hp_ref.at[g, pl.ds(s, 1)], xs_ref.at[pl.ds(slot, 1)],
                                      row_sem).start(priority=k % 2)
        return carry

    lax.fori_loop(0, groups, issue, 0)
    for k in range(TOP_K):
        pltpu.make_async_copy(xs_ref.at[pl.ds(0, bm)], xs_ref.at[pl.ds(0, bm)], row_sem).wait()


def _dispatch(hp, dest, chunk_start, n_chunks, n_expert_rows, n_slots):
    m, c = hp.shape
    bm = DISPATCH_BM
    assert m % bm == 0 and bm % SUBLANES == 0
    hp = hp.reshape(m // SUBLANES, SUBLANES, c)
    grid_spec = pltpu.PrefetchScalarGridSpec(
        num_scalar_prefetch=2,
        grid=(m // bm,),
        in_specs=[
            pl.BlockSpec(memory_space=pl.ANY),
            pl.BlockSpec((bm // SUBLANES, SUBLANES, c), lambda i, st, nc: (i, 0, 0)),
        ],
        out_specs=pl.BlockSpec(memory_space=pl.ANY),
        scratch_shapes=[pltpu.SMEM((TOP_K * bm,), jnp.int32), pltpu.VMEM((MOE_BM, c), jnp.uint32),
                        pltpu.SemaphoreType.DMA(()), pltpu.SemaphoreType.DMA(()), pltpu.SemaphoreType.DMA(())],
    )
    return pl.pallas_call(
        functools.partial(_dispatch_kernel, n_expert_rows=n_expert_rows),
        grid_spec=grid_spec,
        out_shape=jax.ShapeDtypeStruct((n_slots, c), jnp.uint32),
        compiler_params=_cparams("arbitrary"),
        name="moe_dispatch",
    )(chunk_start, n_chunks, dest, hp)


def _expert_kernel(start_ref, nchunk_ref, xs_hbm, wg_hbm, wu_hbm, wd_hbm, yb_hbm, wg_f, wu_f, wd_f, wg_s, wu_s, wd_s,
                   x_buf, o_buf, w_sem, in_sem, out_sem, *, layer):
    e = pl.program_id(0)
    n = nchunk_ref[e]
    bm = MOE_BM
    first = start_ref[e] // bm
    total = start_ref[N_EXPERTS - 1] // bm + nchunk_ref[N_EXPERTS - 1]
    wslot = e % 2

    def w_copies(expert, slot):
        return [pltpu.make_async_copy(src.at[layer, expert], dst.at[slot], w_sem.at[slot])
                for src, dst in ((wg_hbm, wg_f), (wu_hbm, wu_f), (wd_hbm, wd_f))]

    def rows(g):
        return pl.ds(pl.multiple_of(g * bm, bm), bm)

    def x_copy(g):
        slot = g % EXPERT_X_BUFFERS
        return pltpu.make_async_copy(xs_hbm.at[rows(g)], x_buf.at[slot], in_sem.at[slot])

    def o_copy(g):
        slot = g % 2
        return pltpu.make_async_copy(o_buf.at[slot], yb_hbm.at[rows(g)], out_sem.at[slot])

    @pl.when(e == 0)
    def _():
        for ahead in range(EXPERT_X_BUFFERS - 1):
            @pl.when(ahead < total)
            def _():
                x_copy(ahead).start()

        for c in w_copies(0, 0):
            c.start(priority=1)

    @pl.when(e + 1 < N_EXPERTS)
    def _():
        for c in w_copies(e + 1, 1 - wslot):
            c.start(priority=1)

    for c in w_copies(e, wslot):
        c.wait()

    @pl.when(n > 0)
    def _():
        wg_s[...] = wg_f[wslot].astype(BF16)
        wu_s[...] = wu_f[wslot].astype(BF16)
        wd_s[...] = wd_f[wslot].astype(BF16)

        def chunk(j, carry):
            g = first + j
            x_copy(g).wait()

            @pl.when(g + EXPERT_X_BUFFERS - 1 < total)
            def _():
                x_copy(g + EXPERT_X_BUFFERS - 1).start()

            @pl.when(g >= 2)
            def _():
                o_copy(g - 2).wait()

            x = _unpack_bf16_pairs(x_buf[g % EXPERT_X_BUFFERS])
            gate = jnp.dot(x, wg_s[...], preferred_element_type=F32)
            up = jnp.dot(x, wu_s[...], preferred_element_type=F32)
            hdn = (gate * _sigmoid(gate)) * up
            y = jnp.dot(hdn.astype(BF16), wd_s[...], preferred_element_type=F32)
            o_buf[g % 2] = _pack_bf16_pairs(y.astype(BF16))
            o_copy(g).start()
            return carry

        lax.fori_loop(0, n, chunk, 0)

    @pl.when(e == N_EXPERTS - 1)
    def _():
        for back in (2, 1):
            @pl.when(total >= back)
            def _():
                o_copy(total - back).wait()

        o_buf[0] = jnp.zeros(o_buf.shape[1:], o_buf.dtype)

        def fill(g, carry):
            tail = pltpu.make_async_copy(o_buf.at[0], yb_hbm.at[rows(g)], out_sem.at[0])
            tail.start()
            tail.wait()
            return carry

        lax.fori_loop(total, yb_hbm.shape[0] // bm, fill, 0)


def _experts(xs, n_rows, chunk_start, n_chunks, w_gate, w_up, w_down, layer):
    d = D_MODEL
    bm = MOE_BM
    any_spec = pl.BlockSpec(memory_space=pl.ANY)
    grid_spec = pltpu.PrefetchScalarGridSpec(
        num_scalar_prefetch=2,
        grid=(N_EXPERTS,),
        in_specs=[any_spec, any_spec, any_spec, any_spec],
        out_specs=any_spec,
        scratch_shapes=[
            pltpu.VMEM((2, d, D_EXPERT), F32),
            pltpu.VMEM((2, d, D_EXPERT), F32),
            pltpu.VMEM((2, D_EXPERT, d), F32),
            pltpu.VMEM((d, D_EXPERT), BF16),
            pltpu.VMEM((d, D_EXPERT), BF16),
            pltpu.VMEM((D_EXPERT, d), BF16),
            pltpu.VMEM((EXPERT_X_BUFFERS, bm, d // 2), jnp.uint32),
            pltpu.VMEM((2, bm, d // 2), jnp.uint32),
            pltpu.SemaphoreType.DMA((2,)),
            pltpu.SemaphoreType.DMA((EXPERT_X_BUFFERS,)),
            pltpu.SemaphoreType.DMA((2,)),
        ],
    )
    return pl.pallas_call(
        functools.partial(_expert_kernel, layer=layer),
        grid_spec=grid_spec,
        out_shape=jax.ShapeDtypeStruct((n_rows, d // 2), jnp.uint32),
        compiler_params=_cparams("arbitrary"),
        name="expert_mlp",
    )(chunk_start, n_chunks, xs, w_gate, w_up, w_down)


def _combine_ln_kernel(dest_hbm, h_ref, rw_ref, g_ref, b_ref, yb_hbm, *rest, batch, lp, final):
    if final:
        out_hbm, idx_ref, rows_ref, y_buf, idx_sem, row_sem, out_sem = rest
    else:
        hf_ref, hb_ref, idx_ref, rows_ref, idx_sem, row_sem = rest
    i = pl.program_id(0)
    n_tiles = pl.num_programs(0)
    bm = h_ref.shape[0]
    n_idx = TOP_K * bm
    buf = i % 2

    def idx_copy(tile):
        half = pl.ds(pl.multiple_of((tile % 2) * n_idx, n_idx), n_idx)
        return pltpu.make_async_copy(
            dest_hbm.at[pl.ds(pl.multiple_of(tile * n_idx, n_idx), n_idx)], idx_ref.at[half], idx_sem.at[tile % 2])

    def start_gather(tile):
        into = tile % 2
        idx_copy(tile).wait()

        def issue(g, carry):
            base = into * n_idx + g * SUBLANES
            for s in range(SUBLANES):
                for k in range(TOP_K):
                    slot = idx_ref[base + (k * bm + s)]
                    pltpu.make_async_copy(yb_hbm.at[pl.ds(slot, 1)], rows_ref.at[into, k, g, pl.ds(s, 1)],
                                          row_sem.at[into]).start(priority=k % 2)
            return carry

        lax.fori_loop(0, bm // SUBLANES, issue, 0)

        @pl.when(tile + 2 < n_tiles)
        def _():
            idx_copy(tile + 2).start()

    @pl.when(i == 0)
    def _():
        idx_copy(0).start()

        @pl.when(1 < n_tiles)
        def _():
            idx_copy(1).start()

        start_gather(0)

    @pl.when(i + 1 < n_tiles)
    def _():
        start_gather(i + 1)

    for k in range(TOP_K):
        pltpu.make_async_copy(yb_hbm.at[pl.ds(0, bm)], yb_hbm.at[pl.ds(0, bm)], row_sem.at[buf]).wait()

    d = h_ref.shape[1]

    def expert_rows(k):
        return _unpack_bf16_pairs(rows_ref[buf, k].reshape(bm, d // 2), F32)

    ffn = rw_ref[:, 0:1] * expert_rows(0)
    for k in range(1, TOP_K):
        ffn = ffn + rw_ref[:, k:k + 1] * expert_rows(k)
    y = _layer_norm(ALPHA * h_ref[...] + ffn, g_ref[...], b_ref[...])
    if not final:
        y = jnp.where(_flat_valid_rows(i * bm, bm, batch, lp), y, 0.0)
        hf_ref[...] = y
        hb_ref[...] = y.astype(BF16)
        return

    nb = lp // BLOCK

    def for_each_out_block(step, fn):
        for j in range(bm // BLOCK):
            blk = step * (bm // BLOCK) + j
            seq_blk = blk % nb

            @pl.when(seq_blk >= 1)
            def _():
                dst = pl.multiple_of(((blk // nb) * (nb - 1) + seq_blk - 1) * BLOCK, BLOCK)
                fn(pltpu.make_async_copy(y_buf.at[pl.ds(j * BLOCK, BLOCK)], out_hbm.at[pl.ds(dst, BLOCK)], out_sem))

    @pl.when(i > 0)
    def _():
        for_each_out_block(i - 1, lambda c: c.wait())

    y_buf[...] = y
    for_each_out_block(i, lambda c: c.start(priority=1))

    @pl.when(i == n_tiles - 1)
    def _():
        for_each_out_block(i, lambda c: c.wait())


def _combine_ln(h, yb, dest, rw, g, b, batch, lp, final):
    m, d = h.shape
    bm = COMBINE_BM
    assert m % bm == 0 and bm % BLOCK == 0
    row_spec = pl.BlockSpec((bm, d), lambda i: (i, 0))
    vec_spec = pl.BlockSpec((1, d), lambda i: (0, 0))
    scratch = [pltpu.SMEM((2 * TOP_K * bm,), jnp.int32),
               pltpu.VMEM((2, TOP_K, bm // SUBLANES, SUBLANES, d // 2), jnp.uint32)]
    sems = [pltpu.SemaphoreType.DMA((2,)), pltpu.SemaphoreType.DMA((2,))]
    if final:
        out_specs = pl.BlockSpec(memory_space=pl.ANY)
        out_shape = jax.ShapeDtypeStruct((batch * (lp - BLOCK), d), F32)
        scratch = scratch + [pltpu.VMEM((bm, d), F32)] + sems + [pltpu.SemaphoreType.DMA(())]
    else:
        out_specs = [row_spec, row_spec]
        out_shape = [jax.ShapeDtypeStruct((m, d), F32), jax.ShapeDtypeStruct((m, d), BF16)]
        scratch = scratch + sems
    return pl.pallas_call(
        functools.partial(_combine_ln_kernel, batch=batch, lp=lp, final=final),
        grid=(m // bm,),
        in_specs=[
            pl.BlockSpec(memory_space=pl.ANY),
            row_spec,
            pl.BlockSpec((bm, ROUTE_W), lambda i: (i, 0)),
            vec_spec, vec_spec,
            pl.BlockSpec(memory_space=pl.ANY),
        ],
        out_specs=out_specs,
        out_shape=out_shape,
        scratch_shapes=scratch,
        compiler_params=_cparams("arbitrary"),
        name="moe_combine_out" if final else "moe_combine_ln",
    )(dest, h, rw, g.reshape(1, d), b.reshape(1, d), yb)


def _slot_tables(ri, cnt, batch, lp):
    m = ri.shape[1]
    bm = MOE_BM
    eid = ri[0:TOP_K]
    rank = ri[TOP_K:2 * TOP_K]
    counts = cnt[0, :N_EXPERTS].astype(jnp.int32)
    padded = (counts + bm - 1) // bm * bm
    pad_end = jnp.cumsum(padded)
    pad_start = pad_end - padded
    n_real = batch * (lp - PAD) * TOP_K
    nblk = -(-(n_real + N_EXPERTS * (bm - 1)) // bm)
    cap = nblk * bm
    start = jnp.zeros_like(eid)
    for e in range(N_EXPERTS):
        start = jnp.where(eid == e, pad_start[e], start)
    row = jnp.arange(m, dtype=jnp.int32)
    pos = row % lp
    spare = cap + ((row // lp) * PAD + pos)[None, :] * TOP_K + jnp.arange(TOP_K, dtype=jnp.int32)[:, None]
    valid = (pos >= PAD)[None, :]
    dest = jnp.where(valid, start + rank, spare)
    dest_read = jnp.where(valid, dest, 0)
    n_slots = cap + batch * PAD * TOP_K

    def by_tile(idx, tile):
        return idx.reshape(TOP_K, m // tile, tile).transpose(1, 0, 2).reshape(m * TOP_K)

    return (by_tile(dest, DISPATCH_BM), by_tile(dest_read, COMBINE_BM), pad_start.astype(jnp.int32),
            (padded // bm).astype(jnp.int32), cap, n_slots)


def kernel(x, meta, ln_emb_g, ln_emb_b, w_in, pool_w, pool_scale, attn_sink, conv_w, conv_b, lru_wa, lru_ba,
           lru_wx, lru_bx, lru_lambda, proj_pool, proj_attn, proj_lru, w_out, ln1_g, ln1_b, router_grp_w,
           router_grp_b, router_exp_w, router_exp_b, exp_w_gate, exp_w_up, exp_w_down, ln2_g, ln2_b):
    batch, seq, d = x.shape
    lp = PAD + N_META + seq
    m = batch * lp

    hf, hb = _embed(x, meta, ln_emb_g, ln_emb_b)
    w_in_b = w_in.astype(BF16)
    pool_w_b = pool_w.astype(BF16)
    wa_b = lru_wa.astype(BF16)
    wx_b = lru_wx.astype(BF16)
    wp_b = proj_pool.astype(BF16)
    wat_b = proj_attn.astype(BF16)
    wl_b = proj_lru.astype(BF16)
    w_out_b = w_out.astype(BF16)
    route_pad = ROUTE_W - N_GROUPS - N_EXPERTS

    for l in range(DEPTH):
        cols = _inproj(hb, w_in_b, l, gates=False)
        gates = _inproj(hb, w_in_b, l, gates=True)
        pool_o = _pool(cols, pool_w_b[l], pool_scale[l], batch, lp)
        attn_o = _attention(cols, attn_sink[l], batch, lp)
        lru_o = _lru(cols, conv_w[l], conv_b[l], wa_b[l], lru_ba[l], wx_b[l], lru_bx[l], lru_lambda[l],
                     batch, lp)
        merged = _merge(pool_o, attn_o, lru_o, gates, wp_b, wat_b, wl_b, l)
        route_w = jnp.concatenate(
            [router_grp_w[l], router_exp_w[l], jnp.zeros((d, route_pad), F32)], axis=1).astype(BF16)
        route_b = jnp.concatenate(
            [router_grp_b[l], router_exp_b[l], jnp.zeros((route_pad,), F32)]).reshape(1, ROUTE_W)
        h1f, h1p, ri, rw, cnt = _outproj_ln(merged, w_out_b, hf, ln1_g[l], ln1_b[l], route_w, route_b, l,
                                            batch, lp)
        dest, dest_read, chunk_start, n_chunks, cap, n_slots = _slot_tables(ri, cnt, batch, lp)
        xs = _dispatch(h1p, dest, chunk_start, n_chunks, cap, n_slots)
        yb = _experts(xs, cap, chunk_start, n_chunks, exp_w_gate, exp_w_up, exp_w_down, l)
        if l + 1 < DEPTH:
            hf, hb = _combine_ln(h1f, yb, dest_read, rw, ln2_g[l], ln2_b[l], batch, lp, final=False)
        else:
            out = _combine_ln(h1f, yb, dest_read, rw, ln2_g[l], ln2_b[l], batch, lp, final=True)

    return out.reshape(batch, seq, d)
```

```python
import functools

import jax
import jax.numpy as jnp
from jax import lax
from jax.experimental import pallas as pl
from jax.experimental.pallas import tpu as pltpu

F32 = jnp.float32
BF16 = jnp.bfloat16

D_MODEL = 2048
DEPTH = 2
N_META = 16
POOL_WINDOWS = (2, 4, 8, 16)
POOL_WIDTH = D_MODEL // 2
POOL_GROUP = POOL_WIDTH // len(POOL_WINDOWS)
N_HEADS = 16
N_KV_HEADS = 4
HEAD_DIM = 64
Q_PER_KV = N_HEADS // N_KV_HEADS
WINDOW = 128
BLOCK = 128
NEG = -1e30
LRU_WIDTH = D_MODEL // 2
LRU_BLOCKS = 4
LRU_BLOCK = LRU_WIDTH // LRU_BLOCKS
CONV_WIDTH = 4
LRU_C = 8.0
N_GROUPS = 4
EXPERTS_PER_GROUP = 8
N_EXPERTS = N_GROUPS * EXPERTS_PER_GROUP
TOP_K = 2
D_EXPERT = D_MODEL // 4
LN_EPS = 1e-5
ALPHA = (2.0 * DEPTH) ** 0.25

PAD = BLOCK - N_META
ATT_W = N_HEADS * HEAD_DIM
KV_W = N_KV_HEADS * HEAD_DIM
OFF_POOL = 0
OFF_Q = OFF_POOL + POOL_WIDTH
OFF_K = OFF_Q + ATT_W
OFF_V = OFF_K + KV_W
OFF_LX = OFF_V + KV_W
OFF_LY = OFF_LX + LRU_WIDTH
OFF_GATE = OFF_LY + LRU_WIDTH
IN_COLS = OFF_GATE + 3 * D_MODEL

VMEM_LIMIT_BYTES = 56 * 1024 * 1024
SUBLANES = 8

SEQ_TILE = 3 * BLOCK
INPROJ_BM = 1536
INPROJ_BN = 1536
MERGE_BM = 768
MERGE_BN = 1024
OUT_BM = 512
OUT_SPLIT = 2
MOE_BM = 256
DISPATCH_BM = 1536
COMBINE_BM = 512
EXPERT_X_BUFFERS = 3
ROUTE_W = 128
ROUTE_ROWS = 8


def _cparams(*sem):
    return pltpu.CompilerParams(dimension_semantics=sem, vmem_limit_bytes=VMEM_LIMIT_BYTES)


def _layer_norm(x, g, b):
    mu = jnp.mean(x, axis=-1, keepdims=True)
    xc = x - mu
    var = jnp.mean(xc * xc, axis=-1, keepdims=True)
    return xc * lax.rsqrt(var + LN_EPS) * g + b


def _flat_valid_rows(row0, n_rows, batch, lp):
    r = row0 + lax.broadcasted_iota(jnp.int32, (n_rows, 1), 0)
    pad_row = jnp.zeros((n_rows, 1), jnp.bool_)
    for b in range(batch):
        pad_row = pad_row | ((r >= b * lp) & (r < b * lp + PAD))
    return jnp.logical_not(pad_row)


EMBED_BLOCKS = SEQ_TILE // BLOCK


def _embed_kernel(*refs):
    x_refs, (meta_ref, g_ref, b_ref, hf_ref, hb_ref) = refs[:EMBED_BLOCKS], refs[EMBED_BLOCKS:]
    t = pl.program_id(1)
    for j in range(EMBED_BLOCKS):
        src = x_refs[j][...]
        if j == 0:
            src = jnp.where(t == 0, meta_ref[...], src)
        y = _layer_norm(src, g_ref[...], b_ref[...])
        if j == 0:
            row = lax.broadcasted_iota(jnp.int32, (BLOCK, 1), 0)
            y = jnp.where((t > 0) | (row >= PAD), y, 0.0)
        hf_ref[pl.ds(j * BLOCK, BLOCK), :] = y
        hb_ref[pl.ds(j * BLOCK, BLOCK), :] = y.astype(BF16)


def _embed(x, meta, g, b):
    batch, seq, d = x.shape
    nblk = seq // BLOCK
    lp = (nblk + 1) * BLOCK
    nt = lp // SEQ_TILE
    m = batch * lp
    meta_tile = jnp.concatenate([jnp.zeros((PAD, d), F32), meta.astype(F32)], axis=0)
    row_spec = pl.BlockSpec((SEQ_TILE, d), lambda bi, t: (bi * nt + t, 0))
    vec_spec = pl.BlockSpec((1, d), lambda bi, t: (0, 0))
    x_spec = lambda j: pl.BlockSpec(
        (BLOCK, d), lambda bi, t: (bi * nblk + jnp.maximum(EMBED_BLOCKS * t + j - 1, 0), 0))
    x2 = x.reshape(batch * seq, d)
    return pl.pallas_call(
        _embed_kernel,
        grid=(batch, nt),
        in_specs=[x_spec(j) for j in range(EMBED_BLOCKS)]
        + [pl.BlockSpec((BLOCK, d), lambda bi, t: (0, 0)), vec_spec, vec_spec],
        out_specs=[row_spec, row_spec],
        out_shape=[jax.ShapeDtypeStruct((m, d), F32), jax.ShapeDtypeStruct((m, d), BF16)],
        compiler_params=_cparams("parallel", "arbitrary"),
        name="embed_ln",
    )(*([x2] * EMBED_BLOCKS), meta_tile, g.reshape(1, d), b.reshape(1, d))


def _sigmoid(x):
    return 0.5 * jnp.tanh(0.5 * x) + 0.5


def _inproj_kernel(x_ref, w_ref, o_ref, *, gates):
    acc = jnp.dot(x_ref[...], w_ref[...], preferred_element_type=F32)
    o_ref[...] = (_sigmoid(acc) if gates else acc).astype(o_ref.dtype)


def _inproj(hb, w_in_bf16, layer, gates):
    m, d = hb.shape
    bm, bn = INPROJ_BM, INPROJ_BN
    col0, width = (OFF_GATE, IN_COLS - OFF_GATE) if gates else (0, OFF_GATE)
    assert m % bm == 0 and width % bn == 0 and col0 % bn == 0
    return pl.pallas_call(
        functools.partial(_inproj_kernel, gates=gates),
        grid=(m // bm, width // bn),
        in_specs=[
            pl.BlockSpec((bm, d), lambda i, j: (i, 0)),
            pl.BlockSpec((None, d, bn), lambda i, j: (layer, 0, col0 // bn + j)),
        ],
        out_specs=pl.BlockSpec((bm, bn), lambda i, j: (i, j)),
        out_shape=jax.ShapeDtypeStruct((m, width), BF16),
        compiler_params=_cparams("parallel", "arbitrary"),
        name="in_proj_gates" if gates else "in_proj",
    )(hb, w_in_bf16)


POOL_HALO = 2 * max(POOL_WINDOWS)
assert POOL_WINDOWS == tuple(2 ** (g + 1) for g in range(len(POOL_WINDOWS)))


def _pool_kernel(u_ref, w_ref, scale_ref, o_ref, ext_ref, lvl_ref):
    t = pl.program_id(1)
    tile = SEQ_TILE
    halo = POOL_HALO

    @pl.when(t == 0)
    def _():
        ext_ref[pl.ds(0, halo), :] = jnp.zeros((halo, POOL_WIDTH), F32)

    @pl.when(t > 0)
    def _():
        ext_ref[pl.ds(0, halo), :] = ext_ref[pl.ds(tile, halo), :]

    ext_ref[pl.ds(halo, tile), :] = u_ref[...].astype(F32)

    pos = t * tile + lax.broadcasted_iota(jnp.int32, (tile, 1), 0) - PAD
    src, first = ext_ref, 0
    for gi, w in enumerate(POOL_WINDOWS):
        lane0 = gi * POOL_GROUP
        lanes = pl.ds(lane0, POOL_WIDTH - lane0)
        new_first = -(-(first + w // 2) // SUBLANES) * SUBLANES
        n_rows = tile + halo - new_first
        level = src[pl.ds(new_first, n_rows), lanes] + src[pl.ds(new_first - w // 2, n_rows), lanes]
        win = level[halo - new_first:, :POOL_GROUP]
        cols = pl.ds(lane0, POOL_GROUP)
        u = ext_ref[pl.ds(halo, tile), cols]
        cnt = jnp.clip(pos + 1, 1, w).astype(F32)
        delta = win / cnt - u
        mixed = jnp.dot(delta.astype(BF16), w_ref[gi], preferred_element_type=F32)
        o_ref[:, cols] = (mixed * scale_ref[:, cols]).astype(o_ref.dtype)
        if gi + 1 < len(POOL_WINDOWS):
            lvl_ref[gi, pl.ds(new_first, n_rows), lanes] = level
            src, first = lvl_ref.at[gi], new_first


def _pool(cols, pool_w_bf16, pool_scale, batch, lp):
    m = cols.shape[0]
    nt = lp // SEQ_TILE
    maxw = POOL_HALO
    return pl.pallas_call(
        _pool_kernel,
        grid=(batch, nt),
        in_specs=[
            pl.BlockSpec((SEQ_TILE, POOL_WIDTH), lambda b, t: (b * nt + t, OFF_POOL // POOL_WIDTH)),
            pl.BlockSpec((len(POOL_WINDOWS), POOL_GROUP, POOL_GROUP), lambda b, t: (0, 0, 0)),
            pl.BlockSpec((1, POOL_WIDTH), lambda b, t: (0, 0)),
        ],
        out_specs=pl.BlockSpec((SEQ_TILE, POOL_WIDTH), lambda b, t: (b * nt + t, 0)),
        out_shape=jax.ShapeDtypeStruct((m, POOL_WIDTH), BF16),
        scratch_shapes=[pltpu.VMEM((SEQ_TILE + maxw, POOL_WIDTH), F32),
                        pltpu.VMEM((len(POOL_WINDOWS) - 1, SEQ_TILE + maxw, POOL_WIDTH), F32)],
        compiler_params=_cparams("parallel", "arbitrary"),
        name="pool_mixer",
    )(cols, pool_w_bf16, pool_scale.reshape(1, POOL_WIDTH))


def _attn_bias():
    kj = jnp.arange(2 * BLOCK, dtype=jnp.int32)[:, None]
    qi = jnp.arange(BLOCK, dtype=jnp.int32)[None, :]
    dist = BLOCK + qi - kj
    in_window = (dist >= 0) & (dist < WINDOW)
    slopes = 2.0 ** (-8.0 * jnp.arange(1, N_HEADS + 1, dtype=F32) / N_HEADS)
    alibi = -slopes[:, None, None] * dist.astype(F32)[None]
    bias = jnp.where(in_window[None], alibi, NEG).reshape(N_KV_HEADS, Q_PER_KV, 2 * BLOCK, BLOCK)
    return bias.transpose(0, 2, 1, 3).reshape(N_KV_HEADS, 2 * BLOCK, Q_PER_KV * BLOCK)


def _attn_kernel(q_ref, kp_ref, kc_ref, vp_ref, vc_ref, bias_ref, sink_ref, o_ref):
    n = pl.program_id(1)
    q = q_ref[...] * (HEAD_DIM ** -0.5)

    def heads(early_keys):
        for kh in range(N_KV_HEADS):
            hs = pl.ds(kh * HEAD_DIM, HEAD_DIM)
            k2 = jnp.concatenate([kp_ref[:, hs], kc_ref[:, hs]], axis=0)
            v2 = jnp.concatenate([vp_ref[:, hs], vc_ref[:, hs]], axis=0)
            first = kh * Q_PER_KV
            qg = jnp.concatenate([q[:, (first + g) * HEAD_DIM:(first + g + 1) * HEAD_DIM]
                                  for g in range(Q_PER_KV)], axis=0)
            s = lax.dot_general(k2, qg, (((1,), (1,)), ((), ())), preferred_element_type=F32)
            s = s + bias_ref[kh]
            if early_keys is not None:
                s = s + early_keys
            sk = sink_ref[pl.ds(kh, 1), :]
            mx = jnp.maximum(jnp.max(s, axis=0, keepdims=True), sk)
            p = jnp.exp(s - mx)
            den = jnp.sum(p, axis=0, keepdims=True) + jnp.exp(sk - mx)
            pn = (p * (1.0 / den)).astype(BF16)
            o = lax.dot_general(pn, v2, (((0,), (0,)), ((), ())), preferred_element_type=F32)
            for g in range(Q_PER_KV):
                o_ref[:, pl.ds((first + g) * HEAD_DIM, HEAD_DIM)] = o[g * BLOCK:(g + 1) * BLOCK].astype(o_ref.dtype)

    @pl.when(n < 2)
    def _():
        k_pos = (n - 1) * BLOCK + lax.broadcasted_iota(jnp.int32, (2 * BLOCK, 1), 0)
        heads(jnp.where(k_pos < PAD, NEG, 0.0))

    @pl.when(n >= 2)
    def _():
        heads(None)


def _attention(cols, sink, batch, lp):
    m = cols.shape[0]
    nb = lp // BLOCK
    cur = lambda cb: (lambda b, n: (b * nb + n, cb))
    prev = lambda cb: (lambda b, n: (b * nb + jnp.maximum(n - 1, 0), cb))
    return pl.pallas_call(
        _attn_kernel,
        grid=(batch, nb),
        in_specs=[
            pl.BlockSpec((BLOCK, ATT_W), cur(OFF_Q // ATT_W)),
            pl.BlockSpec((BLOCK, KV_W), prev(OFF_K // KV_W)),
            pl.BlockSpec((BLOCK, KV_W), cur(OFF_K // KV_W)),
            pl.BlockSpec((BLOCK, KV_W), prev(OFF_V // KV_W)),
            pl.BlockSpec((BLOCK, KV_W), cur(OFF_V // KV_W)),
            pl.BlockSpec((N_KV_HEADS, 2 * BLOCK, Q_PER_KV * BLOCK), lambda b, n: (0, 0, 0)),
            pl.BlockSpec((N_KV_HEADS, Q_PER_KV * BLOCK), lambda b, n: (0, 0)),
        ],
        out_specs=pl.BlockSpec((BLOCK, ATT_W), lambda b, n: (b * nb + n, 0)),
        out_shape=jax.ShapeDtypeStruct((m, ATT_W), BF16),
        compiler_params=_cparams("parallel", "arbitrary"),
        name="swa_attention",
    )(cols, cols, cols, cols, cols, _attn_bias(),
      jnp.repeat(sink.astype(F32).reshape(N_KV_HEADS, Q_PER_KV), BLOCK, axis=1))


LRU_HALF = LRU_WIDTH // 2
LRU_HALO = 8
LRU_SCAN_UNROLL = 6


LOG2_E = 1.4426950408889634
GELU_C = 0.7978845608028654


def _gelu_tanh(x):
    half = 0.5 * x
    return half + half * jnp.tanh(x * (GELU_C + (GELU_C * 0.044715) * (x * x)))


def _lru_kernel(*refs):
    nh = LRU_WIDTH // LRU_HALF
    x_refs, y_refs = refs[:nh], refs[nh:2 * nh]
    (cw_ref, cb_ref, wa_ref, ba_ref, wx_ref, bx_ref, lam_ref, o_ref,
     ext_ref, a_ref, b_ref, carry_ref) = refs[2 * nh:]
    t = pl.program_id(1)
    tile = SEQ_TILE
    width = LRU_HALF
    pos = t * tile + lax.broadcasted_iota(jnp.int32, (tile, 1), 0)
    row = lax.broadcasted_iota(jnp.int32, (8, width), 0)

    for c in range(nh):
        lanes = pl.ds(c * width, width)
        ext, a_s, b_s, carry = ext_ref.at[c], a_ref.at[c], b_ref.at[c], carry_ref.at[c]

        @pl.when(t == 0)
        def _():
            ext[pl.ds(0, LRU_HALO), :] = jnp.zeros((LRU_HALO, width), F32)
            carry[...] = jnp.zeros((1, width), F32)

        @pl.when(t > 0)
        def _():
            ext[pl.ds(0, LRU_HALO), :] = ext[pl.ds(tile, LRU_HALO), :]

        ext[pl.ds(LRU_HALO, tile), :] = x_refs[c][...].astype(F32)

        xc = cb_ref[:, lanes] + cw_ref[pl.ds(CONV_WIDTH - 1, 1), lanes] * ext[pl.ds(LRU_HALO, tile), :]
        for j in range(CONV_WIDTH - 1):
            shift = CONV_WIDTH - 1 - j
            xc = xc + cw_ref[pl.ds(j, 1), lanes] * ext[pl.ds(LRU_HALO - shift, tile), :]

        xcb = xc.astype(BF16)
        ga_parts, gx_parts = [], []
        for blk in range(width // LRU_BLOCK):
            xb = xcb[:, blk * LRU_BLOCK:(blk + 1) * LRU_BLOCK]
            w_idx = c * (width // LRU_BLOCK) + blk
            ga_parts.append(jnp.dot(xb, wa_ref[w_idx], preferred_element_type=F32))
            gx_parts.append(jnp.dot(xb, wx_ref[w_idx], preferred_element_type=F32))
        gate_a = _sigmoid(jnp.concatenate(ga_parts, axis=1) + ba_ref[:, lanes])
        gate_x = _sigmoid(jnp.concatenate(gx_parts, axis=1) + bx_ref[:, lanes])

        neg_lam = -lam_ref[:, lanes]
        softplus = jnp.maximum(neg_lam, 0.0) + jnp.log1p(jnp.exp(-jnp.abs(neg_lam)))
        a = jnp.exp2(gate_a * ((-LRU_C * LOG2_E) * softplus))
        b_in = jnp.sqrt(1.0 - a * a) * gate_x * xc
        b_in = jnp.where(pos >= PAD, b_in, 0.0)
        a_s[...] = a
        b_s[...] = b_in

        def group(r, h_prev, a_s=a_s, b_s=b_s):
            rows = pl.ds(pl.multiple_of(r * 8, 8), 8)
            av = a_s[rows, :]
            bv = b_s[rows, :]
            for k in (1, 2, 4):
                a_sh = jnp.where(row >= k, pltpu.roll(av, k, 0), 1.0)
                b_sh = jnp.where(row >= k, pltpu.roll(bv, k, 0), 0.0)
                bv = av * b_sh + bv
                av = av * a_sh
            hv = av * h_prev + bv
            b_s[rows, :] = hv
            return hv[7:8, :]

        carry[...] = lax.fori_loop(0, tile // 8, group, carry[...], unroll=LRU_SCAN_UNROLL)
        o_ref[:, lanes] = (b_s[...] * _gelu_tanh(y_refs[c][...].astype(F32))).astype(o_ref.dtype)


def _lru(cols, conv_w, conv_b, wa_bf16, ba, wx_bf16, bx, lam, batch, lp):
    m = cols.shape[0]
    nt = lp // SEQ_TILE
    nh = LRU_WIDTH // LRU_HALF
    vec = lambda v: v.reshape(1, LRU_WIDTH).astype(F32)
    vec_spec = pl.BlockSpec((1, LRU_WIDTH), lambda b, t: (0, 0))
    w_spec = pl.BlockSpec((LRU_BLOCKS, LRU_BLOCK, LRU_BLOCK), lambda b, t: (0, 0, 0))
    half_spec = lambda off, c: pl.BlockSpec((SEQ_TILE, LRU_HALF), lambda b, t: (b * nt + t, off // LRU_HALF + c))
    return pl.pallas_call(
        _lru_kernel,
        grid=(batch, nt),
        in_specs=[half_spec(OFF_LX, c) for c in range(nh)] + [half_spec(OFF_LY, c) for c in range(nh)]
        + [pl.BlockSpec((CONV_WIDTH, LRU_WIDTH), lambda b, t: (0, 0)),
           vec_spec, w_spec, vec_spec, w_spec, vec_spec, vec_spec],
        out_specs=pl.BlockSpec((SEQ_TILE, LRU_WIDTH), lambda b, t: (b * nt + t, 0)),
        out_shape=jax.ShapeDtypeStruct((m, LRU_WIDTH), BF16),
        scratch_shapes=[
            pltpu.VMEM((nh, SEQ_TILE + LRU_HALO, LRU_HALF), F32),
            pltpu.VMEM((nh, SEQ_TILE, LRU_HALF), F32),
            pltpu.VMEM((nh, SEQ_TILE, LRU_HALF), F32),
            pltpu.VMEM((nh, 1, LRU_HALF), F32),
        ],
        compiler_params=_cparams("parallel", "arbitrary"),
        name="rglru",
    )(*([cols] * (2 * nh)), conv_w.astype(F32), vec(conv_b), wa_bf16, vec(ba), wx_bf16, vec(bx), vec(lam))


def _merge_kernel(p_ref, a_ref, r_ref, gp_ref, ga_ref, gr_ref, wp_ref, wa_ref, wr_ref, o_ref):
    acc = gp_ref[...].astype(F32) * jnp.dot(p_ref[...], wp_ref[...], preferred_element_type=F32)
    acc += ga_ref[...].astype(F32) * jnp.dot(a_ref[...], wa_ref[...], preferred_element_type=F32)
    acc += gr_ref[...].astype(F32) * jnp.dot(r_ref[...], wr_ref[...], preferred_element_type=F32)
    o_ref[...] = acc.astype(o_ref.dtype)


def _merge(pool_o, attn_o, lru_o, gates, wp, wa, wr, layer):
    m = pool_o.shape[0]
    bm, bn = MERGE_BM, MERGE_BN
    assert m % bm == 0 and D_MODEL % bn == 0
    x_spec = pl.BlockSpec((bm, POOL_WIDTH), lambda i, j: (i, 0))
    gate_spec = lambda k: pl.BlockSpec((bm, bn), lambda i, j: (i, k * D_MODEL // bn + j))
    w_spec = pl.BlockSpec((None, POOL_WIDTH, bn), lambda i, j: (layer, 0, j))
    return pl.pallas_call(
        _merge_kernel,
        grid=(m // bm, D_MODEL // bn),
        in_specs=[x_spec, x_spec, x_spec, gate_spec(0), gate_spec(1), gate_spec(2), w_spec, w_spec, w_spec],
        out_specs=pl.BlockSpec((bm, bn), lambda i, j: (i, j)),
        out_shape=jax.ShapeDtypeStruct((m, D_MODEL), BF16),
        compiler_params=_cparams("parallel", "arbitrary"),
        name="gated_merge",
    )(pool_o, attn_o, lru_o, gates, gates, gates, wp, wa, wr)


def _pack_bf16_pairs(yb):
    c = yb.shape[1] // 2
    lo = lax.bitcast_convert_type(yb[:, :c].astype(F32), jnp.uint32)
    hi = lax.bitcast_convert_type(yb[:, c:].astype(F32), jnp.uint32)
    return (hi & jnp.uint32(0xFFFF0000)) | (lo >> 16)


def _unpack_bf16_pairs(words, dtype=BF16):
    lo = lax.bitcast_convert_type(words << 16, F32)
    hi = lax.bitcast_convert_type(words & jnp.uint32(0xFFFF0000), F32)
    return jnp.concatenate([lo, hi], axis=1).astype(dtype)


def _route_tile(logits, valid, base):
    bm = logits.shape[0]
    lane = lax.broadcasted_iota(jnp.int32, (bm, ROUTE_W), 1)
    lane_f = lane.astype(F32)
    ninf = -jnp.inf
    big = float(ROUTE_W)

    gl = jnp.where(lane < N_GROUPS, logits, ninf)
    gmax = jnp.max(gl, axis=-1, keepdims=True)
    g = jnp.min(jnp.where(gl == gmax, lane_f, big), axis=-1, keepdims=True)
    p_g = 1.0 / jnp.sum(jnp.exp(gl - gmax), axis=-1, keepdims=True)

    first = N_GROUPS + g * EXPERTS_PER_GROUP
    sl = jnp.where((lane_f >= first) & (lane_f < first + EXPERTS_PER_GROUP), logits, ninf)
    m1 = jnp.max(sl, axis=-1, keepdims=True)
    i1 = jnp.min(jnp.where(sl == m1, lane_f, big), axis=-1, keepdims=True)
    ssum = jnp.sum(jnp.exp(sl - m1), axis=-1, keepdims=True)
    sl2 = jnp.where(lane_f == i1, ninf, sl)
    m2 = jnp.max(sl2, axis=-1, keepdims=True)
    i2 = jnp.min(jnp.where(sl2 == m2, lane_f, big), axis=-1, keepdims=True)
    p1 = 1.0 / ssum
    p2 = jnp.exp(m2 - m1) / ssum
    w1 = p_g * p1 / (p1 + p2)
    w2 = p_g * p2 / (p1 + p2)
    e1 = i1 - N_GROUPS
    e2 = i2 - N_GROUPS

    oh1 = (lane_f == e1) & valid
    oh2 = (lane_f == e2) & valid
    both = (oh1 | oh2).astype(F32)
    earlier = (lax.broadcasted_iota(jnp.int32, (bm, bm), 0) > lax.broadcasted_iota(jnp.int32, (bm, bm), 1))
    prefix = jnp.dot(earlier.astype(BF16), both.astype(BF16), preferred_element_type=F32) + base
    r1 = jnp.sum(jnp.where(oh1, prefix, 0.0), axis=-1, keepdims=True)
    r2 = jnp.sum(jnp.where(oh2, prefix, 0.0), axis=-1, keepdims=True)

    ri = jnp.where(lane == 0, e1, jnp.where(lane == 1, e2, jnp.where(lane == 2, r1, jnp.where(lane == 3, r2, 0.0))))
    rw = jnp.where(lane == 0, w1, jnp.where(lane == 1, w2, 0.0))
    ri_rows = jnp.transpose(ri)[0:ROUTE_ROWS, :].astype(jnp.int32)
    return ri_rows, rw, jnp.sum(both, axis=0, keepdims=True)


def _outproj_kernel(x_ref, w_ref, h_ref, g_ref, b_ref, rw_ref, rb_ref, hf_ref, hp_ref, ri_ref, rwt_ref, cnt_ref,
                    *, batch, lp):
    i = pl.program_id(0)
    bm = x_ref.shape[0]
    sub = bm // OUT_SPLIT

    @pl.when(i == 0)
    def _():
        cnt_ref[...] = jnp.zeros(cnt_ref.shape, F32)

    for s in range(OUT_SPLIT):
        rows = pl.ds(s * sub, sub)
        t = jnp.dot(x_ref[rows, :], w_ref[...], preferred_element_type=F32)
        y = _layer_norm(ALPHA * h_ref[rows, :] + t, g_ref[...], b_ref[...])
        valid = _flat_valid_rows(i * bm + s * sub, sub, batch, lp)
        y = jnp.where(valid, y, 0.0)
        yb = y.astype(BF16)
        hf_ref[rows, :] = y
        hp_ref[rows, :] = _pack_bf16_pairs(yb)
        logits = jnp.dot(yb, rw_ref[...], preferred_element_type=F32) + rb_ref[...]
        ri, rw, tile_cnt = _route_tile(logits, valid, cnt_ref[...])
        ri_ref[:, rows] = ri
        rwt_ref[rows, :] = rw
        cnt_ref[...] += tile_cnt


def _outproj_ln(merged, w_out_bf16, h, g, b, route_w, route_b, layer, batch, lp):
    m, d = h.shape
    bm = OUT_BM
    assert m % bm == 0
    row_spec = pl.BlockSpec((bm, d), lambda i: (i, 0))
    vec_spec = pl.BlockSpec((1, d), lambda i: (0, 0))
    route_spec = pl.BlockSpec((bm, ROUTE_W), lambda i: (i, 0))
    return pl.pallas_call(
        functools.partial(_outproj_kernel, batch=batch, lp=lp),
        grid=(m // bm,),
        in_specs=[
            row_spec,
            pl.BlockSpec((None, d, d), lambda i: (layer, 0, 0)),
            row_spec, vec_spec, vec_spec,
            pl.BlockSpec((d, ROUTE_W), lambda i: (0, 0)),
            pl.BlockSpec((1, ROUTE_W), lambda i: (0, 0)),
        ],
        out_specs=[row_spec, pl.BlockSpec((bm, d // 2), lambda i: (i, 0)),
                   pl.BlockSpec((ROUTE_ROWS, bm), lambda i: (0, i)), route_spec,
                   pl.BlockSpec((1, ROUTE_W), lambda i: (0, 0))],
        out_shape=[jax.ShapeDtypeStruct((m, d), F32), jax.ShapeDtypeStruct((m, d // 2), jnp.uint32),
                   jax.ShapeDtypeStruct((ROUTE_ROWS, m), jnp.int32), jax.ShapeDtypeStruct((m, ROUTE_W), F32),
                   jax.ShapeDtypeStruct((1, ROUTE_W), F32)],
        compiler_params=_cparams("arbitrary"),
        name="out_proj_ln",
    )(merged, w_out_bf16, h, g.reshape(1, d), b.reshape(1, d), route_w, route_b)


def _dispatch_kernel(start_ref, nchunk_ref, dest_hbm, hp_ref, xs_ref, idx_ref, zero_ref, idx_sem, row_sem, zero_sem,
                     *, n_expert_rows):
    i = pl.program_id(0)
    groups = hp_ref.shape[0]
    bm = groups * SUBLANES
    n_idx = TOP_K * bm

    @pl.when(i == 0)
    def _():
        zero_ref[...] = jnp.zeros(zero_ref.shape, zero_ref.dtype)

        def last_chunk(e):
            first = pl.multiple_of(start_ref[e] + (nchunk_ref[e] - 1) * MOE_BM, MOE_BM)
            return pltpu.make_async_copy(zero_ref, xs_ref.at[pl.ds(first, MOE_BM)], zero_sem)

        for e in range(N_EXPERTS):
            @pl.when(nchunk_ref[e] > 0)
            def _():
                last_chunk(e).start()

        for e in range(N_EXPERTS):
            @pl.when(nchunk_ref[e] > 0)
            def _():
                last_chunk(e).wait()

        used = start_ref[N_EXPERTS - 1] + nchunk_ref[N_EXPERTS - 1] * MOE_BM

        def tail_chunk(j):
            return pltpu.make_async_copy(
                zero_ref, xs_ref.at[pl.ds(pl.multiple_of(used + j * MOE_BM, MOE_BM), MOE_BM)], zero_sem)

        n_tail = (n_expert_rows - used) // MOE_BM
        lax.fori_loop(0, n_tail, lambda j, c: (tail_chunk(j).start(), c)[1], 0)
        lax.fori_loop(0, n_tail, lambda j, c: (tail_chunk(j).wait(), c)[1], 0)

    idx_copy = pltpu.make_async_copy(dest_hbm.at[pl.ds(pl.multiple_of(i * n_idx, n_idx), n_idx)], idx_ref, idx_sem)
    idx_copy.start()
    idx_copy.wait()

    def issue(g, carry):
        for s in range(SUBLANES):
            for k in range(TOP_K):
                slot = idx_ref[g * SUBLANES + (k * bm + s)]
                pltpu.make_async_copy(hp_ref.at[g, pl.ds(s, 1)], xs_ref.at[pl.ds(slot, 1)],
                                      row_sem).start(priority=k % 2)
        return carry

    lax.fori_loop(0, groups, issue, 0)
    for k in range(TOP_K):
        pltpu.make_async_copy(xs_ref.at[pl.ds(0, bm)], xs_ref.at[pl.ds(0, bm)], row_sem).wait()


def _dispatch(hp, dest, chunk_start, n_chunks, n_expert_rows, n_slots):
    m, c = hp.shape
    bm = DISPATCH_BM
    assert m % bm == 0 and bm % SUBLANES == 0
    hp = hp.reshape(m // SUBLANES, SUBLANES, c)
    grid_spec = pltpu.PrefetchScalarGridSpec(
        num_scalar_prefetch=2,
        grid=(m // bm,),
        in_specs=[
            pl.BlockSpec(memory_space=pl.ANY),
            pl.BlockSpec((bm // SUBLANES, SUBLANES, c), lambda i, st, nc: (i, 0, 0)),
        ],
        out_specs=pl.BlockSpec(memory_space=pl.ANY),
        scratch_shapes=[pltpu.SMEM((TOP_K * bm,), jnp.int32), pltpu.VMEM((MOE_BM, c), jnp.uint32),
                        pltpu.SemaphoreType.DMA(()), pltpu.SemaphoreType.DMA(()), pltpu.SemaphoreType.DMA(())],
    )
    return pl.pallas_call(
        functools.partial(_dispatch_kernel, n_expert_rows=n_expert_rows),
        grid_spec=grid_spec,
        out_shape=jax.ShapeDtypeStruct((n_slots, c), jnp.uint32),
        compiler_params=_cparams("arbitrary"),
        name="moe_dispatch",
    )(chunk_start, n_chunks, dest, hp)


def _expert_kernel(start_ref, nchunk_ref, xs_hbm, wg_hbm, wu_hbm, wd_hbm, yb_hbm, wg_f, wu_f, wd_f, wg_s, wu_s, wd_s,
                   x_buf, o_buf, w_sem, in_sem, out_sem, *, layer):
    e = pl.program_id(0)
    n = nchunk_ref[e]
    bm = MOE_BM
    first = start_ref[e] // bm
    total = start_ref[N_EXPERTS - 1] // bm + nchunk_ref[N_EXPERTS - 1]
    wslot = e % 2

    def w_copies(expert, slot):
        return [pltpu.make_async_copy(src.at[layer, expert], dst.at[slot], w_sem.at[slot])
                for src, dst in ((wg_hbm, wg_f), (wu_hbm, wu_f), (wd_hbm, wd_f))]

    def rows(g):
        return pl.ds(pl.multiple_of(g * bm, bm), bm)

    def x_copy(g):
        slot = g % EXPERT_X_BUFFERS
        return pltpu.make_async_copy(xs_hbm.at[rows(g)], x_buf.at[slot], in_sem.at[slot])

    def o_copy(g):
        slot = g % 2
        return pltpu.make_async_copy(o_buf.at[slot], yb_hbm.at[rows(g)], out_sem.at[slot])

    @pl.when(e == 0)
    def _():
        for ahead in range(EXPERT_X_BUFFERS - 1):
            @pl.when(ahead < total)
            def _():
                x_copy(ahead).start()

        for c in w_copies(0, 0):
            c.start(priority=1)

    @pl.when(e + 1 < N_EXPERTS)
    def _():
        for c in w_copies(e + 1, 1 - wslot):
            c.start(priority=1)

    for c in w_copies(e, wslot):
        c.wait()

    @pl.when(n > 0)
    def _():
        wg_s[...] = wg_f[wslot].astype(BF16)
        wu_s[...] = wu_f[wslot].astype(BF16)
        wd_s[...] = wd_f[wslot].astype(BF16)

        def chunk(j, carry):
            g = first + j
            x_copy(g).wait()

            @pl.when(g + EXPERT_X_BUFFERS - 1 < total)
            def _():
                x_copy(g + EXPERT_X_BUFFERS - 1).start()

            @pl.when(g >= 2)
            def _():
                o_copy(g - 2).wait()

            x = _unpack_bf16_pairs(x_buf[g % EXPERT_X_BUFFERS])
            gate = jnp.dot(x, wg_s[...], preferred_element_type=F32)
            up = jnp.dot(x, wu_s[...], preferred_element_type=F32)
            hdn = (gate * _sigmoid(gate)) * up
            y = jnp.dot(hdn.astype(BF16), wd_s[...], preferred_element_type=F32)
            o_buf[g % 2] = _pack_bf16_pairs(y.astype(BF16))
            o_copy(g).start()
            return carry

        lax.fori_loop(0, n, chunk, 0)

    @pl.when(e == N_EXPERTS - 1)
    def _():
        for back in (2, 1):
            @pl.when(total >= back)
            def _():
                o_copy(total - back).wait()

        o_buf[0] = jnp.zeros(o_buf.shape[1:], o_buf.dtype)

        def fill(g, carry):
            tail = pltpu.make_async_copy(o_buf.at[0], yb_hbm.at[rows(g)], out_sem.at[0])
            tail.start()
            tail.wait()
            return carry

        lax.fori_loop(total, yb_hbm.shape[0] // bm, fill, 0)


def _experts(xs, n_rows, chunk_start, n_chunks, w_gate, w_up, w_down, layer):
    d = D_MODEL
    bm = MOE_BM
    any_spec = pl.BlockSpec(memory_space=pl.ANY)
    grid_spec = pltpu.PrefetchScalarGridSpec(
        num_scalar_prefetch=2,
        grid=(N_EXPERTS,),
        in_specs=[any_spec, any_spec, any_spec, any_spec],
        out_specs=any_spec,
        scratch_shapes=[
            pltpu.VMEM((2, d, D_EXPERT), F32),
            pltpu.VMEM((2, d, D_EXPERT), F32),
            pltpu.VMEM((2, D_EXPERT, d), F32),
            pltpu.VMEM((d, D_EXPERT), BF16),
            pltpu.VMEM((d, D_EXPERT), BF16),
            pltpu.VMEM((D_EXPERT, d), BF16),
            pltpu.VMEM((EXPERT_X_BUFFERS, bm, d // 2), jnp.uint32),
            pltpu.VMEM((2, bm, d // 2), jnp.uint32),
            pltpu.SemaphoreType.DMA((2,)),
            pltpu.SemaphoreType.DMA((EXPERT_X_BUFFERS,)),
            pltpu.SemaphoreType.DMA((2,)),
        ],
    )
    return pl.pallas_call(
        functools.partial(_expert_kernel, layer=layer),
        grid_spec=grid_spec,
        out_shape=jax.ShapeDtypeStruct((n_rows, d // 2), jnp.uint32),
        compiler_params=_cparams("arbitrary"),
        name="expert_mlp",
    )(chunk_start, n_chunks, xs, w_gate, w_up, w_down)


def _combine_ln_kernel(dest_hbm, h_ref, rw_ref, g_ref, b_ref, yb_hbm, *rest, batch, lp, final):
    if final:
        out_hbm, idx_ref, rows_ref, y_buf, idx_sem, row_sem, out_sem = rest
    else:
        hf_ref, hb_ref, idx_ref, rows_ref, idx_sem, row_sem = rest
    i = pl.program_id(0)
    n_tiles = pl.num_programs(0)
    bm = h_ref.shape[0]
    n_idx = TOP_K * bm
    buf = i % 2

    def idx_copy(tile):
        half = pl.ds(pl.multiple_of((tile % 2) * n_idx, n_idx), n_idx)
        return pltpu.make_async_copy(
            dest_hbm.at[pl.ds(pl.multiple_of(tile * n_idx, n_idx), n_idx)], idx_ref.at[half], idx_sem.at[tile % 2])

    def start_gather(tile):
        into = tile % 2
        idx_copy(tile).wait()

        def issue(g, carry):
            base = into * n_idx + g * SUBLANES
            for s in range(SUBLANES):
                for k in range(TOP_K):
                    slot = idx_ref[base + (k * bm + s)]
                    pltpu.make_async_copy(yb_hbm.at[pl.ds(slot, 1)], rows_ref.at[into, k, g, pl.ds(s, 1)],
                                          row_sem.at[into]).start(priority=k % 2)
            return carry

        lax.fori_loop(0, bm // SUBLANES, issue, 0)

        @pl.when(tile + 2 < n_tiles)
        def _():
            idx_copy(tile + 2).start()

    @pl.when(i == 0)
    def _():
        idx_copy(0).start()

        @pl.when(1 < n_tiles)
        def _():
            idx_copy(1).start()

        start_gather(0)

    @pl.when(i + 1 < n_tiles)
    def _():
        start_gather(i + 1)

    for k in range(TOP_K):
        pltpu.make_async_copy(yb_hbm.at[pl.ds(0, bm)], yb_hbm.at[pl.ds(0, bm)], row_sem.at[buf]).wait()

    d = h_ref.shape[1]

    def expert_rows(k):
        return _unpack_bf16_pairs(rows_ref[buf, k].reshape(bm, d // 2), F32)

    ffn = rw_ref[:, 0:1] * expert_rows(0)
    for k in range(1, TOP_K):
        ffn = ffn + rw_ref[:, k:k + 1] * expert_rows(k)
    y = _layer_norm(ALPHA * h_ref[...] + ffn, g_ref[...], b_ref[...])
    if not final:
        y = jnp.where(_flat_valid_rows(i * bm, bm, batch, lp), y, 0.0)
        hf_ref[...] = y
        hb_ref[...] = y.astype(BF16)
        return

    nb = lp // BLOCK

    def for_each_out_block(step, fn):
        for j in range(bm // BLOCK):
            blk = step * (bm // BLOCK) + j
            seq_blk = blk % nb

            @pl.when(seq_blk >= 1)
            def _():
                dst = pl.multiple_of(((blk // nb) * (nb - 1) + seq_blk - 1) * BLOCK, BLOCK)
                fn(pltpu.make_async_copy(y_buf.at[pl.ds(j * BLOCK, BLOCK)], out_hbm.at[pl.ds(dst, BLOCK)], out_sem))

    @pl.when(i > 0)
    def _():
        for_each_out_block(i - 1, lambda c: c.wait())

    y_buf[...] = y
    for_each_out_block(i, lambda c: c.start(priority=1))

    @pl.when(i == n_tiles - 1)
    def _():
        for_each_out_block(i, lambda c: c.wait())


def _combine_ln(h, yb, dest, rw, g, b, batch, lp, final):
    m, d = h.shape
    bm = COMBINE_BM
    assert m % bm == 0 and bm % BLOCK == 0
    row_spec = pl.BlockSpec((bm, d), lambda i: (i, 0))
    vec_spec = pl.BlockSpec((1, d), lambda i: (0, 0))
    scratch = [pltpu.SMEM((2 * TOP_K * bm,), jnp.int32),
               pltpu.VMEM((2, TOP_K, bm // SUBLANES, SUBLANES, d // 2), jnp.uint32)]
    sems = [pltpu.SemaphoreType.DMA((2,)), pltpu.SemaphoreType.DMA((2,))]
    if final:
        out_specs = pl.BlockSpec(memory_space=pl.ANY)
        out_shape = jax.ShapeDtypeStruct((batch * (lp - BLOCK), d), F32)
        scratch = scratch + [pltpu.VMEM((bm, d), F32)] + sems + [pltpu.SemaphoreType.DMA(())]
    else:
        out_specs = [row_spec, row_spec]
        out_shape = [jax.ShapeDtypeStruct((m, d), F32), jax.ShapeDtypeStruct((m, d), BF16)]
        scratch = scratch + sems
    return pl.pallas_call(
        functools.partial(_combine_ln_kernel, batch=batch, lp=lp, final=final),
        grid=(m // bm,),
        in_specs=[
            pl.BlockSpec(memory_space=pl.ANY),
            row_spec,
            pl.BlockSpec((bm, ROUTE_W), lambda i: (i, 0)),
            vec_spec, vec_spec,
            pl.BlockSpec(memory_space=pl.ANY),
        ],
        out_specs=out_specs,
        out_shape=out_shape,
        scratch_shapes=scratch,
        compiler_params=_cparams("arbitrary"),
        name="moe_combine_out" if final else "moe_combine_ln",
    )(dest, h, rw, g.reshape(1, d), b.reshape(1, d), yb)


def _slot_tables(ri, cnt, batch, lp):
    m = ri.shape[1]
    bm = MOE_BM
    dense = (TOP_K, m // ROUTE_W, ROUTE_W)
    eid = ri[0:TOP_K].reshape(dense)
    rank = ri[TOP_K:2 * TOP_K].reshape(dense)
    counts = cnt[0, :N_EXPERTS].astype(jnp.int32)
    padded = (counts + bm - 1) // bm * bm
    pad_end = jnp.cumsum(padded)
    pad_start = pad_end - padded
    n_real = batch * (lp - PAD) * TOP_K
    nblk = -(-(n_real + N_EXPERTS * (bm - 1)) // bm)
    cap = nblk * bm
    start = jnp.zeros_like(eid)
    for e in range(N_EXPERTS):
        start = jnp.where(eid == e, pad_start[e], start)
    row = jnp.arange(m, dtype=jnp.int32).reshape(dense[1:])
    pos = row % lp
    spare = cap + ((row // lp) * PAD + pos)[None] * TOP_K + jnp.arange(TOP_K, dtype=jnp.int32)[:, None, None]
    valid = (pos >= PAD)[None]
    dest = jnp.where(valid, start + rank, spare)
    dest_read = jnp.where(valid, dest, 0)
    n_slots = cap + batch * PAD * TOP_K

    def by_tile(idx, tile):
        return idx.reshape(TOP_K, m // tile, tile).transpose(1, 0, 2).reshape(m * TOP_K)

    return (by_tile(dest, DISPATCH_BM), by_tile(dest_read, COMBINE_BM), pad_start.astype(jnp.int32),
            (padded // bm).astype(jnp.int32), cap, n_slots)


def kernel(x, meta, ln_emb_g, ln_emb_b, w_in, pool_w, pool_scale, attn_sink, conv_w, conv_b, lru_wa, lru_ba,
           lru_wx, lru_bx, lru_lambda, proj_pool, proj_attn, proj_lru, w_out, ln1_g, ln1_b, router_grp_w,
           router_grp_b, router_exp_w, router_exp_b, exp_w_gate, exp_w_up, exp_w_down, ln2_g, ln2_b):
    batch, seq, d = x.shape
    lp = PAD + N_META + seq
    m = batch * lp

    hf, hb = _embed(x, meta, ln_emb_g, ln_emb_b)
    w_in_b = w_in.astype(BF16)
    pool_w_b = pool_w.astype(BF16)
    wa_b = lru_wa.astype(BF16)
    wx_b = lru_wx.astype(BF16)
    wp_b = proj_pool.astype(BF16)
    wat_b = proj_attn.astype(BF16)
    wl_b = proj_lru.astype(BF16)
    w_out_b = w_out.astype(BF16)
    route_pad = ROUTE_W - N_GROUPS - N_EXPERTS

    for l in range(DEPTH):
        cols = _inproj(hb, w_in_b, l, gates=False)
        gates = _inproj(hb, w_in_b, l, gates=True)
        pool_o = _pool(cols, pool_w_b[l], pool_scale[l], batch, lp)
        attn_o = _attention(cols, attn_sink[l], batch, lp)
        lru_o = _lru(cols, conv_w[l], conv_b[l], wa_b[l], lru_ba[l], wx_b[l], lru_bx[l], lru_lambda[l],
                     batch, lp)
        merged = _merge(pool_o, attn_o, lru_o, gates, wp_b, wat_b, wl_b, l)
        route_w = jnp.concatenate(
            [router_grp_w[l], router_exp_w[l], jnp.zeros((d, route_pad), F32)], axis=1).astype(BF16)
        route_b = jnp.concatenate(
            [router_grp_b[l], router_exp_b[l], jnp.zeros((route_pad,), F32)]).reshape(1, ROUTE_W)
        h1f, h1p, ri, rw, cnt = _outproj_ln(merged, w_out_b, hf, ln1_g[l], ln1_b[l], route_w, route_b, l,
                                            batch, lp)
        dest, dest_read, chunk_start, n_chunks, cap, n_slots = _slot_tables(ri, cnt, batch, lp)
        xs = _dispatch(h1p, dest, chunk_start, n_chunks, cap, n_slots)
        yb = _experts(xs, cap, chunk_start, n_chunks, exp_w_gate, exp_w_up, exp_w_down, l)
        if l + 1 < DEPTH:
            hf, hb = _combine_ln(h1f, yb, dest_read, rw, ln2_g[l], ln2_b[l], batch, lp, final=False)
        else:
            out = _combine_ln(h1f, yb, dest_read, rw, ln2_g[l], ln2_b[l], batch, lp, final=True)

    return out.reshape(batch, seq, d)
```

```python
import functools

import jax
import jax.numpy as jnp
from jax import lax
from jax.experimental import pallas as pl
from jax.experimental.pallas import tpu as pltpu

F32 = jnp.float32
BF16 = jnp.bfloat16

D_MODEL = 2048
DEPTH = 2
N_META = 16
POOL_WINDOWS = (2, 4, 8, 16)
POOL_WIDTH = D_MODEL // 2
POOL_GROUP = POOL_WIDTH // len(POOL_WINDOWS)
N_HEADS = 16
N_KV_HEADS = 4
HEAD_DIM = 64
Q_PER_KV = N_HEADS // N_KV_HEADS
WINDOW = 128
BLOCK = 128
NEG = -1e30
LRU_WIDTH = D_MODEL // 2
LRU_BLOCKS = 4
LRU_BLOCK = LRU_WIDTH // LRU_BLOCKS
CONV_WIDTH = 4
LRU_C = 8.0
N_GROUPS = 4
EXPERTS_PER_GROUP = 8
N_EXPERTS = N_GROUPS * EXPERTS_PER_GROUP
TOP_K = 2
D_EXPERT = D_MODEL // 4
LN_EPS = 1e-5
ALPHA = (2.0 * DEPTH) ** 0.25

PAD = BLOCK - N_META
ATT_W = N_HEADS * HEAD_DIM
KV_W = N_KV_HEADS * HEAD_DIM
OFF_POOL = 0
OFF_Q = OFF_POOL + POOL_WIDTH
OFF_K = OFF_Q + ATT_W
OFF_V = OFF_K + KV_W
OFF_LX = OFF_V + KV_W
OFF_LY = OFF_LX + LRU_WIDTH
OFF_GATE = OFF_LY + LRU_WIDTH
IN_COLS = OFF_GATE + 3 * D_MODEL

VMEM_LIMIT_BYTES = 56 * 1024 * 1024
SUBLANES = 8

SEQ_TILE = 3 * BLOCK
INPROJ_BM = 1536
INPROJ_BN = 1536
MERGE_BM = 768
MERGE_BN = 1024
OUT_BM = 512
OUT_SPLIT = 2
MOE_BM = 256
DISPATCH_BM = 1536
COMBINE_BM = 512
EXPERT_X_BUFFERS = 3
ROW_TILES = D_MODEL // 2 // 128
ROUTE_W = 128
ROUTE_ROWS = 8


def _cparams(*sem):
    return pltpu.CompilerParams(dimension_semantics=sem, vmem_limit_bytes=VMEM_LIMIT_BYTES)


def _layer_norm(x, g, b):
    mu = jnp.mean(x, axis=-1, keepdims=True)
    xc = x - mu
    var = jnp.mean(xc * xc, axis=-1, keepdims=True)
    return xc * lax.rsqrt(var + LN_EPS) * g + b


def _flat_valid_rows(row0, n_rows, batch, lp):
    r = row0 + lax.broadcasted_iota(jnp.int32, (n_rows, 1), 0)
    pad_row = jnp.zeros((n_rows, 1), jnp.bool_)
    for b in range(batch):
        pad_row = pad_row | ((r >= b * lp) & (r < b * lp + PAD))
    return jnp.logical_not(pad_row)


EMBED_BLOCKS = SEQ_TILE // BLOCK


def _embed_kernel(*refs):
    x_refs, (meta_ref, g_ref, b_ref, hf_ref, hb_ref) = refs[:EMBED_BLOCKS], refs[EMBED_BLOCKS:]
    t = pl.program_id(1)
    for j in range(EMBED_BLOCKS):
        src = x_refs[j][...]
        if j == 0:
            src = jnp.where(t == 0, meta_ref[...], src)
        y = _layer_norm(src, g_ref[...], b_ref[...])
        if j == 0:
            row = lax.broadcasted_iota(jnp.int32, (BLOCK, 1), 0)
            y = jnp.where((t > 0) | (row >= PAD), y, 0.0)
        hf_ref[pl.ds(j * BLOCK, BLOCK), :] = y
        hb_ref[pl.ds(j * BLOCK, BLOCK), :] = y.astype(BF16)


def _embed(x, meta, g, b):
    batch, seq, d = x.shape
    nblk = seq // BLOCK
    lp = (nblk + 1) * BLOCK
    nt = lp // SEQ_TILE
    m = batch * lp
    meta_tile = jnp.concatenate([jnp.zeros((PAD, d), F32), meta.astype(F32)], axis=0)
    row_spec = pl.BlockSpec((SEQ_TILE, d), lambda bi, t: (bi * nt + t, 0))
    vec_spec = pl.BlockSpec((1, d), lambda bi, t: (0, 0))
    x_spec = lambda j: pl.BlockSpec(
        (BLOCK, d), lambda bi, t: (bi * nblk + jnp.maximum(EMBED_BLOCKS * t + j - 1, 0), 0))
    x2 = x.reshape(batch * seq, d)
    return pl.pallas_call(
        _embed_kernel,
        grid=(batch, nt),
        in_specs=[x_spec(j) for j in range(EMBED_BLOCKS)]
        + [pl.BlockSpec((BLOCK, d), lambda bi, t: (0, 0)), vec_spec, vec_spec],
        out_specs=[row_spec, row_spec],
        out_shape=[jax.ShapeDtypeStruct((m, d), F32), jax.ShapeDtypeStruct((m, d), BF16)],
        compiler_params=_cparams("parallel", "arbitrary"),
        name="embed_ln",
    )(*([x2] * EMBED_BLOCKS), meta_tile, g.reshape(1, d), b.reshape(1, d))


def _sigmoid(x):
    return 0.5 * jnp.tanh(0.5 * x) + 0.5


def _inproj_kernel(x_ref, w_ref, o_ref, *, gates):
    acc = jnp.dot(x_ref[...], w_ref[...], preferred_element_type=F32)
    o_ref[...] = (_sigmoid(acc) if gates else acc).astype(o_ref.dtype)


def _inproj(hb, w_in_bf16, layer, gates):
    m, d = hb.shape
    bm, bn = INPROJ_BM, INPROJ_BN
    col0, width = (OFF_GATE, IN_COLS - OFF_GATE) if gates else (0, OFF_GATE)
    assert m % bm == 0 and width % bn == 0 and col0 % bn == 0
    return pl.pallas_call(
        functools.partial(_inproj_kernel, gates=gates),
        grid=(m // bm, width // bn),
        in_specs=[
            pl.BlockSpec((bm, d), lambda i, j: (i, 0)),
            pl.BlockSpec((None, d, bn), lambda i, j: (layer, 0, col0 // bn + j)),
        ],
        out_specs=pl.BlockSpec((bm, bn), lambda i, j: (i, j)),
        out_shape=jax.ShapeDtypeStruct((m, width), BF16),
        compiler_params=_cparams("parallel", "arbitrary"),
        name="in_proj_gates" if gates else "in_proj",
    )(hb, w_in_bf16)


POOL_HALO = 2 * max(POOL_WINDOWS)
assert POOL_WINDOWS == tuple(2 ** (g + 1) for g in range(len(POOL_WINDOWS)))


def _pool_kernel(u_ref, w_ref, scale_ref, o_ref, ext_ref, lvl_ref):
    t = pl.program_id(1)
    tile = SEQ_TILE
    halo = POOL_HALO

    @pl.when(t == 0)
    def _():
        ext_ref[pl.ds(0, halo), :] = jnp.zeros((halo, POOL_WIDTH), F32)

    @pl.when(t > 0)
    def _():
        ext_ref[pl.ds(0, halo), :] = ext_ref[pl.ds(tile, halo), :]

    ext_ref[pl.ds(halo, tile), :] = u_ref[...].astype(F32)

    pos = t * tile + lax.broadcasted_iota(jnp.int32, (tile, 1), 0) - PAD
    src, first = ext_ref, 0
    for gi, w in enumerate(POOL_WINDOWS):
        lane0 = gi * POOL_GROUP
        lanes = pl.ds(lane0, POOL_WIDTH - lane0)
        new_first = -(-(first + w // 2) // SUBLANES) * SUBLANES
        n_rows = tile + halo - new_first
        level = src[pl.ds(new_first, n_rows), lanes] + src[pl.ds(new_first - w // 2, n_rows), lanes]
        win = level[halo - new_first:, :POOL_GROUP]
        cols = pl.ds(lane0, POOL_GROUP)
        u = ext_ref[pl.ds(halo, tile), cols]
        cnt = jnp.clip(pos + 1, 1, w).astype(F32)
        delta = win / cnt - u
        mixed = jnp.dot(delta.astype(BF16), w_ref[gi], preferred_element_type=F32)
        o_ref[:, cols] = (mixed * scale_ref[:, cols]).astype(o_ref.dtype)
        if gi + 1 < len(POOL_WINDOWS):
            lvl_ref[gi, pl.ds(new_first, n_rows), lanes] = level
            src, first = lvl_ref.at[gi], new_first


def _pool(cols, pool_w_bf16, pool_scale, batch, lp):
    m = cols.shape[0]
    nt = lp // SEQ_TILE
    maxw = POOL_HALO
    return pl.pallas_call(
        _pool_kernel,
        grid=(batch, nt),
        in_specs=[
            pl.BlockSpec((SEQ_TILE, POOL_WIDTH), lambda b, t: (b * nt + t, OFF_POOL // POOL_WIDTH)),
            pl.BlockSpec((len(POOL_WINDOWS), POOL_GROUP, POOL_GROUP), lambda b, t: (0, 0, 0)),
            pl.BlockSpec((1, POOL_WIDTH), lambda b, t: (0, 0)),
        ],
        out_specs=pl.BlockSpec((SEQ_TILE, POOL_WIDTH), lambda b, t: (b * nt + t, 0)),
        out_shape=jax.ShapeDtypeStruct((m, POOL_WIDTH), BF16),
        scratch_shapes=[pltpu.VMEM((SEQ_TILE + maxw, POOL_WIDTH), F32),
                        pltpu.VMEM((len(POOL_WINDOWS) - 1, SEQ_TILE + maxw, POOL_WIDTH), F32)],
        compiler_params=_cparams("parallel", "arbitrary"),
        name="pool_mixer",
    )(cols, pool_w_bf16, pool_scale.reshape(1, POOL_WIDTH))


def _attn_bias():
    kj = jnp.arange(2 * BLOCK, dtype=jnp.int32)[:, None]
    qi = jnp.arange(BLOCK, dtype=jnp.int32)[None, :]
    dist = BLOCK + qi - kj
    in_window = (dist >= 0) & (dist < WINDOW)
    slopes = 2.0 ** (-8.0 * jnp.arange(1, N_HEADS + 1, dtype=F32) / N_HEADS)
    alibi = -slopes[:, None, None] * dist.astype(F32)[None]
    bias = jnp.where(in_window[None], alibi, NEG).reshape(N_KV_HEADS, Q_PER_KV, 2 * BLOCK, BLOCK)
    return bias.transpose(0, 2, 1, 3).reshape(N_KV_HEADS, 2 * BLOCK, Q_PER_KV * BLOCK)


def _attn_kernel(q_ref, kp_ref, kc_ref, vp_ref, vc_ref, bias_ref, sink_ref, o_ref):
    n = pl.program_id(1)
    q = q_ref[...] * (HEAD_DIM ** -0.5)

    def heads(early_keys):
        for kh in range(N_KV_HEADS):
            hs = pl.ds(kh * HEAD_DIM, HEAD_DIM)
            k2 = jnp.concatenate([kp_ref[:, hs], kc_ref[:, hs]], axis=0)
            v2 = jnp.concatenate([vp_ref[:, hs], vc_ref[:, hs]], axis=0)
            first = kh * Q_PER_KV
            qg = jnp.concatenate([q[:, (first + g) * HEAD_DIM:(first + g + 1) * HEAD_DIM]
                                  for g in range(Q_PER_KV)], axis=0)
            s = lax.dot_general(k2, qg, (((1,), (1,)), ((), ())), preferred_element_type=F32)
            s = s + bias_ref[kh]
            if early_keys is not None:
                s = s + early_keys
            sk = sink_ref[pl.ds(kh, 1), :]
            mx = jnp.maximum(jnp.max(s, axis=0, keepdims=True), sk)
            p = jnp.exp(s - mx)
            den = jnp.sum(p, axis=0, keepdims=True) + jnp.exp(sk - mx)
            pn = (p * (1.0 / den)).astype(BF16)
            o = lax.dot_general(pn, v2, (((0,), (0,)), ((), ())), preferred_element_type=F32)
            for g in range(Q_PER_KV):
                o_ref[:, pl.ds((first + g) * HEAD_DIM, HEAD_DIM)] = o[g * BLOCK:(g + 1) * BLOCK].astype(o_ref.dtype)

    @pl.when(n < 2)
    def _():
        k_pos = (n - 1) * BLOCK + lax.broadcasted_iota(jnp.int32, (2 * BLOCK, 1), 0)
        heads(jnp.where(k_pos < PAD, NEG, 0.0))

    @pl.when(n >= 2)
    def _():
        heads(None)


def _attention(cols, sink, batch, lp):
    m = cols.shape[0]
    nb = lp // BLOCK
    cur = lambda cb: (lambda b, n: (b * nb + n, cb))
    prev = lambda cb: (lambda b, n: (b * nb + jnp.maximum(n - 1, 0), cb))
    return pl.pallas_call(
        _attn_kernel,
        grid=(batch, nb),
        in_specs=[
            pl.BlockSpec((BLOCK, ATT_W), cur(OFF_Q // ATT_W)),
            pl.BlockSpec((BLOCK, KV_W), prev(OFF_K // KV_W)),
            pl.BlockSpec((BLOCK, KV_W), cur(OFF_K // KV_W)),
            pl.BlockSpec((BLOCK, KV_W), prev(OFF_V // KV_W)),
            pl.BlockSpec((BLOCK, KV_W), cur(OFF_V // KV_W)),
            pl.BlockSpec((N_KV_HEADS, 2 * BLOCK, Q_PER_KV * BLOCK), lambda b, n: (0, 0, 0)),
            pl.BlockSpec((N_KV_HEADS, Q_PER_KV * BLOCK), lambda b, n: (0, 0)),
        ],
        out_specs=pl.BlockSpec((BLOCK, ATT_W), lambda b, n: (b * nb + n, 0)),
        out_shape=jax.ShapeDtypeStruct((m, ATT_W), BF16),
        compiler_params=_cparams("parallel", "arbitrary"),
        name="swa_attention",
    )(cols, cols, cols, cols, cols, _attn_bias(),
      jnp.repeat(sink.astype(F32).reshape(N_KV_HEADS, Q_PER_KV), BLOCK, axis=1))


LRU_HALF = LRU_WIDTH // 2
LRU_HALO = 8
LRU_SCAN_UNROLL = 6


LOG2_E = 1.4426950408889634
GELU_C = 0.7978845608028654


def _gelu_tanh(x):
    half = 0.5 * x
    return half + half * jnp.tanh(x * (GELU_C + (GELU_C * 0.044715) * (x * x)))


def _lru_kernel(*refs):
    nh = LRU_WIDTH // LRU_HALF
    x_refs, y_refs = refs[:nh], refs[nh:2 * nh]
    (cw_ref, cb_ref, wa_ref, ba_ref, wx_ref, bx_ref, lam_ref, o_ref,
     ext_ref, a_ref, b_ref, carry_ref) = refs[2 * nh:]
    t = pl.program_id(1)
    tile = SEQ_TILE
    width = LRU_HALF
    pos = t * tile + lax.broadcasted_iota(jnp.int32, (tile, 1), 0)
    row = lax.broadcasted_iota(jnp.int32, (8, width), 0)

    for c in range(nh):
        lanes = pl.ds(c * width, width)
        ext, a_s, b_s, carry = ext_ref.at[c], a_ref.at[c], b_ref.at[c], carry_ref.at[c]

        @pl.when(t == 0)
        def _():
            ext[pl.ds(0, LRU_HALO), :] = jnp.zeros((LRU_HALO, width), F32)
            carry[...] = jnp.zeros((1, width), F32)

        @pl.when(t > 0)
        def _():
            ext[pl.ds(0, LRU_HALO), :] = ext[pl.ds(tile, LRU_HALO), :]

        ext[pl.ds(LRU_HALO, tile), :] = x_refs[c][...].astype(F32)

        xc = cb_ref[:, lanes] + cw_ref[pl.ds(CONV_WIDTH - 1, 1), lanes] * ext[pl.ds(LRU_HALO, tile), :]
        for j in range(CONV_WIDTH - 1):
            shift = CONV_WIDTH - 1 - j
            xc = xc + cw_ref[pl.ds(j, 1), lanes] * ext[pl.ds(LRU_HALO - shift, tile), :]

        xcb = xc.astype(BF16)
        ga_parts, gx_parts = [], []
        for blk in range(width // LRU_BLOCK):
            xb = xcb[:, blk * LRU_BLOCK:(blk + 1) * LRU_BLOCK]
            w_idx = c * (width // LRU_BLOCK) + blk
            ga_parts.append(jnp.dot(xb, wa_ref[w_idx], preferred_element_type=F32))
            gx_parts.append(jnp.dot(xb, wx_ref[w_idx], preferred_element_type=F32))
        gate_a = _sigmoid(jnp.concatenate(ga_parts, axis=1) + ba_ref[:, lanes])
        gate_x = _sigmoid(jnp.concatenate(gx_parts, axis=1) + bx_ref[:, lanes])

        neg_lam = -lam_ref[:, lanes]
        softplus = jnp.maximum(neg_lam, 0.0) + jnp.log1p(jnp.exp(-jnp.abs(neg_lam)))
        a = jnp.exp2(gate_a * ((-LRU_C * LOG2_E) * softplus))
        b_in = jnp.sqrt(1.0 - a * a) * gate_x * xc
        b_in = jnp.where(pos >= PAD, b_in, 0.0)
        a_s[...] = a
        b_s[...] = b_in

        def group(r, h_prev, a_s=a_s, b_s=b_s):
            rows = pl.ds(pl.multiple_of(r * 8, 8), 8)
            av = a_s[rows, :]
            bv = b_s[rows, :]
            for k in (1, 2, 4):
                a_sh = jnp.where(row >= k, pltpu.roll(av, k, 0), 1.0)
                b_sh = jnp.where(row >= k, pltpu.roll(bv, k, 0), 0.0)
                bv = av * b_sh + bv
                av = av * a_sh
            hv = av * h_prev + bv
            b_s[rows, :] = hv
            return hv[7:8, :]

        carry[...] = lax.fori_loop(0, tile // 8, group, carry[...], unroll=LRU_SCAN_UNROLL)
        o_ref[:, lanes] = (b_s[...] * _gelu_tanh(y_refs[c][...].astype(F32))).astype(o_ref.dtype)


def _lru(cols, conv_w, conv_b, wa_bf16, ba, wx_bf16, bx, lam, batch, lp):
    m = cols.shape[0]
    nt = lp // SEQ_TILE
    nh = LRU_WIDTH // LRU_HALF
    vec = lambda v: v.reshape(1, LRU_WIDTH).astype(F32)
    vec_spec = pl.BlockSpec((1, LRU_WIDTH), lambda b, t: (0, 0))
    w_spec = pl.BlockSpec((LRU_BLOCKS, LRU_BLOCK, LRU_BLOCK), lambda b, t: (0, 0, 0))
    half_spec = lambda off, c: pl.BlockSpec((SEQ_TILE, LRU_HALF), lambda b, t: (b * nt + t, off // LRU_HALF + c))
    return pl.pallas_call(
        _lru_kernel,
        grid=(batch, nt),
        in_specs=[half_spec(OFF_LX, c) for c in range(nh)] + [half_spec(OFF_LY, c) for c in range(nh)]
        + [pl.BlockSpec((CONV_WIDTH, LRU_WIDTH), lambda b, t: (0, 0)),
           vec_spec, w_spec, vec_spec, w_spec, vec_spec, vec_spec],
        out_specs=pl.BlockSpec((SEQ_TILE, LRU_WIDTH), lambda b, t: (b * nt + t, 0)),
        out_shape=jax.ShapeDtypeStruct((m, LRU_WIDTH), BF16),
        scratch_shapes=[
            pltpu.VMEM((nh, SEQ_TILE + LRU_HALO, LRU_HALF), F32),
            pltpu.VMEM((nh, SEQ_TILE, LRU_HALF), F32),
            pltpu.VMEM((nh, SEQ_TILE, LRU_HALF), F32),
            pltpu.VMEM((nh, 1, LRU_HALF), F32),
        ],
        compiler_params=_cparams("parallel", "arbitrary"),
        name="rglru",
    )(*([cols] * (2 * nh)), conv_w.astype(F32), vec(conv_b), wa_bf16, vec(ba), wx_bf16, vec(bx), vec(lam))


def _merge_kernel(p_ref, a_ref, r_ref, gp_ref, ga_ref, gr_ref, wp_ref, wa_ref, wr_ref, o_ref):
    acc = gp_ref[...].astype(F32) * jnp.dot(p_ref[...], wp_ref[...], preferred_element_type=F32)
    acc += ga_ref[...].astype(F32) * jnp.dot(a_ref[...], wa_ref[...], preferred_element_type=F32)
    acc += gr_ref[...].astype(F32) * jnp.dot(r_ref[...], wr_ref[...], preferred_element_type=F32)
    o_ref[...] = acc.astype(o_ref.dtype)


def _merge(pool_o, attn_o, lru_o, gates, wp, wa, wr, layer):
    m = pool_o.shape[0]
    bm, bn = MERGE_BM, MERGE_BN
    assert m % bm == 0 and D_MODEL % bn == 0
    x_spec = pl.BlockSpec((bm, POOL_WIDTH), lambda i, j: (i, 0))
    gate_spec = lambda k: pl.BlockSpec((bm, bn), lambda i, j: (i, k * D_MODEL // bn + j))
    w_spec = pl.BlockSpec((None, POOL_WIDTH, bn), lambda i, j: (layer, 0, j))
    return pl.pallas_call(
        _merge_kernel,
        grid=(m // bm, D_MODEL // bn),
        in_specs=[x_spec, x_spec, x_spec, gate_spec(0), gate_spec(1), gate_spec(2), w_spec, w_spec, w_spec],
        out_specs=pl.BlockSpec((bm, bn), lambda i, j: (i, j)),
        out_shape=jax.ShapeDtypeStruct((m, D_MODEL), BF16),
        compiler_params=_cparams("parallel", "arbitrary"),
        name="gated_merge",
    )(pool_o, attn_o, lru_o, gates, gates, gates, wp, wa, wr)


def _pack_bf16_pairs(yb):
    c = yb.shape[1] // 2
    lo = lax.bitcast_convert_type(yb[:, :c].astype(F32), jnp.uint32)
    hi = lax.bitcast_convert_type(yb[:, c:].astype(F32), jnp.uint32)
    return (hi & jnp.uint32(0xFFFF0000)) | (lo >> 16)


def _unpack_bf16_pairs(words, dtype=BF16):
    lo = lax.bitcast_convert_type(words << 16, F32)
    hi = lax.bitcast_convert_type(words & jnp.uint32(0xFFFF0000), F32)
    return jnp.concatenate([lo, hi], axis=1).astype(dtype)


def _route_tile(logits, valid, base):
    bm = logits.shape[0]
    lane = lax.broadcasted_iota(jnp.int32, (bm, ROUTE_W), 1)
    lane_f = lane.astype(F32)
    ninf = -jnp.inf
    big = float(ROUTE_W)

    gl = jnp.where(lane < N_GROUPS, logits, ninf)
    gmax = jnp.max(gl, axis=-1, keepdims=True)
    g = jnp.min(jnp.where(gl == gmax, lane_f, big), axis=-1, keepdims=True)
    p_g = 1.0 / jnp.sum(jnp.exp(gl - gmax), axis=-1, keepdims=True)

    first = N_GROUPS + g * EXPERTS_PER_GROUP
    sl = jnp.where((lane_f >= first) & (lane_f < first + EXPERTS_PER_GROUP), logits, ninf)
    m1 = jnp.max(sl, axis=-1, keepdims=True)
    i1 = jnp.min(jnp.where(sl == m1, lane_f, big), axis=-1, keepdims=True)
    ssum = jnp.sum(jnp.exp(sl - m1), axis=-1, keepdims=True)
    sl2 = jnp.where(lane_f == i1, ninf, sl)
    m2 = jnp.max(sl2, axis=-1, keepdims=True)
    i2 = jnp.min(jnp.where(sl2 == m2, lane_f, big), axis=-1, keepdims=True)
    p1 = 1.0 / ssum
    p2 = jnp.exp(m2 - m1) / ssum
    w1 = p_g * p1 / (p1 + p2)
    w2 = p_g * p2 / (p1 + p2)
    e1 = i1 - N_GROUPS
    e2 = i2 - N_GROUPS

    oh1 = (lane_f == e1) & valid
    oh2 = (lane_f == e2) & valid
    both = (oh1 | oh2).astype(F32)
    earlier = (lax.broadcasted_iota(jnp.int32, (bm, bm), 0) > lax.broadcasted_iota(jnp.int32, (bm, bm), 1))
    prefix = jnp.dot(earlier.astype(BF16), both.astype(BF16), preferred_element_type=F32) + base
    r1 = jnp.sum(jnp.where(oh1, prefix, 0.0), axis=-1, keepdims=True)
    r2 = jnp.sum(jnp.where(oh2, prefix, 0.0), axis=-1, keepdims=True)

    ri = jnp.where(lane == 0, e1, jnp.where(lane == 1, e2, jnp.where(lane == 2, r1, jnp.where(lane == 3, r2, 0.0))))
    rw = jnp.where(lane == 0, w1, jnp.where(lane == 1, w2, 0.0))
    ri_rows = jnp.transpose(ri)[0:ROUTE_ROWS, :].astype(jnp.int32)
    return ri_rows, rw, jnp.sum(both, axis=0, keepdims=True)


def _outproj_kernel(x_ref, w_ref, h_ref, g_ref, b_ref, rw_ref, rb_ref, hf_ref, hp_ref, ri_ref, rwt_ref, cnt_ref,
                    *, batch, lp):
    i = pl.program_id(0)
    bm = x_ref.shape[0]
    sub = bm // OUT_SPLIT

    @pl.when(i == 0)
    def _():
        cnt_ref[...] = jnp.zeros(cnt_ref.shape, F32)

    for s in range(OUT_SPLIT):
        rows = pl.ds(s * sub, sub)
        t = jnp.dot(x_ref[rows, :], w_ref[...], preferred_element_type=F32)
        y = _layer_norm(ALPHA * h_ref[rows, :] + t, g_ref[...], b_ref[...])
        valid = _flat_valid_rows(i * bm + s * sub, sub, batch, lp)
        y = jnp.where(valid, y, 0.0)
        yb = y.astype(BF16)
        hf_ref[rows, :] = y
        packed = _pack_bf16_pairs(yb)
        for c in range(ROW_TILES):
            hp_ref[pl.ds(s * sub * ROW_TILES + c, sub, stride=ROW_TILES), :] = packed[:, c * 128:(c + 1) * 128]
        logits = jnp.dot(yb, rw_ref[...], preferred_element_type=F32) + rb_ref[...]
        ri, rw, tile_cnt = _route_tile(logits, valid, cnt_ref[...])
        ri_ref[:, rows] = ri
        rwt_ref[rows, :] = rw
        cnt_ref[...] += tile_cnt


def _outproj_ln(merged, w_out_bf16, h, g, b, route_w, route_b, layer, batch, lp):
    m, d = h.shape
    bm = OUT_BM
    assert m % bm == 0
    row_spec = pl.BlockSpec((bm, d), lambda i: (i, 0))
    vec_spec = pl.BlockSpec((1, d), lambda i: (0, 0))
    route_spec = pl.BlockSpec((bm, ROUTE_W), lambda i: (i, 0))
    return pl.pallas_call(
        functools.partial(_outproj_kernel, batch=batch, lp=lp),
        grid=(m // bm,),
        in_specs=[
            row_spec,
            pl.BlockSpec((None, d, d), lambda i: (layer, 0, 0)),
            row_spec, vec_spec, vec_spec,
            pl.BlockSpec((d, ROUTE_W), lambda i: (0, 0)),
            pl.BlockSpec((1, ROUTE_W), lambda i: (0, 0)),
        ],
        out_specs=[row_spec, pl.BlockSpec((bm * ROW_TILES, 128), lambda i: (i, 0)),
                   pl.BlockSpec((ROUTE_ROWS, bm), lambda i: (0, i)), route_spec,
                   pl.BlockSpec((1, ROUTE_W), lambda i: (0, 0))],
        out_shape=[jax.ShapeDtypeStruct((m, d), F32), jax.ShapeDtypeStruct((m * ROW_TILES, 128), jnp.uint32),
                   jax.ShapeDtypeStruct((ROUTE_ROWS, m), jnp.int32), jax.ShapeDtypeStruct((m, ROUTE_W), F32),
                   jax.ShapeDtypeStruct((1, ROUTE_W), F32)],
        compiler_params=_cparams("arbitrary"),
        name="out_proj_ln",
    )(merged, w_out_bf16, h, g.reshape(1, d), b.reshape(1, d), route_w, route_b)


def _dispatch_kernel(start_ref, nchunk_ref, dest_hbm, hp_ref, xs_ref, idx_ref, zero_ref, idx_sem, row_sem, zero_sem,
                     *, n_expert_rows):
    i = pl.program_id(0)
    bm = hp_ref.shape[0] // ROW_TILES
    groups = bm // SUBLANES
    n_idx = TOP_K * bm

    def token_rows(first_token, n_tokens):
        return pl.ds(pl.multiple_of(first_token * ROW_TILES, ROW_TILES), n_tokens * ROW_TILES)

    @pl.when(i == 0)
    def _():
        zero_ref[...] = jnp.zeros(zero_ref.shape, zero_ref.dtype)

        def last_chunk(e):
            first = pl.multiple_of(start_ref[e] + (nchunk_ref[e] - 1) * MOE_BM, MOE_BM)
            return pltpu.make_async_copy(zero_ref, xs_ref.at[token_rows(first, MOE_BM)], zero_sem)

        for e in range(N_EXPERTS):
            @pl.when(nchunk_ref[e] > 0)
            def _():
                last_chunk(e).start()

        for e in range(N_EXPERTS):
            @pl.when(nchunk_ref[e] > 0)
            def _():
                last_chunk(e).wait()

        used = start_ref[N_EXPERTS - 1] + nchunk_ref[N_EXPERTS - 1] * MOE_BM

        def tail_chunk(j):
            return pltpu.make_async_copy(
                zero_ref, xs_ref.at[token_rows(pl.multiple_of(used + j * MOE_BM, MOE_BM), MOE_BM)], zero_sem)

        n_tail = (n_expert_rows - used) // MOE_BM
        lax.fori_loop(0, n_tail, lambda j, c: (tail_chunk(j).start(), c)[1], 0)
        lax.fori_loop(0, n_tail, lambda j, c: (tail_chunk(j).wait(), c)[1], 0)

    idx_copy = pltpu.make_async_copy(dest_hbm.at[pl.ds(pl.multiple_of(i * n_idx, n_idx), n_idx)], idx_ref, idx_sem)
    idx_copy.start()
    idx_copy.wait()

    def issue(g, carry):
        for s in range(SUBLANES):
            for k in range(TOP_K):
                slot = idx_ref[g * SUBLANES + (k * bm + s)]
                pltpu.make_async_copy(hp_ref.at[token_rows(g * SUBLANES + s, 1)], xs_ref.at[token_rows(slot, 1)],
                                      row_sem).start(priority=k % 2)
        return carry

    lax.fori_loop(0, groups, issue, 0)
    for k in range(TOP_K):
        pltpu.make_async_copy(xs_ref.at[token_rows(0, bm)], xs_ref.at[token_rows(0, bm)], row_sem).wait()


def _dispatch(hp, dest, chunk_start, n_chunks, n_expert_rows, n_slots):
    m = hp.shape[0] // ROW_TILES
    bm = DISPATCH_BM
    assert m % bm == 0 and bm % SUBLANES == 0
    grid_spec = pltpu.PrefetchScalarGridSpec(
        num_scalar_prefetch=2,
        grid=(m // bm,),
        in_specs=[
            pl.BlockSpec(memory_space=pl.ANY),
            pl.BlockSpec((bm * ROW_TILES, 128), lambda i, st, nc: (i, 0)),
        ],
        out_specs=pl.BlockSpec(memory_space=pl.ANY),
        scratch_shapes=[pltpu.SMEM((TOP_K * bm,), jnp.int32), pltpu.VMEM((MOE_BM * ROW_TILES, 128), jnp.uint32),
                        pltpu.SemaphoreType.DMA(()), pltpu.SemaphoreType.DMA(()), pltpu.SemaphoreType.DMA(())],
    )
    return pl.pallas_call(
        functools.partial(_dispatch_kernel, n_expert_rows=n_expert_rows),
        grid_spec=grid_spec,
        out_shape=jax.ShapeDtypeStruct((n_slots * ROW_TILES, 128), jnp.uint32),
        compiler_params=_cparams("arbitrary"),
        name="moe_dispatch",
    )(chunk_start, n_chunks, dest, hp)


def _expert_kernel(start_ref, nchunk_ref, xs_hbm, wg_hbm, wu_hbm, wd_hbm, yb_hbm, wg_f, wu_f, wd_f, wg_s, wu_s, wd_s,
                   x_buf, o_buf, w_sem, in_sem, out_sem, *, layer):
    e = pl.program_id(0)
    n = nchunk_ref[e]
    bm = MOE_BM
    first = start_ref[e] // bm
    total = start_ref[N_EXPERTS - 1] // bm + nchunk_ref[N_EXPERTS - 1]
    wslot = e % 2

    def w_copies(expert, slot):
        return [pltpu.make_async_copy(src.at[layer, expert], dst.at[slot], w_sem.at[slot])
                for src, dst in ((wg_hbm, wg_f), (wu_hbm, wu_f), (wd_hbm, wd_f))]

    def o_rows(g):
        return pl.ds(pl.multiple_of(g * (bm * ROW_TILES), bm * ROW_TILES), bm * ROW_TILES)

    def x_copy(g):
        slot = g % EXPERT_X_BUFFERS
        return pltpu.make_async_copy(xs_hbm.at[o_rows(g)], x_buf.at[slot], in_sem.at[slot])

    def o_copy(g):
        slot = g % 2
        return pltpu.make_async_copy(o_buf.at[slot], yb_hbm.at[o_rows(g)], out_sem.at[slot])

    @pl.when(e == 0)
    def _():
        for ahead in range(EXPERT_X_BUFFERS - 1):
            @pl.when(ahead < total)
            def _():
                x_copy(ahead).start()

        for c in w_copies(0, 0):
            c.start(priority=1)

    @pl.when(e + 1 < N_EXPERTS)
    def _():
        for c in w_copies(e + 1, 1 - wslot):
            c.start(priority=1)

    for c in w_copies(e, wslot):
        c.wait()

    @pl.when(n > 0)
    def _():
        wg_s[...] = wg_f[wslot].astype(BF16)
        wu_s[...] = wu_f[wslot].astype(BF16)
        wd_s[...] = wd_f[wslot].astype(BF16)

        def chunk(j, carry):
            g = first + j
            x_copy(g).wait()

            @pl.when(g + EXPERT_X_BUFFERS - 1 < total)
            def _():
                x_copy(g + EXPERT_X_BUFFERS - 1).start()

            @pl.when(g >= 2)
            def _():
                o_copy(g - 2).wait()

            x_slot = g % EXPERT_X_BUFFERS
            x = _unpack_bf16_pairs(jnp.concatenate(
                [x_buf[x_slot, pl.ds(c, bm, stride=ROW_TILES), :] for c in range(ROW_TILES)], axis=1))
            gate = jnp.dot(x, wg_s[...], preferred_element_type=F32)
            up = jnp.dot(x, wu_s[...], preferred_element_type=F32)
            hdn = (gate * _sigmoid(gate)) * up
            y = jnp.dot(hdn.astype(BF16), wd_s[...], preferred_element_type=F32)
            packed = _pack_bf16_pairs(y.astype(BF16))
            for c in range(ROW_TILES):
                o_buf[g % 2, pl.ds(c, bm, stride=ROW_TILES), :] = packed[:, c * 128:(c + 1) * 128]
            o_copy(g).start()
            return carry

        lax.fori_loop(0, n, chunk, 0)

    @pl.when(e == N_EXPERTS - 1)
    def _():
        for back in (2, 1):
            @pl.when(total >= back)
            def _():
                o_copy(total - back).wait()

        o_buf[0] = jnp.zeros(o_buf.shape[1:], o_buf.dtype)

        def fill(g, carry):
            tail = pltpu.make_async_copy(o_buf.at[0], yb_hbm.at[o_rows(g)], out_sem.at[0])
            tail.start()
            tail.wait()
            return carry

        lax.fori_loop(total, yb_hbm.shape[0] // (bm * ROW_TILES), fill, 0)


def _experts(xs, n_rows, chunk_start, n_chunks, w_gate, w_up, w_down, layer):
    d = D_MODEL
    bm = MOE_BM
    any_spec = pl.BlockSpec(memory_space=pl.ANY)
    grid_spec = pltpu.PrefetchScalarGridSpec(
        num_scalar_prefetch=2,
        grid=(N_EXPERTS,),
        in_specs=[any_spec, any_spec, any_spec, any_spec],
        out_specs=any_spec,
        scratch_shapes=[
            pltpu.VMEM((2, d, D_EXPERT), F32),
            pltpu.VMEM((2, d, D_EXPERT), F32),
            pltpu.VMEM((2, D_EXPERT, d), F32),
            pltpu.VMEM((d, D_EXPERT), BF16),
            pltpu.VMEM((d, D_EXPERT), BF16),
            pltpu.VMEM((D_EXPERT, d), BF16),
            pltpu.VMEM((EXPERT_X_BUFFERS, bm * ROW_TILES, 128), jnp.uint32),
            pltpu.VMEM((2, bm * ROW_TILES, 128), jnp.uint32),
            pltpu.SemaphoreType.DMA((2,)),
            pltpu.SemaphoreType.DMA((EXPERT_X_BUFFERS,)),
            pltpu.SemaphoreType.DMA((2,)),
        ],
    )
    return pl.pallas_call(
        functools.partial(_expert_kernel, layer=layer),
        grid_spec=grid_spec,
        out_shape=jax.ShapeDtypeStruct((n_rows * ROW_TILES, 128), jnp.uint32),
        compiler_params=_cparams("arbitrary"),
        name="expert_mlp",
    )(chunk_start, n_chunks, xs, w_gate, w_up, w_down)


def _combine_ln_kernel(dest_hbm, h_ref, rw_ref, g_ref, b_ref, yb_hbm, *rest, batch, lp, final):
    if final:
        out_hbm, idx_ref, rows_ref, y_buf, idx_sem, row_sem, out_sem = rest
    else:
        hf_ref, hb_ref, idx_ref, rows_ref, idx_sem, row_sem = rest
    i = pl.program_id(0)
    n_tiles = pl.num_programs(0)
    bm = h_ref.shape[0]
    n_idx = TOP_K * bm
    buf = i % 2

    def idx_copy(tile):
        half = pl.ds(pl.multiple_of((tile % 2) * n_idx, n_idx), n_idx)
        return pltpu.make_async_copy(
            dest_hbm.at[pl.ds(pl.multiple_of(tile * n_idx, n_idx), n_idx)], idx_ref.at[half], idx_sem.at[tile % 2])

    def start_gather(tile):
        into = tile % 2
        idx_copy(tile).wait()

        def issue(g, carry):
            base = into * n_idx + g * SUBLANES
            for s in range(SUBLANES):
                for k in range(TOP_K):
                    slot = idx_ref[base + (k * bm + s)]
                    src = yb_hbm.at[pl.ds(pl.multiple_of(slot * ROW_TILES, ROW_TILES), ROW_TILES)]
                    dst = rows_ref.at[into, k, pl.ds(pl.multiple_of((g * SUBLANES + s) * ROW_TILES, ROW_TILES),
                                                     ROW_TILES)]
                    pltpu.make_async_copy(src, dst, row_sem.at[into]).start(priority=k % 2)
            return carry

        lax.fori_loop(0, bm // SUBLANES, issue, 0)

        @pl.when(tile + 2 < n_tiles)
        def _():
            idx_copy(tile + 2).start()

    @pl.when(i == 0)
    def _():
        idx_copy(0).start()

        @pl.when(1 < n_tiles)
        def _():
            idx_copy(1).start()

        start_gather(0)

    @pl.when(i + 1 < n_tiles)
    def _():
        start_gather(i + 1)

    for k in range(TOP_K):
        whole = pl.ds(0, bm * ROW_TILES)
        pltpu.make_async_copy(yb_hbm.at[whole], yb_hbm.at[whole], row_sem.at[buf]).wait()

    def expert_rows(k):
        pieces = [rows_ref[buf, k, pl.ds(c, bm, stride=ROW_TILES), :] for c in range(ROW_TILES)]
        return _unpack_bf16_pairs(jnp.concatenate(pieces, axis=1), F32)

    ffn = rw_ref[:, 0:1] * expert_rows(0)
    for k in range(1, TOP_K):
        ffn = ffn + rw_ref[:, k:k + 1] * expert_rows(k)
    y = _layer_norm(ALPHA * h_ref[...] + ffn, g_ref[...], b_ref[...])
    if not final:
        y = jnp.where(_flat_valid_rows(i * bm, bm, batch, lp), y, 0.0)
        hf_ref[...] = y
        hb_ref[...] = y.astype(BF16)
        return

    nb = lp // BLOCK

    def for_each_out_block(step, fn):
        for j in range(bm // BLOCK):
            blk = step * (bm // BLOCK) + j
            seq_blk = blk % nb

            @pl.when(seq_blk >= 1)
            def _():
                dst = pl.multiple_of(((blk // nb) * (nb - 1) + seq_blk - 1) * BLOCK, BLOCK)
                fn(pltpu.make_async_copy(y_buf.at[pl.ds(j * BLOCK, BLOCK)], out_hbm.at[pl.ds(dst, BLOCK)], out_sem))

    @pl.when(i > 0)
    def _():
        for_each_out_block(i - 1, lambda c: c.wait())

    y_buf[...] = y
    for_each_out_block(i, lambda c: c.start(priority=1))

    @pl.when(i == n_tiles - 1)
    def _():
        for_each_out_block(i, lambda c: c.wait())


def _combine_ln(h, yb, dest, rw, g, b, batch, lp, final):
    m, d = h.shape
    bm = COMBINE_BM
    assert m % bm == 0 and bm % BLOCK == 0
    row_spec = pl.BlockSpec((bm, d), lambda i: (i, 0))
    vec_spec = pl.BlockSpec((1, d), lambda i: (0, 0))
    scratch = [pltpu.SMEM((2 * TOP_K * bm,), jnp.int32),
               pltpu.VMEM((2, TOP_K, bm * ROW_TILES, 128), jnp.uint32)]
    sems = [pltpu.SemaphoreType.DMA((2,)), pltpu.SemaphoreType.DMA((2,))]
    if final:
        out_specs = pl.BlockSpec(memory_space=pl.ANY)
        out_shape = jax.ShapeDtypeStruct((batch * (lp - BLOCK), d), F32)
        scratch = scratch + [pltpu.VMEM((bm, d), F32)] + sems + [pltpu.SemaphoreType.DMA(())]
    else:
        out_specs = [row_spec, row_spec]
        out_shape = [jax.ShapeDtypeStruct((m, d), F32), jax.ShapeDtypeStruct((m, d), BF16)]
        scratch = scratch + sems
    return pl.pallas_call(
        functools.partial(_combine_ln_kernel, batch=batch, lp=lp, final=final),
        grid=(m // bm,),
        in_specs=[
            pl.BlockSpec(memory_space=pl.ANY),
            row_spec,
            pl.BlockSpec((bm, ROUTE_W), lambda i: (i, 0)),
            vec_spec, vec_spec,
            pl.BlockSpec(memory_space=pl.ANY),
        ],
        out_specs=out_specs,
        out_shape=out_shape,
        scratch_shapes=scratch,
        compiler_params=_cparams("arbitrary"),
        name="moe_combine_out" if final else "moe_combine_ln",
    )(dest, h, rw, g.reshape(1, d), b.reshape(1, d), yb)


def _slot_tables(ri, cnt, batch, lp):
    m = ri.shape[1]
    bm = MOE_BM
    dense = (TOP_K, m // ROUTE_W, ROUTE_W)
    eid = ri[0:TOP_K].reshape(dense)
    rank = ri[TOP_K:2 * TOP_K].reshape(dense)
    counts = cnt[0, :N_EXPERTS].astype(jnp.int32)
    padded = (counts + bm - 1) // bm * bm
    pad_end = jnp.cumsum(padded)
    pad_start = pad_end - padded
    n_real = batch * (lp - PAD) * TOP_K
    nblk = -(-(n_real + N_EXPERTS * (bm - 1)) // bm)
    cap = nblk * bm
    start = jnp.zeros_like(eid)
    for e in range(N_EXPERTS):
        start = jnp.where(eid == e, pad_start[e], start)
    row = jnp.arange(m, dtype=jnp.int32).reshape(dense[1:])
    pos = row % lp
    spare = cap + ((row // lp) * PAD + pos)[None] * TOP_K + jnp.arange(TOP_K, dtype=jnp.int32)[:, None, None]
    valid = (pos >= PAD)[None]
    dest = jnp.where(valid, start + rank, spare)
    dest_read = jnp.where(valid, dest, 0)
    n_slots = cap + batch * PAD * TOP_K

    def by_tile(idx, tile):
        return idx.reshape(TOP_K, m // tile, tile).transpose(1, 0, 2).reshape(m * TOP_K)

    return (by_tile(dest, DISPATCH_BM), by_tile(dest_read, COMBINE_BM), pad_start.astype(jnp.int32),
            (padded // bm).astype(jnp.int32), cap, n_slots)


def kernel(x, meta, ln_emb_g, ln_emb_b, w_in, pool_w, pool_scale, attn_sink, conv_w, conv_b, lru_wa, lru_ba,
           lru_wx, lru_bx, lru_lambda, proj_pool, proj_attn, proj_lru, w_out, ln1_g, ln1_b, router_grp_w,
           router_grp_b, router_exp_w, router_exp_b, exp_w_gate, exp_w_up, exp_w_down, ln2_g, ln2_b):
    batch, seq, d = x.shape
    lp = PAD + N_META + seq
    m = batch * lp

    hf, hb = _embed(x, meta, ln_emb_g, ln_emb_b)
    w_in_b = w_in.astype(BF16)
    pool_w_b = pool_w.astype(BF16)
    wa_b = lru_wa.astype(BF16)
    wx_b = lru_wx.astype(BF16)
    wp_b = proj_pool.astype(BF16)
    wat_b = proj_attn.astype(BF16)
    wl_b = proj_lru.astype(BF16)
    w_out_b = w_out.astype(BF16)
    route_pad = ROUTE_W - N_GROUPS - N_EXPERTS

    for l in range(DEPTH):
        cols = _inproj(hb, w_in_b, l, gates=False)
        gates = _inproj(hb, w_in_b, l, gates=True)
        pool_o = _pool(cols, pool_w_b[l], pool_scale[l], batch, lp)
        attn_o = _attention(cols, attn_sink[l], batch, lp)
        lru_o = _lru(cols, conv_w[l], conv_b[l], wa_b[l], lru_ba[l], wx_b[l], lru_bx[l], lru_lambda[l],
                     batch, lp)
        merged = _merge(pool_o, attn_o, lru_o, gates, wp_b, wat_b, wl_b, l)
        route_w = jnp.concatenate(
            [router_grp_w[l], router_exp_w[l], jnp.zeros((d, route_pad), F32)], axis=1).astype(BF16)
        route_b = jnp.concatenate(
            [router_grp_b[l], router_exp_b[l], jnp.zeros((route_pad,), F32)]).reshape(1, ROUTE_W)
        h1f, h1p, ri, rw, cnt = _outproj_ln(merged, w_out_b, hf, ln1_g[l], ln1_b[l], route_w, route_b, l,
                                            batch, lp)
        dest, dest_read, chunk_start, n_chunks, cap, n_slots = _slot_tables(ri, cnt, batch, lp)
        xs = _dispatch(h1p, dest, chunk_start, n_chunks, cap, n_slots)
        yb = _experts(xs, cap, chunk_start, n_chunks, exp_w_gate, exp_w_up, exp_w_down, l)
        if l + 1 < DEPTH:
            hf, hb = _combine_ln(h1f, yb, dest_read, rw, ln2_g[l], ln2_b[l], batch, lp, final=False)
        else:
            out = _combine_ln(h1f, yb, dest_read, rw, ln2_g[l], ln2_b[l], batch, lp, final=True)

    return out.reshape(batch, seq, d)
```

```python
import functools

import jax
import jax.numpy as jnp
from jax import lax
from jax.experimental import pallas as pl
from jax.experimental.pallas import tpu as pltpu

F32 = jnp.float32
BF16 = jnp.bfloat16

D_MODEL = 2048
DEPTH = 2
N_META = 16
POOL_WINDOWS = (2, 4, 8, 16)
POOL_WIDTH = D_MODEL // 2
POOL_GROUP = POOL_WIDTH // len(POOL_WINDOWS)
N_HEADS = 16
N_KV_HEADS = 4
HEAD_DIM = 64
Q_PER_KV = N_HEADS // N_KV_HEADS
WINDOW = 128
BLOCK = 128
NEG = -1e30
LRU_WIDTH = D_MODEL // 2
LRU_BLOCKS = 4
LRU_BLOCK = LRU_WIDTH // LRU_BLOCKS
CONV_WIDTH = 4
LRU_C = 8.0
N_GROUPS = 4
EXPERTS_PER_GROUP = 8
N_EXPERTS = N_GROUPS * EXPERTS_PER_GROUP
TOP_K = 2
D_EXPERT = D_MODEL // 4
LN_EPS = 1e-5
ALPHA = (2.0 * DEPTH) ** 0.25

PAD = BLOCK - N_META
ATT_W = N_HEADS * HEAD_DIM
KV_W = N_KV_HEADS * HEAD_DIM
OFF_POOL = 0
OFF_Q = OFF_POOL + POOL_WIDTH
OFF_K = OFF_Q + ATT_W
OFF_V = OFF_K + KV_W
OFF_LX = OFF_V + KV_W
OFF_LY = OFF_LX + LRU_WIDTH
OFF_GATE = OFF_LY + LRU_WIDTH
IN_COLS = OFF_GATE + 3 * D_MODEL

VMEM_LIMIT_BYTES = 56 * 1024 * 1024
SUBLANES = 8

SEQ_TILE = 3 * BLOCK
INPROJ_BM = 1536
INPROJ_BN = 1536
MERGE_BM = 768
MERGE_BN = 1024
OUT_BM = 512
OUT_SPLIT = 2
MOE_BM = 256
DISPATCH_BM = 1536
COMBINE_BM = 512
COMBINE_ROW_BUFFERS = 3
EXPERT_X_BUFFERS = 3
ROW_TILES = D_MODEL // 2 // 128
ROUTE_W = 128
ROUTE_ROWS = 8


def _cparams(*sem):
    return pltpu.CompilerParams(dimension_semantics=sem, vmem_limit_bytes=VMEM_LIMIT_BYTES)


def _layer_norm(x, g, b):
    mu = jnp.mean(x, axis=-1, keepdims=True)
    xc = x - mu
    var = jnp.mean(xc * xc, axis=-1, keepdims=True)
    return xc * lax.rsqrt(var + LN_EPS) * g + b


def _flat_valid_rows(row0, n_rows, batch, lp):
    r = row0 + lax.broadcasted_iota(jnp.int32, (n_rows, 1), 0)
    pad_row = jnp.zeros((n_rows, 1), jnp.bool_)
    for b in range(batch):
        pad_row = pad_row | ((r >= b * lp) & (r < b * lp + PAD))
    return jnp.logical_not(pad_row)


EMBED_BLOCKS = SEQ_TILE // BLOCK


def _embed_kernel(*refs):
    x_refs, (meta_ref, g_ref, b_ref, hf_ref, hb_ref) = refs[:EMBED_BLOCKS], refs[EMBED_BLOCKS:]
    t = pl.program_id(1)
    for j in range(EMBED_BLOCKS):
        src = x_refs[j][...]
        if j == 0:
            src = jnp.where(t == 0, meta_ref[...], src)
        y = _layer_norm(src, g_ref[...], b_ref[...])
        if j == 0:
            row = lax.broadcasted_iota(jnp.int32, (BLOCK, 1), 0)
            y = jnp.where((t > 0) | (row >= PAD), y, 0.0)
        hf_ref[pl.ds(j * BLOCK, BLOCK), :] = y
        hb_ref[pl.ds(j * BLOCK, BLOCK), :] = y.astype(BF16)


def _embed(x, meta, g, b):
    batch, seq, d = x.shape
    nblk = seq // BLOCK
    lp = (nblk + 1) * BLOCK
    nt = lp // SEQ_TILE
    m = batch * lp
    meta_tile = jnp.concatenate([jnp.zeros((PAD, d), F32), meta.astype(F32)], axis=0)
    row_spec = pl.BlockSpec((SEQ_TILE, d), lambda bi, t: (bi * nt + t, 0))
    vec_spec = pl.BlockSpec((1, d), lambda bi, t: (0, 0))
    x_spec = lambda j: pl.BlockSpec(
        (BLOCK, d), lambda bi, t: (bi * nblk + jnp.maximum(EMBED_BLOCKS * t + j - 1, 0), 0))
    x2 = x.reshape(batch * seq, d)
    return pl.pallas_call(
        _embed_kernel,
        grid=(batch, nt),
        in_specs=[x_spec(j) for j in range(EMBED_BLOCKS)]
        + [pl.BlockSpec((BLOCK, d), lambda bi, t: (0, 0)), vec_spec, vec_spec],
        out_specs=[row_spec, row_spec],
        out_shape=[jax.ShapeDtypeStruct((m, d), F32), jax.ShapeDtypeStruct((m, d), BF16)],
        compiler_params=_cparams("parallel", "arbitrary"),
        name="embed_ln",
    )(*([x2] * EMBED_BLOCKS), meta_tile, g.reshape(1, d), b.reshape(1, d))


def _sigmoid(x):
    return 0.5 * jnp.tanh(0.5 * x) + 0.5


def _inproj_kernel(x_ref, w_ref, o_ref, *, gates):
    acc = jnp.dot(x_ref[...], w_ref[...], preferred_element_type=F32)
    o_ref[...] = (_sigmoid(acc) if gates else acc).astype(o_ref.dtype)


def _inproj(hb, w_in_bf16, layer, gates):
    m, d = hb.shape
    bm, bn = INPROJ_BM, INPROJ_BN
    col0, width = (OFF_GATE, IN_COLS - OFF_GATE) if gates else (0, OFF_GATE)
    assert m % bm == 0 and width % bn == 0 and col0 % bn == 0
    return pl.pallas_call(
        functools.partial(_inproj_kernel, gates=gates),
        grid=(m // bm, width // bn),
        in_specs=[
            pl.BlockSpec((bm, d), lambda i, j: (i, 0)),
            pl.BlockSpec((None, d, bn), lambda i, j: (layer, 0, col0 // bn + j)),
        ],
        out_specs=pl.BlockSpec((bm, bn), lambda i, j: (i, j)),
        out_shape=jax.ShapeDtypeStruct((m, width), BF16),
        compiler_params=_cparams("parallel", "arbitrary"),
        name="in_proj_gates" if gates else "in_proj",
    )(hb, w_in_bf16)


POOL_HALO = 2 * max(POOL_WINDOWS)
assert POOL_WINDOWS == tuple(2 ** (g + 1) for g in range(len(POOL_WINDOWS)))


def _pool_kernel(u_ref, w_ref, scale_ref, o_ref, ext_ref, lvl_ref):
    t = pl.program_id(1)
    tile = SEQ_TILE
    halo = POOL_HALO

    @pl.when(t == 0)
    def _():
        ext_ref[pl.ds(0, halo), :] = jnp.zeros((halo, POOL_WIDTH), F32)

    @pl.when(t > 0)
    def _():
        ext_ref[pl.ds(0, halo), :] = ext_ref[pl.ds(tile, halo), :]

    ext_ref[pl.ds(halo, tile), :] = u_ref[...].astype(F32)

    pos = t * tile + lax.broadcasted_iota(jnp.int32, (tile, 1), 0) - PAD
    src, first = ext_ref, 0
    for gi, w in enumerate(POOL_WINDOWS):
        lane0 = gi * POOL_GROUP
        lanes = pl.ds(lane0, POOL_WIDTH - lane0)
        new_first = -(-(first + w // 2) // SUBLANES) * SUBLANES
        n_rows = tile + halo - new_first
        level = src[pl.ds(new_first, n_rows), lanes] + src[pl.ds(new_first - w // 2, n_rows), lanes]
        win = level[halo - new_first:, :POOL_GROUP]
        cols = pl.ds(lane0, POOL_GROUP)
        u = ext_ref[pl.ds(halo, tile), cols]
        cnt = jnp.clip(pos + 1, 1, w).astype(F32)
        delta = win / cnt - u
        mixed = jnp.dot(delta.astype(BF16), w_ref[gi], preferred_element_type=F32)
        o_ref[:, cols] = (mixed * scale_ref[:, cols]).astype(o_ref.dtype)
        if gi + 1 < len(POOL_WINDOWS):
            lvl_ref[gi, pl.ds(new_first, n_rows), lanes] = level
            src, first = lvl_ref.at[gi], new_first


def _pool(cols, pool_w_bf16, pool_scale, batch, lp):
    m = cols.shape[0]
    nt = lp // SEQ_TILE
    maxw = POOL_HALO
    return pl.pallas_call(
        _pool_kernel,
        grid=(batch, nt),
        in_specs=[
            pl.BlockSpec((SEQ_TILE, POOL_WIDTH), lambda b, t: (b * nt + t, OFF_POOL // POOL_WIDTH)),
            pl.BlockSpec((len(POOL_WINDOWS), POOL_GROUP, POOL_GROUP), lambda b, t: (0, 0, 0)),
            pl.BlockSpec((1, POOL_WIDTH), lambda b, t: (0, 0)),
        ],
        out_specs=pl.BlockSpec((SEQ_TILE, POOL_WIDTH), lambda b, t: (b * nt + t, 0)),
        out_shape=jax.ShapeDtypeStruct((m, POOL_WIDTH), BF16),
        scratch_shapes=[pltpu.VMEM((SEQ_TILE + maxw, POOL_WIDTH), F32),
                        pltpu.VMEM((len(POOL_WINDOWS) - 1, SEQ_TILE + maxw, POOL_WIDTH), F32)],
        compiler_params=_cparams("parallel", "arbitrary"),
        name="pool_mixer",
    )(cols, pool_w_bf16, pool_scale.reshape(1, POOL_WIDTH))


def _attn_bias():
    kj = jnp.arange(2 * BLOCK, dtype=jnp.int32)[:, None]
    qi = jnp.arange(BLOCK, dtype=jnp.int32)[None, :]
    dist = BLOCK + qi - kj
    in_window = (dist >= 0) & (dist < WINDOW)
    slopes = 2.0 ** (-8.0 * jnp.arange(1, N_HEADS + 1, dtype=F32) / N_HEADS)
    alibi = -slopes[:, None, None] * dist.astype(F32)[None]
    bias = jnp.where(in_window[None], alibi, NEG).reshape(N_KV_HEADS, Q_PER_KV, 2 * BLOCK, BLOCK)
    return bias.transpose(0, 2, 1, 3).reshape(N_KV_HEADS, 2 * BLOCK, Q_PER_KV * BLOCK)


def _attn_kernel(q_ref, kp_ref, kc_ref, vp_ref, vc_ref, bias_ref, sink_ref, o_ref):
    n = pl.program_id(1)
    q = q_ref[...] * (HEAD_DIM ** -0.5)

    def heads(early_keys):
        for kh in range(N_KV_HEADS):
            hs = pl.ds(kh * HEAD_DIM, HEAD_DIM)
            k2 = jnp.concatenate([kp_ref[:, hs], kc_ref[:, hs]], axis=0)
            v2 = jnp.concatenate([vp_ref[:, hs], vc_ref[:, hs]], axis=0)
            first = kh * Q_PER_KV
            qg = jnp.concatenate([q[:, (first + g) * HEAD_DIM:(first + g + 1) * HEAD_DIM]
                                  for g in range(Q_PER_KV)], axis=0)
            s = lax.dot_general(k2, qg, (((1,), (1,)), ((), ())), preferred_element_type=F32)
            s = s + bias_ref[kh]
            if early_keys is not None:
                s = s + early_keys
            sk = sink_ref[pl.ds(kh, 1), :]
            mx = jnp.maximum(jnp.max(s, axis=0, keepdims=True), sk)
            p = jnp.exp(s - mx)
            den = jnp.sum(p, axis=0, keepdims=True) + jnp.exp(sk - mx)
            pn = (p * (1.0 / den)).astype(BF16)
            o = lax.dot_general(pn, v2, (((0,), (0,)), ((), ())), preferred_element_type=F32)
            for g in range(Q_PER_KV):
                o_ref[:, pl.ds((first + g) * HEAD_DIM, HEAD_DIM)] = o[g * BLOCK:(g + 1) * BLOCK].astype(o_ref.dtype)

    @pl.when(n < 2)
    def _():
        k_pos = (n - 1) * BLOCK + lax.broadcasted_iota(jnp.int32, (2 * BLOCK, 1), 0)
        heads(jnp.where(k_pos < PAD, NEG, 0.0))

    @pl.when(n >= 2)
    def _():
        heads(None)


def _attention(cols, sink, batch, lp):
    m = cols.shape[0]
    nb = lp // BLOCK
    cur = lambda cb: (lambda b, n: (b * nb + n, cb))
    prev = lambda cb: (lambda b, n: (b * nb + jnp.maximum(n - 1, 0), cb))
    return pl.pallas_call(
        _attn_kernel,
        grid=(batch, nb),
        in_specs=[
            pl.BlockSpec((BLOCK, ATT_W), cur(OFF_Q // ATT_W)),
            pl.BlockSpec((BLOCK, KV_W), prev(OFF_K // KV_W)),
            pl.BlockSpec((BLOCK, KV_W), cur(OFF_K // KV_W)),
            pl.BlockSpec((BLOCK, KV_W), prev(OFF_V // KV_W)),
            pl.BlockSpec((BLOCK, KV_W), cur(OFF_V // KV_W)),
            pl.BlockSpec((N_KV_HEADS, 2 * BLOCK, Q_PER_KV * BLOCK), lambda b, n: (0, 0, 0)),
            pl.BlockSpec((N_KV_HEADS, Q_PER_KV * BLOCK), lambda b, n: (0, 0)),
        ],
        out_specs=pl.BlockSpec((BLOCK, ATT_W), lambda b, n: (b * nb + n, 0)),
        out_shape=jax.ShapeDtypeStruct((m, ATT_W), BF16),
        compiler_params=_cparams("parallel", "arbitrary"),
        name="swa_attention",
    )(cols, cols, cols, cols, cols, _attn_bias(),
      jnp.repeat(sink.astype(F32).reshape(N_KV_HEADS, Q_PER_KV), BLOCK, axis=1))


LRU_HALF = LRU_WIDTH // 2
LRU_HALO = 8
LRU_SCAN_UNROLL = 6


LOG2_E = 1.4426950408889634
GELU_C = 0.7978845608028654


def _gelu_tanh(x):
    half = 0.5 * x
    return half + half * jnp.tanh(x * (GELU_C + (GELU_C * 0.044715) * (x * x)))


def _lru_kernel(*refs):
    nh = LRU_WIDTH // LRU_HALF
    x_refs, y_refs = refs[:nh], refs[nh:2 * nh]
    (cw_ref, cb_ref, wa_ref, ba_ref, wx_ref, bx_ref, lam_ref, o_ref,
     ext_ref, a_ref, b_ref, carry_ref) = refs[2 * nh:]
    t = pl.program_id(1)
    tile = SEQ_TILE
    width = LRU_HALF
    pos = t * tile + lax.broadcasted_iota(jnp.int32, (tile, 1), 0)
    row = lax.broadcasted_iota(jnp.int32, (8, width), 0)

    for c in range(nh):
        lanes = pl.ds(c * width, width)
        ext, a_s, b_s, carry = ext_ref.at[c], a_ref.at[c], b_ref.at[c], carry_ref.at[c]

        @pl.when(t == 0)
        def _():
            ext[pl.ds(0, LRU_HALO), :] = jnp.zeros((LRU_HALO, width), F32)
            carry[...] = jnp.zeros((1, width), F32)

        @pl.when(t > 0)
        def _():
            ext[pl.ds(0, LRU_HALO), :] = ext[pl.ds(tile, LRU_HALO), :]

        ext[pl.ds(LRU_HALO, tile), :] = x_refs[c][...].astype(F32)

        xc = cb_ref[:, lanes] + cw_ref[pl.ds(CONV_WIDTH - 1, 1), lanes] * ext[pl.ds(LRU_HALO, tile), :]
        for j in range(CONV_WIDTH - 1):
            shift = CONV_WIDTH - 1 - j
            xc = xc + cw_ref[pl.ds(j, 1), lanes] * ext[pl.ds(LRU_HALO - shift, tile), :]

        xcb = xc.astype(BF16)
        ga_parts, gx_parts = [], []
        for blk in range(width // LRU_BLOCK):
            xb = xcb[:, blk * LRU_BLOCK:(blk + 1) * LRU_BLOCK]
            w_idx = c * (width // LRU_BLOCK) + blk
            ga_parts.append(jnp.dot(xb, wa_ref[w_idx], preferred_element_type=F32))
            gx_parts.append(jnp.dot(xb, wx_ref[w_idx], preferred_element_type=F32))
        gate_a = _sigmoid(jnp.concatenate(ga_parts, axis=1) + ba_ref[:, lanes])
        gate_x = _sigmoid(jnp.concatenate(gx_parts, axis=1) + bx_ref[:, lanes])

        neg_lam = -lam_ref[:, lanes]
        softplus = jnp.maximum(neg_lam, 0.0) + jnp.log1p(jnp.exp(-jnp.abs(neg_lam)))
        a = jnp.exp2(gate_a * ((-LRU_C * LOG2_E) * softplus))
        b_in = jnp.sqrt(1.0 - a * a) * gate_x * xc
        b_in = jnp.where(pos >= PAD, b_in, 0.0)
        a_s[...] = a
        b_s[...] = b_in

        def group(r, h_prev, a_s=a_s, b_s=b_s):
            rows = pl.ds(pl.multiple_of(r * 8, 8), 8)
            av = a_s[rows, :]
            bv = b_s[rows, :]
            for k in (1, 2, 4):
                a_sh = jnp.where(row >= k, pltpu.roll(av, k, 0), 1.0)
                b_sh = jnp.where(row >= k, pltpu.roll(bv, k, 0), 0.0)
                bv = av * b_sh + bv
                av = av * a_sh
            hv = av * h_prev + bv
            b_s[rows, :] = hv
            return hv[7:8, :]

        carry[...] = lax.fori_loop(0, tile // 8, group, carry[...], unroll=LRU_SCAN_UNROLL)
        o_ref[:, lanes] = (b_s[...] * _gelu_tanh(y_refs[c][...].astype(F32))).astype(o_ref.dtype)


def _lru(cols, conv_w, conv_b, wa_bf16, ba, wx_bf16, bx, lam, batch, lp):
    m = cols.shape[0]
    nt = lp // SEQ_TILE
    nh = LRU_WIDTH // LRU_HALF
    vec = lambda v: v.reshape(1, LRU_WIDTH).astype(F32)
    vec_spec = pl.BlockSpec((1, LRU_WIDTH), lambda b, t: (0, 0))
    w_spec = pl.BlockSpec((LRU_BLOCKS, LRU_BLOCK, LRU_BLOCK), lambda b, t: (0, 0, 0))
    half_spec = lambda off, c: pl.BlockSpec((SEQ_TILE, LRU_HALF), lambda b, t: (b * nt + t, off // LRU_HALF + c))
    return pl.pallas_call(
        _lru_kernel,
        grid=(batch, nt),
        in_specs=[half_spec(OFF_LX, c) for c in range(nh)] + [half_spec(OFF_LY, c) for c in range(nh)]
        + [pl.BlockSpec((CONV_WIDTH, LRU_WIDTH), lambda b, t: (0, 0)),
           vec_spec, w_spec, vec_spec, w_spec, vec_spec, vec_spec],
        out_specs=pl.BlockSpec((SEQ_TILE, LRU_WIDTH), lambda b, t: (b * nt + t, 0)),
        out_shape=jax.ShapeDtypeStruct((m, LRU_WIDTH), BF16),
        scratch_shapes=[
            pltpu.VMEM((nh, SEQ_TILE + LRU_HALO, LRU_HALF), F32),
            pltpu.VMEM((nh, SEQ_TILE, LRU_HALF), F32),
            pltpu.VMEM((nh, SEQ_TILE, LRU_HALF), F32),
            pltpu.VMEM((nh, 1, LRU_HALF), F32),
        ],
        compiler_params=_cparams("parallel", "arbitrary"),
        name="rglru",
    )(*([cols] * (2 * nh)), conv_w.astype(F32), vec(conv_b), wa_bf16, vec(ba), wx_bf16, vec(bx), vec(lam))


def _merge_kernel(p_ref, a_ref, r_ref, gp_ref, ga_ref, gr_ref, wp_ref, wa_ref, wr_ref, o_ref):
    acc = gp_ref[...].astype(F32) * jnp.dot(p_ref[...], wp_ref[...], preferred_element_type=F32)
    acc += ga_ref[...].astype(F32) * jnp.dot(a_ref[...], wa_ref[...], preferred_element_type=F32)
    acc += gr_ref[...].astype(F32) * jnp.dot(r_ref[...], wr_ref[...], preferred_element_type=F32)
    o_ref[...] = acc.astype(o_ref.dtype)


def _merge(pool_o, attn_o, lru_o, gates, wp, wa, wr, layer):
    m = pool_o.shape[0]
    bm, bn = MERGE_BM, MERGE_BN
    assert m % bm == 0 and D_MODEL % bn == 0
    x_spec = pl.BlockSpec((bm, POOL_WIDTH), lambda i, j: (i, 0))
    gate_spec = lambda k: pl.BlockSpec((bm, bn), lambda i, j: (i, k * D_MODEL // bn + j))
    w_spec = pl.BlockSpec((None, POOL_WIDTH, bn), lambda i, j: (layer, 0, j))
    return pl.pallas_call(
        _merge_kernel,
        grid=(m // bm, D_MODEL // bn),
        in_specs=[x_spec, x_spec, x_spec, gate_spec(0), gate_spec(1), gate_spec(2), w_spec, w_spec, w_spec],
        out_specs=pl.BlockSpec((bm, bn), lambda i, j: (i, j)),
        out_shape=jax.ShapeDtypeStruct((m, D_MODEL), BF16),
        compiler_params=_cparams("parallel", "arbitrary"),
        name="gated_merge",
    )(pool_o, attn_o, lru_o, gates, gates, gates, wp, wa, wr)


def _pack_bf16_pairs(yb):
    c = yb.shape[1] // 2
    lo = lax.bitcast_convert_type(yb[:, :c].astype(F32), jnp.uint32)
    hi = lax.bitcast_convert_type(yb[:, c:].astype(F32), jnp.uint32)
    return (hi & jnp.uint32(0xFFFF0000)) | (lo >> 16)


def _unpack_bf16_pairs(words, dtype=BF16):
    lo = lax.bitcast_convert_type(words << 16, F32)
    hi = lax.bitcast_convert_type(words & jnp.uint32(0xFFFF0000), F32)
    return jnp.concatenate([lo, hi], axis=1).astype(dtype)


def _route_tile(logits, valid, base):
    bm = logits.shape[0]
    lane = lax.broadcasted_iota(jnp.int32, (bm, ROUTE_W), 1)
    lane_f = lane.astype(F32)
    ninf = -jnp.inf
    big = float(ROUTE_W)

    gl = jnp.where(lane < N_GROUPS, logits, ninf)
    gmax = jnp.max(gl, axis=-1, keepdims=True)
    g = jnp.min(jnp.where(gl == gmax, lane_f, big), axis=-1, keepdims=True)
    p_g = 1.0 / jnp.sum(jnp.exp(gl - gmax), axis=-1, keepdims=True)

    first = N_GROUPS + g * EXPERTS_PER_GROUP
    sl = jnp.where((lane_f >= first) & (lane_f < first + EXPERTS_PER_GROUP), logits, ninf)
    m1 = jnp.max(sl, axis=-1, keepdims=True)
    i1 = jnp.min(jnp.where(sl == m1, lane_f, big), axis=-1, keepdims=True)
    ssum = jnp.sum(jnp.exp(sl - m1), axis=-1, keepdims=True)
    sl2 = jnp.where(lane_f == i1, ninf, sl)
    m2 = jnp.max(sl2, axis=-1, keepdims=True)
    i2 = jnp.min(jnp.where(sl2 == m2, lane_f, big), axis=-1, keepdims=True)
    p1 = 1.0 / ssum
    p2 = jnp.exp(m2 - m1) / ssum
    w1 = p_g * p1 / (p1 + p2)
    w2 = p_g * p2 / (p1 + p2)
    e1 = i1 - N_GROUPS
    e2 = i2 - N_GROUPS

    oh1 = (lane_f == e1) & valid
    oh2 = (lane_f == e2) & valid
    both = (oh1 | oh2).astype(F32)
    earlier = (lax.broadcasted_iota(jnp.int32, (bm, bm), 0) > lax.broadcasted_iota(jnp.int32, (bm, bm), 1))
    prefix = jnp.dot(earlier.astype(BF16), both.astype(BF16), preferred_element_type=F32) + base
    r1 = jnp.sum(jnp.where(oh1, prefix, 0.0), axis=-1, keepdims=True)
    r2 = jnp.sum(jnp.where(oh2, prefix, 0.0), axis=-1, keepdims=True)

    ri = jnp.where(lane == 0, e1, jnp.where(lane == 1, e2, jnp.where(lane == 2, r1, jnp.where(lane == 3, r2, 0.0))))
    rw = jnp.where(lane == 0, w1, jnp.where(lane == 1, w2, 0.0))
    ri_rows = jnp.transpose(ri)[0:ROUTE_ROWS, :].astype(jnp.int32)
    return ri_rows, rw, jnp.sum(both, axis=0, keepdims=True)


def _outproj_kernel(x_ref, w_ref, h_ref, g_ref, b_ref, rw_ref, rb_ref, hf_ref, hp_ref, ri_ref, rwt_ref, cnt_ref,
                    *, batch, lp):
    i = pl.program_id(0)
    bm = x_ref.shape[0]
    sub = bm // OUT_SPLIT

    @pl.when(i == 0)
    def _():
        cnt_ref[...] = jnp.zeros(cnt_ref.shape, F32)

    for s in range(OUT_SPLIT):
        rows = pl.ds(s * sub, sub)
        t = jnp.dot(x_ref[rows, :], w_ref[...], preferred_element_type=F32)
        y = _layer_norm(ALPHA * h_ref[rows, :] + t, g_ref[...], b_ref[...])
        valid = _flat_valid_rows(i * bm + s * sub, sub, batch, lp)
        y = jnp.where(valid, y, 0.0)
        yb = y.astype(BF16)
        hf_ref[rows, :] = y
        packed = _pack_bf16_pairs(yb)
        for c in range(ROW_TILES):
            hp_ref[pl.ds(s * sub * ROW_TILES + c, sub, stride=ROW_TILES), :] = packed[:, c * 128:(c + 1) * 128]
        logits = jnp.dot(yb, rw_ref[...], preferred_element_type=F32) + rb_ref[...]
        ri, rw, tile_cnt = _route_tile(logits, valid, cnt_ref[...])
        ri_ref[:, rows] = ri
        rwt_ref[rows, :] = rw
        cnt_ref[...] += tile_cnt


def _outproj_ln(merged, w_out_bf16, h, g, b, route_w, route_b, layer, batch, lp):
    m, d = h.shape
    bm = OUT_BM
    assert m % bm == 0
    row_spec = pl.BlockSpec((bm, d), lambda i: (i, 0))
    vec_spec = pl.BlockSpec((1, d), lambda i: (0, 0))
    route_spec = pl.BlockSpec((bm, ROUTE_W), lambda i: (i, 0))
    return pl.pallas_call(
        functools.partial(_outproj_kernel, batch=batch, lp=lp),
        grid=(m // bm,),
        in_specs=[
            row_spec,
            pl.BlockSpec((None, d, d), lambda i: (layer, 0, 0)),
            row_spec, vec_spec, vec_spec,
            pl.BlockSpec((d, ROUTE_W), lambda i: (0, 0)),
            pl.BlockSpec((1, ROUTE_W), lambda i: (0, 0)),
        ],
        out_specs=[row_spec, pl.BlockSpec((bm * ROW_TILES, 128), lambda i: (i, 0)),
                   pl.BlockSpec((ROUTE_ROWS, bm), lambda i: (0, i)), route_spec,
                   pl.BlockSpec((1, ROUTE_W), lambda i: (0, 0))],
        out_shape=[jax.ShapeDtypeStruct((m, d), F32), jax.ShapeDtypeStruct((m * ROW_TILES, 128), jnp.uint32),
                   jax.ShapeDtypeStruct((ROUTE_ROWS, m), jnp.int32), jax.ShapeDtypeStruct((m, ROUTE_W), F32),
                   jax.ShapeDtypeStruct((1, ROUTE_W), F32)],
        compiler_params=_cparams("arbitrary"),
        name="out_proj_ln",
    )(merged, w_out_bf16, h, g.reshape(1, d), b.reshape(1, d), route_w, route_b)


def _dispatch_kernel(start_ref, nchunk_ref, dest_hbm, hp_ref, xs_ref, idx_ref, zero_ref, idx_sem, row_sem, zero_sem,
                     *, n_expert_rows):
    i = pl.program_id(0)
    bm = hp_ref.shape[0] // ROW_TILES
    groups = bm // SUBLANES
    n_idx = TOP_K * bm

    def token_rows(first_token, n_tokens):
        return pl.ds(pl.multiple_of(first_token * ROW_TILES, ROW_TILES), n_tokens * ROW_TILES)

    @pl.when(i == 0)
    def _():
        zero_ref[...] = jnp.zeros(zero_ref.shape, zero_ref.dtype)

        def last_chunk(e):
            first = pl.multiple_of(start_ref[e] + (nchunk_ref[e] - 1) * MOE_BM, MOE_BM)
            return pltpu.make_async_copy(zero_ref, xs_ref.at[token_rows(first, MOE_BM)], zero_sem)

        for e in range(N_EXPERTS):
            @pl.when(nchunk_ref[e] > 0)
            def _():
                last_chunk(e).start()

        for e in range(N_EXPERTS):
            @pl.when(nchunk_ref[e] > 0)
            def _():
                last_chunk(e).wait()

        used = start_ref[N_EXPERTS - 1] + nchunk_ref[N_EXPERTS - 1] * MOE_BM

        def tail_chunk(j):
            return pltpu.make_async_copy(
                zero_ref, xs_ref.at[token_rows(pl.multiple_of(used + j * MOE_BM, MOE_BM), MOE_BM)], zero_sem)

        n_tail = (n_expert_rows - used) // MOE_BM
        lax.fori_loop(0, n_tail, lambda j, c: (tail_chunk(j).start(), c)[1], 0)
        lax.fori_loop(0, n_tail, lambda j, c: (tail_chunk(j).wait(), c)[1], 0)

    idx_copy = pltpu.make_async_copy(dest_hbm.at[pl.ds(pl.multiple_of(i * n_idx, n_idx), n_idx)], idx_ref, idx_sem)
    idx_copy.start()
    idx_copy.wait()

    def issue(g, carry):
        for s in range(SUBLANES):
            for k in range(TOP_K):
                slot = idx_ref[g * SUBLANES + (k * bm + s)]
                pltpu.make_async_copy(hp_ref.at[token_rows(g * SUBLANES + s, 1)], xs_ref.at[token_rows(slot, 1)],
                                      row_sem).start(priority=k % 2)
        return carry

    lax.fori_loop(0, groups, issue, 0)
    for k in range(TOP_K):
        pltpu.make_async_copy(xs_ref.at[token_rows(0, bm)], xs_ref.at[token_rows(0, bm)], row_sem).wait()


def _dispatch(hp, dest, chunk_start, n_chunks, n_expert_rows, n_slots):
    m = hp.shape[0] // ROW_TILES
    bm = DISPATCH_BM
    assert m % bm == 0 and bm % SUBLANES == 0
    grid_spec = pltpu.PrefetchScalarGridSpec(
        num_scalar_prefetch=2,
        grid=(m // bm,),
        in_specs=[
            pl.BlockSpec(memory_space=pl.ANY),
            pl.BlockSpec((bm * ROW_TILES, 128), lambda i, st, nc: (i, 0)),
        ],
        out_specs=pl.BlockSpec(memory_space=pl.ANY),
        scratch_shapes=[pltpu.SMEM((TOP_K * bm,), jnp.int32), pltpu.VMEM((MOE_BM * ROW_TILES, 128), jnp.uint32),
                        pltpu.SemaphoreType.DMA(()), pltpu.SemaphoreType.DMA(()), pltpu.SemaphoreType.DMA(())],
    )
    return pl.pallas_call(
        functools.partial(_dispatch_kernel, n_expert_rows=n_expert_rows),
        grid_spec=grid_spec,
        out_shape=jax.ShapeDtypeStruct((n_slots * ROW_TILES, 128), jnp.uint32),
        compiler_params=_cparams("arbitrary"),
        name="moe_dispatch",
    )(chunk_start, n_chunks, dest, hp)


def _expert_kernel(start_ref, nchunk_ref, xs_hbm, wg_hbm, wu_hbm, wd_hbm, yb_hbm, wg_f, wu_f, wd_f, wg_s, wu_s, wd_s,
                   x_buf, o_buf, w_sem, in_sem, out_sem, *, layer):
    e = pl.program_id(0)
    n = nchunk_ref[e]
    bm = MOE_BM
    first = start_ref[e] // bm
    total = start_ref[N_EXPERTS - 1] // bm + nchunk_ref[N_EXPERTS - 1]
    wslot = e % 2

    def w_copies(expert, slot):
        return [pltpu.make_async_copy(src.at[layer, expert], dst.at[slot], w_sem.at[slot])
                for src, dst in ((wg_hbm, wg_f), (wu_hbm, wu_f), (wd_hbm, wd_f))]

    def o_rows(g):
        return pl.ds(pl.multiple_of(g * (bm * ROW_TILES), bm * ROW_TILES), bm * ROW_TILES)

    def x_copy(g):
        slot = g % EXPERT_X_BUFFERS
        return pltpu.make_async_copy(xs_hbm.at[o_rows(g)], x_buf.at[slot], in_sem.at[slot])

    def o_copy(g):
        slot = g % 2
        return pltpu.make_async_copy(o_buf.at[slot], yb_hbm.at[o_rows(g)], out_sem.at[slot])

    @pl.when(e == 0)
    def _():
        for ahead in range(EXPERT_X_BUFFERS - 1):
            @pl.when(ahead < total)
            def _():
                x_copy(ahead).start()

        for c in w_copies(0, 0):
            c.start(priority=1)

    @pl.when(e + 1 < N_EXPERTS)
    def _():
        for c in w_copies(e + 1, 1 - wslot):
            c.start(priority=1)

    for c in w_copies(e, wslot):
        c.wait()

    @pl.when(n > 0)
    def _():
        wg_s[...] = wg_f[wslot].astype(BF16)
        wu_s[...] = wu_f[wslot].astype(BF16)
        wd_s[...] = wd_f[wslot].astype(BF16)

        def chunk(j, carry):
            g = first + j
            x_copy(g).wait()

            @pl.when(g + EXPERT_X_BUFFERS - 1 < total)
            def _():
                x_copy(g + EXPERT_X_BUFFERS - 1).start()

            @pl.when(g >= 2)
            def _():
                o_copy(g - 2).wait()

            x_slot = g % EXPERT_X_BUFFERS
            x = _unpack_bf16_pairs(jnp.concatenate(
                [x_buf[x_slot, pl.ds(c, bm, stride=ROW_TILES), :] for c in range(ROW_TILES)], axis=1))
            gate = jnp.dot(x, wg_s[...], preferred_element_type=F32)
            up = jnp.dot(x, wu_s[...], preferred_element_type=F32)
            hdn = (gate * _sigmoid(gate)) * up
            y = jnp.dot(hdn.astype(BF16), wd_s[...], preferred_element_type=F32)
            packed = _pack_bf16_pairs(y.astype(BF16))
            for c in range(ROW_TILES):
                o_buf[g % 2, pl.ds(c, bm, stride=ROW_TILES), :] = packed[:, c * 128:(c + 1) * 128]
            o_copy(g).start()
            return carry

        lax.fori_loop(0, n, chunk, 0)

    @pl.when(e == N_EXPERTS - 1)
    def _():
        for back in (2, 1):
            @pl.when(total >= back)
            def _():
                o_copy(total - back).wait()

        o_buf[0] = jnp.zeros(o_buf.shape[1:], o_buf.dtype)

        def fill(g, carry):
            tail = pltpu.make_async_copy(o_buf.at[0], yb_hbm.at[o_rows(g)], out_sem.at[0])
            tail.start()
            tail.wait()
            return carry

        lax.fori_loop(total, yb_hbm.shape[0] // (bm * ROW_TILES), fill, 0)


def _experts(xs, n_rows, chunk_start, n_chunks, w_gate, w_up, w_down, layer):
    d = D_MODEL
    bm = MOE_BM
    any_spec = pl.BlockSpec(memory_space=pl.ANY)
    grid_spec = pltpu.PrefetchScalarGridSpec(
        num_scalar_prefetch=2,
        grid=(N_EXPERTS,),
        in_specs=[any_spec, any_spec, any_spec, any_spec],
        out_specs=any_spec,
        scratch_shapes=[
            pltpu.VMEM((2, d, D_EXPERT), F32),
            pltpu.VMEM((2, d, D_EXPERT), F32),
            pltpu.VMEM((2, D_EXPERT, d), F32),
            pltpu.VMEM((d, D_EXPERT), BF16),
            pltpu.VMEM((d, D_EXPERT), BF16),
            pltpu.VMEM((D_EXPERT, d), BF16),
            pltpu.VMEM((EXPERT_X_BUFFERS, bm * ROW_TILES, 128), jnp.uint32),
            pltpu.VMEM((2, bm * ROW_TILES, 128), jnp.uint32),
            pltpu.SemaphoreType.DMA((2,)),
            pltpu.SemaphoreType.DMA((EXPERT_X_BUFFERS,)),
            pltpu.SemaphoreType.DMA((2,)),
        ],
    )
    return pl.pallas_call(
        functools.partial(_expert_kernel, layer=layer),
        grid_spec=grid_spec,
        out_shape=jax.ShapeDtypeStruct((n_rows * ROW_TILES, 128), jnp.uint32),
        compiler_params=_cparams("arbitrary"),
        name="expert_mlp",
    )(chunk_start, n_chunks, xs, w_gate, w_up, w_down)


def _combine_ln_kernel(dest_hbm, h_ref, rw_ref, g_ref, b_ref, yb_hbm, *rest, batch, lp, final):
    if final:
        out_hbm, idx_ref, rows_ref, y_buf, idx_sem, row_sem, out_sem = rest
    else:
        hf_ref, hb_ref, idx_ref, rows_ref, idx_sem, row_sem = rest
    i = pl.program_id(0)
    n_tiles = pl.num_programs(0)
    bm = h_ref.shape[0]
    n_idx = TOP_K * bm
    buf = i % COMBINE_ROW_BUFFERS

    def idx_copy(tile):
        half = pl.ds(pl.multiple_of((tile % 2) * n_idx, n_idx), n_idx)
        return pltpu.make_async_copy(
            dest_hbm.at[pl.ds(pl.multiple_of(tile * n_idx, n_idx), n_idx)], idx_ref.at[half], idx_sem.at[tile % 2])

    def start_gather(tile):
        into = tile % COMBINE_ROW_BUFFERS
        idx_copy(tile).wait()

        def issue(g, carry):
            base = (tile % 2) * n_idx + g * SUBLANES
            for s in range(SUBLANES):
                for k in range(TOP_K):
                    slot = idx_ref[base + (k * bm + s)]
                    src = yb_hbm.at[pl.ds(pl.multiple_of(slot * ROW_TILES, ROW_TILES), ROW_TILES)]
                    dst = rows_ref.at[into, k, pl.ds(pl.multiple_of((g * SUBLANES + s) * ROW_TILES, ROW_TILES),
                                                     ROW_TILES)]
                    pltpu.make_async_copy(src, dst, row_sem.at[into]).start(priority=k % 2)
            return carry

        lax.fori_loop(0, bm // SUBLANES, issue, 0)

        @pl.when(tile + 2 < n_tiles)
        def _():
            idx_copy(tile + 2).start()

    @pl.when(i == 0)
    def _():
        idx_copy(0).start()

        @pl.when(1 < n_tiles)
        def _():
            idx_copy(1).start()

        for ahead in range(COMBINE_ROW_BUFFERS - 1):
            @pl.when(ahead < n_tiles)
            def _():
                start_gather(ahead)

    @pl.when(i + COMBINE_ROW_BUFFERS - 1 < n_tiles)
    def _():
        start_gather(i + COMBINE_ROW_BUFFERS - 1)

    for k in range(TOP_K):
        whole = pl.ds(0, bm * ROW_TILES)
        pltpu.make_async_copy(yb_hbm.at[whole], yb_hbm.at[whole], row_sem.at[buf]).wait()

    def expert_rows(k):
        pieces = [rows_ref[buf, k, pl.ds(c, bm, stride=ROW_TILES), :] for c in range(ROW_TILES)]
        return _unpack_bf16_pairs(jnp.concatenate(pieces, axis=1), F32)

    ffn = rw_ref[:, 0:1] * expert_rows(0)
    for k in range(1, TOP_K):
        ffn = ffn + rw_ref[:, k:k + 1] * expert_rows(k)
    y = _layer_norm(ALPHA * h_ref[...] + ffn, g_ref[...], b_ref[...])
    if not final:
        y = jnp.where(_flat_valid_rows(i * bm, bm, batch, lp), y, 0.0)
        hf_ref[...] = y
        hb_ref[...] = y.astype(BF16)
        return

    nb = lp // BLOCK

    def for_each_out_block(step, fn):
        for j in range(bm // BLOCK):
            blk = step * (bm // BLOCK) + j
            seq_blk = blk % nb

            @pl.when(seq_blk >= 1)
            def _():
                dst = pl.multiple_of(((blk // nb) * (nb - 1) + seq_blk - 1) * BLOCK, BLOCK)
                fn(pltpu.make_async_copy(y_buf.at[pl.ds(j * BLOCK, BLOCK)], out_hbm.at[pl.ds(dst, BLOCK)], out_sem))

    @pl.when(i > 0)
    def _():
        for_each_out_block(i - 1, lambda c: c.wait())

    y_buf[...] = y
    for_each_out_block(i, lambda c: c.start(priority=1))

    @pl.when(i == n_tiles - 1)
    def _():
        for_each_out_block(i, lambda c: c.wait())


def _combine_ln(h, yb, dest, rw, g, b, batch, lp, final):
    m, d = h.shape
    bm = COMBINE_BM
    assert m % bm == 0 and bm % BLOCK == 0
    row_spec = pl.BlockSpec((bm, d), lambda i: (i, 0))
    vec_spec = pl.BlockSpec((1, d), lambda i: (0, 0))
    scratch = [pltpu.SMEM((2 * TOP_K * bm,), jnp.int32),
               pltpu.VMEM((COMBINE_ROW_BUFFERS, TOP_K, bm * ROW_TILES, 128), jnp.uint32)]
    sems = [pltpu.SemaphoreType.DMA((2,)), pltpu.SemaphoreType.DMA((COMBINE_ROW_BUFFERS,))]
    if final:
        out_specs = pl.BlockSpec(memory_space=pl.ANY)
        out_shape = jax.ShapeDtypeStruct((batch * (lp - BLOCK), d), F32)
        scratch = scratch + [pltpu.VMEM((bm, d), F32)] + sems + [pltpu.SemaphoreType.DMA(())]
    else:
        out_specs = [row_spec, row_spec]
        out_shape = [jax.ShapeDtypeStruct((m, d), F32), jax.ShapeDtypeStruct((m, d), BF16)]
        scratch = scratch + sems
    return pl.pallas_call(
        functools.partial(_combine_ln_kernel, batch=batch, lp=lp, final=final),
        grid=(m // bm,),
        in_specs=[
            pl.BlockSpec(memory_space=pl.ANY),
            row_spec,
            pl.BlockSpec((bm, ROUTE_W), lambda i: (i, 0)),
            vec_spec, vec_spec,
            pl.BlockSpec(memory_space=pl.ANY),
        ],
        out_specs=out_specs,
        out_shape=out_shape,
        scratch_shapes=scratch,
        compiler_params=_cparams("arbitrary"),
        name="moe_combine_out" if final else "moe_combine_ln",
    )(dest, h, rw, g.reshape(1, d), b.reshape(1, d), yb)


def _slot_tables(ri, cnt, batch, lp):
    m = ri.shape[1]
    bm = MOE_BM
    dense = (TOP_K, m // ROUTE_W, ROUTE_W)
    eid = ri[0:TOP_K].reshape(dense)
    rank = ri[TOP_K:2 * TOP_K].reshape(dense)
    counts = cnt[0, :N_EXPERTS].astype(jnp.int32)
    padded = (counts + bm - 1) // bm * bm
    pad_end = jnp.cumsum(padded)
    pad_start = pad_end - padded
    n_real = batch * (lp - PAD) * TOP_K
    nblk = -(-(n_real + N_EXPERTS * (bm - 1)) // bm)
    cap = nblk * bm
    start = jnp.zeros_like(eid)
    for e in range(N_EXPERTS):
        start = jnp.where(eid == e, pad_start[e], start)
    row = jnp.arange(m, dtype=jnp.int32).reshape(dense[1:])
    pos = row % lp
    spare = cap + ((row // lp) * PAD + pos)[None] * TOP_K + jnp.arange(TOP_K, dtype=jnp.int32)[:, None, None]
    valid = (pos >= PAD)[None]
    dest = jnp.where(valid, start + rank, spare)
    dest_read = jnp.where(valid, dest, 0)
    n_slots = cap + batch * PAD * TOP_K

    def by_tile(idx, tile):
        return idx.reshape(TOP_K, m // tile, tile).transpose(1, 0, 2).reshape(m * TOP_K)

    return (by_tile(dest, DISPATCH_BM), by_tile(dest_read, COMBINE_BM), pad_start.astype(jnp.int32),
            (padded // bm).astype(jnp.int32), cap, n_slots)


def kernel(x, meta, ln_emb_g, ln_emb_b, w_in, pool_w, pool_scale, attn_sink, conv_w, conv_b, lru_wa, lru_ba,
           lru_wx, lru_bx, lru_lambda, proj_pool, proj_attn, proj_lru, w_out, ln1_g, ln1_b, router_grp_w,
           router_grp_b, router_exp_w, router_exp_b, exp_w_gate, exp_w_up, exp_w_down, ln2_g, ln2_b):
    batch, seq, d = x.shape
    lp = PAD + N_META + seq
    m = batch * lp

    hf, hb = _embed(x, meta, ln_emb_g, ln_emb_b)
    w_in_b = w_in.astype(BF16)
    pool_w_b = pool_w.astype(BF16)
    wa_b = lru_wa.astype(BF16)
    wx_b = lru_wx.astype(BF16)
    wp_b = proj_pool.astype(BF16)
    wat_b = proj_attn.astype(BF16)
    wl_b = proj_lru.astype(BF16)
    w_out_b = w_out.astype(BF16)
    route_pad = ROUTE_W - N_GROUPS - N_EXPERTS

    for l in range(DEPTH):
        cols = _inproj(hb, w_in_b, l, gates=False)
        gates = _inproj(hb, w_in_b, l, gates=True)
        pool_o = _pool(cols, pool_w_b[l], pool_scale[l], batch, lp)
        attn_o = _attention(cols, attn_sink[l], batch, lp)
        lru_o = _lru(cols, conv_w[l], conv_b[l], wa_b[l], lru_ba[l], wx_b[l], lru_bx[l], lru_lambda[l],
                     batch, lp)
        merged = _merge(pool_o, attn_o, lru_o, gates, wp_b, wat_b, wl_b, l)
        route_w = jnp.concatenate(
            [router_grp_w[l], router_exp_w[l], jnp.zeros((d, route_pad), F32)], axis=1).astype(BF16)
        route_b = jnp.concatenate(
            [router_grp_b[l], router_exp_b[l], jnp.zeros((route_pad,), F32)]).reshape(1, ROUTE_W)
        h1f, h1p, ri, rw, cnt = _outproj_ln(merged, w_out_b, hf, ln1_g[l], ln1_b[l], route_w, route_b, l,
                                            batch, lp)
        dest, dest_read, chunk_start, n_chunks, cap, n_slots = _slot_tables(ri, cnt, batch, lp)
        xs = _dispatch(h1p, dest, chunk_start, n_chunks, cap, n_slots)
        yb = _experts(xs, cap, chunk_start, n_chunks, exp_w_gate, exp_w_up, exp_w_down, l)
        if l + 1 < DEPTH:
            hf, hb = _combine_ln(h1f, yb, dest_read, rw, ln2_g[l], ln2_b[l], batch, lp, final=False)
        else:
            out = _combine_ln(h1f, yb, dest_read, rw, ln2_g[l], ln2_b[l], batch, lp, final=True)

    return out.reshape(batch, seq, d)
```

```python
import functools

import jax
import jax.numpy as jnp
from jax import lax
from jax.experimental import pallas as pl
from jax.experimental.pallas import tpu as pltpu

F32 = jnp.float32
BF16 = jnp.bfloat16

D_MODEL = 2048
DEPTH = 2
N_META = 16
POOL_WINDOWS = (2, 4, 8, 16)
POOL_WIDTH = D_MODEL // 2
POOL_GROUP = POOL_WIDTH // len(POOL_WINDOWS)
N_HEADS = 16
N_KV_HEADS = 4
HEAD_DIM = 64
Q_PER_KV = N_HEADS // N_KV_HEADS
WINDOW = 128
BLOCK = 128
NEG = -1e30
LRU_WIDTH = D_MODEL // 2
LRU_BLOCKS = 4
LRU_BLOCK = LRU_WIDTH // LRU_BLOCKS
CONV_WIDTH = 4
LRU_C = 8.0
N_GROUPS = 4
EXPERTS_PER_GROUP = 8
N_EXPERTS = N_GROUPS * EXPERTS_PER_GROUP
TOP_K = 2
D_EXPERT = D_MODEL // 4
LN_EPS = 1e-5
ALPHA = (2.0 * DEPTH) ** 0.25

PAD = BLOCK - N_META
ATT_W = N_HEADS * HEAD_DIM
KV_W = N_KV_HEADS * HEAD_DIM
OFF_POOL = 0
OFF_Q = OFF_POOL + POOL_WIDTH
OFF_K = OFF_Q + ATT_W
OFF_V = OFF_K + KV_W
OFF_LX = OFF_V + KV_W
OFF_LY = OFF_LX + LRU_WIDTH
OFF_GATE = OFF_LY + LRU_WIDTH
IN_COLS = OFF_GATE + 3 * D_MODEL

VMEM_LIMIT_BYTES = 56 * 1024 * 1024
SUBLANES = 8

SEQ_TILE = 3 * BLOCK
INPROJ_BM = 1536
INPROJ_BN = 1536
MERGE_BM = 768
MERGE_BN = 1024
OUT_BM = 512
OUT_SPLIT = 2
MOE_BM = 256
DISPATCH_BM = 1536
COMBINE_BM = 512
COMBINE_ROW_BUFFERS = 3
EXPERT_X_BUFFERS = 3
ROW_TILES = D_MODEL // 2 // 128
ROUTE_W = 128
ROUTE_ROWS = 8


def _cparams(*sem):
    return pltpu.CompilerParams(dimension_semantics=sem, vmem_limit_bytes=VMEM_LIMIT_BYTES)


def _layer_norm(x, g, b):
    mu = jnp.mean(x, axis=-1, keepdims=True)
    xc = x - mu
    var = jnp.mean(xc * xc, axis=-1, keepdims=True)
    return xc * lax.rsqrt(var + LN_EPS) * g + b


def _flat_valid_rows(row0, n_rows, batch, lp):
    r = row0 + lax.broadcasted_iota(jnp.int32, (n_rows, 1), 0)
    pad_row = jnp.zeros((n_rows, 1), jnp.bool_)
    for b in range(batch):
        pad_row = pad_row | ((r >= b * lp) & (r < b * lp + PAD))
    return jnp.logical_not(pad_row)


EMBED_BLOCKS = SEQ_TILE // BLOCK


def _embed_kernel(*refs):
    x_refs, (meta_ref, g_ref, b_ref, hf_ref, hb_ref) = refs[:EMBED_BLOCKS], refs[EMBED_BLOCKS:]
    t = pl.program_id(1)
    for j in range(EMBED_BLOCKS):
        src = x_refs[j][...]
        if j == 0:
            src = jnp.where(t == 0, meta_ref[...], src)
        y = _layer_norm(src, g_ref[...], b_ref[...])
        if j == 0:
            row = lax.broadcasted_iota(jnp.int32, (BLOCK, 1), 0)
            y = jnp.where((t > 0) | (row >= PAD), y, 0.0)
        hf_ref[pl.ds(j * BLOCK, BLOCK), :] = y
        hb_ref[pl.ds(j * BLOCK, BLOCK), :] = y.astype(BF16)


def _embed(x, meta, g, b):
    batch, seq, d = x.shape
    nblk = seq // BLOCK
    lp = (nblk + 1) * BLOCK
    nt = lp // SEQ_TILE
    m = batch * lp
    meta_tile = jnp.concatenate([jnp.zeros((PAD, d), F32), meta.astype(F32)], axis=0)
    row_spec = pl.BlockSpec((SEQ_TILE, d), lambda bi, t: (bi * nt + t, 0))
    vec_spec = pl.BlockSpec((1, d), lambda bi, t: (0, 0))
    x_spec = lambda j: pl.BlockSpec(
        (BLOCK, d), lambda bi, t: (bi * nblk + jnp.maximum(EMBED_BLOCKS * t + j - 1, 0), 0))
    x2 = x.reshape(batch * seq, d)
    return pl.pallas_call(
        _embed_kernel,
        grid=(batch, nt),
        in_specs=[x_spec(j) for j in range(EMBED_BLOCKS)]
        + [pl.BlockSpec((BLOCK, d), lambda bi, t: (0, 0)), vec_spec, vec_spec],
        out_specs=[row_spec, row_spec],
        out_shape=[jax.ShapeDtypeStruct((m, d), F32), jax.ShapeDtypeStruct((m, d), BF16)],
        compiler_params=_cparams("parallel", "arbitrary"),
        name="embed_ln",
    )(*([x2] * EMBED_BLOCKS), meta_tile, g.reshape(1, d), b.reshape(1, d))


def _sigmoid(x):
    return 0.5 * jnp.tanh(0.5 * x) + 0.5


def _inproj_kernel(x_ref, w_ref, o_ref, *, gates):
    acc = jnp.dot(x_ref[...], w_ref[...], preferred_element_type=F32)
    o_ref[...] = (_sigmoid(acc) if gates else acc).astype(o_ref.dtype)


def _inproj(hb, w_in_bf16, layer, gates):
    m, d = hb.shape
    bm, bn = INPROJ_BM, INPROJ_BN
    col0, width = (OFF_GATE, IN_COLS - OFF_GATE) if gates else (0, OFF_GATE)
    assert m % bm == 0 and width % bn == 0 and col0 % bn == 0
    return pl.pallas_call(
        functools.partial(_inproj_kernel, gates=gates),
        grid=(m // bm, width // bn),
        in_specs=[
            pl.BlockSpec((bm, d), lambda i, j: (i, 0)),
            pl.BlockSpec((None, d, bn), lambda i, j: (layer, 0, col0 // bn + j)),
        ],
        out_specs=pl.BlockSpec((bm, bn), lambda i, j: (i, j)),
        out_shape=jax.ShapeDtypeStruct((m, width), BF16),
        compiler_params=_cparams("parallel", "arbitrary"),
        name="in_proj_gates" if gates else "in_proj",
    )(hb, w_in_bf16)


POOL_HALO = 2 * max(POOL_WINDOWS)
assert POOL_WINDOWS == tuple(2 ** (g + 1) for g in range(len(POOL_WINDOWS)))


def _pool_kernel(u_ref, w_ref, scale_ref, o_ref, ext_ref, lvl_ref):
    t = pl.program_id(1)
    tile = SEQ_TILE
    halo = POOL_HALO

    @pl.when(t == 0)
    def _():
        ext_ref[pl.ds(0, halo), :] = jnp.zeros((halo, POOL_WIDTH), F32)

    @pl.when(t > 0)
    def _():
        ext_ref[pl.ds(0, halo), :] = ext_ref[pl.ds(tile, halo), :]

    ext_ref[pl.ds(halo, tile), :] = u_ref[...].astype(F32)

    pos = t * tile + lax.broadcasted_iota(jnp.int32, (tile, 1), 0) - PAD
    src, first = ext_ref, 0
    for gi, w in enumerate(POOL_WINDOWS):
        lane0 = gi * POOL_GROUP
        lanes = pl.ds(lane0, POOL_WIDTH - lane0)
        new_first = -(-(first + w // 2) // SUBLANES) * SUBLANES
        n_rows = tile + halo - new_first
        level = src[pl.ds(new_first, n_rows), lanes] + src[pl.ds(new_first - w // 2, n_rows), lanes]
        win = level[halo - new_first:, :POOL_GROUP]
        cols = pl.ds(lane0, POOL_GROUP)
        u = ext_ref[pl.ds(halo, tile), cols]
        cnt = jnp.clip(pos + 1, 1, w).astype(F32)
        delta = win / cnt - u
        mixed = jnp.dot(delta.astype(BF16), w_ref[gi], preferred_element_type=F32)
        o_ref[:, cols] = (mixed * scale_ref[:, cols]).astype(o_ref.dtype)
        if gi + 1 < len(POOL_WINDOWS):
            lvl_ref[gi, pl.ds(new_first, n_rows), lanes] = level
            src, first = lvl_ref.at[gi], new_first


def _pool(cols, pool_w_bf16, pool_scale, batch, lp):
    m = cols.shape[0]
    nt = lp // SEQ_TILE
    maxw = POOL_HALO
    return pl.pallas_call(
        _pool_kernel,
        grid=(batch, nt),
        in_specs=[
            pl.BlockSpec((SEQ_TILE, POOL_WIDTH), lambda b, t: (b * nt + t, OFF_POOL // POOL_WIDTH)),
            pl.BlockSpec((len(POOL_WINDOWS), POOL_GROUP, POOL_GROUP), lambda b, t: (0, 0, 0)),
            pl.BlockSpec((1, POOL_WIDTH), lambda b, t: (0, 0)),
        ],
        out_specs=pl.BlockSpec((SEQ_TILE, POOL_WIDTH), lambda b, t: (b * nt + t, 0)),
        out_shape=jax.ShapeDtypeStruct((m, POOL_WIDTH), BF16),
        scratch_shapes=[pltpu.VMEM((SEQ_TILE + maxw, POOL_WIDTH), F32),
                        pltpu.VMEM((len(POOL_WINDOWS) - 1, SEQ_TILE + maxw, POOL_WIDTH), F32)],
        compiler_params=_cparams("parallel", "arbitrary"),
        name="pool_mixer",
    )(cols, pool_w_bf16, pool_scale.reshape(1, POOL_WIDTH))


def _attn_bias():
    kj = jnp.arange(2 * BLOCK, dtype=jnp.int32)[:, None]
    qi = jnp.arange(BLOCK, dtype=jnp.int32)[None, :]
    dist = BLOCK + qi - kj
    in_window = (dist >= 0) & (dist < WINDOW)
    slopes = 2.0 ** (-8.0 * jnp.arange(1, N_HEADS + 1, dtype=F32) / N_HEADS)
    alibi = -slopes[:, None, None] * dist.astype(F32)[None]
    bias = jnp.where(in_window[None], alibi, NEG).reshape(N_KV_HEADS, Q_PER_KV, 2 * BLOCK, BLOCK)
    return bias.transpose(0, 2, 1, 3).reshape(N_KV_HEADS, 2 * BLOCK, Q_PER_KV * BLOCK)


def _attn_kernel(q_ref, kp_ref, kc_ref, vp_ref, vc_ref, bias_ref, sink_ref, o_ref):
    n = pl.program_id(1)
    q = q_ref[...] * (HEAD_DIM ** -0.5)

    def heads(early_keys):
        for kh in range(N_KV_HEADS):
            hs = pl.ds(kh * HEAD_DIM, HEAD_DIM)
            k2 = jnp.concatenate([kp_ref[:, hs], kc_ref[:, hs]], axis=0)
            v2 = jnp.concatenate([vp_ref[:, hs], vc_ref[:, hs]], axis=0)
            first = kh * Q_PER_KV
            qg = jnp.concatenate([q[:, (first + g) * HEAD_DIM:(first + g + 1) * HEAD_DIM]
                                  for g in range(Q_PER_KV)], axis=0)
            s = lax.dot_general(k2, qg, (((1,), (1,)), ((), ())), preferred_element_type=F32)
            s = s + bias_ref[kh]
            if early_keys is not None:
                s = s + early_keys
            sk = sink_ref[pl.ds(kh, 1), :]
            mx = jnp.maximum(jnp.max(s, axis=0, keepdims=True), sk)
            p = jnp.exp(s - mx)
            den = jnp.sum(p, axis=0, keepdims=True) + jnp.exp(sk - mx)
            pn = (p * (1.0 / den)).astype(BF16)
            o = lax.dot_general(pn, v2, (((0,), (0,)), ((), ())), preferred_element_type=F32)
            for g in range(Q_PER_KV):
                o_ref[:, pl.ds((first + g) * HEAD_DIM, HEAD_DIM)] = o[g * BLOCK:(g + 1) * BLOCK].astype(o_ref.dtype)

    @pl.when(n < 2)
    def _():
        k_pos = (n - 1) * BLOCK + lax.broadcasted_iota(jnp.int32, (2 * BLOCK, 1), 0)
        heads(jnp.where(k_pos < PAD, NEG, 0.0))

    @pl.when(n >= 2)
    def _():
        heads(None)


def _attention(cols, sink, batch, lp):
    m = cols.shape[0]
    nb = lp // BLOCK
    cur = lambda cb: (lambda b, n: (b * nb + n, cb))
    prev = lambda cb: (lambda b, n: (b * nb + jnp.maximum(n - 1, 0), cb))
    return pl.pallas_call(
        _attn_kernel,
        grid=(batch, nb),
        in_specs=[
            pl.BlockSpec((BLOCK, ATT_W), cur(OFF_Q // ATT_W)),
            pl.BlockSpec((BLOCK, KV_W), prev(OFF_K // KV_W)),
            pl.BlockSpec((BLOCK, KV_W), cur(OFF_K // KV_W)),
            pl.BlockSpec((BLOCK, KV_W), prev(OFF_V // KV_W)),
            pl.BlockSpec((BLOCK, KV_W), cur(OFF_V // KV_W)),
            pl.BlockSpec((N_KV_HEADS, 2 * BLOCK, Q_PER_KV * BLOCK), lambda b, n: (0, 0, 0)),
            pl.BlockSpec((N_KV_HEADS, Q_PER_KV * BLOCK), lambda b, n: (0, 0)),
        ],
        out_specs=pl.BlockSpec((BLOCK, ATT_W), lambda b, n: (b * nb + n, 0)),
        out_shape=jax.ShapeDtypeStruct((m, ATT_W), BF16),
        compiler_params=_cparams("parallel", "arbitrary"),
        name="swa_attention",
    )(cols, cols, cols, cols, cols, _attn_bias(),
      jnp.repeat(sink.astype(F32).reshape(N_KV_HEADS, Q_PER_KV), BLOCK, axis=1))


LRU_HALF = LRU_WIDTH // 2
LRU_HALO = 8
LRU_SCAN_UNROLL = 6


LOG2_E = 1.4426950408889634
GELU_C = 0.7978845608028654


def _gelu_tanh(x):
    half = 0.5 * x
    return half + half * jnp.tanh(x * (GELU_C + (GELU_C * 0.044715) * (x * x)))


def _lru_kernel(*refs):
    nh = LRU_WIDTH // LRU_HALF
    x_refs, y_refs = refs[:nh], refs[nh:2 * nh]
    (cw_ref, cb_ref, wa_ref, ba_ref, wx_ref, bx_ref, lam_ref, o_ref,
     ext_ref, a_ref, b_ref, carry_ref) = refs[2 * nh:]
    t = pl.program_id(1)
    tile = SEQ_TILE
    width = LRU_HALF
    pos = t * tile + lax.broadcasted_iota(jnp.int32, (tile, 1), 0)
    row = lax.broadcasted_iota(jnp.int32, (8, width), 0)

    for c in range(nh):
        lanes = pl.ds(c * width, width)
        ext, a_s, b_s, carry = ext_ref.at[c], a_ref.at[c], b_ref.at[c], carry_ref.at[c]

        @pl.when(t == 0)
        def _():
            ext[pl.ds(0, LRU_HALO), :] = jnp.zeros((LRU_HALO, width), F32)
            carry[...] = jnp.zeros((1, width), F32)

        @pl.when(t > 0)
        def _():
            ext[pl.ds(0, LRU_HALO), :] = ext[pl.ds(tile, LRU_HALO), :]

        ext[pl.ds(LRU_HALO, tile), :] = x_refs[c][...].astype(F32)

        xc = cb_ref[:, lanes] + cw_ref[pl.ds(CONV_WIDTH - 1, 1), lanes] * ext[pl.ds(LRU_HALO, tile), :]
        for j in range(CONV_WIDTH - 1):
            shift = CONV_WIDTH - 1 - j
            xc = xc + cw_ref[pl.ds(j, 1), lanes] * ext[pl.ds(LRU_HALO - shift, tile), :]

        xcb = xc.astype(BF16)
        ga_parts, gx_parts = [], []
        for blk in range(width // LRU_BLOCK):
            xb = xcb[:, blk * LRU_BLOCK:(blk + 1) * LRU_BLOCK]
            w_idx = c * (width // LRU_BLOCK) + blk
            ga_parts.append(jnp.dot(xb, wa_ref[w_idx], preferred_element_type=F32))
            gx_parts.append(jnp.dot(xb, wx_ref[w_idx], preferred_element_type=F32))
        gate_a = _sigmoid(jnp.concatenate(ga_parts, axis=1) + ba_ref[:, lanes])
        gate_x = _sigmoid(jnp.concatenate(gx_parts, axis=1) + bx_ref[:, lanes])

        neg_lam = -lam_ref[:, lanes]
        softplus = jnp.maximum(neg_lam, 0.0) + jnp.log1p(jnp.exp(-jnp.abs(neg_lam)))
        a = jnp.exp2(gate_a * ((-LRU_C * LOG2_E) * softplus))
        b_in = jnp.sqrt(1.0 - a * a) * gate_x * xc
        b_in = jnp.where(pos >= PAD, b_in, 0.0)
        a_s[...] = a
        b_s[...] = b_in

        def group(r, h_prev, a_s=a_s, b_s=b_s):
            rows = pl.ds(pl.multiple_of(r * 8, 8), 8)
            av = a_s[rows, :]
            bv = b_s[rows, :]
            for k in (1, 2, 4):
                a_sh = jnp.where(row >= k, pltpu.roll(av, k, 0), 1.0)
                b_sh = jnp.where(row >= k, pltpu.roll(bv, k, 0), 0.0)
                bv = av * b_sh + bv
                av = av * a_sh
            hv = av * h_prev + bv
            b_s[rows, :] = hv
            return hv[7:8, :]

        carry[...] = lax.fori_loop(0, tile // 8, group, carry[...], unroll=LRU_SCAN_UNROLL)
        o_ref[:, lanes] = (b_s[...] * _gelu_tanh(y_refs[c][...].astype(F32))).astype(o_ref.dtype)


def _lru(cols, conv_w, conv_b, wa_bf16, ba, wx_bf16, bx, lam, batch, lp):
    m = cols.shape[0]
    nt = lp // SEQ_TILE
    nh = LRU_WIDTH // LRU_HALF
    vec = lambda v: v.reshape(1, LRU_WIDTH).astype(F32)
    vec_spec = pl.BlockSpec((1, LRU_WIDTH), lambda b, t: (0, 0))
    w_spec = pl.BlockSpec((LRU_BLOCKS, LRU_BLOCK, LRU_BLOCK), lambda b, t: (0, 0, 0))
    half_spec = lambda off, c: pl.BlockSpec((SEQ_TILE, LRU_HALF), lambda b, t: (b * nt + t, off // LRU_HALF + c))
    return pl.pallas_call(
        _lru_kernel,
        grid=(batch, nt),
        in_specs=[half_spec(OFF_LX, c) for c in range(nh)] + [half_spec(OFF_LY, c) for c in range(nh)]
        + [pl.BlockSpec((CONV_WIDTH, LRU_WIDTH), lambda b, t: (0, 0)),
           vec_spec, w_spec, vec_spec, w_spec, vec_spec, vec_spec],
        out_specs=pl.BlockSpec((SEQ_TILE, LRU_WIDTH), lambda b, t: (b * nt + t, 0)),
        out_shape=jax.ShapeDtypeStruct((m, LRU_WIDTH), BF16),
        scratch_shapes=[
            pltpu.VMEM((nh, SEQ_TILE + LRU_HALO, LRU_HALF), F32),
            pltpu.VMEM((nh, SEQ_TILE, LRU_HALF), F32),
            pltpu.VMEM((nh, SEQ_TILE, LRU_HALF), F32),
            pltpu.VMEM((nh, 1, LRU_HALF), F32),
        ],
        compiler_params=_cparams("parallel", "arbitrary"),
        name="rglru",
    )(*([cols] * (2 * nh)), conv_w.astype(F32), vec(conv_b), wa_bf16, vec(ba), wx_bf16, vec(bx), vec(lam))


def _merge_kernel(p_ref, a_ref, r_ref, gp_ref, ga_ref, gr_ref, wp_ref, wa_ref, wr_ref, o_ref):
    acc = gp_ref[...].astype(F32) * jnp.dot(p_ref[...], wp_ref[...], preferred_element_type=F32)
    acc += ga_ref[...].astype(F32) * jnp.dot(a_ref[...], wa_ref[...], preferred_element_type=F32)
    acc += gr_ref[...].astype(F32) * jnp.dot(r_ref[...], wr_ref[...], preferred_element_type=F32)
    o_ref[...] = acc.astype(o_ref.dtype)


def _merge(pool_o, attn_o, lru_o, gates, wp, wa, wr, layer):
    m = pool_o.shape[0]
    bm, bn = MERGE_BM, MERGE_BN
    assert m % bm == 0 and D_MODEL % bn == 0
    x_spec = pl.BlockSpec((bm, POOL_WIDTH), lambda i, j: (i, 0))
    gate_spec = lambda k: pl.BlockSpec((bm, bn), lambda i, j: (i, k * D_MODEL // bn + j))
    w_spec = pl.BlockSpec((None, POOL_WIDTH, bn), lambda i, j: (layer, 0, j))
    return pl.pallas_call(
        _merge_kernel,
        grid=(m // bm, D_MODEL // bn),
        in_specs=[x_spec, x_spec, x_spec, gate_spec(0), gate_spec(1), gate_spec(2), w_spec, w_spec, w_spec],
        out_specs=pl.BlockSpec((bm, bn), lambda i, j: (i, j)),
        out_shape=jax.ShapeDtypeStruct((m, D_MODEL), BF16),
        compiler_params=_cparams("parallel", "arbitrary"),
        name="gated_merge",
    )(pool_o, attn_o, lru_o, gates, gates, gates, wp, wa, wr)


def _pack_bf16_pairs(yb):
    c = yb.shape[1] // 2
    lo = lax.bitcast_convert_type(yb[:, :c].astype(F32), jnp.uint32)
    hi = lax.bitcast_convert_type(yb[:, c:].astype(F32), jnp.uint32)
    return (hi & jnp.uint32(0xFFFF0000)) | (lo >> 16)


def _unpack_bf16_pairs(words, dtype=BF16):
    lo = lax.bitcast_convert_type(words << 16, F32)
    hi = lax.bitcast_convert_type(words & jnp.uint32(0xFFFF0000), F32)
    return jnp.concatenate([lo, hi], axis=1).astype(dtype)


def _route_tile(logits, valid, base):
    bm = logits.shape[0]
    lane = lax.broadcasted_iota(jnp.int32, (bm, ROUTE_W), 1)
    lane_f = lane.astype(F32)
    ninf = -jnp.inf
    big = float(ROUTE_W)

    gl = jnp.where(lane < N_GROUPS, logits, ninf)
    gmax = jnp.max(gl, axis=-1, keepdims=True)
    g = jnp.min(jnp.where(gl == gmax, lane_f, big), axis=-1, keepdims=True)
    p_g = 1.0 / jnp.sum(jnp.exp(gl - gmax), axis=-1, keepdims=True)

    first = N_GROUPS + g * EXPERTS_PER_GROUP
    sl = jnp.where((lane_f >= first) & (lane_f < first + EXPERTS_PER_GROUP), logits, ninf)
    m1 = jnp.max(sl, axis=-1, keepdims=True)
    i1 = jnp.min(jnp.where(sl == m1, lane_f, big), axis=-1, keepdims=True)
    ssum = jnp.sum(jnp.exp(sl - m1), axis=-1, keepdims=True)
    sl2 = jnp.where(lane_f == i1, ninf, sl)
    m2 = jnp.max(sl2, axis=-1, keepdims=True)
    i2 = jnp.min(jnp.where(sl2 == m2, lane_f, big), axis=-1, keepdims=True)
    p1 = 1.0 / ssum
    p2 = jnp.exp(m2 - m1) / ssum
    w1 = p_g * p1 / (p1 + p2)
    w2 = p_g * p2 / (p1 + p2)
    e1 = i1 - N_GROUPS
    e2 = i2 - N_GROUPS

    oh1 = (lane_f == e1) & valid
    oh2 = (lane_f == e2) & valid
    both = (oh1 | oh2).astype(F32)
    earlier = (lax.broadcasted_iota(jnp.int32, (bm, bm), 0) > lax.broadcasted_iota(jnp.int32, (bm, bm), 1))
    prefix = jnp.dot(earlier.astype(BF16), both.astype(BF16), preferred_element_type=F32) + base
    r1 = jnp.sum(jnp.where(oh1, prefix, 0.0), axis=-1, keepdims=True)
    r2 = jnp.sum(jnp.where(oh2, prefix, 0.0), axis=-1, keepdims=True)

    ri = jnp.where(lane == 0, e1, jnp.where(lane == 1, e2, jnp.where(lane == 2, r1, jnp.where(lane == 3, r2, 0.0))))
    rw = jnp.where(lane == 0, w1, jnp.where(lane == 1, w2, 0.0))
    ri_rows = jnp.transpose(ri)[0:ROUTE_ROWS, :].astype(jnp.int32)
    return ri_rows, rw, jnp.sum(both, axis=0, keepdims=True)


def _outproj_kernel(x_ref, w_ref, h_ref, g_ref, b_ref, rw_ref, rb_ref, hf_ref, hp_ref, ri_ref, rwt_ref, cnt_ref,
                    *, batch, lp):
    i = pl.program_id(0)
    bm = x_ref.shape[0]
    sub = bm // OUT_SPLIT

    @pl.when(i == 0)
    def _():
        cnt_ref[...] = jnp.zeros(cnt_ref.shape, F32)

    for s in range(OUT_SPLIT):
        rows = pl.ds(s * sub, sub)
        t = jnp.dot(x_ref[rows, :], w_ref[...], preferred_element_type=F32)
        y = _layer_norm(ALPHA * h_ref[rows, :] + t, g_ref[...], b_ref[...])
        valid = _flat_valid_rows(i * bm + s * sub, sub, batch, lp)
        y = jnp.where(valid, y, 0.0)
        yb = y.astype(BF16)
        hf_ref[rows, :] = y
        packed = _pack_bf16_pairs(yb)
        for c in range(ROW_TILES):
            hp_ref[pl.ds(s * sub * ROW_TILES + c, sub, stride=ROW_TILES), :] = packed[:, c * 128:(c + 1) * 128]
        logits = jnp.dot(yb, rw_ref[...], preferred_element_type=F32) + rb_ref[...]
        ri, rw, tile_cnt = _route_tile(logits, valid, cnt_ref[...])
        ri_ref[:, rows] = ri
        rwt_ref[rows, :] = rw
        cnt_ref[...] += tile_cnt


def _outproj_ln(merged, w_out_bf16, h, g, b, route_w, route_b, layer, batch, lp):
    m, d = h.shape
    bm = OUT_BM
    assert m % bm == 0
    row_spec = pl.BlockSpec((bm, d), lambda i: (i, 0))
    vec_spec = pl.BlockSpec((1, d), lambda i: (0, 0))
    route_spec = pl.BlockSpec((bm, ROUTE_W), lambda i: (i, 0))
    return pl.pallas_call(
        functools.partial(_outproj_kernel, batch=batch, lp=lp),
        grid=(m // bm,),
        in_specs=[
            row_spec,
            pl.BlockSpec((None, d, d), lambda i: (layer, 0, 0)),
            row_spec, vec_spec, vec_spec,
            pl.BlockSpec((d, ROUTE_W), lambda i: (0, 0)),
            pl.BlockSpec((1, ROUTE_W), lambda i: (0, 0)),
        ],
        out_specs=[row_spec, pl.BlockSpec((bm * ROW_TILES, 128), lambda i: (i, 0)),
                   pl.BlockSpec((ROUTE_ROWS, bm), lambda i: (0, i)), route_spec,
                   pl.BlockSpec((1, ROUTE_W), lambda i: (0, 0))],
        out_shape=[jax.ShapeDtypeStruct((m, d), F32), jax.ShapeDtypeStruct((m * ROW_TILES, 128), jnp.uint32),
                   jax.ShapeDtypeStruct((ROUTE_ROWS, m), jnp.int32), jax.ShapeDtypeStruct((m, ROUTE_W), F32),
                   jax.ShapeDtypeStruct((1, ROUTE_W), F32)],
        compiler_params=_cparams("arbitrary"),
        name="out_proj_ln",
    )(merged, w_out_bf16, h, g.reshape(1, d), b.reshape(1, d), route_w, route_b)


def _dispatch_kernel(start_ref, nchunk_ref, dest_hbm, hp_ref, xs_ref, idx_ref, zero_ref, idx_sem, row_sem, zero_sem,
                     *, n_expert_rows):
    i = pl.program_id(0)
    bm = hp_ref.shape[0] // ROW_TILES
    groups = bm // SUBLANES
    n_idx = TOP_K * bm

    def token_rows(first_token, n_tokens):
        return pl.ds(pl.multiple_of(first_token * ROW_TILES, ROW_TILES), n_tokens * ROW_TILES)

    @pl.when(i == 0)
    def _():
        zero_ref[...] = jnp.zeros(zero_ref.shape, zero_ref.dtype)

        def last_chunk(e):
            first = pl.multiple_of(start_ref[e] + (nchunk_ref[e] - 1) * MOE_BM, MOE_BM)
            return pltpu.make_async_copy(zero_ref, xs_ref.at[token_rows(first, MOE_BM)], zero_sem)

        for e in range(N_EXPERTS):
            @pl.when(nchunk_ref[e] > 0)
            def _():
                last_chunk(e).start()

        for e in range(N_EXPERTS):
            @pl.when(nchunk_ref[e] > 0)
            def _():
                last_chunk(e).wait()

        used = start_ref[N_EXPERTS - 1] + nchunk_ref[N_EXPERTS - 1] * MOE_BM

        def tail_chunk(j):
            return pltpu.make_async_copy(
                zero_ref, xs_ref.at[token_rows(pl.multiple_of(used + j * MOE_BM, MOE_BM), MOE_BM)], zero_sem)

        n_tail = (n_expert_rows - used) // MOE_BM
        lax.fori_loop(0, n_tail, lambda j, c: (tail_chunk(j).start(), c)[1], 0)
        lax.fori_loop(0, n_tail, lambda j, c: (tail_chunk(j).wait(), c)[1], 0)

    idx_copy = pltpu.make_async_copy(dest_hbm.at[pl.ds(pl.multiple_of(i * n_idx, n_idx), n_idx)], idx_ref, idx_sem)
    idx_copy.start()
    idx_copy.wait()

    def issue(g, carry):
        for s in range(SUBLANES):
            for k in range(TOP_K):
                slot = idx_ref[g * SUBLANES + (k * bm + s)]
                pltpu.make_async_copy(hp_ref.at[token_rows(g * SUBLANES + s, 1)], xs_ref.at[token_rows(slot, 1)],
                                      row_sem).start(priority=k % 2)
        return carry

    lax.fori_loop(0, groups, issue, 0)
    for k in range(TOP_K):
        pltpu.make_async_copy(xs_ref.at[token_rows(0, bm)], xs_ref.at[token_rows(0, bm)], row_sem).wait()


def _dispatch(hp, dest, chunk_start, n_chunks, n_expert_rows, n_slots):
    m = hp.shape[0] // ROW_TILES
    bm = DISPATCH_BM
    assert m % bm == 0 and bm % SUBLANES == 0
    grid_spec = pltpu.PrefetchScalarGridSpec(
        num_scalar_prefetch=2,
        grid=(m // bm,),
        in_specs=[
            pl.BlockSpec(memory_space=pl.ANY),
            pl.BlockSpec((bm * ROW_TILES, 128), lambda i, st, nc: (i, 0)),
        ],
        out_specs=pl.BlockSpec(memory_space=pl.ANY),
        scratch_shapes=[pltpu.SMEM((TOP_K * bm,), jnp.int32), pltpu.VMEM((MOE_BM * ROW_TILES, 128), jnp.uint32),
                        pltpu.SemaphoreType.DMA(()), pltpu.SemaphoreType.DMA(()), pltpu.SemaphoreType.DMA(())],
    )
    return pl.pallas_call(
        functools.partial(_dispatch_kernel, n_expert_rows=n_expert_rows),
        grid_spec=grid_spec,
        out_shape=jax.ShapeDtypeStruct((n_slots * ROW_TILES, 128), jnp.uint32),
        compiler_params=_cparams("arbitrary"),
        name="moe_dispatch",
    )(chunk_start, n_chunks, dest, hp)


def _expert_kernel(start_ref, nchunk_ref, xs_hbm, wg_hbm, wu_hbm, wd_hbm, yb_hbm, wg_f, wu_f, wd_f, wg_s, wu_s, wd_s,
                   x_buf, o_buf, w_sem, in_sem, out_sem, *, layer):
    e = pl.program_id(0)
    n = nchunk_ref[e]
    bm = MOE_BM
    first = start_ref[e] // bm
    total = start_ref[N_EXPERTS - 1] // bm + nchunk_ref[N_EXPERTS - 1]
    wslot = e % 2

    def w_copies(expert, slot):
        return [pltpu.make_async_copy(src.at[layer, expert], dst.at[slot], w_sem.at[slot])
                for src, dst in ((wg_hbm, wg_f), (wu_hbm, wu_f), (wd_hbm, wd_f))]

    def o_rows(g):
        return pl.ds(pl.multiple_of(g * (bm * ROW_TILES), bm * ROW_TILES), bm * ROW_TILES)

    def x_copy(g):
        slot = g % EXPERT_X_BUFFERS
        return pltpu.make_async_copy(xs_hbm.at[o_rows(g)], x_buf.at[slot], in_sem.at[slot])

    def o_copy(g):
        slot = g % 2
        return pltpu.make_async_copy(o_buf.at[slot], yb_hbm.at[o_rows(g)], out_sem.at[slot])

    @pl.when(e == 0)
    def _():
        for ahead in range(EXPERT_X_BUFFERS - 1):
            @pl.when(ahead < total)
            def _():
                x_copy(ahead).start()

        for c in w_copies(0, 0):
            c.start(priority=1)

    @pl.when(e + 1 < N_EXPERTS)
    def _():
        for c in w_copies(e + 1, 1 - wslot):
            c.start(priority=1)

    for c in w_copies(e, wslot):
        c.wait()

    @pl.when(n > 0)
    def _():
        wg_s[...] = wg_f[wslot].astype(BF16)
        wu_s[...] = wu_f[wslot].astype(BF16)
        wd_s[...] = wd_f[wslot].astype(BF16)

        def chunk(j, carry):
            g = first + j
            x_copy(g).wait()

            @pl.when(g + EXPERT_X_BUFFERS - 1 < total)
            def _():
                x_copy(g + EXPERT_X_BUFFERS - 1).start()

            @pl.when(g >= 2)
            def _():
                o_copy(g - 2).wait()

            x_slot = g % EXPERT_X_BUFFERS
            x = _unpack_bf16_pairs(jnp.concatenate(
                [x_buf[x_slot, pl.ds(c, bm, stride=ROW_TILES), :] for c in range(ROW_TILES)], axis=1))
            gate = jnp.dot(x, wg_s[...], preferred_element_type=F32)
            up = jnp.dot(x, wu_s[...], preferred_element_type=F32)
            hdn = (gate * _sigmoid(gate)) * up
            y = jnp.dot(hdn.astype(BF16), wd_s[...], preferred_element_type=F32)
            packed = _pack_bf16_pairs(y.astype(BF16))
            for c in range(ROW_TILES):
                o_buf[g % 2, pl.ds(c, bm, stride=ROW_TILES), :] = packed[:, c * 128:(c + 1) * 128]
            o_copy(g).start()
            return carry

        lax.fori_loop(0, n, chunk, 0)

    @pl.when(e == N_EXPERTS - 1)
    def _():
        for back in (2, 1):
            @pl.when(total >= back)
            def _():
                o_copy(total - back).wait()

        o_buf[0] = jnp.zeros(o_buf.shape[1:], o_buf.dtype)

        def fill(g, carry):
            tail = pltpu.make_async_copy(o_buf.at[0], yb_hbm.at[o_rows(g)], out_sem.at[0])
            tail.start()
            tail.wait()
            return carry

        lax.fori_loop(total, yb_hbm.shape[0] // (bm * ROW_TILES), fill, 0)


def _experts(xs, n_rows, chunk_start, n_chunks, w_gate, w_up, w_down, layer):
    d = D_MODEL
    bm = MOE_BM
    any_spec = pl.BlockSpec(memory_space=pl.ANY)
    grid_spec = pltpu.PrefetchScalarGridSpec(
        num_scalar_prefetch=2,
        grid=(N_EXPERTS,),
        in_specs=[any_spec, any_spec, any_spec, any_spec],
        out_specs=any_spec,
        scratch_shapes=[
            pltpu.VMEM((2, d, D_EXPERT), F32),
            pltpu.VMEM((2, d, D_EXPERT), F32),
            pltpu.VMEM((2, D_EXPERT, d), F32),
            pltpu.VMEM((d, D_EXPERT), BF16),
            pltpu.VMEM((d, D_EXPERT), BF16),
            pltpu.VMEM((D_EXPERT, d), BF16),
            pltpu.VMEM((EXPERT_X_BUFFERS, bm * ROW_TILES, 128), jnp.uint32),
            pltpu.VMEM((2, bm * ROW_TILES, 128), jnp.uint32),
            pltpu.SemaphoreType.DMA((2,)),
            pltpu.SemaphoreType.DMA((EXPERT_X_BUFFERS,)),
            pltpu.SemaphoreType.DMA((2,)),
        ],
    )
    return pl.pallas_call(
        functools.partial(_expert_kernel, layer=layer),
        grid_spec=grid_spec,
        out_shape=jax.ShapeDtypeStruct((n_rows * ROW_TILES, 128), jnp.uint32),
        compiler_params=_cparams("arbitrary"),
        name="expert_mlp",
    )(chunk_start, n_chunks, xs, w_gate, w_up, w_down)


def _combine_ln_kernel(dest_hbm, h_ref, rw_ref, g_ref, b_ref, yb_hbm, *rest, batch, lp, final):
    if final:
        out_hbm, idx_ref, rows_ref, y_buf, idx_sem, row_sem, out_sem = rest
    else:
        hf_ref, hb_ref, idx_ref, rows_ref, idx_sem, row_sem = rest
    i = pl.program_id(0)
    n_tiles = pl.num_programs(0)
    bm = h_ref.shape[0]
    n_idx = TOP_K * bm
    buf = i % COMBINE_ROW_BUFFERS

    def idx_copy(tile):
        half = pl.ds(pl.multiple_of((tile % 2) * n_idx, n_idx), n_idx)
        return pltpu.make_async_copy(
            dest_hbm.at[pl.ds(pl.multiple_of(tile * n_idx, n_idx), n_idx)], idx_ref.at[half], idx_sem.at[tile % 2])

    def start_gather(tile):
        into = tile % COMBINE_ROW_BUFFERS
        idx_copy(tile).wait()

        def issue(g, carry):
            base = (tile % 2) * n_idx + g * SUBLANES
            for s in range(SUBLANES):
                for k in range(TOP_K):
                    slot = idx_ref[base + (k * bm + s)]
                    src = yb_hbm.at[pl.ds(pl.multiple_of(slot * ROW_TILES, ROW_TILES), ROW_TILES)]
                    dst = rows_ref.at[into, k, pl.ds(pl.multiple_of((g * SUBLANES + s) * ROW_TILES, ROW_TILES),
                                                     ROW_TILES)]
                    pltpu.make_async_copy(src, dst, row_sem.at[into]).start(priority=k % 2)
            return carry

        lax.fori_loop(0, bm // SUBLANES, issue, 0)

        @pl.when(tile + 2 < n_tiles)
        def _():
            idx_copy(tile + 2).start()

    @pl.when(i == 0)
    def _():
        idx_copy(0).start()

        @pl.when(1 < n_tiles)
        def _():
            idx_copy(1).start()

        for ahead in range(COMBINE_ROW_BUFFERS - 1):
            @pl.when(ahead < n_tiles)
            def _():
                start_gather(ahead)

    @pl.when(i + COMBINE_ROW_BUFFERS - 1 < n_tiles)
    def _():
        start_gather(i + COMBINE_ROW_BUFFERS - 1)

    for k in range(TOP_K):
        whole = pl.ds(0, bm * ROW_TILES)
        pltpu.make_async_copy(yb_hbm.at[whole], yb_hbm.at[whole], row_sem.at[buf]).wait()

    def expert_rows(k):
        pieces = [rows_ref[buf, k, pl.ds(c, bm, stride=ROW_TILES), :] for c in range(ROW_TILES)]
        return _unpack_bf16_pairs(jnp.concatenate(pieces, axis=1), F32)

    ffn = rw_ref[:, 0:1] * expert_rows(0)
    for k in range(1, TOP_K):
        ffn = ffn + rw_ref[:, k:k + 1] * expert_rows(k)
    y = _layer_norm(ALPHA * h_ref[...] + ffn, g_ref[...], b_ref[...])
    if not final:
        y = jnp.where(_flat_valid_rows(i * bm, bm, batch, lp), y, 0.0)
        hf_ref[...] = y
        hb_ref[...] = y.astype(BF16)
        return

    nb = lp // BLOCK

    def for_each_out_block(step, fn):
        for j in range(bm // BLOCK):
            blk = step * (bm // BLOCK) + j
            seq_blk = blk % nb

            @pl.when(seq_blk >= 1)
            def _():
                dst = pl.multiple_of(((blk // nb) * (nb - 1) + seq_blk - 1) * BLOCK, BLOCK)
                fn(pltpu.make_async_copy(y_buf.at[pl.ds(j * BLOCK, BLOCK)], out_hbm.at[pl.ds(dst, BLOCK)], out_sem))

    @pl.when(i > 0)
    def _():
        for_each_out_block(i - 1, lambda c: c.wait())

    y_buf[...] = y
    for_each_out_block(i, lambda c: c.start(priority=1))

    @pl.when(i == n_tiles - 1)
    def _():
        for_each_out_block(i, lambda c: c.wait())


def _combine_ln(h, yb, dest, rw, g, b, batch, lp, final):
    m, d = h.shape
    bm = COMBINE_BM
    assert m % bm == 0 and bm % BLOCK == 0
    row_spec = pl.BlockSpec((bm, d), lambda i: (i, 0))
    vec_spec = pl.BlockSpec((1, d), lambda i: (0, 0))
    scratch = [pltpu.SMEM((2 * TOP_K * bm,), jnp.int32),
               pltpu.VMEM((COMBINE_ROW_BUFFERS, TOP_K, bm * ROW_TILES, 128), jnp.uint32)]
    sems = [pltpu.SemaphoreType.DMA((2,)), pltpu.SemaphoreType.DMA((COMBINE_ROW_BUFFERS,))]
    if final:
        out_specs = pl.BlockSpec(memory_space=pl.ANY)
        out_shape = jax.ShapeDtypeStruct((batch * (lp - BLOCK), d), F32)
        scratch = scratch + [pltpu.VMEM((bm, d), F32)] + sems + [pltpu.SemaphoreType.DMA(())]
    else:
        out_specs = [row_spec, row_spec]
        out_shape = [jax.ShapeDtypeStruct((m, d), F32), jax.ShapeDtypeStruct((m, d), BF16)]
        scratch = scratch + sems
    return pl.pallas_call(
        functools.partial(_combine_ln_kernel, batch=batch, lp=lp, final=final),
        grid=(m // bm,),
        in_specs=[
            pl.BlockSpec(memory_space=pl.ANY),
            row_spec,
            pl.BlockSpec((bm, ROUTE_W), lambda i: (i, 0)),
            vec_spec, vec_spec,
            pl.BlockSpec(memory_space=pl.ANY),
        ],
        out_specs=out_specs,
        out_shape=out_shape,
        scratch_shapes=scratch,
        compiler_params=_cparams("arbitrary"),
        name="moe_combine_out" if final else "moe_combine_ln",
    )(dest, h, rw, g.reshape(1, d), b.reshape(1, d), yb)


def _slot_tables(ri, cnt, batch, lp):
    m = ri.shape[1]
    bm = MOE_BM
    eid = ri[0:TOP_K]
    rank = ri[TOP_K:2 * TOP_K]
    counts = cnt[0, :N_EXPERTS].astype(jnp.int32)
    padded = (counts + bm - 1) // bm * bm
    pad_end = jnp.cumsum(padded)
    pad_start = pad_end - padded
    n_real = batch * (lp - PAD) * TOP_K
    nblk = -(-(n_real + N_EXPERTS * (bm - 1)) // bm)
    cap = nblk * bm
    start = jnp.zeros_like(eid)
    for e in range(N_EXPERTS):
        start = jnp.where(eid == e, pad_start[e], start)
    row = jnp.arange(m, dtype=jnp.int32)
    pos = row % lp
    spare = cap + ((row // lp) * PAD + pos)[None, :] * TOP_K + jnp.arange(TOP_K, dtype=jnp.int32)[:, None]
    valid = (pos >= PAD)[None, :]
    dest = jnp.where(valid, start + rank, spare)
    dest_read = jnp.where(valid, dest, 0)
    n_slots = cap + batch * PAD * TOP_K

    def by_tile(idx, tile):
        return idx.reshape(TOP_K, m // tile, tile).transpose(1, 0, 2).reshape(m * TOP_K)

    return (by_tile(dest, DISPATCH_BM), by_tile(dest_read, COMBINE_BM), pad_start.astype(jnp.int32),
            (padded // bm).astype(jnp.int32), cap, n_slots)


def kernel(x, meta, ln_emb_g, ln_emb_b, w_in, pool_w, pool_scale, attn_sink, conv_w, conv_b, lru_wa, lru_ba,
           lru_wx, lru_bx, lru_lambda, proj_pool, proj_attn, proj_lru, w_out, ln1_g, ln1_b, router_grp_w,
           router_grp_b, router_exp_w, router_exp_b, exp_w_gate, exp_w_up, exp_w_down, ln2_g, ln2_b):
    batch, seq, d = x.shape
    lp = PAD + N_META + seq
    m = batch * lp

    hf, hb = _embed(x, meta, ln_emb_g, ln_emb_b)
    w_in_b = w_in.astype(BF16)
    pool_w_b = pool_w.astype(BF16)
    wa_b = lru_wa.astype(BF16)
    wx_b = lru_wx.astype(BF16)
    wp_b = proj_pool.astype(BF16)
    wat_b = proj_attn.astype(BF16)
    wl_b = proj_lru.astype(BF16)
    w_out_b = w_out.astype(BF16)
    route_pad = ROUTE_W - N_GROUPS - N_EXPERTS

    for l in range(DEPTH):
        cols = _inproj(hb, w_in_b, l, gates=False)
        gates = _inproj(hb, w_in_b, l, gates=True)
        pool_o = _pool(cols, pool_w_b[l], pool_scale[l], batch, lp)
        attn_o = _attention(cols, attn_sink[l], batch, lp)
        lru_o = _lru(cols, conv_w[l], conv_b[l], wa_b[l], lru_ba[l], wx_b[l], lru_bx[l], lru_lambda[l],
                     batch, lp)
        merged = _merge(pool_o, attn_o, lru_o, gates, wp_b, wat_b, wl_b, l)
        route_w = jnp.concatenate(
            [router_grp_w[l], router_exp_w[l], jnp.zeros((d, route_pad), F32)], axis=1).astype(BF16)
        route_b = jnp.concatenate(
            [router_grp_b[l], router_exp_b[l], jnp.zeros((route_pad,), F32)]).reshape(1, ROUTE_W)
        h1f, h1p, ri, rw, cnt = _outproj_ln(merged, w_out_b, hf, ln1_g[l], ln1_b[l], route_w, route_b, l,
                                            batch, lp)
        dest, dest_read, chunk_start, n_chunks, cap, n_slots = _slot_tables(ri, cnt, batch, lp)
        xs = _dispatch(h1p, dest, chunk_start, n_chunks, cap, n_slots)
        yb = _experts(xs, cap, chunk_start, n_chunks, exp_w_gate, exp_w_up, exp_w_down, l)
        if l + 1 < DEPTH:
            hf, hb = _combine_ln(h1f, yb, dest_read, rw, ln2_g[l], ln2_b[l], batch, lp, final=False)
        else:
            out = _combine_ln(h1f, yb, dest_read, rw, ln2_g[l], ln2_b[l], batch, lp, final=True)

    return out.reshape(batch, seq, d)
```

```python
import functools

import jax
import jax.numpy as jnp
from jax import lax
from jax.experimental import pallas as pl
from jax.experimental.pallas import tpu as pltpu

F32 = jnp.float32
BF16 = jnp.bfloat16

D_MODEL = 2048
DEPTH = 2
N_META = 16
POOL_WINDOWS = (2, 4, 8, 16)
POOL_WIDTH = D_MODEL // 2
POOL_GROUP = POOL_WIDTH // len(POOL_WINDOWS)
N_HEADS = 16
N_KV_HEADS = 4
HEAD_DIM = 64
Q_PER_KV = N_HEADS // N_KV_HEADS
WINDOW = 128
BLOCK = 128
NEG = -1e30
LRU_WIDTH = D_MODEL // 2
LRU_BLOCKS = 4
LRU_BLOCK = LRU_WIDTH // LRU_BLOCKS
CONV_WIDTH = 4
LRU_C = 8.0
N_GROUPS = 4
EXPERTS_PER_GROUP = 8
N_EXPERTS = N_GROUPS * EXPERTS_PER_GROUP
TOP_K = 2
D_EXPERT = D_MODEL // 4
LN_EPS = 1e-5
ALPHA = (2.0 * DEPTH) ** 0.25

PAD = BLOCK - N_META
ATT_W = N_HEADS * HEAD_DIM
KV_W = N_KV_HEADS * HEAD_DIM
OFF_POOL = 0
OFF_Q = OFF_POOL + POOL_WIDTH
OFF_K = OFF_Q + ATT_W
OFF_V = OFF_K + KV_W
OFF_LX = OFF_V + KV_W
OFF_LY = OFF_LX + LRU_WIDTH
OFF_GATE = OFF_LY + LRU_WIDTH
IN_COLS = OFF_GATE + 3 * D_MODEL

VMEM_LIMIT_BYTES = 56 * 1024 * 1024
SUBLANES = 8
LANES = 128

SEQ_TILE = 3 * BLOCK
INPROJ_BM = 1536
INPROJ_BN = 1536
MERGE_BM = 768
MERGE_BN = 1024
OUT_BM = 512
OUT_SPLIT = 2
MOE_BM = 256
DISPATCH_BM = 1536
COMBINE_BM = 512
COMBINE_ROW_BUFFERS = 3
EXPERT_X_BUFFERS = 3
ROW_TILES = D_MODEL // 2 // LANES
ROUTE_W = 128
ROUTE_ROWS = 8


def _cparams(*sem):
    return pltpu.CompilerParams(dimension_semantics=sem, vmem_limit_bytes=VMEM_LIMIT_BYTES)


def _layer_norm(x, g, b):
    mu = jnp.mean(x, axis=-1, keepdims=True)
    xc = x - mu
    var = jnp.mean(xc * xc, axis=-1, keepdims=True)
    return xc * lax.rsqrt(var + LN_EPS) * g + b


def _flat_valid_rows(row0, n_rows, batch, lp):
    r = row0 + lax.broadcasted_iota(jnp.int32, (n_rows, 1), 0)
    pad_row = jnp.zeros((n_rows, 1), jnp.bool_)
    for b in range(batch):
        pad_row = pad_row | ((r >= b * lp) & (r < b * lp + PAD))
    return jnp.logical_not(pad_row)


EMBED_BLOCKS = SEQ_TILE // BLOCK


def _embed_kernel(*refs):
    x_refs, (meta_ref, g_ref, b_ref, hf_ref, hb_ref) = refs[:EMBED_BLOCKS], refs[EMBED_BLOCKS:]
    t = pl.program_id(1)
    for j in range(EMBED_BLOCKS):
        src = x_refs[j][...]
        if j == 0:
            src = jnp.where(t == 0, meta_ref[...], src)
        y = _layer_norm(src, g_ref[...], b_ref[...])
        if j == 0:
            row = lax.broadcasted_iota(jnp.int32, (BLOCK, 1), 0)
            y = jnp.where((t > 0) | (row >= PAD), y, 0.0)
        hf_ref[pl.ds(j * BLOCK, BLOCK), :] = y
        hb_ref[pl.ds(j * BLOCK, BLOCK), :] = y.astype(BF16)


def _embed(x, meta, g, b):
    batch, seq, d = x.shape
    nblk = seq // BLOCK
    lp = (nblk + 1) * BLOCK
    nt = lp // SEQ_TILE
    m = batch * lp
    meta_tile = jnp.concatenate([jnp.zeros((PAD, d), F32), meta.astype(F32)], axis=0)
    row_spec = pl.BlockSpec((SEQ_TILE, d), lambda bi, t: (bi * nt + t, 0))
    vec_spec = pl.BlockSpec((1, d), lambda bi, t: (0, 0))
    x_spec = lambda j: pl.BlockSpec(
        (BLOCK, d), lambda bi, t: (bi * nblk + jnp.maximum(EMBED_BLOCKS * t + j - 1, 0), 0))
    x2 = x.reshape(batch * seq, d)
    return pl.pallas_call(
        _embed_kernel,
        grid=(batch, nt),
        in_specs=[x_spec(j) for j in range(EMBED_BLOCKS)]
        + [pl.BlockSpec((BLOCK, d), lambda bi, t: (0, 0)), vec_spec, vec_spec],
        out_specs=[row_spec, row_spec],
        out_shape=[jax.ShapeDtypeStruct((m, d), F32), jax.ShapeDtypeStruct((m, d), BF16)],
        compiler_params=_cparams("parallel", "arbitrary"),
        name="embed_ln",
    )(*([x2] * EMBED_BLOCKS), meta_tile, g.reshape(1, d), b.reshape(1, d))


def _sigmoid(x):
    return 0.5 * jnp.tanh(0.5 * x) + 0.5


def _inproj_kernel(x_ref, w_ref, o_ref, *, gates):
    acc = jnp.dot(x_ref[...], w_ref[...], preferred_element_type=F32)
    o_ref[...] = (_sigmoid(acc) if gates else acc).astype(o_ref.dtype)


def _inproj(hb, w_in_bf16, layer, gates):
    m, d = hb.shape
    bm, bn = INPROJ_BM, INPROJ_BN
    col0, width = (OFF_GATE, IN_COLS - OFF_GATE) if gates else (0, OFF_GATE)
    assert m % bm == 0 and width % bn == 0 and col0 % bn == 0
    return pl.pallas_call(
        functools.partial(_inproj_kernel, gates=gates),
        grid=(m // bm, width // bn),
        in_specs=[
            pl.BlockSpec((bm, d), lambda i, j: (i, 0)),
            pl.BlockSpec((None, d, bn), lambda i, j: (layer, 0, col0 // bn + j)),
        ],
        out_specs=pl.BlockSpec((bm, bn), lambda i, j: (i, j)),
        out_shape=jax.ShapeDtypeStruct((m, width), BF16),
        compiler_params=_cparams("parallel", "arbitrary"),
        name="in_proj_gates" if gates else "in_proj",
    )(hb, w_in_bf16)


POOL_HALO = 2 * max(POOL_WINDOWS)
assert POOL_WINDOWS == tuple(2 ** (g + 1) for g in range(len(POOL_WINDOWS)))


def _pool_kernel(u_ref, w_ref, scale_ref, o_ref, ext_ref, lvl_ref):
    t = pl.program_id(1)
    tile = SEQ_TILE
    halo = POOL_HALO

    @pl.when(t == 0)
    def _():
        ext_ref[pl.ds(0, halo), :] = jnp.zeros((halo, POOL_WIDTH), F32)

    @pl.when(t > 0)
    def _():
        ext_ref[pl.ds(0, halo), :] = ext_ref[pl.ds(tile, halo), :]

    ext_ref[pl.ds(halo, tile), :] = u_ref[...].astype(F32)

    pos = t * tile + lax.broadcasted_iota(jnp.int32, (tile, 1), 0) - PAD
    src, first = ext_ref, 0
    for gi, w in enumerate(POOL_WINDOWS):
        lane0 = gi * POOL_GROUP
        lanes = pl.ds(lane0, POOL_WIDTH - lane0)
        new_first = -(-(first + w // 2) // SUBLANES) * SUBLANES
        n_rows = tile + halo - new_first
        level = src[pl.ds(new_first, n_rows), lanes] + src[pl.ds(new_first - w // 2, n_rows), lanes]
        win = level[halo - new_first:, :POOL_GROUP]
        cols = pl.ds(lane0, POOL_GROUP)
        u = ext_ref[pl.ds(halo, tile), cols]
        cnt = jnp.clip(pos + 1, 1, w).astype(F32)
        delta = win / cnt - u
        mixed = jnp.dot(delta.astype(BF16), w_ref[gi], preferred_element_type=F32)
        o_ref[:, cols] = (mixed * scale_ref[:, cols]).astype(o_ref.dtype)
        if gi + 1 < len(POOL_WINDOWS):
            lvl_ref[gi, pl.ds(new_first, n_rows), lanes] = level
            src, first = lvl_ref.at[gi], new_first


def _pool(cols, pool_w_bf16, pool_scale, batch, lp):
    m = cols.shape[0]
    nt = lp // SEQ_TILE
    maxw = POOL_HALO
    return pl.pallas_call(
        _pool_kernel,
        grid=(batch, nt),
        in_specs=[
            pl.BlockSpec((SEQ_TILE, POOL_WIDTH), lambda b, t: (b * nt + t, OFF_POOL // POOL_WIDTH)),
            pl.BlockSpec((len(POOL_WINDOWS), POOL_GROUP, POOL_GROUP), lambda b, t: (0, 0, 0)),
            pl.BlockSpec((1, POOL_WIDTH), lambda b, t: (0, 0)),
        ],
        out_specs=pl.BlockSpec((SEQ_TILE, POOL_WIDTH), lambda b, t: (b * nt + t, 0)),
        out_shape=jax.ShapeDtypeStruct((m, POOL_WIDTH), BF16),
        scratch_shapes=[pltpu.VMEM((SEQ_TILE + maxw, POOL_WIDTH), F32),
                        pltpu.VMEM((len(POOL_WINDOWS) - 1, SEQ_TILE + maxw, POOL_WIDTH), F32)],
        compiler_params=_cparams("parallel", "arbitrary"),
        name="pool_mixer",
    )(cols, pool_w_bf16, pool_scale.reshape(1, POOL_WIDTH))


def _attn_bias():
    kj = jnp.arange(2 * BLOCK, dtype=jnp.int32)[:, None]
    qi = jnp.arange(BLOCK, dtype=jnp.int32)[None, :]
    dist = BLOCK + qi - kj
    in_window = (dist >= 0) & (dist < WINDOW)
    slopes = 2.0 ** (-8.0 * jnp.arange(1, N_HEADS + 1, dtype=F32) / N_HEADS)
    alibi = -slopes[:, None, None] * dist.astype(F32)[None]
    bias = jnp.where(in_window[None], alibi, NEG).reshape(N_KV_HEADS, Q_PER_KV, 2 * BLOCK, BLOCK)
    return bias.transpose(0, 2, 1, 3).reshape(N_KV_HEADS, 2 * BLOCK, Q_PER_KV * BLOCK)


def _attn_kernel(q_ref, kp_ref, kc_ref, vp_ref, vc_ref, bias_ref, sink_ref, o_ref):
    n = pl.program_id(1)
    q = q_ref[...] * (HEAD_DIM ** -0.5)

    def heads(early_keys):
        for kh in range(N_KV_HEADS):
            hs = pl.ds(kh * HEAD_DIM, HEAD_DIM)
            k2 = jnp.concatenate([kp_ref[:, hs], kc_ref[:, hs]], axis=0)
            v2 = jnp.concatenate([vp_ref[:, hs], vc_ref[:, hs]], axis=0)
            first = kh * Q_PER_KV
            qg = jnp.concatenate([q[:, (first + g) * HEAD_DIM:(first + g + 1) * HEAD_DIM]
                                  for g in range(Q_PER_KV)], axis=0)
            s = lax.dot_general(k2, qg, (((1,), (1,)), ((), ())), preferred_element_type=F32)
            s = s + bias_ref[kh]
            if early_keys is not None:
                s = s + early_keys
            sk = sink_ref[pl.ds(kh, 1), :]
            mx = jnp.maximum(jnp.max(s, axis=0, keepdims=True), sk)
            p = jnp.exp(s - mx)
            den = jnp.sum(p, axis=0, keepdims=True) + jnp.exp(sk - mx)
            pn = (p * (1.0 / den)).astype(BF16)
            o = lax.dot_general(pn, v2, (((0,), (0,)), ((), ())), preferred_element_type=F32)
            for g in range(Q_PER_KV):
                o_ref[:, pl.ds((first + g) * HEAD_DIM, HEAD_DIM)] = o[g * BLOCK:(g + 1) * BLOCK].astype(o_ref.dtype)

    @pl.when(n < 2)
    def _():
        k_pos = (n - 1) * BLOCK + lax.broadcasted_iota(jnp.int32, (2 * BLOCK, 1), 0)
        heads(jnp.where(k_pos < PAD, NEG, 0.0))

    @pl.when(n >= 2)
    def _():
        heads(None)


def _attention(cols, sink, batch, lp):
    m = cols.shape[0]
    nb = lp // BLOCK
    cur = lambda cb: (lambda b, n: (b * nb + n, cb))
    prev = lambda cb: (lambda b, n: (b * nb + jnp.maximum(n - 1, 0), cb))
    return pl.pallas_call(
        _attn_kernel,
        grid=(batch, nb),
        in_specs=[
            pl.BlockSpec((BLOCK, ATT_W), cur(OFF_Q // ATT_W)),
            pl.BlockSpec((BLOCK, KV_W), prev(OFF_K // KV_W)),
            pl.BlockSpec((BLOCK, KV_W), cur(OFF_K // KV_W)),
            pl.BlockSpec((BLOCK, KV_W), prev(OFF_V // KV_W)),
            pl.BlockSpec((BLOCK, KV_W), cur(OFF_V // KV_W)),
            pl.BlockSpec((N_KV_HEADS, 2 * BLOCK, Q_PER_KV * BLOCK), lambda b, n: (0, 0, 0)),
            pl.BlockSpec((N_KV_HEADS, Q_PER_KV * BLOCK), lambda b, n: (0, 0)),
        ],
        out_specs=pl.BlockSpec((BLOCK, ATT_W), lambda b, n: (b * nb + n, 0)),
        out_shape=jax.ShapeDtypeStruct((m, ATT_W), BF16),
        compiler_params=_cparams("parallel", "arbitrary"),
        name="swa_attention",
    )(cols, cols, cols, cols, cols, _attn_bias(),
      jnp.repeat(sink.astype(F32).reshape(N_KV_HEADS, Q_PER_KV), BLOCK, axis=1))


LRU_HALF = LRU_WIDTH // 2
LRU_HALO = 8
LRU_SCAN_UNROLL = 6


LOG2_E = 1.4426950408889634
GELU_C = 0.7978845608028654


def _gelu_tanh(x):
    half = 0.5 * x
    return half + half * jnp.tanh(x * (GELU_C + (GELU_C * 0.044715) * (x * x)))


def _lru_kernel(*refs):
    nh = LRU_WIDTH // LRU_HALF
    x_refs, y_refs = refs[:nh], refs[nh:2 * nh]
    (cw_ref, cb_ref, wa_ref, ba_ref, wx_ref, bx_ref, lam_ref, o_ref,
     ext_ref, a_ref, b_ref, carry_ref) = refs[2 * nh:]
    t = pl.program_id(1)
    tile = SEQ_TILE
    width = LRU_HALF
    pos = t * tile + lax.broadcasted_iota(jnp.int32, (tile, 1), 0)
    row = lax.broadcasted_iota(jnp.int32, (8, width), 0)

    for c in range(nh):
        lanes = pl.ds(c * width, width)
        ext, a_s, b_s, carry = ext_ref.at[c], a_ref.at[c], b_ref.at[c], carry_ref.at[c]

        @pl.when(t == 0)
        def _():
            ext[pl.ds(0, LRU_HALO), :] = jnp.zeros((LRU_HALO, width), F32)
            carry[...] = jnp.zeros((1, width), F32)

        @pl.when(t > 0)
        def _():
            ext[pl.ds(0, LRU_HALO), :] = ext[pl.ds(tile, LRU_HALO), :]

        ext[pl.ds(LRU_HALO, tile), :] = x_refs[c][...].astype(F32)

        xc = cb_ref[:, lanes] + cw_ref[pl.ds(CONV_WIDTH - 1, 1), lanes] * ext[pl.ds(LRU_HALO, tile), :]
        for j in range(CONV_WIDTH - 1):
            shift = CONV_WIDTH - 1 - j
            xc = xc + cw_ref[pl.ds(j, 1), lanes] * ext[pl.ds(LRU_HALO - shift, tile), :]

        xcb = xc.astype(BF16)
        ga_parts, gx_parts = [], []
        for blk in range(width // LRU_BLOCK):
            xb = xcb[:, blk * LRU_BLOCK:(blk + 1) * LRU_BLOCK]
            w_idx = c * (width // LRU_BLOCK) + blk
            ga_parts.append(jnp.dot(xb, wa_ref[w_idx], preferred_element_type=F32))
            gx_parts.append(jnp.dot(xb, wx_ref[w_idx], preferred_element_type=F32))
        gate_a = _sigmoid(jnp.concatenate(ga_parts, axis=1) + ba_ref[:, lanes])
        gate_x = _sigmoid(jnp.concatenate(gx_parts, axis=1) + bx_ref[:, lanes])

        neg_lam = -lam_ref[:, lanes]
        softplus = jnp.maximum(neg_lam, 0.0) + jnp.log1p(jnp.exp(-jnp.abs(neg_lam)))
        a = jnp.exp2(gate_a * ((-LRU_C * LOG2_E) * softplus))
        b_in = jnp.sqrt(1.0 - a * a) * gate_x * xc
        b_in = jnp.where(pos >= PAD, b_in, 0.0)
        a_s[...] = a
        b_s[...] = b_in

        def group(r, h_prev, a_s=a_s, b_s=b_s):
            rows = pl.ds(pl.multiple_of(r * 8, 8), 8)
            av = a_s[rows, :]
            bv = b_s[rows, :]
            for k in (1, 2, 4):
                a_sh = jnp.where(row >= k, pltpu.roll(av, k, 0), 1.0)
                b_sh = jnp.where(row >= k, pltpu.roll(bv, k, 0), 0.0)
                bv = av * b_sh + bv
                av = av * a_sh
            hv = av * h_prev + bv
            b_s[rows, :] = hv
            return hv[7:8, :]

        carry[...] = lax.fori_loop(0, tile // 8, group, carry[...], unroll=LRU_SCAN_UNROLL)
        o_ref[:, lanes] = (b_s[...] * _gelu_tanh(y_refs[c][...].astype(F32))).astype(o_ref.dtype)


def _lru(cols, conv_w, conv_b, wa_bf16, ba, wx_bf16, bx, lam, batch, lp):
    m = cols.shape[0]
    nt = lp // SEQ_TILE
    nh = LRU_WIDTH // LRU_HALF
    vec = lambda v: v.reshape(1, LRU_WIDTH).astype(F32)
    vec_spec = pl.BlockSpec((1, LRU_WIDTH), lambda b, t: (0, 0))
    w_spec = pl.BlockSpec((LRU_BLOCKS, LRU_BLOCK, LRU_BLOCK), lambda b, t: (0, 0, 0))
    half_spec = lambda off, c: pl.BlockSpec((SEQ_TILE, LRU_HALF), lambda b, t: (b * nt + t, off // LRU_HALF + c))
    return pl.pallas_call(
        _lru_kernel,
        grid=(batch, nt),
        in_specs=[half_spec(OFF_LX, c) for c in range(nh)] + [half_spec(OFF_LY, c) for c in range(nh)]
        + [pl.BlockSpec((CONV_WIDTH, LRU_WIDTH), lambda b, t: (0, 0)),
           vec_spec, w_spec, vec_spec, w_spec, vec_spec, vec_spec],
        out_specs=pl.BlockSpec((SEQ_TILE, LRU_WIDTH), lambda b, t: (b * nt + t, 0)),
        out_shape=jax.ShapeDtypeStruct((m, LRU_WIDTH), BF16),
        scratch_shapes=[
            pltpu.VMEM((nh, SEQ_TILE + LRU_HALO, LRU_HALF), F32),
            pltpu.VMEM((nh, SEQ_TILE, LRU_HALF), F32),
            pltpu.VMEM((nh, SEQ_TILE, LRU_HALF), F32),
            pltpu.VMEM((nh, 1, LRU_HALF), F32),
        ],
        compiler_params=_cparams("parallel", "arbitrary"),
        name="rglru",
    )(*([cols] * (2 * nh)), conv_w.astype(F32), vec(conv_b), wa_bf16, vec(ba), wx_bf16, vec(bx), vec(lam))


def _merge_kernel(p_ref, a_ref, r_ref, gp_ref, ga_ref, gr_ref, wp_ref, wa_ref, wr_ref, o_ref):
    acc = gp_ref[...].astype(F32) * jnp.dot(p_ref[...], wp_ref[...], preferred_element_type=F32)
    acc += ga_ref[...].astype(F32) * jnp.dot(a_ref[...], wa_ref[...], preferred_element_type=F32)
    acc += gr_ref[...].astype(F32) * jnp.dot(r_ref[...], wr_ref[...], preferred_element_type=F32)
    o_ref[...] = acc.astype(o_ref.dtype)


def _merge(pool_o, attn_o, lru_o, gates, wp, wa, wr, layer):
    m = pool_o.shape[0]
    bm, bn = MERGE_BM, MERGE_BN
    assert m % bm == 0 and D_MODEL % bn == 0
    x_spec = pl.BlockSpec((bm, POOL_WIDTH), lambda i, j: (i, 0))
    gate_spec = lambda k: pl.BlockSpec((bm, bn), lambda i, j: (i, k * D_MODEL // bn + j))
    w_spec = pl.BlockSpec((None, POOL_WIDTH, bn), lambda i, j: (layer, 0, j))
    return pl.pallas_call(
        _merge_kernel,
        grid=(m // bm, D_MODEL // bn),
        in_specs=[x_spec, x_spec, x_spec, gate_spec(0), gate_spec(1), gate_spec(2), w_spec, w_spec, w_spec],
        out_specs=pl.BlockSpec((bm, bn), lambda i, j: (i, j)),
        out_shape=jax.ShapeDtypeStruct((m, D_MODEL), BF16),
        compiler_params=_cparams("parallel", "arbitrary"),
        name="gated_merge",
    )(pool_o, attn_o, lru_o, gates, gates, gates, wp, wa, wr)


def _pack_bf16_pairs(yb):
    c = yb.shape[1] // 2
    lo = lax.bitcast_convert_type(yb[:, :c].astype(F32), jnp.uint32)
    hi = lax.bitcast_convert_type(yb[:, c:].astype(F32), jnp.uint32)
    return (hi & jnp.uint32(0xFFFF0000)) | (lo >> 16)


def _unpack_bf16_pairs(words, dtype=BF16):
    lo = lax.bitcast_convert_type(words << 16, F32)
    hi = lax.bitcast_convert_type(words & jnp.uint32(0xFFFF0000), F32)
    return jnp.concatenate([lo, hi], axis=1).astype(dtype)


def _route_tile(logits, valid, base):
    bm = logits.shape[0]
    lane = lax.broadcasted_iota(jnp.int32, (bm, ROUTE_W), 1)
    lane_f = lane.astype(F32)
    ninf = -jnp.inf
    big = float(ROUTE_W)

    gl = jnp.where(lane < N_GROUPS, logits, ninf)
    gmax = jnp.max(gl, axis=-1, keepdims=True)
    g = jnp.min(jnp.where(gl == gmax, lane_f, big), axis=-1, keepdims=True)
    p_g = 1.0 / jnp.sum(jnp.exp(gl - gmax), axis=-1, keepdims=True)

    first = N_GROUPS + g * EXPERTS_PER_GROUP
    sl = jnp.where((lane_f >= first) & (lane_f < first + EXPERTS_PER_GROUP), logits, ninf)
    m1 = jnp.max(sl, axis=-1, keepdims=True)
    i1 = jnp.min(jnp.where(sl == m1, lane_f, big), axis=-1, keepdims=True)
    ssum = jnp.sum(jnp.exp(sl - m1), axis=-1, keepdims=True)
    sl2 = jnp.where(lane_f == i1, ninf, sl)
    m2 = jnp.max(sl2, axis=-1, keepdims=True)
    i2 = jnp.min(jnp.where(sl2 == m2, lane_f, big), axis=-1, keepdims=True)
    p1 = 1.0 / ssum
    p2 = jnp.exp(m2 - m1) / ssum
    w1 = p_g * p1 / (p1 + p2)
    w2 = p_g * p2 / (p1 + p2)
    e1 = i1 - N_GROUPS
    e2 = i2 - N_GROUPS

    oh1 = (lane_f == e1) & valid
    oh2 = (lane_f == e2) & valid
    both = (oh1 | oh2).astype(F32)
    earlier = (lax.broadcasted_iota(jnp.int32, (bm, bm), 0) > lax.broadcasted_iota(jnp.int32, (bm, bm), 1))
    prefix = jnp.dot(earlier.astype(BF16), both.astype(BF16), preferred_element_type=F32) + base
    r1 = jnp.sum(jnp.where(oh1, prefix, 0.0), axis=-1, keepdims=True)
    r2 = jnp.sum(jnp.where(oh2, prefix, 0.0), axis=-1, keepdims=True)

    ri = jnp.where(lane == 0, e1, jnp.where(lane == 1, e2, jnp.where(lane == 2, r1, jnp.where(lane == 3, r2, 0.0))))
    rw = jnp.where(lane == 0, w1, jnp.where(lane == 1, w2, 0.0))
    ri_rows = jnp.transpose(ri)[0:ROUTE_ROWS, :].astype(jnp.int32)
    return ri_rows, rw, jnp.sum(both, axis=0, keepdims=True)


def _outproj_kernel(x_ref, w_ref, h_ref, g_ref, b_ref, rw_ref, rb_ref, hf_ref, hp_ref, ri_ref, rwt_ref, cnt_ref,
                    *, batch, lp):
    i = pl.program_id(0)
    bm = x_ref.shape[0]
    sub = bm // OUT_SPLIT

    @pl.when(i == 0)
    def _():
        cnt_ref[...] = jnp.zeros(cnt_ref.shape, F32)

    for s in range(OUT_SPLIT):
        rows = pl.ds(s * sub, sub)
        t = jnp.dot(x_ref[rows, :], w_ref[...], preferred_element_type=F32)
        y = _layer_norm(ALPHA * h_ref[rows, :] + t, g_ref[...], b_ref[...])
        valid = _flat_valid_rows(i * bm + s * sub, sub, batch, lp)
        y = jnp.where(valid, y, 0.0)
        yb = y.astype(BF16)
        hf_ref[rows, :] = y
        packed = _pack_bf16_pairs(yb)
        for c in range(ROW_TILES):
            hp_ref[pl.ds(s * sub * ROW_TILES + c, sub, stride=ROW_TILES), :] = packed[:, c * LANES:(c + 1) * LANES]
        logits = jnp.dot(yb, rw_ref[...], preferred_element_type=F32) + rb_ref[...]
        ri, rw, tile_cnt = _route_tile(logits, valid, cnt_ref[...])
        ri_ref[:, rows] = ri
        rwt_ref[rows, :] = rw
        cnt_ref[...] += tile_cnt


def _outproj_ln(merged, w_out_bf16, h, g, b, route_w, route_b, layer, batch, lp):
    m, d = h.shape
    bm = OUT_BM
    assert m % bm == 0
    row_spec = pl.BlockSpec((bm, d), lambda i: (i, 0))
    vec_spec = pl.BlockSpec((1, d), lambda i: (0, 0))
    route_spec = pl.BlockSpec((bm, ROUTE_W), lambda i: (i, 0))
    return pl.pallas_call(
        functools.partial(_outproj_kernel, batch=batch, lp=lp),
        grid=(m // bm,),
        in_specs=[
            row_spec,
            pl.BlockSpec((None, d, d), lambda i: (layer, 0, 0)),
            row_spec, vec_spec, vec_spec,
            pl.BlockSpec((d, ROUTE_W), lambda i: (0, 0)),
            pl.BlockSpec((1, ROUTE_W), lambda i: (0, 0)),
        ],
        out_specs=[row_spec, pl.BlockSpec((bm * ROW_TILES, LANES), lambda i: (i, 0)),
                   pl.BlockSpec((ROUTE_ROWS, bm), lambda i: (0, i)), route_spec,
                   pl.BlockSpec((1, ROUTE_W), lambda i: (0, 0))],
        out_shape=[jax.ShapeDtypeStruct((m, d), F32), jax.ShapeDtypeStruct((m * ROW_TILES, LANES), jnp.uint32),
                   jax.ShapeDtypeStruct((ROUTE_ROWS, m), jnp.int32), jax.ShapeDtypeStruct((m, ROUTE_W), F32),
                   jax.ShapeDtypeStruct((1, ROUTE_W), F32)],
        compiler_params=_cparams("arbitrary"),
        name="out_proj_ln",
    )(merged, w_out_bf16, h, g.reshape(1, d), b.reshape(1, d), route_w, route_b)


def _dispatch_kernel(start_ref, nchunk_ref, dest_hbm, hp_ref, xs_ref, idx_ref, zero_ref, idx_sem, row_sem, zero_sem,
                     *, n_expert_rows):
    i = pl.program_id(0)
    bm = hp_ref.shape[0] // ROW_TILES
    groups = bm // SUBLANES
    n_idx = TOP_K * bm

    def token_rows(first_token, n_tokens):
        return pl.ds(pl.multiple_of(first_token * ROW_TILES, ROW_TILES), n_tokens * ROW_TILES)

    @pl.when(i == 0)
    def _():
        zero_ref[...] = jnp.zeros(zero_ref.shape, zero_ref.dtype)

        def last_chunk(e):
            first = pl.multiple_of(start_ref[e] + (nchunk_ref[e] - 1) * MOE_BM, MOE_BM)
            return pltpu.make_async_copy(zero_ref, xs_ref.at[token_rows(first, MOE_BM)], zero_sem)

        for e in range(N_EXPERTS):
            @pl.when(nchunk_ref[e] > 0)
            def _():
                last_chunk(e).start()

        for e in range(N_EXPERTS):
            @pl.when(nchunk_ref[e] > 0)
            def _():
                last_chunk(e).wait()

        used = start_ref[N_EXPERTS - 1] + nchunk_ref[N_EXPERTS - 1] * MOE_BM

        def tail_chunk(j):
            return pltpu.make_async_copy(
                zero_ref, xs_ref.at[token_rows(pl.multiple_of(used + j * MOE_BM, MOE_BM), MOE_BM)], zero_sem)

        n_tail = (n_expert_rows - used) // MOE_BM
        lax.fori_loop(0, n_tail, lambda j, c: (tail_chunk(j).start(), c)[1], 0)
        lax.fori_loop(0, n_tail, lambda j, c: (tail_chunk(j).wait(), c)[1], 0)

    idx_copy = pltpu.make_async_copy(dest_hbm.at[pl.ds(pl.multiple_of(i * n_idx, n_idx), n_idx)], idx_ref, idx_sem)
    idx_copy.start()
    idx_copy.wait()

    def issue(g, carry):
        for s in range(SUBLANES):
            for k in range(TOP_K):
                slot = idx_ref[g * SUBLANES + (k * bm + s)]
                pltpu.make_async_copy(hp_ref.at[token_rows(g * SUBLANES + s, 1)], xs_ref.at[token_rows(slot, 1)],
                                      row_sem).start(priority=k % 2)
        return carry

    lax.fori_loop(0, groups, issue, 0)
    for k in range(TOP_K):
        pltpu.make_async_copy(xs_ref.at[token_rows(0, bm)], xs_ref.at[token_rows(0, bm)], row_sem).wait()


def _dispatch(hp, dest, chunk_start, n_chunks, n_expert_rows, n_slots):
    m = hp.shape[0] // ROW_TILES
    bm = DISPATCH_BM
    assert m % bm == 0 and bm % SUBLANES == 0
    grid_spec = pltpu.PrefetchScalarGridSpec(
        num_scalar_prefetch=2,
        grid=(m // bm,),
        in_specs=[
            pl.BlockSpec(memory_space=pl.ANY),
            pl.BlockSpec((bm * ROW_TILES, LANES), lambda i, st, nc: (i, 0)),
        ],
        out_specs=pl.BlockSpec(memory_space=pl.ANY),
        scratch_shapes=[pltpu.SMEM((TOP_K * bm,), jnp.int32), pltpu.VMEM((MOE_BM * ROW_TILES, LANES), jnp.uint32),
                        pltpu.SemaphoreType.DMA(()), pltpu.SemaphoreType.DMA(()), pltpu.SemaphoreType.DMA(())],
    )
    return pl.pallas_call(
        functools.partial(_dispatch_kernel, n_expert_rows=n_expert_rows),
        grid_spec=grid_spec,
        out_shape=jax.ShapeDtypeStruct((n_slots * ROW_TILES, LANES), jnp.uint32),
        compiler_params=_cparams("arbitrary"),
        name="moe_dispatch",
    )(chunk_start, n_chunks, dest, hp)


def _expert_kernel(start_ref, nchunk_ref, xs_hbm, wg_hbm, wu_hbm, wd_hbm, yb_hbm, wg_f, wu_f, wd_f, wg_s, wu_s, wd_s,
                   x_buf, o_buf, w_sem, in_sem, out_sem, *, layer):
    e = pl.program_id(0)
    n = nchunk_ref[e]
    bm = MOE_BM
    first = start_ref[e] // bm
    total = start_ref[N_EXPERTS - 1] // bm + nchunk_ref[N_EXPERTS - 1]
    wslot = e % 2

    def w_copies(expert, slot):
        return [pltpu.make_async_copy(src.at[layer, expert], dst.at[slot], w_sem.at[slot])
                for src, dst in ((wg_hbm, wg_f), (wu_hbm, wu_f), (wd_hbm, wd_f))]

    def o_rows(g):
        return pl.ds(pl.multiple_of(g * (bm * ROW_TILES), bm * ROW_TILES), bm * ROW_TILES)

    def x_copy(g):
        slot = g % EXPERT_X_BUFFERS
        return pltpu.make_async_copy(xs_hbm.at[o_rows(g)], x_buf.at[slot], in_sem.at[slot])

    def o_copy(g):
        slot = g % 2
        return pltpu.make_async_copy(o_buf.at[slot], yb_hbm.at[o_rows(g)], out_sem.at[slot])

    @pl.when(e == 0)
    def _():
        for ahead in range(EXPERT_X_BUFFERS - 1):
            @pl.when(ahead < total)
            def _():
                x_copy(ahead).start()

        for c in w_copies(0, 0):
            c.start(priority=1)

    @pl.when(e + 1 < N_EXPERTS)
    def _():
        for c in w_copies(e + 1, 1 - wslot):
            c.start(priority=1)

    for c in w_copies(e, wslot):
        c.wait()

    @pl.when(n > 0)
    def _():
        wg_s[...] = wg_f[wslot].astype(BF16)
        wu_s[...] = wu_f[wslot].astype(BF16)
        wd_s[...] = wd_f[wslot].astype(BF16)

        def chunk(j, carry):
            g = first + j
            x_copy(g).wait()

            @pl.when(g + EXPERT_X_BUFFERS - 1 < total)
            def _():
                x_copy(g + EXPERT_X_BUFFERS - 1).start()

            @pl.when(g >= 2)
            def _():
                o_copy(g - 2).wait()

            x_slot = g % EXPERT_X_BUFFERS
            x = _unpack_bf16_pairs(jnp.concatenate(
                [x_buf[x_slot, pl.ds(c, bm, stride=ROW_TILES), :] for c in range(ROW_TILES)], axis=1))
            gate = jnp.dot(x, wg_s[...], preferred_element_type=F32)
            up = jnp.dot(x, wu_s[...], preferred_element_type=F32)
            hdn = (gate * _sigmoid(gate)) * up
            y = jnp.dot(hdn.astype(BF16), wd_s[...], preferred_element_type=F32)
            packed = _pack_bf16_pairs(y.astype(BF16))
            for c in range(ROW_TILES):
                o_buf[g % 2, pl.ds(c, bm, stride=ROW_TILES), :] = packed[:, c * LANES:(c + 1) * LANES]
            o_copy(g).start()
            return carry

        lax.fori_loop(0, n, chunk, 0)

    @pl.when(e == N_EXPERTS - 1)
    def _():
        for back in (2, 1):
            @pl.when(total >= back)
            def _():
                o_copy(total - back).wait()

        o_buf[0] = jnp.zeros(o_buf.shape[1:], o_buf.dtype)

        def fill(g, carry):
            tail = pltpu.make_async_copy(o_buf.at[0], yb_hbm.at[o_rows(g)], out_sem.at[0])
            tail.start()
            tail.wait()
            return carry

        lax.fori_loop(total, yb_hbm.shape[0] // (bm * ROW_TILES), fill, 0)


def _experts(xs, n_rows, chunk_start, n_chunks, w_gate, w_up, w_down, layer):
    d = D_MODEL
    bm = MOE_BM
    any_spec = pl.BlockSpec(memory_space=pl.ANY)
    grid_spec = pltpu.PrefetchScalarGridSpec(
        num_scalar_prefetch=2,
        grid=(N_EXPERTS,),
        in_specs=[any_spec, any_spec, any_spec, any_spec],
        out_specs=any_spec,
        scratch_shapes=[
            pltpu.VMEM((2, d, D_EXPERT), F32),
            pltpu.VMEM((2, d, D_EXPERT), F32),
            pltpu.VMEM((2, D_EXPERT, d), F32),
            pltpu.VMEM((d, D_EXPERT), BF16),
            pltpu.VMEM((d, D_EXPERT), BF16),
            pltpu.VMEM((D_EXPERT, d), BF16),
            pltpu.VMEM((EXPERT_X_BUFFERS, bm * ROW_TILES, LANES), jnp.uint32),
            pltpu.VMEM((2, bm * ROW_TILES, LANES), jnp.uint32),
            pltpu.SemaphoreType.DMA((2,)),
            pltpu.SemaphoreType.DMA((EXPERT_X_BUFFERS,)),
            pltpu.SemaphoreType.DMA((2,)),
        ],
    )
    return pl.pallas_call(
        functools.partial(_expert_kernel, layer=layer),
        grid_spec=grid_spec,
        out_shape=jax.ShapeDtypeStruct((n_rows * ROW_TILES, LANES), jnp.uint32),
        compiler_params=_cparams("arbitrary"),
        name="expert_mlp",
    )(chunk_start, n_chunks, xs, w_gate, w_up, w_down)


def _combine_ln_kernel(dest_hbm, h_ref, rw_ref, g_ref, b_ref, yb_hbm, *rest, batch, lp, final):
    if final:
        out_hbm, idx_ref, rows_ref, y_buf, idx_sem, row_sem, out_sem = rest
    else:
        hf_ref, hb_ref, idx_ref, rows_ref, idx_sem, row_sem = rest
    i = pl.program_id(0)
    n_tiles = pl.num_programs(0)
    bm = h_ref.shape[0]
    n_idx = TOP_K * bm
    buf = i % COMBINE_ROW_BUFFERS

    def idx_copy(tile):
        half = pl.ds(pl.multiple_of((tile % 2) * n_idx, n_idx), n_idx)
        return pltpu.make_async_copy(
            dest_hbm.at[pl.ds(pl.multiple_of(tile * n_idx, n_idx), n_idx)], idx_ref.at[half], idx_sem.at[tile % 2])

    def start_gather(tile):
        into = tile % COMBINE_ROW_BUFFERS
        idx_copy(tile).wait()

        def issue(g, carry):
            base = (tile % 2) * n_idx + g * SUBLANES
            for s in range(SUBLANES):
                for k in range(TOP_K):
                    slot = idx_ref[base + (k * bm + s)]
                    src = yb_hbm.at[pl.ds(pl.multiple_of(slot * ROW_TILES, ROW_TILES), ROW_TILES)]
                    dst = rows_ref.at[into, k, pl.ds(pl.multiple_of((g * SUBLANES + s) * ROW_TILES, ROW_TILES),
                                                     ROW_TILES)]
                    pltpu.make_async_copy(src, dst, row_sem.at[into]).start(priority=k % 2)
            return carry

        lax.fori_loop(0, bm // SUBLANES, issue, 0)

        @pl.when(tile + 2 < n_tiles)
        def _():
            idx_copy(tile + 2).start()

    @pl.when(i == 0)
    def _():
        idx_copy(0).start()

        @pl.when(1 < n_tiles)
        def _():
            idx_copy(1).start()

        for ahead in range(COMBINE_ROW_BUFFERS - 1):
            @pl.when(ahead < n_tiles)
            def _():
                start_gather(ahead)

    @pl.when(i + COMBINE_ROW_BUFFERS - 1 < n_tiles)
    def _():
        start_gather(i + COMBINE_ROW_BUFFERS - 1)

    for k in range(TOP_K):
        whole = pl.ds(0, bm * ROW_TILES)
        pltpu.make_async_copy(yb_hbm.at[whole], yb_hbm.at[whole], row_sem.at[buf]).wait()

    def expert_rows(k):
        pieces = [rows_ref[buf, k, pl.ds(c, bm, stride=ROW_TILES), :] for c in range(ROW_TILES)]
        return _unpack_bf16_pairs(jnp.concatenate(pieces, axis=1), F32)

    ffn = rw_ref[:, 0:1] * expert_rows(0)
    for k in range(1, TOP_K):
        ffn = ffn + rw_ref[:, k:k + 1] * expert_rows(k)
    y = _layer_norm(ALPHA * h_ref[...] + ffn, g_ref[...], b_ref[...])
    if not final:
        y = jnp.where(_flat_valid_rows(i * bm, bm, batch, lp), y, 0.0)
        hf_ref[...] = y
        hb_ref[...] = y.astype(BF16)
        return

    nb = lp // BLOCK

    def for_each_out_block(step, fn):
        for j in range(bm // BLOCK):
            blk = step * (bm // BLOCK) + j
            seq_blk = blk % nb

            @pl.when(seq_blk >= 1)
            def _():
                dst = pl.multiple_of(((blk // nb) * (nb - 1) + seq_blk - 1) * BLOCK, BLOCK)
                fn(pltpu.make_async_copy(y_buf.at[pl.ds(j * BLOCK, BLOCK)], out_hbm.at[pl.ds(dst, BLOCK)], out_sem))

    @pl.when(i > 0)
    def _():
        for_each_out_block(i - 1, lambda c: c.wait())

    y_buf[...] = y
    for_each_out_block(i, lambda c: c.start(priority=1))

    @pl.when(i == n_tiles - 1)
    def _():
        for_each_out_block(i, lambda c: c.wait())


def _combine_ln(h, yb, dest, rw, g, b, batch, lp, final):
    m, d = h.shape
    bm = COMBINE_BM
    assert m % bm == 0 and bm % BLOCK == 0
    row_spec = pl.BlockSpec((bm, d), lambda i: (i, 0))
    vec_spec = pl.BlockSpec((1, d), lambda i: (0, 0))
    scratch = [pltpu.SMEM((2 * TOP_K * bm,), jnp.int32),
               pltpu.VMEM((COMBINE_ROW_BUFFERS, TOP_K, bm * ROW_TILES, LANES), jnp.uint32)]
    sems = [pltpu.SemaphoreType.DMA((2,)), pltpu.SemaphoreType.DMA((COMBINE_ROW_BUFFERS,))]
    if final:
        out_specs = pl.BlockSpec(memory_space=pl.ANY)
        out_shape = jax.ShapeDtypeStruct((batch * (lp - BLOCK), d), F32)
        scratch = scratch + [pltpu.VMEM((bm, d), F32)] + sems + [pltpu.SemaphoreType.DMA(())]
    else:
        out_specs = [row_spec, row_spec]
        out_shape = [jax.ShapeDtypeStruct((m, d), F32), jax.ShapeDtypeStruct((m, d), BF16)]
        scratch = scratch + sems
    return pl.pallas_call(
        functools.partial(_combine_ln_kernel, batch=batch, lp=lp, final=final),
        grid=(m // bm,),
        in_specs=[
            pl.BlockSpec(memory_space=pl.ANY),
            row_spec,
            pl.BlockSpec((bm, ROUTE_W), lambda i: (i, 0)),
            vec_spec, vec_spec,
            pl.BlockSpec(memory_space=pl.ANY),
        ],
        out_specs=out_specs,
        out_shape=out_shape,
        scratch_shapes=scratch,
        compiler_params=_cparams("arbitrary"),
        name="moe_combine_out" if final else "moe_combine_ln",
    )(dest, h, rw, g.reshape(1, d), b.reshape(1, d), yb)


def _slot_tables(ri, cnt, batch, lp):
    m = ri.shape[1]
    bm = MOE_BM
    eid = ri[0:TOP_K]
    rank = ri[TOP_K:2 * TOP_K]
    counts = cnt[0, :N_EXPERTS].astype(jnp.int32)
    padded = (counts + bm - 1) // bm * bm
    pad_end = jnp.cumsum(padded)
    pad_start = pad_end - padded
    n_real = batch * (lp - PAD) * TOP_K
    nblk = -(-(n_real + N_EXPERTS * (bm - 1)) // bm)
    cap = nblk * bm
    start = jnp.zeros_like(eid)
    for e in range(N_EXPERTS):
        start = jnp.where(eid == e, pad_start[e], start)
    row = jnp.arange(m, dtype=jnp.int32)
    pos = row % lp
    spare = cap + ((row // lp) * PAD + pos)[None, :] * TOP_K + jnp.arange(TOP_K, dtype=jnp.int32)[:, None]
    valid = (pos >= PAD)[None, :]
    dest = jnp.where(valid, start + rank, spare)
    dest_read = jnp.where(valid, dest, 0)
    n_slots = cap + batch * PAD * TOP_K

    def by_tile(idx, tile):
        return idx.reshape(TOP_K, m // tile, tile).transpose(1, 0, 2).reshape(m * TOP_K)

    return (by_tile(dest, DISPATCH_BM), by_tile(dest_read, COMBINE_BM), pad_start.astype(jnp.int32),
            (padded // bm).astype(jnp.int32), cap, n_slots)


def kernel(x, meta, ln_emb_g, ln_emb_b, w_in, pool_w, pool_scale, attn_sink, conv_w, conv_b, lru_wa, lru_ba,
           lru_wx, lru_bx, lru_lambda, proj_pool, proj_attn, proj_lru, w_out, ln1_g, ln1_b, router_grp_w,
           router_grp_b, router_exp_w, router_exp_b, exp_w_gate, exp_w_up, exp_w_down, ln2_g, ln2_b):
    batch, seq, d = x.shape
    lp = PAD + N_META + seq
    m = batch * lp

    hf, hb = _embed(x, meta, ln_emb_g, ln_emb_b)
    w_in_b = w_in.astype(BF16)
    pool_w_b = pool_w.astype(BF16)
    wa_b = lru_wa.astype(BF16)
    wx_b = lru_wx.astype(BF16)
    wp_b = proj_pool.astype(BF16)
    wat_b = proj_attn.astype(BF16)
    wl_b = proj_lru.astype(BF16)
    w_out_b = w_out.astype(BF16)
    route_pad = ROUTE_W - N_GROUPS - N_EXPERTS

    for l in range(DEPTH):
        cols = _inproj(hb, w_in_b, l, gates=False)
        gates = _inproj(hb, w_in_b, l, gates=True)
        pool_o = _pool(cols, pool_w_b[l], pool_scale[l], batch, lp)
        attn_o = _attention(cols, attn_sink[l], batch, lp)
        lru_o = _lru(cols, conv_w[l], conv_b[l], wa_b[l], lru_ba[l], wx_b[l], lru_bx[l], lru_lambda[l],
                     batch, lp)
        merged = _merge(pool_o, attn_o, lru_o, gates, wp_b, wat_b, wl_b, l)
        route_w = jnp.concatenate(
            [router_grp_w[l], router_exp_w[l], jnp.zeros((d, route_pad), F32)], axis=1).astype(BF16)
        route_b = jnp.concatenate(
            [router_grp_b[l], router_exp_b[l], jnp.zeros((route_pad,), F32)]).reshape(1, ROUTE_W)
        h1f, h1p, ri, rw, cnt = _outproj_ln(merged, w_out_b, hf, ln1_g[l], ln1_b[l], route_w, route_b, l,
                                            batch, lp)
        dest, dest_read, chunk_start, n_chunks, cap, n_slots = _slot_tables(ri, cnt, batch, lp)
        xs = _dispatch(h1p, dest, chunk_start, n_chunks, cap, n_slots)
        yb = _experts(xs, cap, chunk_start, n_chunks, exp_w_gate, exp_w_up, exp_w_down, l)
        if l + 1 < DEPTH:
            hf, hb = _combine_ln(h1f, yb, dest_read, rw, ln2_g[l], ln2_b[l], batch, lp, final=False)
        else:
            out = _combine_ln(h1f, yb, dest_read, rw, ln2_g[l], ln2_b[l], batch, lp, final=True)

    return out.reshape(batch, seq, d)
```

```python
import functools

import jax
import jax.numpy as jnp
from jax import lax
from jax.experimental import pallas as pl
from jax.experimental.pallas import tpu as pltpu

F32 = jnp.float32
BF16 = jnp.bfloat16

D_MODEL = 2048
DEPTH = 2
N_META = 16
POOL_WINDOWS = (2, 4, 8, 16)
POOL_WIDTH = D_MODEL // 2
POOL_GROUP = POOL_WIDTH // len(POOL_WINDOWS)
N_HEADS = 16
N_KV_HEADS = 4
HEAD_DIM = 64
Q_PER_KV = N_HEADS // N_KV_HEADS
WINDOW = 128
BLOCK = 128
NEG = -1e30
LRU_WIDTH = D_MODEL // 2
LRU_BLOCKS = 4
LRU_BLOCK = LRU_WIDTH // LRU_BLOCKS
CONV_WIDTH = 4
LRU_C = 8.0
N_GROUPS = 4
EXPERTS_PER_GROUP = 8
N_EXPERTS = N_GROUPS * EXPERTS_PER_GROUP
TOP_K = 2
D_EXPERT = D_MODEL // 4
LN_EPS = 1e-5
ALPHA = (2.0 * DEPTH) ** 0.25

PAD = BLOCK - N_META
ATT_W = N_HEADS * HEAD_DIM
KV_W = N_KV_HEADS * HEAD_DIM
OFF_POOL = 0
OFF_Q = OFF_POOL + POOL_WIDTH
OFF_K = OFF_Q + ATT_W
OFF_V = OFF_K + KV_W
OFF_LX = OFF_V + KV_W
OFF_LY = OFF_LX + LRU_WIDTH
OFF_GATE = OFF_LY + LRU_WIDTH
IN_COLS = OFF_GATE + 3 * D_MODEL

VMEM_LIMIT_BYTES = 56 * 1024 * 1024
SUBLANES = 8
LANES = 128

SEQ_TILE = 3 * BLOCK
INPROJ_BM = 1536
INPROJ_BN = 1536
MERGE_BM = 768
MERGE_BN = 1024
OUT_BM = 512
OUT_SPLIT = 2
MOE_BM = 256
DISPATCH_BM = 1536
COMBINE_BM = 512
COMBINE_ROW_BUFFERS = 3
EXPERT_X_BUFFERS = 3
ROW_TILES = D_MODEL // 2 // LANES
ROUTE_W = 128
ROUTE_ROWS = 8


def _cparams(*sem):
    return pltpu.CompilerParams(dimension_semantics=sem, vmem_limit_bytes=VMEM_LIMIT_BYTES)


def _layer_norm(x, g, b):
    mu = jnp.mean(x, axis=-1, keepdims=True)
    xc = x - mu
    var = jnp.mean(xc * xc, axis=-1, keepdims=True)
    return xc * lax.rsqrt(var + LN_EPS) * g + b


def _flat_valid_rows(row0, n_rows, batch, lp):
    r = row0 + lax.broadcasted_iota(jnp.int32, (n_rows, 1), 0)
    pad_row = jnp.zeros((n_rows, 1), jnp.bool_)
    for b in range(batch):
        pad_row = pad_row | ((r >= b * lp) & (r < b * lp + PAD))
    return jnp.logical_not(pad_row)


EMBED_BLOCKS = SEQ_TILE // BLOCK


def _embed_kernel(*refs):
    x_refs, (meta_ref, g_ref, b_ref, hf_ref, hb_ref) = refs[:EMBED_BLOCKS], refs[EMBED_BLOCKS:]
    t = pl.program_id(1)
    for j in range(EMBED_BLOCKS):
        src = x_refs[j][...]
        if j == 0:
            src = jnp.where(t == 0, meta_ref[...], src)
        y = _layer_norm(src, g_ref[...], b_ref[...])
        if j == 0:
            row = lax.broadcasted_iota(jnp.int32, (BLOCK, 1), 0)
            y = jnp.where((t > 0) | (row >= PAD), y, 0.0)
        hf_ref[pl.ds(j * BLOCK, BLOCK), :] = y
        hb_ref[pl.ds(j * BLOCK, BLOCK), :] = y.astype(BF16)


def _embed(x, meta, g, b):
    batch, seq, d = x.shape
    nblk = seq // BLOCK
    lp = (nblk + 1) * BLOCK
    nt = lp // SEQ_TILE
    m = batch * lp
    meta_tile = jnp.concatenate([jnp.zeros((PAD, d), F32), meta.astype(F32)], axis=0)
    row_spec = pl.BlockSpec((SEQ_TILE, d), lambda bi, t: (bi * nt + t, 0))
    vec_spec = pl.BlockSpec((1, d), lambda bi, t: (0, 0))
    x_spec = lambda j: pl.BlockSpec(
        (BLOCK, d), lambda bi, t: (bi * nblk + jnp.maximum(EMBED_BLOCKS * t + j - 1, 0), 0))
    x2 = x.reshape(batch * seq, d)
    return pl.pallas_call(
        _embed_kernel,
        grid=(batch, nt),
        in_specs=[x_spec(j) for j in range(EMBED_BLOCKS)]
        + [pl.BlockSpec((BLOCK, d), lambda bi, t: (0, 0)), vec_spec, vec_spec],
        out_specs=[row_spec, row_spec],
        out_shape=[jax.ShapeDtypeStruct((m, d), F32), jax.ShapeDtypeStruct((m, d), BF16)],
        compiler_params=_cparams("parallel", "arbitrary"),
        name="embed_ln",
    )(*([x2] * EMBED_BLOCKS), meta_tile, g.reshape(1, d), b.reshape(1, d))


def _sigmoid(x):
    return 0.5 * jnp.tanh(0.5 * x) + 0.5


def _inproj_kernel(x_ref, w_ref, o_ref, *, gates):
    acc = jnp.dot(x_ref[...], w_ref[...], preferred_element_type=F32)
    o_ref[...] = (_sigmoid(acc) if gates else acc).astype(o_ref.dtype)


def _inproj(hb, w_in_bf16, layer, gates):
    m, d = hb.shape
    bm, bn = INPROJ_BM, INPROJ_BN
    col0, width = (OFF_GATE, IN_COLS - OFF_GATE) if gates else (0, OFF_GATE)
    assert m % bm == 0 and width % bn == 0 and col0 % bn == 0
    return pl.pallas_call(
        functools.partial(_inproj_kernel, gates=gates),
        grid=(m // bm, width // bn),
        in_specs=[
            pl.BlockSpec((bm, d), lambda i, j: (i, 0)),
            pl.BlockSpec((None, d, bn), lambda i, j: (layer, 0, col0 // bn + j)),
        ],
        out_specs=pl.BlockSpec((bm, bn), lambda i, j: (i, j)),
        out_shape=jax.ShapeDtypeStruct((m, width), BF16),
        compiler_params=_cparams("parallel", "arbitrary"),
        name="in_proj_gates" if gates else "in_proj",
    )(hb, w_in_bf16)


POOL_HALO = 2 * max(POOL_WINDOWS)
assert POOL_WINDOWS == tuple(2 ** (g + 1) for g in range(len(POOL_WINDOWS)))


def _pool_kernel(u_ref, w_ref, scale_ref, o_ref, ext_ref, lvl_ref):
    t = pl.program_id(1)
    tile = SEQ_TILE
    halo = POOL_HALO

    @pl.when(t == 0)
    def _():
        ext_ref[pl.ds(0, halo), :] = jnp.zeros((halo, POOL_WIDTH), F32)

    @pl.when(t > 0)
    def _():
        ext_ref[pl.ds(0, halo), :] = ext_ref[pl.ds(tile, halo), :]

    ext_ref[pl.ds(halo, tile), :] = u_ref[...].astype(F32)

    pos = t * tile + lax.broadcasted_iota(jnp.int32, (tile, 1), 0) - PAD
    src, first = ext_ref, 0
    for gi, w in enumerate(POOL_WINDOWS):
        lane0 = gi * POOL_GROUP
        lanes = pl.ds(lane0, POOL_WIDTH - lane0)
        new_first = -(-(first + w // 2) // SUBLANES) * SUBLANES
        n_rows = tile + halo - new_first
        level = src[pl.ds(new_first, n_rows), lanes] + src[pl.ds(new_first - w // 2, n_rows), lanes]
        win = level[halo - new_first:, :POOL_GROUP]
        cols = pl.ds(lane0, POOL_GROUP)
        u = ext_ref[pl.ds(halo, tile), cols]
        cnt = jnp.clip(pos + 1, 1, w).astype(F32)
        delta = win / cnt - u
        mixed = jnp.dot(delta.astype(BF16), w_ref[gi], preferred_element_type=F32)
        o_ref[:, cols] = (mixed * scale_ref[:, cols]).astype(o_ref.dtype)
        if gi + 1 < len(POOL_WINDOWS):
            lvl_ref[gi, pl.ds(new_first, n_rows), lanes] = level
            src, first = lvl_ref.at[gi], new_first


def _pool(cols, pool_w_bf16, pool_scale, batch, lp):
    m = cols.shape[0]
    nt = lp // SEQ_TILE
    maxw = POOL_HALO
    return pl.pallas_call(
        _pool_kernel,
        grid=(batch, nt),
        in_specs=[
            pl.BlockSpec((SEQ_TILE, POOL_WIDTH), lambda b, t: (b * nt + t, OFF_POOL // POOL_WIDTH)),
            pl.BlockSpec((len(POOL_WINDOWS), POOL_GROUP, POOL_GROUP), lambda b, t: (0, 0, 0)),
            pl.BlockSpec((1, POOL_WIDTH), lambda b, t: (0, 0)),
        ],
        out_specs=pl.BlockSpec((SEQ_TILE, POOL_WIDTH), lambda b, t: (b * nt + t, 0)),
        out_shape=jax.ShapeDtypeStruct((m, POOL_WIDTH), BF16),
        scratch_shapes=[pltpu.VMEM((SEQ_TILE + maxw, POOL_WIDTH), F32),
                        pltpu.VMEM((len(POOL_WINDOWS) - 1, SEQ_TILE + maxw, POOL_WIDTH), F32)],
        compiler_params=_cparams("parallel", "arbitrary"),
        name="pool_mixer",
    )(cols, pool_w_bf16, pool_scale.reshape(1, POOL_WIDTH))


def _attn_bias():
    kj = jnp.arange(2 * BLOCK, dtype=jnp.int32)[:, None]
    qi = jnp.arange(BLOCK, dtype=jnp.int32)[None, :]
    dist = BLOCK + qi - kj
    in_window = (dist >= 0) & (dist < WINDOW)
    slopes = 2.0 ** (-8.0 * jnp.arange(1, N_HEADS + 1, dtype=F32) / N_HEADS)
    alibi = -slopes[:, None, None] * dist.astype(F32)[None]
    bias = jnp.where(in_window[None], alibi, NEG).reshape(N_KV_HEADS, Q_PER_KV, 2 * BLOCK, BLOCK)
    return bias.transpose(0, 2, 1, 3).reshape(N_KV_HEADS, 2 * BLOCK, Q_PER_KV * BLOCK)


def _attn_kernel(q_ref, kp_ref, kc_ref, vp_ref, vc_ref, bias_ref, sink_ref, o_ref):
    n = pl.program_id(1)
    q = q_ref[...] * (HEAD_DIM ** -0.5)

    def heads(early_keys):
        for kh in range(N_KV_HEADS):
            hs = pl.ds(kh * HEAD_DIM, HEAD_DIM)
            k2 = jnp.concatenate([kp_ref[:, hs], kc_ref[:, hs]], axis=0)
            v2 = jnp.concatenate([vp_ref[:, hs], vc_ref[:, hs]], axis=0)
            first = kh * Q_PER_KV
            qg = jnp.concatenate([q[:, (first + g) * HEAD_DIM:(first + g + 1) * HEAD_DIM]
                                  for g in range(Q_PER_KV)], axis=0)
            s = lax.dot_general(k2, qg, (((1,), (1,)), ((), ())), preferred_element_type=F32)
            s = s + bias_ref[kh]
            if early_keys is not None:
                s = s + early_keys
            sk = sink_ref[pl.ds(kh, 1), :]
            mx = jnp.maximum(jnp.max(s, axis=0, keepdims=True), sk)
            p = jnp.exp(s - mx)
            den = jnp.sum(p, axis=0, keepdims=True) + jnp.exp(sk - mx)
            pn = (p * (1.0 / den)).astype(BF16)
            o = lax.dot_general(pn, v2, (((0,), (0,)), ((), ())), preferred_element_type=F32)
            for g in range(Q_PER_KV):
                o_ref[:, pl.ds((first + g) * HEAD_DIM, HEAD_DIM)] = o[g * BLOCK:(g + 1) * BLOCK].astype(o_ref.dtype)

    @pl.when(n < 2)
    def _():
        k_pos = (n - 1) * BLOCK + lax.broadcasted_iota(jnp.int32, (2 * BLOCK, 1), 0)
        heads(jnp.where(k_pos < PAD, NEG, 0.0))

    @pl.when(n >= 2)
    def _():
        heads(None)


def _attention(cols, sink, batch, lp):
    m = cols.shape[0]
    nb = lp // BLOCK
    cur = lambda cb: (lambda b, n: (b * nb + n, cb))
    prev = lambda cb: (lambda b, n: (b * nb + jnp.maximum(n - 1, 0), cb))
    return pl.pallas_call(
        _attn_kernel,
        grid=(batch, nb),
        in_specs=[
            pl.BlockSpec((BLOCK, ATT_W), cur(OFF_Q // ATT_W)),
            pl.BlockSpec((BLOCK, KV_W), prev(OFF_K // KV_W)),
            pl.BlockSpec((BLOCK, KV_W), cur(OFF_K // KV_W)),
            pl.BlockSpec((BLOCK, KV_W), prev(OFF_V // KV_W)),
            pl.BlockSpec((BLOCK, KV_W), cur(OFF_V // KV_W)),
            pl.BlockSpec((N_KV_HEADS, 2 * BLOCK, Q_PER_KV * BLOCK), lambda b, n: (0, 0, 0)),
            pl.BlockSpec((N_KV_HEADS, Q_PER_KV * BLOCK), lambda b, n: (0, 0)),
        ],
        out_specs=pl.BlockSpec((BLOCK, ATT_W), lambda b, n: (b * nb + n, 0)),
        out_shape=jax.ShapeDtypeStruct((m, ATT_W), BF16),
        compiler_params=_cparams("parallel", "arbitrary"),
        name="swa_attention",
    )(cols, cols, cols, cols, cols, _attn_bias(),
      jnp.repeat(sink.astype(F32).reshape(N_KV_HEADS, Q_PER_KV), BLOCK, axis=1))


LRU_HALF = LRU_WIDTH // 2
LRU_HALO = 8
LRU_SCAN_UNROLL = 6


LOG2_E = 1.4426950408889634
GELU_C = 0.7978845608028654


def _gelu_tanh(x):
    half = 0.5 * x
    return half + half * jnp.tanh(x * (GELU_C + (GELU_C * 0.044715) * (x * x)))


def _lru_kernel(*refs):
    nh = LRU_WIDTH // LRU_HALF
    x_refs, y_refs = refs[:nh], refs[nh:2 * nh]
    (cw_ref, cb_ref, wa_ref, ba_ref, wx_ref, bx_ref, lam_ref, o_ref,
     ext_ref, a_ref, b_ref, carry_ref) = refs[2 * nh:]
    t = pl.program_id(1)
    tile = SEQ_TILE
    width = LRU_HALF
    pos = t * tile + lax.broadcasted_iota(jnp.int32, (tile, 1), 0)
    row = lax.broadcasted_iota(jnp.int32, (8, width), 0)

    for c in range(nh):
        lanes = pl.ds(c * width, width)
        ext, a_s, b_s, carry = ext_ref.at[c], a_ref.at[c], b_ref.at[c], carry_ref.at[c]

        @pl.when(t == 0)
        def _():
            ext[pl.ds(0, LRU_HALO), :] = jnp.zeros((LRU_HALO, width), F32)
            carry[...] = jnp.zeros((1, width), F32)

        @pl.when(t > 0)
        def _():
            ext[pl.ds(0, LRU_HALO), :] = ext[pl.ds(tile, LRU_HALO), :]

        ext[pl.ds(LRU_HALO, tile), :] = x_refs[c][...].astype(F32)

        xc = cb_ref[:, lanes] + cw_ref[pl.ds(CONV_WIDTH - 1, 1), lanes] * ext[pl.ds(LRU_HALO, tile), :]
        for j in range(CONV_WIDTH - 1):
            shift = CONV_WIDTH - 1 - j
            xc = xc + cw_ref[pl.ds(j, 1), lanes] * ext[pl.ds(LRU_HALO - shift, tile), :]

        xcb = xc.astype(BF16)
        ga_parts, gx_parts = [], []
        for blk in range(width // LRU_BLOCK):
            xb = xcb[:, blk * LRU_BLOCK:(blk + 1) * LRU_BLOCK]
            w_idx = c * (width // LRU_BLOCK) + blk
            ga_parts.append(jnp.dot(xb, wa_ref[w_idx], preferred_element_type=F32))
            gx_parts.append(jnp.dot(xb, wx_ref[w_idx], preferred_element_type=F32))
        gate_a = _sigmoid(jnp.concatenate(ga_parts, axis=1) + ba_ref[:, lanes])
        gate_x = _sigmoid(jnp.concatenate(gx_parts, axis=1) + bx_ref[:, lanes])

        neg_lam = -lam_ref[:, lanes]
        softplus = jnp.maximum(neg_lam, 0.0) + jnp.log1p(jnp.exp(-jnp.abs(neg_lam)))
        a = jnp.exp2(gate_a * ((-LRU_C * LOG2_E) * softplus))
        b_in = jnp.sqrt(1.0 - a * a) * gate_x * xc
        b_in = jnp.where(pos >= PAD, b_in, 0.0)
        a_s[...] = a
        b_s[...] = b_in

        def group(r, h_prev, a_s=a_s, b_s=b_s):
            rows = pl.ds(pl.multiple_of(r * 8, 8), 8)
            av = a_s[rows, :]
            bv = b_s[rows, :]
            for k in (1, 2, 4):
                a_sh = jnp.where(row >= k, pltpu.roll(av, k, 0), 1.0)
                b_sh = jnp.where(row >= k, pltpu.roll(bv, k, 0), 0.0)
                bv = av * b_sh + bv
                av = av * a_sh
            hv = av * h_prev + bv
            b_s[rows, :] = hv
            return hv[7:8, :]

        carry[...] = lax.fori_loop(0, tile // 8, group, carry[...], unroll=LRU_SCAN_UNROLL)
        o_ref[:, lanes] = (b_s[...] * _gelu_tanh(y_refs[c][...].astype(F32))).astype(o_ref.dtype)


def _lru(cols, conv_w, conv_b, wa_bf16, ba, wx_bf16, bx, lam, batch, lp):
    m = cols.shape[0]
    nt = lp // SEQ_TILE
    nh = LRU_WIDTH // LRU_HALF
    vec = lambda v: v.reshape(1, LRU_WIDTH).astype(F32)
    vec_spec = pl.BlockSpec((1, LRU_WIDTH), lambda b, t: (0, 0))
    w_spec = pl.BlockSpec((LRU_BLOCKS, LRU_BLOCK, LRU_BLOCK), lambda b, t: (0, 0, 0))
    half_spec = lambda off, c: pl.BlockSpec((SEQ_TILE, LRU_HALF), lambda b, t: (b * nt + t, off // LRU_HALF + c))
    return pl.pallas_call(
        _lru_kernel,
        grid=(batch, nt),
        in_specs=[half_spec(OFF_LX, c) for c in range(nh)] + [half_spec(OFF_LY, c) for c in range(nh)]
        + [pl.BlockSpec((CONV_WIDTH, LRU_WIDTH), lambda b, t: (0, 0)),
           vec_spec, w_spec, vec_spec, w_spec, vec_spec, vec_spec],
        out_specs=pl.BlockSpec((SEQ_TILE, LRU_WIDTH), lambda b, t: (b * nt + t, 0)),
        out_shape=jax.ShapeDtypeStruct((m, LRU_WIDTH), BF16),
        scratch_shapes=[
            pltpu.VMEM((nh, SEQ_TILE + LRU_HALO, LRU_HALF), F32),
            pltpu.VMEM((nh, SEQ_TILE, LRU_HALF), F32),
            pltpu.VMEM((nh, SEQ_TILE, LRU_HALF), F32),
            pltpu.VMEM((nh, 1, LRU_HALF), F32),
        ],
        compiler_params=_cparams("parallel", "arbitrary"),
        name="rglru",
    )(*([cols] * (2 * nh)), conv_w.astype(F32), vec(conv_b), wa_bf16, vec(ba), wx_bf16, vec(bx), vec(lam))


def _merge_kernel(p_ref, a_ref, r_ref, gp_ref, ga_ref, gr_ref, wp_ref, wa_ref, wr_ref, o_ref):
    acc = gp_ref[...].astype(F32) * jnp.dot(p_ref[...], wp_ref[...], preferred_element_type=F32)
    acc += ga_ref[...].astype(F32) * jnp.dot(a_ref[...], wa_ref[...], preferred_element_type=F32)
    acc += gr_ref[...].astype(F32) * jnp.dot(r_ref[...], wr_ref[...], preferred_element_type=F32)
    o_ref[...] = acc.astype(o_ref.dtype)


def _merge(pool_o, attn_o, lru_o, gates, wp, wa, wr, layer):
    m = pool_o.shape[0]
    bm, bn = MERGE_BM, MERGE_BN
    assert m % bm == 0 and D_MODEL % bn == 0
    x_spec = pl.BlockSpec((bm, POOL_WIDTH), lambda i, j: (i, 0))
    gate_spec = lambda k: pl.BlockSpec((bm, bn), lambda i, j: (i, k * D_MODEL // bn + j))
    w_spec = pl.BlockSpec((None, POOL_WIDTH, bn), lambda i, j: (layer, 0, j))
    return pl.pallas_call(
        _merge_kernel,
        grid=(m // bm, D_MODEL // bn),
        in_specs=[x_spec, x_spec, x_spec, gate_spec(0), gate_spec(1), gate_spec(2), w_spec, w_spec, w_spec],
        out_specs=pl.BlockSpec((bm, bn), lambda i, j: (i, j)),
        out_shape=jax.ShapeDtypeStruct((m, D_MODEL), BF16),
        compiler_params=_cparams("parallel", "arbitrary"),
        name="gated_merge",
    )(pool_o, attn_o, lru_o, gates, gates, gates, wp, wa, wr)


def _pack_bf16_pairs(yb):
    c = yb.shape[1] // 2
    lo = lax.bitcast_convert_type(yb[:, :c].astype(F32), jnp.uint32)
    hi = lax.bitcast_convert_type(yb[:, c:].astype(F32), jnp.uint32)
    return (hi & jnp.uint32(0xFFFF0000)) | (lo >> 16)


def _unpack_bf16_pairs(words, dtype=BF16):
    lo = lax.bitcast_convert_type(words << 16, F32)
    hi = lax.bitcast_convert_type(words & jnp.uint32(0xFFFF0000), F32)
    return jnp.concatenate([lo, hi], axis=1).astype(dtype)


def _route_tile(logits, valid, base):
    bm = logits.shape[0]
    lane = lax.broadcasted_iota(jnp.int32, (bm, ROUTE_W), 1)
    lane_f = lane.astype(F32)
    ninf = -jnp.inf
    big = float(ROUTE_W)

    gl = jnp.where(lane < N_GROUPS, logits, ninf)
    gmax = jnp.max(gl, axis=-1, keepdims=True)
    g = jnp.min(jnp.where(gl == gmax, lane_f, big), axis=-1, keepdims=True)
    p_g = 1.0 / jnp.sum(jnp.exp(gl - gmax), axis=-1, keepdims=True)

    first = N_GROUPS + g * EXPERTS_PER_GROUP
    sl = jnp.where((lane_f >= first) & (lane_f < first + EXPERTS_PER_GROUP), logits, ninf)
    m1 = jnp.max(sl, axis=-1, keepdims=True)
    i1 = jnp.min(jnp.where(sl == m1, lane_f, big), axis=-1, keepdims=True)
    ssum = jnp.sum(jnp.exp(sl - m1), axis=-1, keepdims=True)
    sl2 = jnp.where(lane_f == i1, ninf, sl)
    m2 = jnp.max(sl2, axis=-1, keepdims=True)
    i2 = jnp.min(jnp.where(sl2 == m2, lane_f, big), axis=-1, keepdims=True)
    p1 = 1.0 / ssum
    p2 = jnp.exp(m2 - m1) / ssum
    w1 = p_g * p1 / (p1 + p2)
    w2 = p_g * p2 / (p1 + p2)
    e1 = i1 - N_GROUPS
    e2 = i2 - N_GROUPS

    oh1 = (lane_f == e1) & valid
    oh2 = (lane_f == e2) & valid
    both = (oh1 | oh2).astype(F32)
    earlier = (lax.broadcasted_iota(jnp.int32, (bm, bm), 0) > lax.broadcasted_iota(jnp.int32, (bm, bm), 1))
    prefix = jnp.dot(earlier.astype(BF16), both.astype(BF16), preferred_element_type=F32) + base
    r1 = jnp.sum(jnp.where(oh1, prefix, 0.0), axis=-1, keepdims=True)
    r2 = jnp.sum(jnp.where(oh2, prefix, 0.0), axis=-1, keepdims=True)

    ri = jnp.where(lane == 0, e1, jnp.where(lane == 1, e2, jnp.where(lane == 2, r1, jnp.where(lane == 3, r2, 0.0))))
    rw = jnp.where(lane == 0, w1, jnp.where(lane == 1, w2, 0.0))
    ri_rows = jnp.transpose(ri)[0:ROUTE_ROWS, :].astype(jnp.int32)
    return ri_rows, rw, jnp.sum(both, axis=0, keepdims=True)


def _outproj_kernel(x_ref, w_ref, h_ref, g_ref, b_ref, rw_ref, rb_ref, hf_ref, hp_ref, ri_ref, rwt_ref, cnt_ref,
                    *, batch, lp):
    i = pl.program_id(0)
    bm = x_ref.shape[0]
    sub = bm // OUT_SPLIT

    @pl.when(i == 0)
    def _():
        cnt_ref[...] = jnp.zeros(cnt_ref.shape, F32)

    for s in range(OUT_SPLIT):
        rows = pl.ds(s * sub, sub)
        t = jnp.dot(x_ref[rows, :], w_ref[...], preferred_element_type=F32)
        y = _layer_norm(ALPHA * h_ref[rows, :] + t, g_ref[...], b_ref[...])
        valid = _flat_valid_rows(i * bm + s * sub, sub, batch, lp)
        y = jnp.where(valid, y, 0.0)
        yb = y.astype(BF16)
        hf_ref[rows, :] = y
        packed = _pack_bf16_pairs(yb)
        for c in range(ROW_TILES):
            hp_ref[pl.ds(s * sub * ROW_TILES + c, sub, stride=ROW_TILES), :] = packed[:, c * LANES:(c + 1) * LANES]
        logits = jnp.dot(yb, rw_ref[...], preferred_element_type=F32) + rb_ref[...]
        ri, rw, tile_cnt = _route_tile(logits, valid, cnt_ref[...])
        ri_ref[:, rows] = ri
        rwt_ref[rows, :] = rw
        cnt_ref[...] += tile_cnt


def _outproj_ln(merged, w_out_bf16, h, g, b, route_w, route_b, layer, batch, lp):
    m, d = h.shape
    bm = OUT_BM
    assert m % bm == 0
    row_spec = pl.BlockSpec((bm, d), lambda i: (i, 0))
    vec_spec = pl.BlockSpec((1, d), lambda i: (0, 0))
    route_spec = pl.BlockSpec((bm, ROUTE_W), lambda i: (i, 0))
    return pl.pallas_call(
        functools.partial(_outproj_kernel, batch=batch, lp=lp),
        grid=(m // bm,),
        in_specs=[
            row_spec,
            pl.BlockSpec((None, d, d), lambda i: (layer, 0, 0)),
            row_spec, vec_spec, vec_spec,
            pl.BlockSpec((d, ROUTE_W), lambda i: (0, 0)),
            pl.BlockSpec((1, ROUTE_W), lambda i: (0, 0)),
        ],
        out_specs=[row_spec, pl.BlockSpec((bm * ROW_TILES, LANES), lambda i: (i, 0)),
                   pl.BlockSpec((ROUTE_ROWS, bm), lambda i: (0, i)), route_spec,
                   pl.BlockSpec((1, ROUTE_W), lambda i: (0, 0))],
        out_shape=[jax.ShapeDtypeStruct((m, d), F32), jax.ShapeDtypeStruct((m * ROW_TILES, LANES), jnp.uint32),
                   jax.ShapeDtypeStruct((ROUTE_ROWS, m), jnp.int32), jax.ShapeDtypeStruct((m, ROUTE_W), F32),
                   jax.ShapeDtypeStruct((1, ROUTE_W), F32)],
        compiler_params=_cparams("arbitrary"),
        name="out_proj_ln",
    )(merged, w_out_bf16, h, g.reshape(1, d), b.reshape(1, d), route_w, route_b)


def _dispatch_kernel(start_ref, nchunk_ref, dest_hbm, hp_ref, xs_ref, idx_ref, zero_ref, idx_sem, row_sem, zero_sem,
                     *, n_expert_rows):
    i = pl.program_id(0)
    bm = hp_ref.shape[0] // ROW_TILES
    groups = bm // SUBLANES
    n_idx = TOP_K * bm

    def token_rows(first_token, n_tokens):
        return pl.ds(pl.multiple_of(first_token * ROW_TILES, ROW_TILES), n_tokens * ROW_TILES)

    @pl.when(i == 0)
    def _():
        zero_ref[...] = jnp.zeros(zero_ref.shape, zero_ref.dtype)

        def last_chunk(e):
            first = pl.multiple_of(start_ref[e] + (nchunk_ref[e] - 1) * MOE_BM, MOE_BM)
            return pltpu.make_async_copy(zero_ref, xs_ref.at[token_rows(first, MOE_BM)], zero_sem)

        for e in range(N_EXPERTS):
            @pl.when(nchunk_ref[e] > 0)
            def _():
                last_chunk(e).start()

        for e in range(N_EXPERTS):
            @pl.when(nchunk_ref[e] > 0)
            def _():
                last_chunk(e).wait()

        used = start_ref[N_EXPERTS - 1] + nchunk_ref[N_EXPERTS - 1] * MOE_BM

        def tail_chunk(j):
            return pltpu.make_async_copy(
                zero_ref, xs_ref.at[token_rows(pl.multiple_of(used + j * MOE_BM, MOE_BM), MOE_BM)], zero_sem)

        n_tail = (n_expert_rows - used) // MOE_BM
        lax.fori_loop(0, n_tail, lambda j, c: (tail_chunk(j).start(), c)[1], 0)
        lax.fori_loop(0, n_tail, lambda j, c: (tail_chunk(j).wait(), c)[1], 0)

    idx_copy = pltpu.make_async_copy(dest_hbm.at[pl.ds(pl.multiple_of(i * n_idx, n_idx), n_idx)], idx_ref, idx_sem)
    idx_copy.start()
    idx_copy.wait()

    def issue(g, carry):
        for s in range(SUBLANES):
            for k in range(TOP_K):
                slot = idx_ref[g * SUBLANES + (k * bm + s)]
                pltpu.make_async_copy(hp_ref.at[token_rows(g * SUBLANES + s, 1)], xs_ref.at[token_rows(slot, 1)],
                                      row_sem).start(priority=k % 2)
        return carry

    lax.fori_loop(0, groups, issue, 0)
    for k in range(TOP_K):
        pltpu.make_async_copy(xs_ref.at[token_rows(0, bm)], xs_ref.at[token_rows(0, bm)], row_sem).wait()


def _dispatch(hp, dest, chunk_start, n_chunks, n_expert_rows, n_slots):
    m = hp.shape[0] // ROW_TILES
    bm = DISPATCH_BM
    assert m % bm == 0 and bm % SUBLANES == 0
    grid_spec = pltpu.PrefetchScalarGridSpec(
        num_scalar_prefetch=2,
        grid=(m // bm,),
        in_specs=[
            pl.BlockSpec(memory_space=pl.ANY),
            pl.BlockSpec((bm * ROW_TILES, LANES), lambda i, st, nc: (i, 0)),
        ],
        out_specs=pl.BlockSpec(memory_space=pl.ANY),
        scratch_shapes=[pltpu.SMEM((TOP_K * bm,), jnp.int32), pltpu.VMEM((MOE_BM * ROW_TILES, LANES), jnp.uint32),
                        pltpu.SemaphoreType.DMA(()), pltpu.SemaphoreType.DMA(()), pltpu.SemaphoreType.DMA(())],
    )
    return pl.pallas_call(
        functools.partial(_dispatch_kernel, n_expert_rows=n_expert_rows),
        grid_spec=grid_spec,
        out_shape=jax.ShapeDtypeStruct((n_slots * ROW_TILES, LANES), jnp.uint32),
        compiler_params=_cparams("arbitrary"),
        name="moe_dispatch",
    )(chunk_start, n_chunks, dest, hp)


def _expert_kernel(start_ref, nchunk_ref, xs_hbm, wg_hbm, wu_hbm, wd_hbm, yb_hbm, wg_f, wu_f, wd_f,
                   x_buf, o_buf, w_sem, in_sem, out_sem, *, layer):
    e = pl.program_id(0)
    n = nchunk_ref[e]
    bm = MOE_BM
    first = start_ref[e] // bm
    total = start_ref[N_EXPERTS - 1] // bm + nchunk_ref[N_EXPERTS - 1]
    wslot = e % 2

    def w_copies(expert, slot):
        return [pltpu.make_async_copy(src.at[layer, expert], dst.at[slot], w_sem.at[slot])
                for src, dst in ((wg_hbm, wg_f), (wu_hbm, wu_f), (wd_hbm, wd_f))]

    def o_rows(g):
        return pl.ds(pl.multiple_of(g * (bm * ROW_TILES), bm * ROW_TILES), bm * ROW_TILES)

    def x_copy(g):
        slot = g % EXPERT_X_BUFFERS
        return pltpu.make_async_copy(xs_hbm.at[o_rows(g)], x_buf.at[slot], in_sem.at[slot])

    def o_copy(g):
        slot = g % 2
        return pltpu.make_async_copy(o_buf.at[slot], yb_hbm.at[o_rows(g)], out_sem.at[slot])

    @pl.when(e == 0)
    def _():
        for ahead in range(EXPERT_X_BUFFERS - 1):
            @pl.when(ahead < total)
            def _():
                x_copy(ahead).start()

        for c in w_copies(0, 0):
            c.start(priority=1)

    @pl.when(e + 1 < N_EXPERTS)
    def _():
        for c in w_copies(e + 1, 1 - wslot):
            c.start(priority=1)

    for c in w_copies(e, wslot):
        c.wait()

    @pl.when(n > 0)
    def _():
        def chunk(j, carry):
            g = first + j
            x_copy(g).wait()

            @pl.when(g + EXPERT_X_BUFFERS - 1 < total)
            def _():
                x_copy(g + EXPERT_X_BUFFERS - 1).start()

            @pl.when(g >= 2)
            def _():
                o_copy(g - 2).wait()

            x_slot = g % EXPERT_X_BUFFERS
            x = _unpack_bf16_pairs(jnp.concatenate(
                [x_buf[x_slot, pl.ds(c, bm, stride=ROW_TILES), :] for c in range(ROW_TILES)], axis=1))
            gate = jnp.dot(x, wg_f[wslot].astype(BF16), preferred_element_type=F32)
            up = jnp.dot(x, wu_f[wslot].astype(BF16), preferred_element_type=F32)
            hdn = (gate * _sigmoid(gate)) * up
            y = jnp.dot(hdn.astype(BF16), wd_f[wslot].astype(BF16), preferred_element_type=F32)
            packed = _pack_bf16_pairs(y.astype(BF16))
            for c in range(ROW_TILES):
                o_buf[g % 2, pl.ds(c, bm, stride=ROW_TILES), :] = packed[:, c * LANES:(c + 1) * LANES]
            o_copy(g).start()
            return carry

        lax.fori_loop(0, n, chunk, 0)

    @pl.when(e == N_EXPERTS - 1)
    def _():
        for back in (2, 1):
            @pl.when(total >= back)
            def _():
                o_copy(total - back).wait()

        o_buf[0] = jnp.zeros(o_buf.shape[1:], o_buf.dtype)

        def fill(g, carry):
            tail = pltpu.make_async_copy(o_buf.at[0], yb_hbm.at[o_rows(g)], out_sem.at[0])
            tail.start()
            tail.wait()
            return carry

        lax.fori_loop(total, yb_hbm.shape[0] // (bm * ROW_TILES), fill, 0)


def _experts(xs, n_rows, chunk_start, n_chunks, w_gate, w_up, w_down, layer):
    d = D_MODEL
    bm = MOE_BM
    any_spec = pl.BlockSpec(memory_space=pl.ANY)
    grid_spec = pltpu.PrefetchScalarGridSpec(
        num_scalar_prefetch=2,
        grid=(N_EXPERTS,),
        in_specs=[any_spec, any_spec, any_spec, any_spec],
        out_specs=any_spec,
        scratch_shapes=[
            pltpu.VMEM((2, d, D_EXPERT), F32),
            pltpu.VMEM((2, d, D_EXPERT), F32),
            pltpu.VMEM((2, D_EXPERT, d), F32),
            pltpu.VMEM((EXPERT_X_BUFFERS, bm * ROW_TILES, LANES), jnp.uint32),
            pltpu.VMEM((2, bm * ROW_TILES, LANES), jnp.uint32),
            pltpu.SemaphoreType.DMA((2,)),
            pltpu.SemaphoreType.DMA((EXPERT_X_BUFFERS,)),
            pltpu.SemaphoreType.DMA((2,)),
        ],
    )
    return pl.pallas_call(
        functools.partial(_expert_kernel, layer=layer),
        grid_spec=grid_spec,
        out_shape=jax.ShapeDtypeStruct((n_rows * ROW_TILES, LANES), jnp.uint32),
        compiler_params=_cparams("arbitrary"),
        name="expert_mlp",
    )(chunk_start, n_chunks, xs, w_gate, w_up, w_down)


def _combine_ln_kernel(dest_hbm, h_ref, rw_ref, g_ref, b_ref, yb_hbm, *rest, batch, lp, final):
    if final:
        out_hbm, idx_ref, rows_ref, y_buf, idx_sem, row_sem, out_sem = rest
    else:
        hf_ref, hb_ref, idx_ref, rows_ref, idx_sem, row_sem = rest
    i = pl.program_id(0)
    n_tiles = pl.num_programs(0)
    bm = h_ref.shape[0]
    n_idx = TOP_K * bm
    buf = i % COMBINE_ROW_BUFFERS

    def idx_copy(tile):
        half = pl.ds(pl.multiple_of((tile % 2) * n_idx, n_idx), n_idx)
        return pltpu.make_async_copy(
            dest_hbm.at[pl.ds(pl.multiple_of(tile * n_idx, n_idx), n_idx)], idx_ref.at[half], idx_sem.at[tile % 2])

    def start_gather(tile):
        into = tile % COMBINE_ROW_BUFFERS
        idx_copy(tile).wait()

        def issue(g, carry):
            base = (tile % 2) * n_idx + g * SUBLANES
            for s in range(SUBLANES):
                for k in range(TOP_K):
                    slot = idx_ref[base + (k * bm + s)]
                    src = yb_hbm.at[pl.ds(pl.multiple_of(slot * ROW_TILES, ROW_TILES), ROW_TILES)]
                    dst = rows_ref.at[into, k, pl.ds(pl.multiple_of((g * SUBLANES + s) * ROW_TILES, ROW_TILES),
                                                     ROW_TILES)]
                    pltpu.make_async_copy(src, dst, row_sem.at[into]).start(priority=k % 2)
            return carry

        lax.fori_loop(0, bm // SUBLANES, issue, 0)

        @pl.when(tile + 2 < n_tiles)
        def _():
            idx_copy(tile + 2).start()

    @pl.when(i == 0)
    def _():
        idx_copy(0).start()

        @pl.when(1 < n_tiles)
        def _():
            idx_copy(1).start()

        for ahead in range(COMBINE_ROW_BUFFERS - 1):
            @pl.when(ahead < n_tiles)
            def _():
                start_gather(ahead)

    @pl.when(i + COMBINE_ROW_BUFFERS - 1 < n_tiles)
    def _():
        start_gather(i + COMBINE_ROW_BUFFERS - 1)

    for k in range(TOP_K):
        whole = pl.ds(0, bm * ROW_TILES)
        pltpu.make_async_copy(yb_hbm.at[whole], yb_hbm.at[whole], row_sem.at[buf]).wait()

    def expert_rows(k):
        pieces = [rows_ref[buf, k, pl.ds(c, bm, stride=ROW_TILES), :] for c in range(ROW_TILES)]
        return _unpack_bf16_pairs(jnp.concatenate(pieces, axis=1), F32)

    ffn = rw_ref[:, 0:1] * expert_rows(0)
    for k in range(1, TOP_K):
        ffn = ffn + rw_ref[:, k:k + 1] * expert_rows(k)
    y = _layer_norm(ALPHA * h_ref[...] + ffn, g_ref[...], b_ref[...])
    if not final:
        y = jnp.where(_flat_valid_rows(i * bm, bm, batch, lp), y, 0.0)
        hf_ref[...] = y
        hb_ref[...] = y.astype(BF16)
        return

    nb = lp // BLOCK

    def for_each_out_block(step, fn):
        for j in range(bm // BLOCK):
            blk = step * (bm // BLOCK) + j
            seq_blk = blk % nb

            @pl.when(seq_blk >= 1)
            def _():
                dst = pl.multiple_of(((blk // nb) * (nb - 1) + seq_blk - 1) * BLOCK, BLOCK)
                fn(pltpu.make_async_copy(y_buf.at[pl.ds(j * BLOCK, BLOCK)], out_hbm.at[pl.ds(dst, BLOCK)], out_sem))

    @pl.when(i > 0)
    def _():
        for_each_out_block(i - 1, lambda c: c.wait())

    y_buf[...] = y
    for_each_out_block(i, lambda c: c.start(priority=1))

    @pl.when(i == n_tiles - 1)
    def _():
        for_each_out_block(i, lambda c: c.wait())


def _combine_ln(h, yb, dest, rw, g, b, batch, lp, final):
    m, d = h.shape
    bm = COMBINE_BM
    assert m % bm == 0 and bm % BLOCK == 0
    row_spec = pl.BlockSpec((bm, d), lambda i: (i, 0))
    vec_spec = pl.BlockSpec((1, d), lambda i: (0, 0))
    scratch = [pltpu.SMEM((2 * TOP_K * bm,), jnp.int32),
               pltpu.VMEM((COMBINE_ROW_BUFFERS, TOP_K, bm * ROW_TILES, LANES), jnp.uint32)]
    sems = [pltpu.SemaphoreType.DMA((2,)), pltpu.SemaphoreType.DMA((COMBINE_ROW_BUFFERS,))]
    if final:
        out_specs = pl.BlockSpec(memory_space=pl.ANY)
        out_shape = jax.ShapeDtypeStruct((batch * (lp - BLOCK), d), F32)
        scratch = scratch + [pltpu.VMEM((bm, d), F32)] + sems + [pltpu.SemaphoreType.DMA(())]
    else:
        out_specs = [row_spec, row_spec]
        out_shape = [jax.ShapeDtypeStruct((m, d), F32), jax.ShapeDtypeStruct((m, d), BF16)]
        scratch = scratch + sems
    return pl.pallas_call(
        functools.partial(_combine_ln_kernel, batch=batch, lp=lp, final=final),
        grid=(m // bm,),
        in_specs=[
            pl.BlockSpec(memory_space=pl.ANY),
            row_spec,
            pl.BlockSpec((bm, ROUTE_W), lambda i: (i, 0)),
            vec_spec, vec_spec,
            pl.BlockSpec(memory_space=pl.ANY),
        ],
        out_specs=out_specs,
        out_shape=out_shape,
        scratch_shapes=scratch,
        compiler_params=_cparams("arbitrary"),
        name="moe_combine_out" if final else "moe_combine_ln",
    )(dest, h, rw, g.reshape(1, d), b.reshape(1, d), yb)


def _slot_tables(ri, cnt, batch, lp):
    m = ri.shape[1]
    bm = MOE_BM
    eid = ri[0:TOP_K]
    rank = ri[TOP_K:2 * TOP_K]
    counts = cnt[0, :N_EXPERTS].astype(jnp.int32)
    padded = (counts + bm - 1) // bm * bm
    pad_end = jnp.cumsum(padded)
    pad_start = pad_end - padded
    n_real = batch * (lp - PAD) * TOP_K
    nblk = -(-(n_real + N_EXPERTS * (bm - 1)) // bm)
    cap = nblk * bm
    start = jnp.zeros_like(eid)
    for e in range(N_EXPERTS):
        start = jnp.where(eid == e, pad_start[e], start)
    row = jnp.arange(m, dtype=jnp.int32)
    pos = row % lp
    spare = cap + ((row // lp) * PAD + pos)[None, :] * TOP_K + jnp.arange(TOP_K, dtype=jnp.int32)[:, None]
    valid = (pos >= PAD)[None, :]
    dest = jnp.where(valid, start + rank, spare)
    dest_read = jnp.where(valid, dest, 0)
    n_slots = cap + batch * PAD * TOP_K

    def by_tile(idx, tile):
        return idx.reshape(TOP_K, m // tile, tile).transpose(1, 0, 2).reshape(m * TOP_K)

    return (by_tile(dest, DISPATCH_BM), by_tile(dest_read, COMBINE_BM), pad_start.astype(jnp.int32),
            (padded // bm).astype(jnp.int32), cap, n_slots)


def kernel(x, meta, ln_emb_g, ln_emb_b, w_in, pool_w, pool_scale, attn_sink, conv_w, conv_b, lru_wa, lru_ba,
           lru_wx, lru_bx, lru_lambda, proj_pool, proj_attn, proj_lru, w_out, ln1_g, ln1_b, router_grp_w,
           router_grp_b, router_exp_w, router_exp_b, exp_w_gate, exp_w_up, exp_w_down, ln2_g, ln2_b):
    batch, seq, d = x.shape
    lp = PAD + N_META + seq
    m = batch * lp

    hf, hb = _embed(x, meta, ln_emb_g, ln_emb_b)
    w_in_b = w_in.astype(BF16)
    pool_w_b = pool_w.astype(BF16)
    wa_b = lru_wa.astype(BF16)
    wx_b = lru_wx.astype(BF16)
    wp_b = proj_pool.astype(BF16)
    wat_b = proj_attn.astype(BF16)
    wl_b = proj_lru.astype(BF16)
    w_out_b = w_out.astype(BF16)
    route_pad = ROUTE_W - N_GROUPS - N_EXPERTS

    for l in range(DEPTH):
        cols = _inproj(hb, w_in_b, l, gates=False)
        gates = _inproj(hb, w_in_b, l, gates=True)
        pool_o = _pool(cols, pool_w_b[l], pool_scale[l], batch, lp)
        attn_o = _attention(cols, attn_sink[l], batch, lp)
        lru_o = _lru(cols, conv_w[l], conv_b[l], wa_b[l], lru_ba[l], wx_b[l], lru_bx[l], lru_lambda[l],
                     batch, lp)
        merged = _merge(pool_o, attn_o, lru_o, gates, wp_b, wat_b, wl_b, l)
        route_w = jnp.concatenate(
            [router_grp_w[l], router_exp_w[l], jnp.zeros((d, route_pad), F32)], axis=1).astype(BF16)
        route_b = jnp.concatenate(
            [router_grp_b[l], router_exp_b[l], jnp.zeros((route_pad,), F32)]).reshape(1, ROUTE_W)
        h1f, h1p, ri, rw, cnt = _outproj_ln(merged, w_out_b, hf, ln1_g[l], ln1_b[l], route_w, route_b, l,
                                            batch, lp)
        dest, dest_read, chunk_start, n_chunks, cap, n_slots = _slot_tables(ri, cnt, batch, lp)
        xs = _dispatch(h1p, dest, chunk_start, n_chunks, cap, n_slots)
        yb = _experts(xs, cap, chunk_start, n_chunks, exp_w_gate, exp_w_up, exp_w_down, l)
        if l + 1 < DEPTH:
            hf, hb = _combine_ln(h1f, yb, dest_read, rw, ln2_g[l], ln2_b[l], batch, lp, final=False)
        else:
            out = _combine_ln(h1f, yb, dest_read, rw, ln2_g[l], ln2_b[l], batch, lp, final=True)

    return out.reshape(batch, seq, d)
```

```python
import functools

import jax
import jax.numpy as jnp
from jax import lax
from jax.experimental import pallas as pl
from jax.experimental.pallas import tpu as pltpu

F32 = jnp.float32
BF16 = jnp.bfloat16

D_MODEL = 2048
DEPTH = 2
N_META = 16
POOL_WINDOWS = (2, 4, 8, 16)
POOL_WIDTH = D_MODEL // 2
POOL_GROUP = POOL_WIDTH // len(POOL_WINDOWS)
N_HEADS = 16
N_KV_HEADS = 4
HEAD_DIM = 64
Q_PER_KV = N_HEADS // N_KV_HEADS
WINDOW = 128
BLOCK = 128
NEG = -1e30
LRU_WIDTH = D_MODEL // 2
LRU_BLOCKS = 4
LRU_BLOCK = LRU_WIDTH // LRU_BLOCKS
CONV_WIDTH = 4
LRU_C = 8.0
N_GROUPS = 4
EXPERTS_PER_GROUP = 8
N_EXPERTS = N_GROUPS * EXPERTS_PER_GROUP
TOP_K = 2
D_EXPERT = D_MODEL // 4
LN_EPS = 1e-5
ALPHA = (2.0 * DEPTH) ** 0.25

PAD = BLOCK - N_META
ATT_W = N_HEADS * HEAD_DIM
KV_W = N_KV_HEADS * HEAD_DIM
OFF_POOL = 0
OFF_Q = OFF_POOL + POOL_WIDTH
OFF_K = OFF_Q + ATT_W
OFF_V = OFF_K + KV_W
OFF_LX = OFF_V + KV_W
OFF_LY = OFF_LX + LRU_WIDTH
OFF_GATE = OFF_LY + LRU_WIDTH
IN_COLS = OFF_GATE + 3 * D_MODEL

VMEM_LIMIT_BYTES = 56 * 1024 * 1024
SUBLANES = 8
LANES = 128

SEQ_TILE = 3 * BLOCK
INPROJ_BM = 1536
INPROJ_BN = 1536
MERGE_BM = 768
MERGE_BN = 1024
OUT_BM = 512
OUT_SPLIT = 2
MOE_BM = 256
DISPATCH_BM = 1536
COMBINE_BM = 512
COMBINE_ROW_BUFFERS = 3
EXPERT_X_BUFFERS = 3
ROW_TILES = D_MODEL // 2 // LANES
ROUTE_W = 128
ROUTE_ROWS = 8


def _cparams(*sem):
    return pltpu.CompilerParams(dimension_semantics=sem, vmem_limit_bytes=VMEM_LIMIT_BYTES)


def _layer_norm(x, g, b):
    mu = jnp.mean(x, axis=-1, keepdims=True)
    xc = x - mu
    var = jnp.mean(xc * xc, axis=-1, keepdims=True)
    return xc * lax.rsqrt(var + LN_EPS) * g + b


def _flat_valid_rows(row0, n_rows, batch, lp):
    r = row0 + lax.broadcasted_iota(jnp.int32, (n_rows, 1), 0)
    pad_row = jnp.zeros((n_rows, 1), jnp.bool_)
    for b in range(batch):
        pad_row = pad_row | ((r >= b * lp) & (r < b * lp + PAD))
    return jnp.logical_not(pad_row)


EMBED_BLOCKS = SEQ_TILE // BLOCK


def _embed_kernel(*refs):
    x_refs, (meta_ref, g_ref, b_ref, hf_ref, hb_ref) = refs[:EMBED_BLOCKS], refs[EMBED_BLOCKS:]
    t = pl.program_id(1)
    for j in range(EMBED_BLOCKS):
        src = x_refs[j][...]
        if j == 0:
            src = jnp.where(t == 0, meta_ref[...], src)
        y = _layer_norm(src, g_ref[...], b_ref[...])
        if j == 0:
            row = lax.broadcasted_iota(jnp.int32, (BLOCK, 1), 0)
            y = jnp.where((t > 0) | (row >= PAD), y, 0.0)
        hf_ref[pl.ds(j * BLOCK, BLOCK), :] = y
        hb_ref[pl.ds(j * BLOCK, BLOCK), :] = y.astype(BF16)


def _embed(x, meta, g, b):
    batch, seq, d = x.shape
    nblk = seq // BLOCK
    lp = (nblk + 1) * BLOCK
    nt = lp // SEQ_TILE
    m = batch * lp
    meta_tile = jnp.concatenate([jnp.zeros((PAD, d), F32), meta.astype(F32)], axis=0)
    row_spec = pl.BlockSpec((SEQ_TILE, d), lambda bi, t: (bi * nt + t, 0))
    vec_spec = pl.BlockSpec((1, d), lambda bi, t: (0, 0))
    x_spec = lambda j: pl.BlockSpec(
        (BLOCK, d), lambda bi, t: (bi * nblk + jnp.maximum(EMBED_BLOCKS * t + j - 1, 0), 0))
    x2 = x.reshape(batch * seq, d)
    return pl.pallas_call(
        _embed_kernel,
        grid=(batch, nt),
        in_specs=[x_spec(j) for j in range(EMBED_BLOCKS)]
        + [pl.BlockSpec((BLOCK, d), lambda bi, t: (0, 0)), vec_spec, vec_spec],
        out_specs=[row_spec, row_spec],
        out_shape=[jax.ShapeDtypeStruct((m, d), F32), jax.ShapeDtypeStruct((m, d), BF16)],
        compiler_params=_cparams("parallel", "arbitrary"),
        name="embed_ln",
    )(*([x2] * EMBED_BLOCKS), meta_tile, g.reshape(1, d), b.reshape(1, d))


def _sigmoid(x):
    return 0.5 * jnp.tanh(0.5 * x) + 0.5


def _inproj_kernel(x_ref, w_ref, o_ref, *, gates):
    acc = jnp.dot(x_ref[...], w_ref[...], preferred_element_type=F32)
    o_ref[...] = (_sigmoid(acc) if gates else acc).astype(o_ref.dtype)


def _inproj(hb, w_in_bf16, layer, gates):
    m, d = hb.shape
    bm, bn = INPROJ_BM, INPROJ_BN
    col0, width = (OFF_GATE, IN_COLS - OFF_GATE) if gates else (0, OFF_GATE)
    assert m % bm == 0 and width % bn == 0 and col0 % bn == 0
    return pl.pallas_call(
        functools.partial(_inproj_kernel, gates=gates),
        grid=(m // bm, width // bn),
        in_specs=[
            pl.BlockSpec((bm, d), lambda i, j: (i, 0)),
            pl.BlockSpec((None, d, bn), lambda i, j: (layer, 0, col0 // bn + j)),
        ],
        out_specs=pl.BlockSpec((bm, bn), lambda i, j: (i, j)),
        out_shape=jax.ShapeDtypeStruct((m, width), BF16),
        compiler_params=_cparams("parallel", "arbitrary"),
        name="in_proj_gates" if gates else "in_proj",
    )(hb, w_in_bf16)


POOL_HALO = 2 * max(POOL_WINDOWS)
assert POOL_WINDOWS == tuple(2 ** (g + 1) for g in range(len(POOL_WINDOWS)))


def _pool_kernel(u_ref, w_ref, scale_ref, o_ref, ext_ref, lvl_ref):
    t = pl.program_id(1)
    tile = SEQ_TILE
    halo = POOL_HALO

    @pl.when(t == 0)
    def _():
        ext_ref[pl.ds(0, halo), :] = jnp.zeros((halo, POOL_WIDTH), F32)

    @pl.when(t > 0)
    def _():
        ext_ref[pl.ds(0, halo), :] = ext_ref[pl.ds(tile, halo), :]

    ext_ref[pl.ds(halo, tile), :] = u_ref[...].astype(F32)

    pos = t * tile + lax.broadcasted_iota(jnp.int32, (tile, 1), 0) - PAD
    src, first = ext_ref, 0
    for gi, w in enumerate(POOL_WINDOWS):
        lane0 = gi * POOL_GROUP
        lanes = pl.ds(lane0, POOL_WIDTH - lane0)
        new_first = -(-(first + w // 2) // SUBLANES) * SUBLANES
        n_rows = tile + halo - new_first
        level = src[pl.ds(new_first, n_rows), lanes] + src[pl.ds(new_first - w // 2, n_rows), lanes]
        win = level[halo - new_first:, :POOL_GROUP]
        cols = pl.ds(lane0, POOL_GROUP)
        u = ext_ref[pl.ds(halo, tile), cols]
        cnt = jnp.clip(pos + 1, 1, w).astype(F32)
        delta = win / cnt - u
        mixed = jnp.dot(delta.astype(BF16), w_ref[gi], preferred_element_type=F32)
        o_ref[:, cols] = (mixed * scale_ref[:, cols]).astype(o_ref.dtype)
        if gi + 1 < len(POOL_WINDOWS):
            lvl_ref[gi, pl.ds(new_first, n_rows), lanes] = level
            src, first = lvl_ref.at[gi], new_first


def _pool(cols, pool_w_bf16, pool_scale, batch, lp):
    m = cols.shape[0]
    nt = lp // SEQ_TILE
    maxw = POOL_HALO
    return pl.pallas_call(
        _pool_kernel,
        grid=(batch, nt),
        in_specs=[
            pl.BlockSpec((SEQ_TILE, POOL_WIDTH), lambda b, t: (b * nt + t, OFF_POOL // POOL_WIDTH)),
            pl.BlockSpec((len(POOL_WINDOWS), POOL_GROUP, POOL_GROUP), lambda b, t: (0, 0, 0)),
            pl.BlockSpec((1, POOL_WIDTH), lambda b, t: (0, 0)),
        ],
        out_specs=pl.BlockSpec((SEQ_TILE, POOL_WIDTH), lambda b, t: (b * nt + t, 0)),
        out_shape=jax.ShapeDtypeStruct((m, POOL_WIDTH), BF16),
        scratch_shapes=[pltpu.VMEM((SEQ_TILE + maxw, POOL_WIDTH), F32),
                        pltpu.VMEM((len(POOL_WINDOWS) - 1, SEQ_TILE + maxw, POOL_WIDTH), F32)],
        compiler_params=_cparams("parallel", "arbitrary"),
        name="pool_mixer",
    )(cols, pool_w_bf16, pool_scale.reshape(1, POOL_WIDTH))


def _attn_bias():
    kj = jnp.arange(2 * BLOCK, dtype=jnp.int32)[:, None]
    qi = jnp.arange(BLOCK, dtype=jnp.int32)[None, :]
    dist = BLOCK + qi - kj
    in_window = (dist >= 0) & (dist < WINDOW)
    slopes = 2.0 ** (-8.0 * jnp.arange(1, N_HEADS + 1, dtype=F32) / N_HEADS)
    alibi = -slopes[:, None, None] * dist.astype(F32)[None]
    bias = jnp.where(in_window[None], alibi, NEG).reshape(N_KV_HEADS, Q_PER_KV, 2 * BLOCK, BLOCK)
    return bias.transpose(0, 2, 1, 3).reshape(N_KV_HEADS, 2 * BLOCK, Q_PER_KV * BLOCK)


def _attn_kernel(q_ref, kp_ref, kc_ref, vp_ref, vc_ref, bias_ref, sink_ref, o_ref):
    n = pl.program_id(1)
    q = q_ref[...] * (HEAD_DIM ** -0.5)

    def heads(early_keys):
        for kh in range(N_KV_HEADS):
            hs = pl.ds(kh * HEAD_DIM, HEAD_DIM)
            k2 = jnp.concatenate([kp_ref[:, hs], kc_ref[:, hs]], axis=0)
            v2 = jnp.concatenate([vp_ref[:, hs], vc_ref[:, hs]], axis=0)
            first = kh * Q_PER_KV
            qg = jnp.concatenate([q[:, (first + g) * HEAD_DIM:(first + g + 1) * HEAD_DIM]
                                  for g in range(Q_PER_KV)], axis=0)
            s_all = lax.dot_general(k2, qg, (((1,), (1,)), ((), ())), preferred_element_type=F32)
            pn_parts = []
            for g in range(Q_PER_KV):
                cols = slice(g * BLOCK, (g + 1) * BLOCK)
                s = s_all[:, cols] + bias_ref[kh, :, cols]
                if early_keys is not None:
                    s = s + early_keys
                sk = sink_ref[pl.ds(kh, 1), cols]
                mx = jnp.maximum(jnp.max(s, axis=0, keepdims=True), sk)
                p = jnp.exp(s - mx)
                den = jnp.sum(p, axis=0, keepdims=True) + jnp.exp(sk - mx)
                pn_parts.append((p * (1.0 / den)).astype(BF16))
            pn = jnp.concatenate(pn_parts, axis=1)
            o = lax.dot_general(pn, v2, (((0,), (0,)), ((), ())), preferred_element_type=F32)
            for g in range(Q_PER_KV):
                o_ref[:, pl.ds((first + g) * HEAD_DIM, HEAD_DIM)] = o[g * BLOCK:(g + 1) * BLOCK].astype(o_ref.dtype)

    @pl.when(n < 2)
    def _():
        k_pos = (n - 1) * BLOCK + lax.broadcasted_iota(jnp.int32, (2 * BLOCK, 1), 0)
        heads(jnp.where(k_pos < PAD, NEG, 0.0))

    @pl.when(n >= 2)
    def _():
        heads(None)


def _attention(cols, sink, batch, lp):
    m = cols.shape[0]
    nb = lp // BLOCK
    cur = lambda cb: (lambda b, n: (b * nb + n, cb))
    prev = lambda cb: (lambda b, n: (b * nb + jnp.maximum(n - 1, 0), cb))
    return pl.pallas_call(
        _attn_kernel,
        grid=(batch, nb),
        in_specs=[
            pl.BlockSpec((BLOCK, ATT_W), cur(OFF_Q // ATT_W)),
            pl.BlockSpec((BLOCK, KV_W), prev(OFF_K // KV_W)),
            pl.BlockSpec((BLOCK, KV_W), cur(OFF_K // KV_W)),
            pl.BlockSpec((BLOCK, KV_W), prev(OFF_V // KV_W)),
            pl.BlockSpec((BLOCK, KV_W), cur(OFF_V // KV_W)),
            pl.BlockSpec((N_KV_HEADS, 2 * BLOCK, Q_PER_KV * BLOCK), lambda b, n: (0, 0, 0)),
            pl.BlockSpec((N_KV_HEADS, Q_PER_KV * BLOCK), lambda b, n: (0, 0)),
        ],
        out_specs=pl.BlockSpec((BLOCK, ATT_W), lambda b, n: (b * nb + n, 0)),
        out_shape=jax.ShapeDtypeStruct((m, ATT_W), BF16),
        compiler_params=_cparams("parallel", "arbitrary"),
        name="swa_attention",
    )(cols, cols, cols, cols, cols, _attn_bias(),
      jnp.repeat(sink.astype(F32).reshape(N_KV_HEADS, Q_PER_KV), BLOCK, axis=1))


LRU_HALF = LRU_WIDTH // 2
LRU_HALO = 8
LRU_SCAN_UNROLL = 6


LOG2_E = 1.4426950408889634
GELU_C = 0.7978845608028654


def _gelu_tanh(x):
    half = 0.5 * x
    return half + half * jnp.tanh(x * (GELU_C + (GELU_C * 0.044715) * (x * x)))


def _lru_kernel(*refs):
    nh = LRU_WIDTH // LRU_HALF
    x_refs, y_refs = refs[:nh], refs[nh:2 * nh]
    (cw_ref, cb_ref, wa_ref, ba_ref, wx_ref, bx_ref, lam_ref, o_ref,
     ext_ref, a_ref, b_ref, carry_ref) = refs[2 * nh:]
    t = pl.program_id(1)
    tile = SEQ_TILE
    width = LRU_HALF
    pos = t * tile + lax.broadcasted_iota(jnp.int32, (tile, 1), 0)
    row = lax.broadcasted_iota(jnp.int32, (8, width), 0)

    for c in range(nh):
        lanes = pl.ds(c * width, width)
        ext, a_s, b_s, carry = ext_ref.at[c], a_ref.at[c], b_ref.at[c], carry_ref.at[c]

        @pl.when(t == 0)
        def _():
            ext[pl.ds(0, LRU_HALO), :] = jnp.zeros((LRU_HALO, width), F32)
            carry[...] = jnp.zeros((1, width), F32)

        @pl.when(t > 0)
        def _():
            ext[pl.ds(0, LRU_HALO), :] = ext[pl.ds(tile, LRU_HALO), :]

        ext[pl.ds(LRU_HALO, tile), :] = x_refs[c][...].astype(F32)

        xc = cb_ref[:, lanes] + cw_ref[pl.ds(CONV_WIDTH - 1, 1), lanes] * ext[pl.ds(LRU_HALO, tile), :]
        for j in range(CONV_WIDTH - 1):
            shift = CONV_WIDTH - 1 - j
            xc = xc + cw_ref[pl.ds(j, 1), lanes] * ext[pl.ds(LRU_HALO - shift, tile), :]

        xcb = xc.astype(BF16)
        ga_parts, gx_parts = [], []
        for blk in range(width // LRU_BLOCK):
            xb = xcb[:, blk * LRU_BLOCK:(blk + 1) * LRU_BLOCK]
            w_idx = c * (width // LRU_BLOCK) + blk
            ga_parts.append(jnp.dot(xb, wa_ref[w_idx], preferred_element_type=F32))
            gx_parts.append(jnp.dot(xb, wx_ref[w_idx], preferred_element_type=F32))
        gate_a = _sigmoid(jnp.concatenate(ga_parts, axis=1) + ba_ref[:, lanes])
        gate_x = _sigmoid(jnp.concatenate(gx_parts, axis=1) + bx_ref[:, lanes])

        neg_lam = -lam_ref[:, lanes]
        softplus = jnp.maximum(neg_lam, 0.0) + jnp.log1p(jnp.exp(-jnp.abs(neg_lam)))
        a = jnp.exp2(gate_a * ((-LRU_C * LOG2_E) * softplus))
        b_in = jnp.sqrt(1.0 - a * a) * gate_x * xc
        b_in = jnp.where(pos >= PAD, b_in, 0.0)
        a_s[...] = a
        b_s[...] = b_in

        def group(r, h_prev, a_s=a_s, b_s=b_s):
            rows = pl.ds(pl.multiple_of(r * 8, 8), 8)
            av = a_s[rows, :]
            bv = b_s[rows, :]
            for k in (1, 2, 4):
                a_sh = jnp.where(row >= k, pltpu.roll(av, k, 0), 1.0)
                b_sh = jnp.where(row >= k, pltpu.roll(bv, k, 0), 0.0)
                bv = av * b_sh + bv
                av = av * a_sh
            hv = av * h_prev + bv
            b_s[rows, :] = hv
            return hv[7:8, :]

        carry[...] = lax.fori_loop(0, tile // 8, group, carry[...], unroll=LRU_SCAN_UNROLL)
        o_ref[:, lanes] = (b_s[...] * _gelu_tanh(y_refs[c][...].astype(F32))).astype(o_ref.dtype)


def _lru(cols, conv_w, conv_b, wa_bf16, ba, wx_bf16, bx, lam, batch, lp):
    m = cols.shape[0]
    nt = lp // SEQ_TILE
    nh = LRU_WIDTH // LRU_HALF
    vec = lambda v: v.reshape(1, LRU_WIDTH).astype(F32)
    vec_spec = pl.BlockSpec((1, LRU_WIDTH), lambda b, t: (0, 0))
    w_spec = pl.BlockSpec((LRU_BLOCKS, LRU_BLOCK, LRU_BLOCK), lambda b, t: (0, 0, 0))
    half_spec = lambda off, c: pl.BlockSpec((SEQ_TILE, LRU_HALF), lambda b, t: (b * nt + t, off // LRU_HALF + c))
    return pl.pallas_call(
        _lru_kernel,
        grid=(batch, nt),
        in_specs=[half_spec(OFF_LX, c) for c in range(nh)] + [half_spec(OFF_LY, c) for c in range(nh)]
        + [pl.BlockSpec((CONV_WIDTH, LRU_WIDTH), lambda b, t: (0, 0)),
           vec_spec, w_spec, vec_spec, w_spec, vec_spec, vec_spec],
        out_specs=pl.BlockSpec((SEQ_TILE, LRU_WIDTH), lambda b, t: (b * nt + t, 0)),
        out_shape=jax.ShapeDtypeStruct((m, LRU_WIDTH), BF16),
        scratch_shapes=[
            pltpu.VMEM((nh, SEQ_TILE + LRU_HALO, LRU_HALF), F32),
            pltpu.VMEM((nh, SEQ_TILE, LRU_HALF), F32),
            pltpu.VMEM((nh, SEQ_TILE, LRU_HALF), F32),
            pltpu.VMEM((nh, 1, LRU_HALF), F32),
        ],
        compiler_params=_cparams("parallel", "arbitrary"),
        name="rglru",
    )(*([cols] * (2 * nh)), conv_w.astype(F32), vec(conv_b), wa_bf16, vec(ba), wx_bf16, vec(bx), vec(lam))


def _merge_kernel(p_ref, a_ref, r_ref, gp_ref, ga_ref, gr_ref, wp_ref, wa_ref, wr_ref, o_ref):
    acc = gp_ref[...].astype(F32) * jnp.dot(p_ref[...], wp_ref[...], preferred_element_type=F32)
    acc += ga_ref[...].astype(F32) * jnp.dot(a_ref[...], wa_ref[...], preferred_element_type=F32)
    acc += gr_ref[...].astype(F32) * jnp.dot(r_ref[...], wr_ref[...], preferred_element_type=F32)
    o_ref[...] = acc.astype(o_ref.dtype)


def _merge(pool_o, attn_o, lru_o, gates, wp, wa, wr, layer):
    m = pool_o.shape[0]
    bm, bn = MERGE_BM, MERGE_BN
    assert m % bm == 0 and D_MODEL % bn == 0
    x_spec = pl.BlockSpec((bm, POOL_WIDTH), lambda i, j: (i, 0))
    gate_spec = lambda k: pl.BlockSpec((bm, bn), lambda i, j: (i, k * D_MODEL // bn + j))
    w_spec = pl.BlockSpec((None, POOL_WIDTH, bn), lambda i, j: (layer, 0, j))
    return pl.pallas_call(
        _merge_kernel,
        grid=(m // bm, D_MODEL // bn),
        in_specs=[x_spec, x_spec, x_spec, gate_spec(0), gate_spec(1), gate_spec(2), w_spec, w_spec, w_spec],
        out_specs=pl.BlockSpec((bm, bn), lambda i, j: (i, j)),
        out_shape=jax.ShapeDtypeStruct((m, D_MODEL), BF16),
        compiler_params=_cparams("parallel", "arbitrary"),
        name="gated_merge",
    )(pool_o, attn_o, lru_o, gates, gates, gates, wp, wa, wr)


def _pack_bf16_pairs(yb):
    c = yb.shape[1] // 2
    lo = lax.bitcast_convert_type(yb[:, :c].astype(F32), jnp.uint32)
    hi = lax.bitcast_convert_type(yb[:, c:].astype(F32), jnp.uint32)
    return (hi & jnp.uint32(0xFFFF0000)) | (lo >> 16)


def _unpack_bf16_pairs(words, dtype=BF16):
    lo = lax.bitcast_convert_type(words << 16, F32)
    hi = lax.bitcast_convert_type(words & jnp.uint32(0xFFFF0000), F32)
    return jnp.concatenate([lo, hi], axis=1).astype(dtype)


def _route_tile(logits, valid, base):
    bm = logits.shape[0]
    lane = lax.broadcasted_iota(jnp.int32, (bm, ROUTE_W), 1)
    lane_f = lane.astype(F32)
    ninf = -jnp.inf
    big = float(ROUTE_W)

    gl = jnp.where(lane < N_GROUPS, logits, ninf)
    gmax = jnp.max(gl, axis=-1, keepdims=True)
    g = jnp.min(jnp.where(gl == gmax, lane_f, big), axis=-1, keepdims=True)
    p_g = 1.0 / jnp.sum(jnp.exp(gl - gmax), axis=-1, keepdims=True)

    first = N_GROUPS + g * EXPERTS_PER_GROUP
    sl = jnp.where((lane_f >= first) & (lane_f < first + EXPERTS_PER_GROUP), logits, ninf)
    m1 = jnp.max(sl, axis=-1, keepdims=True)
    i1 = jnp.min(jnp.where(sl == m1, lane_f, big), axis=-1, keepdims=True)
    ssum = jnp.sum(jnp.exp(sl - m1), axis=-1, keepdims=True)
    sl2 = jnp.where(lane_f == i1, ninf, sl)
    m2 = jnp.max(sl2, axis=-1, keepdims=True)
    i2 = jnp.min(jnp.where(sl2 == m2, lane_f, big), axis=-1, keepdims=True)
    p1 = 1.0 / ssum
    p2 = jnp.exp(m2 - m1) / ssum
    w1 = p_g * p1 / (p1 + p2)
    w2 = p_g * p2 / (p1 + p2)
    e1 = i1 - N_GROUPS
    e2 = i2 - N_GROUPS

    oh1 = (lane_f == e1) & valid
    oh2 = (lane_f == e2) & valid
    both = (oh1 | oh2).astype(F32)
    earlier = (lax.broadcasted_iota(jnp.int32, (bm, bm), 0) > lax.broadcasted_iota(jnp.int32, (bm, bm), 1))
    prefix = jnp.dot(earlier.astype(BF16), both.astype(BF16), preferred_element_type=F32) + base
    r1 = jnp.sum(jnp.where(oh1, prefix, 0.0), axis=-1, keepdims=True)
    r2 = jnp.sum(jnp.where(oh2, prefix, 0.0), axis=-1, keepdims=True)

    ri = jnp.where(lane == 0, e1, jnp.where(lane == 1, e2, jnp.where(lane == 2, r1, jnp.where(lane == 3, r2, 0.0))))
    rw = jnp.where(lane == 0, w1, jnp.where(lane == 1, w2, 0.0))
    ri_rows = jnp.transpose(ri)[0:ROUTE_ROWS, :].astype(jnp.int32)
    return ri_rows, rw, jnp.sum(both, axis=0, keepdims=True)


def _outproj_kernel(x_ref, w_ref, h_ref, g_ref, b_ref, rw_ref, rb_ref, hf_ref, hp_ref, ri_ref, rwt_ref, cnt_ref,
                    *, batch, lp):
    i = pl.program_id(0)
    bm = x_ref.shape[0]
    sub = bm // OUT_SPLIT

    @pl.when(i == 0)
    def _():
        cnt_ref[...] = jnp.zeros(cnt_ref.shape, F32)

    for s in range(OUT_SPLIT):
        rows = pl.ds(s * sub, sub)
        t = jnp.dot(x_ref[rows, :], w_ref[...], preferred_element_type=F32)
        y = _layer_norm(ALPHA * h_ref[rows, :] + t, g_ref[...], b_ref[...])
        valid = _flat_valid_rows(i * bm + s * sub, sub, batch, lp)
        y = jnp.where(valid, y, 0.0)
        yb = y.astype(BF16)
        hf_ref[rows, :] = y
        packed = _pack_bf16_pairs(yb)
        for c in range(ROW_TILES):
            hp_ref[pl.ds(s * sub * ROW_TILES + c, sub, stride=ROW_TILES), :] = packed[:, c * LANES:(c + 1) * LANES]
        logits = jnp.dot(yb, rw_ref[...], preferred_element_type=F32) + rb_ref[...]
        ri, rw, tile_cnt = _route_tile(logits, valid, cnt_ref[...])
        ri_ref[:, rows] = ri
        rwt_ref[rows, :] = rw
        cnt_ref[...] += tile_cnt


def _outproj_ln(merged, w_out_bf16, h, g, b, route_w, route_b, layer, batch, lp):
    m, d = h.shape
    bm = OUT_BM
    assert m % bm == 0
    row_spec = pl.BlockSpec((bm, d), lambda i: (i, 0))
    vec_spec = pl.BlockSpec((1, d), lambda i: (0, 0))
    route_spec = pl.BlockSpec((bm, ROUTE_W), lambda i: (i, 0))
    return pl.pallas_call(
        functools.partial(_outproj_kernel, batch=batch, lp=lp),
        grid=(m // bm,),
        in_specs=[
            row_spec,
            pl.BlockSpec((None, d, d), lambda i: (layer, 0, 0)),
            row_spec, vec_spec, vec_spec,
            pl.BlockSpec((d, ROUTE_W), lambda i: (0, 0)),
            pl.BlockSpec((1, ROUTE_W), lambda i: (0, 0)),
        ],
        out_specs=[row_spec, pl.BlockSpec((bm * ROW_TILES, LANES), lambda i: (i, 0)),
                   pl.BlockSpec((ROUTE_ROWS, bm), lambda i: (0, i)), route_spec,
                   pl.BlockSpec((1, ROUTE_W), lambda i: (0, 0))],
        out_shape=[jax.ShapeDtypeStruct((m, d), F32), jax.ShapeDtypeStruct((m * ROW_TILES, LANES), jnp.uint32),
                   jax.ShapeDtypeStruct((ROUTE_ROWS, m), jnp.int32), jax.ShapeDtypeStruct((m, ROUTE_W), F32),
                   jax.ShapeDtypeStruct((1, ROUTE_W), F32)],
        compiler_params=_cparams("arbitrary"),
        name="out_proj_ln",
    )(merged, w_out_bf16, h, g.reshape(1, d), b.reshape(1, d), route_w, route_b)


def _dispatch_kernel(start_ref, nchunk_ref, dest_hbm, hp_ref, xs_ref, idx_ref, zero_ref, idx_sem, row_sem, zero_sem,
                     *, n_expert_rows):
    i = pl.program_id(0)
    bm = hp_ref.shape[0] // ROW_TILES
    groups = bm // SUBLANES
    n_idx = TOP_K * bm

    def token_rows(first_token, n_tokens):
        return pl.ds(pl.multiple_of(first_token * ROW_TILES, ROW_TILES), n_tokens * ROW_TILES)

    @pl.when(i == 0)
    def _():
        zero_ref[...] = jnp.zeros(zero_ref.shape, zero_ref.dtype)

        def last_chunk(e):
            first = pl.multiple_of(start_ref[e] + (nchunk_ref[e] - 1) * MOE_BM, MOE_BM)
            return pltpu.make_async_copy(zero_ref, xs_ref.at[token_rows(first, MOE_BM)], zero_sem)

        for e in range(N_EXPERTS):
            @pl.when(nchunk_ref[e] > 0)
            def _():
                last_chunk(e).start()

        for e in range(N_EXPERTS):
            @pl.when(nchunk_ref[e] > 0)
            def _():
                last_chunk(e).wait()

        used = start_ref[N_EXPERTS - 1] + nchunk_ref[N_EXPERTS - 1] * MOE_BM

        def tail_chunk(j):
            return pltpu.make_async_copy(
                zero_ref, xs_ref.at[token_rows(pl.multiple_of(used + j * MOE_BM, MOE_BM), MOE_BM)], zero_sem)

        n_tail = (n_expert_rows - used) // MOE_BM
        lax.fori_loop(0, n_tail, lambda j, c: (tail_chunk(j).start(), c)[1], 0)
        lax.fori_loop(0, n_tail, lambda j, c: (tail_chunk(j).wait(), c)[1], 0)

    idx_copy = pltpu.make_async_copy(dest_hbm.at[pl.ds(pl.multiple_of(i * n_idx, n_idx), n_idx)], idx_ref, idx_sem)
    idx_copy.start()
    idx_copy.wait()

    def issue(g, carry):
        for s in range(SUBLANES):
            for k in range(TOP_K):
                slot = idx_ref[g * SUBLANES + (k * bm + s)]
                pltpu.make_async_copy(hp_ref.at[token_rows(g * SUBLANES + s, 1)], xs_ref.at[token_rows(slot, 1)],
                                      row_sem).start(priority=k % 2)
        return carry

    lax.fori_loop(0, groups, issue, 0)
    for k in range(TOP_K):
        pltpu.make_async_copy(xs_ref.at[token_rows(0, bm)], xs_ref.at[token_rows(0, bm)], row_sem).wait()


def _dispatch(hp, dest, chunk_start, n_chunks, n_expert_rows, n_slots):
    m = hp.shape[0] // ROW_TILES
    bm = DISPATCH_BM
    assert m % bm == 0 and bm % SUBLANES == 0
    grid_spec = pltpu.PrefetchScalarGridSpec(
        num_scalar_prefetch=2,
        grid=(m // bm,),
        in_specs=[
            pl.BlockSpec(memory_space=pl.ANY),
            pl.BlockSpec((bm * ROW_TILES, LANES), lambda i, st, nc: (i, 0)),
        ],
        out_specs=pl.BlockSpec(memory_space=pl.ANY),
        scratch_shapes=[pltpu.SMEM((TOP_K * bm,), jnp.int32), pltpu.VMEM((MOE_BM * ROW_TILES, LANES), jnp.uint32),
                        pltpu.SemaphoreType.DMA(()), pltpu.SemaphoreType.DMA(()), pltpu.SemaphoreType.DMA(())],
    )
    return pl.pallas_call(
        functools.partial(_dispatch_kernel, n_expert_rows=n_expert_rows),
        grid_spec=grid_spec,
        out_shape=jax.ShapeDtypeStruct((n_slots * ROW_TILES, LANES), jnp.uint32),
        compiler_params=_cparams("arbitrary"),
        name="moe_dispatch",
    )(chunk_start, n_chunks, dest, hp)


def _expert_kernel(start_ref, nchunk_ref, xs_hbm, wg_hbm, wu_hbm, wd_hbm, yb_hbm, wg_f, wu_f, wd_f,
                   x_buf, o_buf, w_sem, in_sem, out_sem, *, layer):
    e = pl.program_id(0)
    n = nchunk_ref[e]
    bm = MOE_BM
    first = start_ref[e] // bm
    total = start_ref[N_EXPERTS - 1] // bm + nchunk_ref[N_EXPERTS - 1]
    wslot = e % 2

    def w_copies(expert, slot):
        return [pltpu.make_async_copy(src.at[layer, expert], dst.at[slot], w_sem.at[slot])
                for src, dst in ((wg_hbm, wg_f), (wu_hbm, wu_f), (wd_hbm, wd_f))]

    def o_rows(g):
        return pl.ds(pl.multiple_of(g * (bm * ROW_TILES), bm * ROW_TILES), bm * ROW_TILES)

    def x_copy(g):
        slot = g % EXPERT_X_BUFFERS
        return pltpu.make_async_copy(xs_hbm.at[o_rows(g)], x_buf.at[slot], in_sem.at[slot])

    def o_copy(g):
        slot = g % 2
        return pltpu.make_async_copy(o_buf.at[slot], yb_hbm.at[o_rows(g)], out_sem.at[slot])

    @pl.when(e == 0)
    def _():
        for ahead in range(EXPERT_X_BUFFERS - 1):
            @pl.when(ahead < total)
            def _():
                x_copy(ahead).start()

        for c in w_copies(0, 0):
            c.start(priority=1)

    @pl.when(e + 1 < N_EXPERTS)
    def _():
        for c in w_copies(e + 1, 1 - wslot):
            c.start(priority=1)

    for c in w_copies(e, wslot):
        c.wait()

    @pl.when(n > 0)
    def _():
        def chunk(j, carry):
            g = first + j
            x_copy(g).wait()

            @pl.when(g + EXPERT_X_BUFFERS - 1 < total)
            def _():
                x_copy(g + EXPERT_X_BUFFERS - 1).start()

            @pl.when(g >= 2)
            def _():
                o_copy(g - 2).wait()

            x_slot = g % EXPERT_X_BUFFERS
            x = _unpack_bf16_pairs(jnp.concatenate(
                [x_buf[x_slot, pl.ds(c, bm, stride=ROW_TILES), :] for c in range(ROW_TILES)], axis=1))
            gate = jnp.dot(x, wg_f[wslot].astype(BF16), preferred_element_type=F32)
            up = jnp.dot(x, wu_f[wslot].astype(BF16), preferred_element_type=F32)
            hdn = (gate * _sigmoid(gate)) * up
            y = jnp.dot(hdn.astype(BF16), wd_f[wslot].astype(BF16), preferred_element_type=F32)
            packed = _pack_bf16_pairs(y.astype(BF16))
            for c in range(ROW_TILES):
                o_buf[g % 2, pl.ds(c, bm, stride=ROW_TILES), :] = packed[:, c * LANES:(c + 1) * LANES]
            o_copy(g).start()
            return carry

        lax.fori_loop(0, n, chunk, 0)

    @pl.when(e == N_EXPERTS - 1)
    def _():
        for back in (2, 1):
            @pl.when(total >= back)
            def _():
                o_copy(total - back).wait()

        o_buf[0] = jnp.zeros(o_buf.shape[1:], o_buf.dtype)

        def fill(g, carry):
            tail = pltpu.make_async_copy(o_buf.at[0], yb_hbm.at[o_rows(g)], out_sem.at[0])
            tail.start()
            tail.wait()
            return carry

        lax.fori_loop(total, yb_hbm.shape[0] // (bm * ROW_TILES), fill, 0)


def _experts(xs, n_rows, chunk_start, n_chunks, w_gate, w_up, w_down, layer):
    d = D_MODEL
    bm = MOE_BM
    any_spec = pl.BlockSpec(memory_space=pl.ANY)
    grid_spec = pltpu.PrefetchScalarGridSpec(
        num_scalar_prefetch=2,
        grid=(N_EXPERTS,),
        in_specs=[any_spec, any_spec, any_spec, any_spec],
        out_specs=any_spec,
        scratch_shapes=[
            pltpu.VMEM((2, d, D_EXPERT), F32),
            pltpu.VMEM((2, d, D_EXPERT), F32),
            pltpu.VMEM((2, D_EXPERT, d), F32),
            pltpu.VMEM((EXPERT_X_BUFFERS, bm * ROW_TILES, LANES), jnp.uint32),
            pltpu.VMEM((2, bm * ROW_TILES, LANES), jnp.uint32),
            pltpu.SemaphoreType.DMA((2,)),
            pltpu.SemaphoreType.DMA((EXPERT_X_BUFFERS,)),
            pltpu.SemaphoreType.DMA((2,)),
        ],
    )
    return pl.pallas_call(
        functools.partial(_expert_kernel, layer=layer),
        grid_spec=grid_spec,
        out_shape=jax.ShapeDtypeStruct((n_rows * ROW_TILES, LANES), jnp.uint32),
        compiler_params=_cparams("arbitrary"),
        name="expert_mlp",
    )(chunk_start, n_chunks, xs, w_gate, w_up, w_down)


def _combine_ln_kernel(dest_hbm, h_ref, rw_ref, g_ref, b_ref, yb_hbm, *rest, batch, lp, final):
    if final:
        out_hbm, idx_ref, rows_ref, y_buf, idx_sem, row_sem, out_sem = rest
    else:
        hf_ref, hb_ref, idx_ref, rows_ref, idx_sem, row_sem = rest
    i = pl.program_id(0)
    n_tiles = pl.num_programs(0)
    bm = h_ref.shape[0]
    n_idx = TOP_K * bm
    buf = i % COMBINE_ROW_BUFFERS

    def idx_copy(tile):
        half = pl.ds(pl.multiple_of((tile % 2) * n_idx, n_idx), n_idx)
        return pltpu.make_async_copy(
            dest_hbm.at[pl.ds(pl.multiple_of(tile * n_idx, n_idx), n_idx)], idx_ref.at[half], idx_sem.at[tile % 2])

    def start_gather(tile):
        into = tile % COMBINE_ROW_BUFFERS
        idx_copy(tile).wait()

        def issue(g, carry):
            base = (tile % 2) * n_idx + g * SUBLANES
            for s in range(SUBLANES):
                for k in range(TOP_K):
                    slot = idx_ref[base + (k * bm + s)]
                    src = yb_hbm.at[pl.ds(pl.multiple_of(slot * ROW_TILES, ROW_TILES), ROW_TILES)]
                    dst = rows_ref.at[into, k, pl.ds(pl.multiple_of((g * SUBLANES + s) * ROW_TILES, ROW_TILES),
                                                     ROW_TILES)]
                    pltpu.make_async_copy(src, dst, row_sem.at[into]).start(priority=k % 2)
            return carry

        lax.fori_loop(0, bm // SUBLANES, issue, 0)

        @pl.when(tile + 2 < n_tiles)
        def _():
            idx_copy(tile + 2).start()

    @pl.when(i == 0)
    def _():
        idx_copy(0).start()

        @pl.when(1 < n_tiles)
        def _():
            idx_copy(1).start()

        for ahead in range(COMBINE_ROW_BUFFERS - 1):
            @pl.when(ahead < n_tiles)
            def _():
                start_gather(ahead)

    @pl.when(i + COMBINE_ROW_BUFFERS - 1 < n_tiles)
    def _():
        start_gather(i + COMBINE_ROW_BUFFERS - 1)

    for k in range(TOP_K):
        whole = pl.ds(0, bm * ROW_TILES)
        pltpu.make_async_copy(yb_hbm.at[whole], yb_hbm.at[whole], row_sem.at[buf]).wait()

    def expert_rows(k):
        pieces = [rows_ref[buf, k, pl.ds(c, bm, stride=ROW_TILES), :] for c in range(ROW_TILES)]
        return _unpack_bf16_pairs(jnp.concatenate(pieces, axis=1), F32)

    ffn = rw_ref[:, 0:1] * expert_rows(0)
    for k in range(1, TOP_K):
        ffn = ffn + rw_ref[:, k:k + 1] * expert_rows(k)
    y = _layer_norm(ALPHA * h_ref[...] + ffn, g_ref[...], b_ref[...])
    if not final:
        y = jnp.where(_flat_valid_rows(i * bm, bm, batch, lp), y, 0.0)
        hf_ref[...] = y
        hb_ref[...] = y.astype(BF16)
        return

    nb = lp // BLOCK

    def for_each_out_block(step, fn):
        for j in range(bm // BLOCK):
            blk = step * (bm // BLOCK) + j
            seq_blk = blk % nb

            @pl.when(seq_blk >= 1)
            def _():
                dst = pl.multiple_of(((blk // nb) * (nb - 1) + seq_blk - 1) * BLOCK, BLOCK)
                fn(pltpu.make_async_copy(y_buf.at[pl.ds(j * BLOCK, BLOCK)], out_hbm.at[pl.ds(dst, BLOCK)], out_sem))

    @pl.when(i > 0)
    def _():
        for_each_out_block(i - 1, lambda c: c.wait())

    y_buf[...] = y
    for_each_out_block(i, lambda c: c.start(priority=1))

    @pl.when(i == n_tiles - 1)
    def _():
        for_each_out_block(i, lambda c: c.wait())


def _combine_ln(h, yb, dest, rw, g, b, batch, lp, final):
    m, d = h.shape
    bm = COMBINE_BM
    assert m % bm == 0 and bm % BLOCK == 0
    row_spec = pl.BlockSpec((bm, d), lambda i: (i, 0))
    vec_spec = pl.BlockSpec((1, d), lambda i: (0, 0))
    scratch = [pltpu.SMEM((2 * TOP_K * bm,), jnp.int32),
               pltpu.VMEM((COMBINE_ROW_BUFFERS, TOP_K, bm * ROW_TILES, LANES), jnp.uint32)]
    sems = [pltpu.SemaphoreType.DMA((2,)), pltpu.SemaphoreType.DMA((COMBINE_ROW_BUFFERS,))]
    if final:
        out_specs = pl.BlockSpec(memory_space=pl.ANY)
        out_shape = jax.ShapeDtypeStruct((batch * (lp - BLOCK), d), F32)
        scratch = scratch + [pltpu.VMEM((bm, d), F32)] + sems + [pltpu.SemaphoreType.DMA(())]
    else:
        out_specs = [row_spec, row_spec]
        out_shape = [jax.ShapeDtypeStruct((m, d), F32), jax.ShapeDtypeStruct((m, d), BF16)]
        scratch = scratch + sems
    return pl.pallas_call(
        functools.partial(_combine_ln_kernel, batch=batch, lp=lp, final=final),
        grid=(m // bm,),
        in_specs=[
            pl.BlockSpec(memory_space=pl.ANY),
            row_spec,
            pl.BlockSpec((bm, ROUTE_W), lambda i: (i, 0)),
            vec_spec, vec_spec,
            pl.BlockSpec(memory_space=pl.ANY),
        ],
        out_specs=out_specs,
        out_shape=out_shape,
        scratch_shapes=scratch,
        compiler_params=_cparams("arbitrary"),
        name="moe_combine_out" if final else "moe_combine_ln",
    )(dest, h, rw, g.reshape(1, d), b.reshape(1, d), yb)


def _slot_tables(ri, cnt, batch, lp):
    m = ri.shape[1]
    bm = MOE_BM
    eid = ri[0:TOP_K]
    rank = ri[TOP_K:2 * TOP_K]
    counts = cnt[0, :N_EXPERTS].astype(jnp.int32)
    padded = (counts + bm - 1) // bm * bm
    pad_end = jnp.cumsum(padded)
    pad_start = pad_end - padded
    n_real = batch * (lp - PAD) * TOP_K
    nblk = -(-(n_real + N_EXPERTS * (bm - 1)) // bm)
    cap = nblk * bm
    start = jnp.zeros_like(eid)
    for e in range(N_EXPERTS):
        start = jnp.where(eid == e, pad_start[e], start)
    row = jnp.arange(m, dtype=jnp.int32)
    pos = row % lp
    spare = cap + ((row // lp) * PAD + pos)[None, :] * TOP_K + jnp.arange(TOP_K, dtype=jnp.int32)[:, None]
    valid = (pos >= PAD)[None, :]
    dest = jnp.where(valid, start + rank, spare)
    dest_read = jnp.where(valid, dest, 0)
    n_slots = cap + batch * PAD * TOP_K

    def by_tile(idx, tile):
        return idx.reshape(TOP_K, m // tile, tile).transpose(1, 0, 2).reshape(m * TOP_K)

    return (by_tile(dest, DISPATCH_BM), by_tile(dest_read, COMBINE_BM), pad_start.astype(jnp.int32),
            (padded // bm).astype(jnp.int32), cap, n_slots)


def kernel(x, meta, ln_emb_g, ln_emb_b, w_in, pool_w, pool_scale, attn_sink, conv_w, conv_b, lru_wa, lru_ba,
           lru_wx, lru_bx, lru_lambda, proj_pool, proj_attn, proj_lru, w_out, ln1_g, ln1_b, router_grp_w,
           router_grp_b, router_exp_w, router_exp_b, exp_w_gate, exp_w_up, exp_w_down, ln2_g, ln2_b):
    batch, seq, d = x.shape
    lp = PAD + N_META + seq
    m = batch * lp

    hf, hb = _embed(x, meta, ln_emb_g, ln_emb_b)
    w_in_b = w_in.astype(BF16)
    pool_w_b = pool_w.astype(BF16)
    wa_b = lru_wa.astype(BF16)
    wx_b = lru_wx.astype(BF16)
    wp_b = proj_pool.astype(BF16)
    wat_b = proj_attn.astype(BF16)
    wl_b = proj_lru.astype(BF16)
    w_out_b = w_out.astype(BF16)
    route_pad = ROUTE_W - N_GROUPS - N_EXPERTS

    for l in range(DEPTH):
        cols = _inproj(hb, w_in_b, l, gates=False)
        gates = _inproj(hb, w_in_b, l, gates=True)
        pool_o = _pool(cols, pool_w_b[l], pool_scale[l], batch, lp)
        attn_o = _attention(cols, attn_sink[l], batch, lp)
        lru_o = _lru(cols, conv_w[l], conv_b[l], wa_b[l], lru_ba[l], wx_b[l], lru_bx[l], lru_lambda[l],
                     batch, lp)
        merged = _merge(pool_o, attn_o, lru_o, gates, wp_b, wat_b, wl_b, l)
        route_w = jnp.concatenate(
            [router_grp_w[l], router_exp_w[l], jnp.zeros((d, route_pad), F32)], axis=1).astype(BF16)
        route_b = jnp.concatenate(
            [router_grp_b[l], router_exp_b[l], jnp.zeros((route_pad,), F32)]).reshape(1, ROUTE_W)
        h1f, h1p, ri, rw, cnt = _outproj_ln(merged, w_out_b, hf, ln1_g[l], ln1_b[l], route_w, route_b, l,
                                            batch, lp)
        dest, dest_read, chunk_start, n_chunks, cap, n_slots = _slot_tables(ri, cnt, batch, lp)
        xs = _dispatch(h1p, dest, chunk_start, n_chunks, cap, n_slots)
        yb = _experts(xs, cap, chunk_start, n_chunks, exp_w_gate, exp_w_up, exp_w_down, l)
        if l + 1 < DEPTH:
            hf, hb = _combine_ln(h1f, yb, dest_read, rw, ln2_g[l], ln2_b[l], batch, lp, final=False)
        else:
            out = _combine_ln(h1f, yb, dest_read, rw, ln2_g[l], ln2_b[l], batch, lp, final=True)

    return out.reshape(batch, seq, d)
```
